```python
import jax
import jax.numpy as jnp
from jax import lax
import numpy as np

D_MODEL = 1024
BATCH = 4
SEQ = 4096
DEPTH = 4

GRID_W = 64
CTX_LEN = 256
HEAD_DIM = 64
GROUP_WIDTH = D_MODEL // 4
N_HEADS_A = GROUP_WIDTH // HEAD_DIM
N_HEADS_B = GROUP_WIDTH // HEAD_DIM
N_HEADS_C = GROUP_WIDTH // HEAD_DIM
N_KV_C = N_HEADS_C // 2
KV_WIDTH_C = N_KV_C * HEAD_DIM
DECAY_RANK = GROUP_WIDTH // 8
ICLR_RANK = GROUP_WIDTH // 8
GATE_RANK = GROUP_WIDTH // 4
RWKV_GN_EPS = 1e-5 * HEAD_DIM
NA_ROWS = 8
NA_COLS = 16
Q_BLOCK = 128
ROPE_THETA = 10000.0
POOL_WINDOWS = (2, 4, 8, 16)
POOL_GROUP = GROUP_WIDTH // 4
D_FF = 4 * D_MODEL
NORM_EPS = 1e-6
N_MOD = 6
PROJ_SPLITS = (GROUP_WIDTH, GROUP_WIDTH, GROUP_WIDTH, DECAY_RANK, DECAY_RANK, ICLR_RANK, ICLR_RANK, GATE_RANK,
               GROUP_WIDTH, GROUP_WIDTH, GROUP_WIDTH,
               GROUP_WIDTH, KV_WIDTH_C, KV_WIDTH_C,
               GROUP_WIDTH)
D_IN = sum(PROJ_SPLITS)
SLICE_A = slice(0, 8)
SLICE_B = slice(8, 11)
SLICE_C = slice(11, 14)
INDEX_D = 14

kernel_name = 'hybrid_rwkv7_natten_gqa_pool_dit_trunk'


def rms_norm(x, gain):
    xf = x.astype(jnp.float32)
    y = xf * lax.rsqrt(jnp.mean(xf * xf, axis=-1, keepdims=True) + NORM_EPS)
    return (y * gain.astype(jnp.float32)).astype(x.dtype)


def modulate(h, shift, scale):
    return h * (1.0 + scale) + shift


def split_projection(p):
    points = [int(s) for s in np.cumsum(PROJ_SPLITS)[:-1]]
    return jnp.split(p, points, axis=-1)


def heads_first(t, n_heads):
    b, l, _ = t.shape
    return t.reshape(b, l, n_heads, HEAD_DIM).transpose(0, 2, 1, 3)


def heads_last(t):
    b, h, l, d = t.shape
    return t.transpose(0, 2, 1, 3).reshape(b, l, h * d)


def axial_rope_tables(n_tokens):
    t = jnp.arange(n_tokens)
    row = (t // GRID_W).astype(jnp.float32)
    col = (t % GRID_W).astype(jnp.float32)
    n_freq = HEAD_DIM // 4
    inv_freq = ROPE_THETA ** (-jnp.arange(n_freq, dtype=jnp.float32) / n_freq)
    ang = jnp.concatenate([row[:, None] * inv_freq, col[:, None] * inv_freq], axis=-1)
    return jnp.cos(ang), jnp.sin(ang)


def apply_rope(x, cos, sin):
    xf = x.astype(jnp.float32)
    x1, x2 = jnp.split(xf, 2, axis=-1)
    return jnp.concatenate([x1 * cos - x2 * sin, x1 * sin + x2 * cos], axis=-1).astype(x.dtype)


def softmax_attention(q, k, v):
    s = jnp.einsum('bhqd,bhkd->bhqk', q, k).astype(jnp.float32) * HEAD_DIM ** -0.5
    p = jax.nn.softmax(s, axis=-1).astype(v.dtype)
    return jnp.einsum('bhqk,bhkd->bhqd', p, v)


def gqa_block_sweep(q, k, v):
    b, hq, lq, d = q.shape
    hkv = k.shape[1]
    groups = hq // hkv
    n_blocks = lq // Q_BLOCK
    qb = jnp.moveaxis(q.reshape(b, hkv, groups, n_blocks, Q_BLOCK, d), 3, 0)

    def one_block(q_blk):
        s = jnp.einsum('bkgqd,bksd->bkgqs', q_blk, k).astype(jnp.float32) * d ** -0.5
        p = jax.nn.softmax(s, axis=-1).astype(v.dtype)
        return jnp.einsum('bkgqs,bksd->bkgqd', p, v)

    out = lax.map(one_block, qb)
    return jnp.moveaxis(out, 0, 3).reshape(b, hq, lq, d)


def rwkv_heads(t):
    b, l, _ = t.shape
    return t.astype(jnp.float32).reshape(b, l, N_HEADS_A, HEAD_DIM)


def rwkv_direction_terms(k, w_lo, a_lo, w0, w2, a0, a2, k_k, k_a):
    kf = k.astype(jnp.float32)
    w = -jax.nn.softplus(-(w0 + jnp.tanh(w_lo) @ w2).astype(jnp.float32)) - 0.5
    decay = jnp.exp(-jnp.exp(w))
    a = jax.nn.sigmoid((a0 + a_lo @ a2).astype(jnp.float32))
    kk = rwkv_heads(kf * k_k)
    kk = kk / jnp.maximum(jnp.sqrt(jnp.sum(kk * kk, axis=-1, keepdims=True)), 1e-12)
    k_mod = kf * (1.0 + (a - 1.0) * k_a)
    return rwkv_heads(decay), kk, kk * rwkv_heads(a), rwkv_heads(k_mod)


def wkv7_scan(state0, r, v, decay, kk, b, k, reverse, with_output):
    to_time = lambda t: jnp.moveaxis(t, 1, 0)

    def update(s, w_t, kk_t, b_t, k_t, v_t):
        sa = jnp.einsum('bhvk,bhk->bhv', s, kk_t)
        return s * w_t[:, :, None, :] - sa[..., None] * b_t[:, :, None, :] + v_t[..., None] * k_t[:, :, None, :]

    if with_output:
        def step(s, xs):
            r_t, w_t, kk_t, b_t, k_t, v_t = xs
            s = update(s, w_t, kk_t, b_t, k_t, v_t)
            return s, jnp.einsum('bhvk,bhk->bhv', s, r_t)
        s_final, ys = lax.scan(step, state0, tuple(map(to_time, (r, decay, kk, b, k, v))), reverse=reverse)
        return s_final, jnp.moveaxis(ys, 0, 1)

    def step_state(s, xs):
        return update(s, *xs), None
    s_final, _ = lax.scan(step_state, state0, tuple(map(to_time, (decay, kk, b, k, v))), reverse=reverse)
    return s_final, None


def rwkv_readout(y, r, v, k_f, k_b, g_lo, r_k, g2, gn_w, gn_b):
    mu = jnp.mean(y, axis=-1, keepdims=True)
    var = jnp.mean(jnp.square(y - mu), axis=-1, keepdims=True)
    yn = ((y - mu) * lax.rsqrt(var + RWKV_GN_EPS)).reshape(y.shape[0], y.shape[1], GROUP_WIDTH) * gn_w + gn_b
    bonus = jnp.sum(r * (k_f + k_b) * r_k, axis=-1, keepdims=True) * v
    gate = jax.nn.sigmoid(g_lo) @ g2
    return (yn + bonus.reshape(yn.shape)) * gate


def rwkv_mixer(p_lat, p_ctx, w0, w2, a0, a2, k_k, k_a, r_k, g2, gn_w, gn_b, need_ctx_out):
    def terms(p):
        r, k, v, w_f, w_b, a_f, a_b, g = p
        fwd = rwkv_direction_terms(k, w_f, a_f, w0[0], w2[0], a0[0], a2[0], k_k, k_a)
        bwd = rwkv_direction_terms(k, w_b, a_b, w0[1], w2[1], a0[1], a2[1], k_k, k_a)
        return rwkv_heads(r), rwkv_heads(v), fwd, bwd, g

    r_c, v_c, fwd_c, bwd_c, g_c = terms(p_ctx)
    zero = jnp.zeros((r_c.shape[0], N_HEADS_A, HEAD_DIM, HEAD_DIM), jnp.float32)
    s_f, y_cf = wkv7_scan(zero, r_c, v_c, *fwd_c, reverse=False, with_output=need_ctx_out)
    s_b, y_cb = wkv7_scan(zero, r_c, v_c, *bwd_c, reverse=True, with_output=need_ctx_out)
    r_l, v_l, fwd_l, bwd_l, g_l = terms(p_lat)
    _, y_lf = wkv7_scan(s_f, r_l, v_l, *fwd_l, reverse=False, with_output=True)
    _, y_lb = wkv7_scan(s_b, r_l, v_l, *bwd_l, reverse=True, with_output=True)
    out_l = rwkv_readout(y_lf + y_lb, r_l, v_l, fwd_l[3], bwd_l[3], g_l, r_k, g2, gn_w, gn_b)
    out_c = None
    if need_ctx_out:
        out_c = rwkv_readout(y_cf + y_cb, r_c, v_c, fwd_c[3], bwd_c[3], g_c, r_k, g2, gn_w, gn_b)
    return out_l, out_c


def neighbourhood_mixer(p_lat, p_ctx, rpb, need_ctx_out):
    q, k, v = p_lat
    q_c, k_c, v_c = p_ctx
    bsz, n_lat, _ = q.shape
    rows = n_lat // GRID_W
    kh = min(NA_ROWS, rows)

    def grid(t):
        return t.reshape(bsz, rows, GRID_W, N_HEADS_B, HEAD_DIM).transpose(0, 3, 1, 2, 4)

    qg, kg, vg = grid(q), grid(k), grid(v)
    r_idx = jnp.arange(rows)
    row_start = jnp.clip(r_idx - kh // 2, 0, rows - kh)
    row_idx = row_start[:, None] + jnp.arange(kh)
    k_band = kg[:, :, row_idx]
    v_band = vg[:, :, row_idx]
    col = jnp.arange(GRID_W)
    col_start = jnp.clip(col - NA_COLS // 2, 0, GRID_W - NA_COLS)
    in_win = (col[None, :] >= col_start[:, None]) & (col[None, :] < col_start[:, None] + NA_COLS)
    col_off = jnp.clip(col[None, :] - col[:, None], -(NA_COLS - 1), NA_COLS - 1) + NA_COLS - 1
    row_off = row_idx - r_idx[:, None] + NA_ROWS - 1
    bias = rpb[:, row_off[:, None, :, None], col_off[None, :, None, :]].astype(jnp.float32)
    bias = jnp.where(in_win[None, None, :, None, :], bias, -jnp.inf)
    scale = HEAD_DIM ** -0.5
    s_loc = jnp.einsum('bhrqd,bhrikd->bhrqik', qg, k_band).astype(jnp.float32) * scale + bias[None]
    kcg, vcg = heads_first(k_c, N_HEADS_B), heads_first(v_c, N_HEADS_B)
    s_ctx = jnp.einsum('bhrqd,bhcd->bhrqc', qg, kcg).astype(jnp.float32) * scale
    n_loc = kh * GRID_W
    s = jnp.concatenate([s_loc.reshape(bsz, N_HEADS_B, rows, GRID_W, n_loc), s_ctx], axis=-1)
    p = jax.nn.softmax(s, axis=-1).astype(v.dtype)
    p_loc = p[..., :n_loc].reshape(bsz, N_HEADS_B, rows, GRID_W, kh, GRID_W)
    o = (jnp.einsum('bhrqik,bhrikd->bhrqd', p_loc, v_band)
         + jnp.einsum('bhrqc,bhcd->bhrqd', p[..., n_loc:], vcg))
    out_l = o.transpose(0, 2, 3, 1, 4).reshape(bsz, n_lat, GROUP_WIDTH)
    out_c = None
    if need_ctx_out:
        out_c = heads_last(softmax_attention(heads_first(q_c, N_HEADS_B), kcg, vcg))
    return out_l, out_c


def gqa_mixer(p_lat, p_ctx, q_gain, k_gain, cos, sin, need_ctx_out):
    q_l, k_l, v_l = p_lat
    q_c, k_c, v_c = p_ctx
    ql = apply_rope(rms_norm(heads_first(q_l, N_HEADS_C), q_gain), cos, sin)
    kl = apply_rope(rms_norm(heads_first(k_l, N_KV_C), k_gain), cos, sin)
    kc = rms_norm(heads_first(k_c, N_KV_C), k_gain)
    vc = heads_first(v_c, N_KV_C)
    k_all = jnp.concatenate([kc, kl], axis=2)
    v_all = jnp.concatenate([vc, heads_first(v_l, N_KV_C)], axis=2)
    out_l = heads_last(gqa_block_sweep(ql, k_all, v_all))
    out_c = None
    if need_ctx_out:
        qc = rms_norm(heads_first(q_c, N_HEADS_C), q_gain)
        out_c = heads_last(gqa_block_sweep(qc, kc, vc))
    return out_l, out_c


def centred_window_mean(x, window):
    n = x.shape[1]
    cs = jnp.pad(jnp.cumsum(x.astype(jnp.float32), axis=1), ((0, 0), (1, 0), (0, 0)))
    t = jnp.arange(n)
    lo = jnp.clip(t - window // 2, 0, n)
    hi = jnp.clip(t - window // 2 + window, 0, n)
    return (cs[:, hi] - cs[:, lo]) / (hi - lo).astype(jnp.float32)[None, :, None]


def pool_mixer(p, pool_w, pool_scale):
    groups = jnp.split(p, len(POOL_WINDOWS), axis=-1)
    outs = [jnp.einsum('blc,cd->bld', centred_window_mean(g, w) - g.astype(jnp.float32), pool_w[j])
            for j, (g, w) in enumerate(zip(groups, POOL_WINDOWS))]
    return jnp.concatenate(outs, axis=-1) * pool_scale


def squared_relu_mlp(h, w1, w2):
    return jnp.square(jax.nn.relu(h @ w1)) @ w2


def setup_inputs(seed: int = 0) -> dict:
    key = jax.random.key(seed)
    ks = jax.random.split(key, 32)
    nrm = jax.random.normal
    f32 = jnp.float32
    d = D_MODEL
    return {
        'x': nrm(ks[0], (BATCH, SEQ, d), f32),
        'c': nrm(ks[1], (BATCH, d), f32),
        'ctx': nrm(ks[2], (BATCH, CTX_LEN, d), f32),
        'c_ctx': nrm(ks[3], (d,), f32),
        'ada_w': nrm(ks[4], (DEPTH, d, N_MOD * d), f32) * (0.5 * d ** -0.5),
        'ada_b': nrm(ks[5], (DEPTH, N_MOD * d), f32) * 0.01,
        'norm1_g': 1.0 + 0.02 * nrm(ks[6], (DEPTH, d), f32),
        'norm2_g': 1.0 + 0.02 * nrm(ks[7], (DEPTH, d), f32),
        'w_in': nrm(ks[8], (DEPTH, d, D_IN), f32) * d ** -0.5,
        'w_out': nrm(ks[9], (DEPTH, d, d), f32) * d ** -0.5,
        'rwkv_w0': jax.random.uniform(ks[10], (DEPTH, 2, GROUP_WIDTH), f32, -5.0, 1.0),
        'rwkv_w2': nrm(ks[11], (DEPTH, 2, DECAY_RANK, GROUP_WIDTH), f32) * 0.1,
        'rwkv_a0': nrm(ks[12], (DEPTH, 2, GROUP_WIDTH), f32) * 0.1,
        'rwkv_a2': nrm(ks[13], (DEPTH, 2, ICLR_RANK, GROUP_WIDTH), f32) * 0.1,
        'rwkv_k_k': 0.85 + 0.02 * nrm(ks[14], (DEPTH, GROUP_WIDTH), f32),
        'rwkv_k_a': 1.0 + 0.02 * nrm(ks[15], (DEPTH, GROUP_WIDTH), f32),
        'rwkv_r_k': nrm(ks[16], (DEPTH, N_HEADS_A, HEAD_DIM), f32) * 0.1,
        'rwkv_g2': nrm(ks[17], (DEPTH, GATE_RANK, GROUP_WIDTH), f32) * GATE_RANK ** -0.5,
        'rwkv_gn_w': 1.0 + 0.02 * nrm(ks[18], (DEPTH, GROUP_WIDTH), f32),
        'rwkv_gn_b': 0.01 * nrm(ks[19], (DEPTH, GROUP_WIDTH), f32),
        'na_rpb': 0.02 * nrm(ks[20], (DEPTH, N_HEADS_B, 2 * NA_ROWS - 1, 2 * NA_COLS - 1), f32),
        'gqa_q_gain': 1.0 + 0.02 * nrm(ks[21], (DEPTH, HEAD_DIM), f32),
        'gqa_k_gain': 1.0 + 0.02 * nrm(ks[22], (DEPTH, HEAD_DIM), f32),
        'pool_w': nrm(ks[23], (DEPTH, len(POOL_WINDOWS), POOL_GROUP, POOL_GROUP), f32) * POOL_GROUP ** -0.5,
        'pool_scale': 1.0 + 0.02 * nrm(ks[24], (DEPTH, GROUP_WIDTH), f32),
        'mlp_w1': nrm(ks[25], (DEPTH, d, D_FF), f32) * d ** -0.5,
        'mlp_w2': nrm(ks[26], (DEPTH, D_FF, d), f32) * D_FF ** -0.5,
        'final_g': 1.0 + 0.02 * nrm(ks[27], (d,), f32),
    }


def reference(x, c, ctx, c_ctx, ada_w, ada_b, norm1_g, norm2_g, w_in, w_out,
              rwkv_w0, rwkv_w2, rwkv_a0, rwkv_a2, rwkv_k_k, rwkv_k_a, rwkv_r_k, rwkv_g2,
              rwkv_gn_w, rwkv_gn_b, na_rpb, gqa_q_gain, gqa_k_gain, pool_w, pool_scale,
              mlp_w1, mlp_w2, final_g):
    bsz, n_lat, _ = x.shape
    cos, sin = axial_rope_tables(n_lat)
    silu_c = jax.nn.silu(c.astype(jnp.float32))
    silu_c_ctx = jax.nn.silu(c_ctx.astype(jnp.float32))
    xc = ctx
    for i in range(DEPTH):
        need_ctx_out = i < DEPTH - 1
        mod_l = (silu_c @ ada_w[i] + ada_b[i]).reshape(bsz, N_MOD, 1, D_MODEL).astype(x.dtype)
        mod_c = (silu_c_ctx @ ada_w[i] + ada_b[i]).reshape(N_MOD, D_MODEL).astype(x.dtype)

        h = modulate(rms_norm(x, norm1_g[i]), mod_l[:, 0], mod_l[:, 1])
        hc = modulate(rms_norm(xc, norm1_g[i]), mod_c[0], mod_c[1])
        p_lat = split_projection(h @ w_in[i])
        p_ctx = split_projection(hc @ w_in[i])
        a_l, a_c = rwkv_mixer(p_lat[SLICE_A], p_ctx[SLICE_A], rwkv_w0[i], rwkv_w2[i], rwkv_a0[i], rwkv_a2[i],
                              rwkv_k_k[i], rwkv_k_a[i], rwkv_r_k[i], rwkv_g2[i], rwkv_gn_w[i], rwkv_gn_b[i],
                              need_ctx_out)
        b_l, b_c = neighbourhood_mixer(p_lat[SLICE_B], p_ctx[SLICE_B], na_rpb[i], need_ctx_out)
        c_l, c_c = gqa_mixer(p_lat[SLICE_C], p_ctx[SLICE_C], gqa_q_gain[i], gqa_k_gain[i], cos, sin, need_ctx_out)
        d_l = pool_mixer(p_lat[INDEX_D], pool_w[i], pool_scale[i])
        mix_l = jnp.concatenate([a_l.astype(x.dtype), b_l.astype(x.dtype), c_l.astype(x.dtype),
                                 d_l.astype(x.dtype)], axis=-1)
        x = x + mod_l[:, 2] * (mix_l @ w_out[i])

        h2 = modulate(rms_norm(x, norm2_g[i]), mod_l[:, 3], mod_l[:, 4])
        x = x + mod_l[:, 5] * squared_relu_mlp(h2, mlp_w1[i], mlp_w2[i])

        if need_ctx_out:
            d_c = pool_mixer(p_ctx[INDEX_D], pool_w[i], pool_scale[i])
            mix_c = jnp.concatenate([a_c.astype(xc.dtype), b_c.astype(xc.dtype), c_c.astype(xc.dtype),
                                     d_c.astype(xc.dtype)], axis=-1)
            xc = xc + mod_c[2] * (mix_c @ w_out[i])
            h2c = modulate(rms_norm(xc, norm2_g[i]), mod_c[3], mod_c[4])
            xc = xc + mod_c[5] * squared_relu_mlp(h2c, mlp_w1[i], mlp_w2[i])
    return rms_norm(x, final_g)
```

```python
import functools

import jax
import jax.numpy as jnp
import numpy as np
from jax import lax
from jax.experimental import pallas as pl
from jax.experimental.pallas import tpu as pltpu

F32 = jnp.float32
BF16 = jnp.bfloat16

HEAD_DIM = 64
GRID_W = 64
NA_ROWS = 8
NA_COLS = 16
ROPE_THETA = 10000.0
POOL_WINDOWS = (2, 4, 8, 16)
NORM_EPS = 1e-6
RWKV_GN_EPS = 1e-5 * HEAD_DIM
N_MOD = 6
LANES = 128
SCAN_BLOCK = 128
SCAN_SUB = 64
VMEM_LIMIT = 52 * 1024 * 1024


def _params(sem):
    return pltpu.CompilerParams(dimension_semantics=sem, vmem_limit_bytes=VMEM_LIMIT)


def _tile(n, pref):
    t = min(n, pref)
    assert n % t == 0, (n, pref)
    return t


def _dot(a, b):
    return jnp.dot(a, b, preferred_element_type=F32)


def _dot_nt(a, b):
    return lax.dot_general(a, b, (((1,), (1,)), ((), ())), preferred_element_type=F32)


def _seg_sum(x, e):
    hi = x.astype(BF16)
    lo = (x - hi.astype(F32)).astype(BF16)
    return _dot(hi, e) + _dot(lo, e)


def _norm_mod(x, g, shift, scale):
    ms = jnp.mean(x * x, axis=-1, keepdims=True)
    h = x * lax.rsqrt(ms + NORM_EPS) * g
    return h * (1.0 + scale) + shift


def _adaln_kernel(c_ref, w_ref, b_ref, o_ref):
    c = c_ref[...]
    s = c * jax.nn.sigmoid(c)
    o_ref[0] = jnp.dot(s, w_ref[0], preferred_element_type=F32, precision=lax.Precision.HIGHEST) + b_ref[0]


def adaln(cpad, ada_w, ada_b):
    depth, d, n = ada_w.shape
    tn = _tile(n, 1536)
    return pl.pallas_call(
        _adaln_kernel,
        grid=(depth, n // tn),
        in_specs=[pl.BlockSpec((8, d), lambda i, j: (0, 0)),
                  pl.BlockSpec((1, d, tn), lambda i, j: (i, 0, j)),
                  pl.BlockSpec((1, 1, tn), lambda i, j: (i, 0, j))],
        out_specs=pl.BlockSpec((1, 8, tn), lambda i, j: (i, 0, j)),
        out_shape=jax.ShapeDtypeStruct((depth, 8, n), F32),
        compiler_params=_params(("arbitrary", "arbitrary")),
        name="adaln",
    )(cpad, ada_w, ada_b.reshape(depth, 1, n))


def _inproj_kernel(x_ref, g_ref, sh_ref, sc_ref, w_ref, oa_ref, ob_ref, oc_ref, od_ref):
    h = _norm_mod(x_ref[...], g_ref[...], sh_ref[0], sc_ref[0]).astype(BF16)
    off = 0
    for o_ref in (oa_ref, ob_ref, oc_ref, od_ref):
        n = o_ref.shape[1]
        o_ref[...] = _dot(h, w_ref[:, off:off + n])
        off += n


def inproj(x, g, shift, scale, w, rows_per_mod, widths):
    r, d = x.shape
    tm = _tile(rows_per_mod, 512)
    tpb = rows_per_mod // tm
    modspec = pl.BlockSpec((1, 1, d), lambda i: (i // tpb, 0, 0))
    return pl.pallas_call(
        _inproj_kernel,
        grid=(r // tm,),
        in_specs=[pl.BlockSpec((tm, d), lambda i: (i, 0)),
                  pl.BlockSpec((1, d), lambda i: (0, 0)),
                  modspec, modspec,
                  pl.BlockSpec(w.shape, lambda i: (0, 0))],
        out_specs=[pl.BlockSpec((tm, n), lambda i: (i, 0)) for n in widths],
        out_shape=[jax.ShapeDtypeStruct((r, n), F32) for n in widths],
        compiler_params=_params(("arbitrary",)),
        name="inproj",
    )(x, g, shift, scale, w)


def _outproj_kernel(a_ref, b_ref, c_ref, d_ref, w_ref, x_ref, gate_ref, o_ref):
    acc = _dot(a_ref[...].astype(BF16), w_ref[0])
    acc += _dot(b_ref[...].astype(BF16), w_ref[1])
    acc += _dot(c_ref[...].astype(BF16), w_ref[2])
    acc += _dot(d_ref[...].astype(BF16), w_ref[3])
    o_ref[...] = x_ref[...] + gate_ref[0] * acc


def outproj(mix, w4, x, gate, rows_per_mod):
    r, d = x.shape
    gw = w4.shape[1]
    tm = _tile(rows_per_mod, 512)
    tpb = rows_per_mod // tm
    mixspec = pl.BlockSpec((tm, gw), lambda i: (i, 0))
    return pl.pallas_call(
        _outproj_kernel,
        grid=(r // tm,),
        in_specs=[mixspec, mixspec, mixspec, mixspec,
                  pl.BlockSpec(w4.shape, lambda i: (0, 0, 0)),
                  pl.BlockSpec((tm, d), lambda i: (i, 0)),
                  pl.BlockSpec((1, 1, d), lambda i: (i // tpb, 0, 0))],
        out_specs=pl.BlockSpec((tm, d), lambda i: (i, 0)),
        out_shape=jax.ShapeDtypeStruct((r, d), F32),
        compiler_params=_params(("arbitrary",)),
        name="outproj",
    )(*mix, w4, x, gate)


def _mlp_kernel(x_ref, g_ref, sh_ref, sc_ref, gate_ref, w1_ref, w2_ref, o_ref, h_scr, acc_scr):
    j = pl.program_id(1)

    @pl.when(j == 0)
    def _():
        h_scr[...] = _norm_mod(x_ref[...], g_ref[...], sh_ref[0], sc_ref[0]).astype(BF16)
        acc_scr[...] = jnp.zeros_like(acc_scr)

    u = jnp.maximum(_dot(h_scr[...], w1_ref[...]), 0.0)
    acc_scr[...] += _dot((u * u).astype(BF16), w2_ref[...])

    @pl.when(j == pl.num_programs(1) - 1)
    def _():
        o_ref[...] = x_ref[...] + gate_ref[0] * acc_scr[...]


def mlp(x, g, shift, scale, gate, w1, w2, rows_per_mod):
    r, d = x.shape
    dff = w1.shape[1]
    tm = _tile(rows_per_mod, 1024)
    tf = _tile(dff, 512)
    tpb = rows_per_mod // tm
    modspec = pl.BlockSpec((1, 1, d), lambda i, j: (i // tpb, 0, 0))
    return pl.pallas_call(
        _mlp_kernel,
        grid=(r // tm, dff // tf),
        in_specs=[pl.BlockSpec((tm, d), lambda i, j: (i, 0)),
                  pl.BlockSpec((1, d), lambda i, j: (0, 0)),
                  modspec, modspec, modspec,
                  pl.BlockSpec((d, tf), lambda i, j: (0, j)),
                  pl.BlockSpec((tf, d), lambda i, j: (j, 0))],
        out_specs=pl.BlockSpec((tm, d), lambda i, j: (i, 0)),
        out_shape=jax.ShapeDtypeStruct((r, d), F32),
        scratch_shapes=[pltpu.VMEM((tm, d), BF16), pltpu.VMEM((tm, d), F32)],
        compiler_params=_params(("arbitrary", "arbitrary")),
        name="mlp",
    )(x, g, shift, scale, gate, w1, w2)


def _final_norm_kernel(x_ref, g_ref, o_ref):
    x = x_ref[...]
    ms = jnp.mean(x * x, axis=-1, keepdims=True)
    o_ref[...] = x * lax.rsqrt(ms + NORM_EPS) * g_ref[...]


def final_norm(x, g):
    r, d = x.shape
    tm = _tile(r, 1024)
    return pl.pallas_call(
        _final_norm_kernel,
        grid=(r // tm,),
        in_specs=[pl.BlockSpec((tm, d), lambda i: (i, 0)), pl.BlockSpec((1, d), lambda i: (0, 0))],
        out_specs=pl.BlockSpec((tm, d), lambda i: (i, 0)),
        out_shape=jax.ShapeDtypeStruct((r, d), F32),
        compiler_params=_params(("arbitrary",)),
        name="final_norm",
    )(x, g)


def _swap_halves(y):
    n = y.shape[-1]
    lane = lax.broadcasted_iota(jnp.int32, y.shape, 1)
    half = HEAD_DIM // 2
    return jnp.where(lane % HEAD_DIM < half, pltpu.roll(y, n - half, 1), pltpu.roll(y, half, 1))


def _qknorm_rope_kernel(p_ref, qg_ref, kg_ref, cos_ref, sin_ref, eq_ref, ek_ref, q_ref, k_ref, v_ref, *, rope):
    gq = q_ref.shape[1]
    gk = k_ref.shape[1]
    p = p_ref[...]

    def normed(x, gain, e):
        ms = _seg_sum(x * x, e) * (1.0 / HEAD_DIM)
        return x * lax.rsqrt(ms + NORM_EPS) * gain

    q = normed(p[:, :gq], qg_ref[...], eq_ref[...])
    k = normed(p[:, gq:gq + gk], kg_ref[...], ek_ref[...])
    if rope:
        cos = cos_ref[...]
        sin = sin_ref[...]
        q = q * cos + _swap_halves(q) * sin
        k = k * cos[:, :gk] + _swap_halves(k) * sin[:, :gk]
    q_ref[...] = (q * HEAD_DIM ** -0.5).astype(BF16)
    k_ref[...] = k.astype(BF16)
    v_ref[...] = p[:, gq + gk:].astype(BF16)


def qknorm_rope(pc, q_gain, k_gain, cos, sin, eq, ek, seq_len, rope):
    r, n = pc.shape
    gq = n // 2
    gk = n // 4
    tm = _tile(seq_len, 512)
    nseq = seq_len // tm
    return pl.pallas_call(
        functools.partial(_qknorm_rope_kernel, rope=rope),
        grid=(r // tm,),
        in_specs=[pl.BlockSpec((tm, n), lambda i: (i, 0)),
                  pl.BlockSpec((1, gq), lambda i: (0, 0)),
                  pl.BlockSpec((1, gk), lambda i: (0, 0)),
                  pl.BlockSpec((tm, gq), lambda i: (i % nseq, 0)),
                  pl.BlockSpec((tm, gq), lambda i: (i % nseq, 0)),
                  pl.BlockSpec(eq.shape, lambda i: (0, 0)),
                  pl.BlockSpec(ek.shape, lambda i: (0, 0))],
        out_specs=[pl.BlockSpec((tm, gq), lambda i: (i, 0)),
                   pl.BlockSpec((tm, gk), lambda i: (i, 0)),
                   pl.BlockSpec((tm, gk), lambda i: (i, 0))],
        out_shape=[jax.ShapeDtypeStruct((r, gq), BF16),
                   jax.ShapeDtypeStruct((r, gk), BF16),
                   jax.ShapeDtypeStruct((r, gk), BF16)],
        compiler_params=_params(("arbitrary",)),
        name="qknorm_rope",
    )(pc, q_gain, k_gain, cos, sin, eq, ek)


def _attn_kernel(q_ref, k_ref, v_ref, o_ref):
    nq, tq, d = q_ref.shape[1:]
    q = q_ref[0].reshape(nq * tq, d)
    s = _dot_nt(q, k_ref[0])
    m = jnp.max(s, axis=-1, keepdims=True)
    p = jnp.exp(s - m)
    l = jnp.sum(p, axis=-1, keepdims=True)
    o = _dot(p.astype(BF16), v_ref[0]) / l
    o_ref[0] = o.reshape(nq, tq, d)


def attention(q, k, v):
    g, nq, lq, d = q.shape
    lk = k.shape[1]
    tq = _tile(lq, 128)
    return pl.pallas_call(
        _attn_kernel,
        grid=(g, lq // tq),
        in_specs=[pl.BlockSpec((1, nq, tq, d), lambda i, j: (i, 0, j, 0)),
                  pl.BlockSpec((1, lk, d), lambda i, j: (i, 0, 0)),
                  pl.BlockSpec((1, lk, d), lambda i, j: (i, 0, 0))],
        out_specs=pl.BlockSpec((1, nq, tq, d), lambda i, j: (i, 0, j, 0)),
        out_shape=jax.ShapeDtypeStruct((g, nq, lq, d), F32),
        compiler_params=_params(("arbitrary", "arbitrary")),
        name="attention",
    )(q, k, v)


def _na_kernel(q_ref, k_ref, v_ref, kc_ref, vc_ref, bias_ref, o_ref, *, rows, kh):
    band = kh * GRID_W
    kc = kc_ref[0]
    vc = vc_ref[0]

    def one_row(r, carry):
        rs = jnp.clip(r - kh // 2, 0, rows - kh)
        q = q_ref[0, pl.ds(pl.multiple_of(r * GRID_W, GRID_W), GRID_W), :]
        k0 = pl.multiple_of(rs * GRID_W, GRID_W)
        s1 = _dot_nt(q, k_ref[0, pl.ds(k0, band), :]) + bias_ref[0, r - rs]
        s2 = _dot_nt(q, kc)
        m = jnp.maximum(jnp.max(s1, axis=-1, keepdims=True), jnp.max(s2, axis=-1, keepdims=True))
        p1 = jnp.exp(s1 - m)
        p2 = jnp.exp(s2 - m)
        l = jnp.sum(p1, axis=-1, keepdims=True) + jnp.sum(p2, axis=-1, keepdims=True)
        o = _dot(p1.astype(BF16), v_ref[0, pl.ds(k0, band), :]) + _dot(p2.astype(BF16), vc)
        o_ref[0, pl.ds(pl.multiple_of(r * GRID_W, GRID_W), GRID_W), :] = o / l
        return carry

    lax.fori_loop(0, rows, one_row, 0)


def neighbourhood_attention(q, k, v, kc, vc, bias, n_heads):
    g, l, d = q.shape
    lc = kc.shape[1]
    rows = l // GRID_W
    kh = min(NA_ROWS, rows)
    seq = pl.BlockSpec((1, l, d), lambda i: (i, 0, 0))
    cseq = pl.BlockSpec((1, lc, d), lambda i: (i, 0, 0))
    return pl.pallas_call(
        functools.partial(_na_kernel, rows=rows, kh=kh),
        grid=(g,),
        in_specs=[seq, seq, seq, cseq, cseq,
                  pl.BlockSpec((1,) + bias.shape[1:], lambda i: (i % n_heads, 0, 0, 0))],
        out_specs=seq,
        out_shape=jax.ShapeDtypeStruct((g, l, d), F32),
        compiler_params=_params(("arbitrary",)),
        name="neighbourhood_attention",
    )(q, k, v, kc, vc, bias)


def na_bias_table(rpb, rows):
    kh = min(NA_ROWS, rows)
    col = np.arange(GRID_W)
    col_start = np.clip(col - NA_COLS // 2, 0, GRID_W - NA_COLS)
    in_win = (col[None, :] >= col_start[:, None]) & (col[None, :] < col_start[:, None] + NA_COLS)
    col_off = np.clip(col[None, :] - col[:, None], -(NA_COLS - 1), NA_COLS - 1) + NA_COLS - 1
    di = np.arange(kh)
    row_off = np.arange(kh)[None, :] - di[:, None] + NA_ROWS - 1
    b = rpb[:, row_off[:, None, :, None], col_off[None, :, None, :]].astype(F32)
    b = jnp.where(in_win[None, None, :, None, :], b, -jnp.inf)
    return b.reshape(rpb.shape[0], kh, GRID_W, kh * GRID_W)


def _shift_rows(x, d, t):
    n = x.shape[0]
    y = pltpu.roll(x, d % n, 0)
    src = t - d
    return jnp.where((src >= 0) & (src < n), y, 0.0)


def _pool_kernel(x_ref, w_ref, scale_ref, o_ref):
    x = x_ref[0]
    n, g = x.shape
    pg = g // len(POOL_WINDOWS)
    t = lax.broadcasted_iota(jnp.int32, x.shape, 0)
    group = lax.broadcasted_iota(jnp.int32, x.shape, 1) // pg
    fwd = x
    bwd = x
    cur = 1
    total = jnp.zeros_like(x)
    count = jnp.ones_like(x)
    for j, w in enumerate(POOL_WINDOWS):
        half = w // 2
        while cur < half:
            fwd = fwd + _shift_rows(fwd, -cur, t)
            bwd = bwd + _shift_rows(bwd, cur, t)
            cur *= 2
        win = _shift_rows(bwd, 1, t) + fwd
        lo = jnp.clip(t - half, 0, n)
        hi = jnp.clip(t - half + w, 0, n)
        total = jnp.where(group == j, win, total)
        count = jnp.where(group == j, (hi - lo).astype(F32), count)
    diff = total / count - x
    o_ref[0] = _dot(diff.astype(BF16), w_ref[...]) * scale_ref[...]


def pool_mixer(p, w_bd, scale):
    b, n, g = p.shape
    return pl.pallas_call(
        _pool_kernel,
        grid=(b,),
        in_specs=[pl.BlockSpec((1, n, g), lambda i: (i, 0, 0)),
                  pl.BlockSpec((g, g), lambda i: (0, 0)),
                  pl.BlockSpec((1, g), lambda i: (0, 0))],
        out_specs=pl.BlockSpec((1, n, g), lambda i: (i, 0, 0)),
        out_shape=jax.ShapeDtypeStruct((b, n, g), F32),
        compiler_params=_params(("arbitrary",)),
        name="pool_mixer",
    )(p, w_bd, scale)


def _rwkv_prep_kernel(p_ref, w0_ref, a0_ref, w2_ref, a2_ref, kk_w_ref, ka_ref, e_ref,
                      kk_ref, decf_ref, bf_ref, kmf_ref, decb_ref, bb_ref, kmb_ref):
    g = kk_ref.shape[1]
    k = p_ref[:, g:2 * g]
    lr = p_ref[:, 3 * g:4 * g]
    kx = k * kk_w_ref[...]
    norm = jnp.sqrt(_seg_sum(kx * kx, e_ref[...]))
    kk = kx / jnp.maximum(norm, 1e-12)
    kk_ref[...] = kk
    lr_t = jnp.tanh(lr).astype(BF16)
    lr_b = lr.astype(BF16)
    for d, (dec_ref, b_ref, km_ref) in enumerate(((decf_ref, bf_ref, kmf_ref), (decb_ref, bb_ref, kmb_ref))):
        z = w0_ref[d:d + 1, :] + _dot(lr_t, w2_ref[d])
        softplus_neg = jnp.maximum(-z, 0.0) + jnp.log1p(jnp.exp(-jnp.abs(z)))
        w = -softplus_neg - 0.5
        dec_ref[...] = jnp.exp(-jnp.exp(w))
        a = jax.nn.sigmoid(a0_ref[d:d + 1, :] + _dot(lr_b, a2_ref[d]))
        b_ref[...] = kk * a
        km_ref[...] = k * (1.0 + (a - 1.0) * ka_ref[...])


def rwkv_prep(pa, w0, a0, w2p, a2p, k_k, k_a, eseg):
    r, n = pa.shape
    g = n // 4
    tm = _tile(r, 512)
    full2 = lambda shape: pl.BlockSpec(shape, lambda i: (0,) * len(shape))
    out = pl.BlockSpec((tm, g), lambda i: (i, 0))
    return pl.pallas_call(
        _rwkv_prep_kernel,
        grid=(r // tm,),
        in_specs=[pl.BlockSpec((tm, n), lambda i: (i, 0)),
                  full2(w0.shape), full2(a0.shape), full2(w2p.shape), full2(a2p.shape),
                  full2(k_k.shape), full2(k_a.shape), full2(eseg.shape)],
        out_specs=[out] * 7,
        out_shape=[jax.ShapeDtypeStruct((r, g), F32)] * 7,
        compiler_params=_params(("arbitrary",)),
        name="rwkv_prep",
    )(pa, w0, a0, w2p, a2p, k_k, k_a, eseg)


def _wkv_scan_kernel(rf_ref, rb_ref, kkf_ref, kkb_ref, decf_ref, bf_ref, kmf_ref, decb_ref, bb_ref, kmb_ref,
                     vtf_ref, vtb_ref, e_ref, s0_ref, yf_ref, yb_ref, send_ref, s_scr):
    nb, t_blk, _ = rf_ref.shape
    npair = vtf_ref.shape[1]
    step_id = pl.program_id(0)

    @pl.when(step_id == 0)
    def _():
        s_scr[...] = s0_ref[...]

    yf_ref[...] = jnp.zeros_like(yf_ref)
    yb_ref[...] = jnp.zeros_like(yb_ref)
    e = e_ref[...]
    lane = lax.broadcasted_iota(jnp.int32, (HEAD_DIM, LANES), 1) % SCAN_SUB
    dirs = ((rf_ref, kkf_ref, decf_ref, bf_ref, kmf_ref, vtf_ref, yf_ref),
            (rb_ref, kkb_ref, decb_ref, bb_ref, kmb_ref, vtb_ref, yb_ref))

    def one_token(tt, carry):
        for d, (r_ref, kk_ref, dec_ref, b_ref, km_ref, vt_ref, y_ref) in enumerate(dirs):
            tl = tt if d == 0 else t_blk - 1 - tt
            sub = tl // SCAN_SUB
            hit = lane == tl % SCAN_SUB
            for b in range(nb):
                rows = [ref[b, pl.ds(tl, 1), :] for ref in (kk_ref, dec_ref, b_ref, km_ref, r_ref)]
                for j in range(npair):
                    cols = slice(j * LANES, (j + 1) * LANES)
                    kk_t, dec_t, b_t, km_t, r_t = [jnp.broadcast_to(x[:, cols], (HEAD_DIM, LANES)) for x in rows]
                    s = s_scr[d, b, j]
                    sa = _seg_sum(s * kk_t, e)
                    vcol = _dot(jnp.where(hit, vt_ref[b, j, sub], 0.0).astype(BF16), e)
                    s = s * dec_t - sa * b_t + vcol * km_t
                    s_scr[d, b, j] = s
                    y = _dot((s * r_t).astype(BF16), e)
                    y_ref[b, j, sub] = jnp.where(hit, y, y_ref[b, j, sub])
        return carry

    lax.fori_loop(0, t_blk, one_token, 0)

    @pl.when(step_id == pl.num_programs(0) - 1)
    def _():
        send_ref[...] = s_scr[...]


def wkv_scan(pa3, kk, dec_f, b_f, km_f, dec_b, b_b, km_b, vt, epair, s0):
    nb, l, g = kk.shape
    npair = g // LANES
    t_blk = SCAN_BLOCK
    nsub = t_blk // SCAN_SUB
    nblk = l // t_blk
    assert l % t_blk == 0
    fwd = lambda i: (0, i, 0)
    bwd = lambda i: (0, nblk - 1 - i, 0)
    seq_f = pl.BlockSpec((nb, t_blk, g), fwd)
    seq_b = pl.BlockSpec((nb, t_blk, g), bwd)
    vt_f = pl.BlockSpec((nb, npair, nsub, HEAD_DIM, LANES), lambda i: (0, 0, i, 0, 0))
    vt_b = pl.BlockSpec((nb, npair, nsub, HEAD_DIM, LANES), lambda i: (0, 0, nblk - 1 - i, 0, 0))
    state = pl.BlockSpec(s0.shape, lambda i: (0, 0, 0, 0, 0))
    return pl.pallas_call(
        _wkv_scan_kernel,
        grid=(nblk,),
        in_specs=[seq_f, seq_b, seq_f, seq_b, seq_f, seq_f, seq_f, seq_b, seq_b, seq_b,
                  vt_f, vt_b, pl.BlockSpec(epair.shape, lambda i: (0, 0)), state],
        out_specs=[vt_f, vt_b, state],
        out_shape=[jax.ShapeDtypeStruct(vt.shape, F32), jax.ShapeDtypeStruct(vt.shape, F32),
                   jax.ShapeDtypeStruct(s0.shape, F32)],
        scratch_shapes=[pltpu.VMEM(s0.shape, F32)],
        compiler_params=_params(("arbitrary",)),
        name="wkv_scan",
    )(pa3, pa3, kk, kk, dec_f, b_f, km_f, dec_b, b_b, km_b, vt, vt, epair, s0)


def to_pair_tiles(x):
    b, l, g = x.shape
    x = x.reshape(b, l // SCAN_SUB, SCAN_SUB, g // LANES, 2, HEAD_DIM)
    return x.transpose(0, 3, 1, 5, 4, 2).reshape(b, g // LANES, l // SCAN_SUB, HEAD_DIM, LANES)


def from_pair_tiles(y):
    b, npair, nblk, _, _ = y.shape
    y = y.reshape(b, npair, nblk, HEAD_DIM, 2, SCAN_SUB)
    return y.transpose(0, 2, 5, 1, 4, 3).reshape(b, nblk * SCAN_SUB, npair * LANES)


def _rwkv_readout_kernel(yf_ref, yb_ref, p_ref, kmf_ref, kmb_ref, rk_ref, g2_ref, gnw_ref, gnb_ref, e_ref, o_ref):
    g = o_ref.shape[1]
    e = e_ref[...]
    r = p_ref[:, :g]
    v = p_ref[:, 2 * g:3 * g]
    lr = p_ref[:, 3 * g:4 * g]
    y = yf_ref[...] + yb_ref[...]
    mu = _seg_sum(y, e) * (1.0 / HEAD_DIM)
    yc = y - mu
    var = _seg_sum(yc * yc, e) * (1.0 / HEAD_DIM)
    yn = yc * lax.rsqrt(var + RWKV_GN_EPS) * gnw_ref[...] + gnb_ref[...]
    bonus = _seg_sum(r * (kmf_ref[...] + kmb_ref[...]) * rk_ref[...], e) * v
    gate = _dot(jax.nn.sigmoid(lr).astype(BF16), g2_ref[...])
    o_ref[...] = (yn + bonus) * gate


def rwkv_readout(yf, yb, pa, km_f, km_b, r_k, g2p, gn_w, gn_b, eseg):
    r, g = yf.shape
    tm = _tile(r, 512)
    row = pl.BlockSpec((tm, g), lambda i: (i, 0))
    full2 = lambda shape: pl.BlockSpec(shape, lambda i: (0,) * len(shape))
    return pl.pallas_call(
        _rwkv_readout_kernel,
        grid=(r // tm,),
        in_specs=[row, row, pl.BlockSpec((tm, 4 * g), lambda i: (i, 0)), row, row,
                  full2(r_k.shape), full2(g2p.shape), full2(gn_w.shape), full2(gn_b.shape), full2(eseg.shape)],
        out_specs=row,
        out_shape=jax.ShapeDtypeStruct((r, g), F32),
        compiler_params=_params(("arbitrary",)),
        name="rwkv_readout",
    )(yf, yb, pa, km_f, km_b, r_k, g2p, gn_w, gn_b, eseg)


def _block_ones(n, seg):
    idx = np.arange(n) // seg
    return jnp.asarray(idx[:, None] == idx[None, :], dtype=BF16)


def _padded_rows(w, offset, n):
    return jnp.zeros((n, w.shape[1]), w.dtype).at[offset:offset + w.shape[0]].set(w)


def _rope_tables(n_tokens, n_rep):
    t = jnp.arange(n_tokens)
    row = (t // GRID_W).astype(F32)
    col = (t % GRID_W).astype(F32)
    n_freq = HEAD_DIM // 4
    inv_freq = ROPE_THETA ** (-jnp.arange(n_freq, dtype=F32) / n_freq)
    ang = jnp.concatenate([row[:, None] * inv_freq, col[:, None] * inv_freq], axis=-1)
    cos, sin = jnp.cos(ang), jnp.sin(ang)
    cos_h = jnp.concatenate([cos, cos], axis=-1)
    sin_h = jnp.concatenate([-sin, sin], axis=-1)
    return jnp.tile(cos_h, (1, n_rep)), jnp.tile(sin_h, (1, n_rep))


def _heads_first(t, b, l):
    return t.reshape(b, l, -1, HEAD_DIM).transpose(0, 2, 1, 3)


def _heads_last(t):
    b, h, l, d = t.shape
    return t.transpose(0, 2, 1, 3).reshape(b * l, h * d)


def kernel(x, c, ctx, c_ctx, ada_w, ada_b, norm1_g, norm2_g, w_in, w_out, rwkv_w0, rwkv_w2, rwkv_a0, rwkv_a2,
           rwkv_k_k, rwkv_k_a, rwkv_r_k, rwkv_g2, rwkv_gn_w, rwkv_gn_b, na_rpb, gqa_q_gain, gqa_k_gain, pool_w,
           pool_scale, mlp_w1, mlp_w2, final_g):
    nb, l, d = x.shape
    lc = ctx.shape[1]
    depth = ada_w.shape[0]
    g = d // 4
    nh = g // HEAD_DIM
    nkv = nh // 2
    kvw = nkv * HEAD_DIM
    dr = rwkv_w2.shape[2]
    ir = rwkv_a2.shape[2]
    gr = rwkv_g2.shape[1]
    assert 2 * dr + 2 * ir + gr <= g and nb + 1 <= 8
    rows = l // GRID_W

    cpad = jnp.zeros((8, d), F32).at[:nb].set(c).at[nb].set(c_ctx)
    mods = adaln(cpad, ada_w, ada_b)

    eseg = _block_ones(g, HEAD_DIM)
    ekv = _block_ones(kvw, HEAD_DIM)
    epair = _block_ones(LANES, HEAD_DIM)
    cos_q, sin_q = _rope_tables(l, nh)
    ones_c, zeros_c = jnp.ones((lc, g), F32), jnp.zeros((lc, g), F32)

    splits = np.cumsum([0, g, g, g, dr, dr, ir, ir, gr, g, g, g, g, kvw, kvw, g])
    lowrank_w = splits[8] - splits[3]
    widths = (4 * g, 3 * g, 2 * g, g)

    xl = x.reshape(nb * l, d)
    xc = ctx.reshape(nb * lc, d)
    zero_state = jnp.zeros((2, nb, g // LANES, HEAD_DIM, LANES), F32)

    for i in range(depth):
        need_ctx_out = i < depth - 1
        mod_l = mods[i, :nb].reshape(nb, N_MOD, 1, d)
        mod_c = mods[i, nb].reshape(N_MOD, 1, 1, d)
        ml = [mod_l[:, k] for k in range(N_MOD)]
        mc = [mod_c[k] for k in range(N_MOD)]

        wi = w_in[i]
        w_inp = jnp.concatenate([wi[:, :splits[8]], jnp.zeros((d, g - lowrank_w), F32), wi[:, splits[8]:]],
                                axis=1).astype(BF16)
        g1 = norm1_g[i].reshape(1, d)
        g2 = norm2_g[i].reshape(1, d)
        pa_l, pb_l, pc_l, pd_l = inproj(xl, g1, ml[0], ml[1], w_inp, l, widths)
        pa_c, pb_c, pc_c, pd_c = inproj(xc, g1, mc[0], mc[1], w_inp, nb * lc, widths)

        w2p = jnp.stack([_padded_rows(rwkv_w2[i, 0], 0, g), _padded_rows(rwkv_w2[i, 1], dr, g)]).astype(BF16)
        a2p = jnp.stack([_padded_rows(rwkv_a2[i, 0], 2 * dr, g),
                         _padded_rows(rwkv_a2[i, 1], 2 * dr + ir, g)]).astype(BF16)
        g2p = _padded_rows(rwkv_g2[i], 2 * dr + 2 * ir, g).astype(BF16)
        k_k = rwkv_k_k[i].reshape(1, g)
        k_a = rwkv_k_a[i].reshape(1, g)
        r_k = rwkv_r_k[i].reshape(1, g)
        gn_w = rwkv_gn_w[i].reshape(1, g)
        gn_b = rwkv_gn_b[i].reshape(1, g)

        def rwkv_terms(pa, seq):
            terms = rwkv_prep(pa, rwkv_w0[i], rwkv_a0[i], w2p, a2p, k_k, k_a, eseg)
            terms3 = [t.reshape(nb, seq, g) for t in terms]
            vt = to_pair_tiles(pa[:, 2 * g:3 * g].reshape(nb, seq, g))
            return terms, terms3, vt

        terms_c, terms3_c, vt_c = rwkv_terms(pa_c, lc)
        yf_c, yb_c, state_c = wkv_scan(pa_c.reshape(nb, lc, 4 * g), *terms3_c, vt_c, epair, zero_state)
        terms_l, terms3_l, vt_l = rwkv_terms(pa_l, l)
        yf_l, yb_l, _ = wkv_scan(pa_l.reshape(nb, l, 4 * g), *terms3_l, vt_l, epair, state_c)
        a_l = rwkv_readout(from_pair_tiles(yf_l).reshape(nb * l, g), from_pair_tiles(yb_l).reshape(nb * l, g),
                           pa_l, terms_l[3], terms_l[6], r_k, g2p, gn_w, gn_b, eseg)

        def heads_b(p, seq, scale):
            out = []
            for k in range(3):
                t = p[:, k * g:(k + 1) * g]
                if k == 0:
                    t = t * scale
                out.append(_heads_first(t.astype(BF16), nb, seq).reshape(nb * nh, seq, HEAD_DIM))
            return out

        qb_l, kb_l, vb_l = heads_b(pb_l, l, HEAD_DIM ** -0.5)
        qb_c, kb_c, vb_c = heads_b(pb_c, lc, HEAD_DIM ** -0.5)
        bias = na_bias_table(na_rpb[i], rows)
        b_l = neighbourhood_attention(qb_l, kb_l, vb_l, kb_c, vb_c, bias, nh)
        b_l = _heads_last(b_l.reshape(nb, nh, l, HEAD_DIM))

        q_gain = jnp.tile(gqa_q_gain[i], nh).reshape(1, g)
        k_gain = jnp.tile(gqa_k_gain[i], nkv).reshape(1, kvw)
        qc_l, kc_l, vc_l = qknorm_rope(pc_l, q_gain, k_gain, cos_q, sin_q, eseg, ekv, l, True)
        qc_c, kc_c, vc_c = qknorm_rope(pc_c, q_gain, k_gain, ones_c, zeros_c, eseg, ekv, lc, False)

        def kv_heads(t, seq):
            return _heads_first(t, nb, seq).reshape(nb * nkv, seq, HEAD_DIM)

        def q_heads(t, seq):
            return _heads_first(t, nb, seq).reshape(nb * nkv, nh // nkv, seq, HEAD_DIM)

        k_all = jnp.concatenate([kv_heads(kc_c, lc), kv_heads(kc_l, l)], axis=1)
        v_all = jnp.concatenate([kv_heads(vc_c, lc), kv_heads(vc_l, l)], axis=1)
        c_l = attention(q_heads(qc_l, l), k_all, v_all)
        c_l = _heads_last(c_l.reshape(nb, nh, l, HEAD_DIM))

        w_bd = jax.scipy.linalg.block_diag(*[pool_w[i, k] for k in range(len(POOL_WINDOWS))]).astype(BF16)
        p_scale = pool_scale[i].reshape(1, g)
        d_l = pool_mixer(pd_l.reshape(nb, l, g), w_bd, p_scale).reshape(nb * l, g)

        w_out4 = w_out[i].reshape(4, g, d).astype(BF16)
        w1 = mlp_w1[i].astype(BF16)
        w2 = mlp_w2[i].astype(BF16)
        xl = outproj((a_l, b_l, c_l, d_l), w_out4, xl, ml[2], l)
        xl = mlp(xl, g2, ml[3], ml[4], ml[5], w1, w2, l)

        if need_ctx_out:
            a_c = rwkv_readout(from_pair_tiles(yf_c).reshape(nb * lc, g), from_pair_tiles(yb_c).reshape(nb * lc, g),
                               pa_c, terms_c[3], terms_c[6], r_k, g2p, gn_w, gn_b, eseg)
            b_c = attention(qb_c.reshape(nb * nh, 1, lc, HEAD_DIM), kb_c, vb_c)
            b_c = _heads_last(b_c.reshape(nb, nh, lc, HEAD_DIM))
            c_c = attention(q_heads(qc_c, lc), kv_heads(kc_c, lc), kv_heads(vc_c, lc))
            c_c = _heads_last(c_c.reshape(nb, nh, lc, HEAD_DIM))
            d_c = pool_mixer(pd_c.reshape(nb, lc, g), w_bd, p_scale).reshape(nb * lc, g)
            xc = outproj((a_c, b_c, c_c, d_c), w_out4, xc, mc[2], nb * lc)
            xc = mlp(xc, g2, mc[3], mc[4], mc[5], w1, w2, nb * lc)

    return final_norm(xl, final_g.reshape(1, d)).reshape(nb, l, d)
```

```python
import functools

import jax
import jax.numpy as jnp
import numpy as np
from jax import lax
from jax.experimental import pallas as pl
from jax.experimental.pallas import tpu as pltpu

F32 = jnp.float32
BF16 = jnp.bfloat16

HEAD_DIM = 64
GRID_W = 64
NA_ROWS = 8
NA_COLS = 16
ROPE_THETA = 10000.0
POOL_WINDOWS = (2, 4, 8, 16)
NORM_EPS = 1e-6
RWKV_GN_EPS = 1e-5 * HEAD_DIM
N_MOD = 6
LANES = 128
SCAN_BLOCK = 128
SCAN_SUB = 64
VMEM_LIMIT = 52 * 1024 * 1024


def _params(sem):
    return pltpu.CompilerParams(dimension_semantics=sem, vmem_limit_bytes=VMEM_LIMIT)


def _tile(n, pref):
    t = min(n, pref)
    assert n % t == 0, (n, pref)
    return t


def _dot(a, b):
    return jnp.dot(a, b, preferred_element_type=F32)


def _dot_nt(a, b):
    return lax.dot_general(a, b, (((1,), (1,)), ((), ())), preferred_element_type=F32)


def _seg_sum(x, e):
    hi = x.astype(BF16)
    lo = (x - hi.astype(F32)).astype(BF16)
    return _dot(hi, e) + _dot(lo, e)


def _norm_mod(x, g, shift, scale):
    ms = jnp.mean(x * x, axis=-1, keepdims=True)
    h = x * lax.rsqrt(ms + NORM_EPS) * g
    return h * (1.0 + scale) + shift


def _adaln_kernel(c_ref, w_ref, b_ref, o_ref):
    c = c_ref[...]
    s = c * jax.nn.sigmoid(c)
    o_ref[0] = jnp.dot(s, w_ref[0], preferred_element_type=F32, precision=lax.Precision.HIGHEST) + b_ref[0]


def adaln(cpad, ada_w, ada_b):
    depth, d, n = ada_w.shape
    tn = _tile(n, 1536)
    return pl.pallas_call(
        _adaln_kernel,
        grid=(depth, n // tn),
        in_specs=[pl.BlockSpec((8, d), lambda i, j: (0, 0)),
                  pl.BlockSpec((1, d, tn), lambda i, j: (i, 0, j)),
                  pl.BlockSpec((1, 1, tn), lambda i, j: (i, 0, j))],
        out_specs=pl.BlockSpec((1, 8, tn), lambda i, j: (i, 0, j)),
        out_shape=jax.ShapeDtypeStruct((depth, 8, n), F32),
        compiler_params=_params(("arbitrary", "arbitrary")),
        name="adaln",
    )(cpad, ada_w, ada_b.reshape(depth, 1, n))


def _inproj_kernel(x_ref, g_ref, sh_ref, sc_ref, w_ref, oa_ref, ob_ref, oc_ref, od_ref):
    h = _norm_mod(x_ref[...], g_ref[...], sh_ref[0], sc_ref[0]).astype(BF16)
    off = 0
    for o_ref in (oa_ref, ob_ref, oc_ref, od_ref):
        n = o_ref.shape[1]
        o_ref[...] = _dot(h, w_ref[:, off:off + n])
        off += n


def inproj(x, g, shift, scale, w, rows_per_mod, widths):
    r, d = x.shape
    tm = _tile(rows_per_mod, 512)
    tpb = rows_per_mod // tm
    modspec = pl.BlockSpec((1, 1, d), lambda i: (i // tpb, 0, 0))
    return pl.pallas_call(
        _inproj_kernel,
        grid=(r // tm,),
        in_specs=[pl.BlockSpec((tm, d), lambda i: (i, 0)),
                  pl.BlockSpec((1, d), lambda i: (0, 0)),
                  modspec, modspec,
                  pl.BlockSpec(w.shape, lambda i: (0, 0))],
        out_specs=[pl.BlockSpec((tm, n), lambda i: (i, 0)) for n in widths],
        out_shape=[jax.ShapeDtypeStruct((r, n), F32) for n in widths],
        compiler_params=_params(("arbitrary",)),
        name="inproj",
    )(x, g, shift, scale, w)


def _outproj_kernel(a_ref, b_ref, c_ref, d_ref, w_ref, x_ref, gate_ref, o_ref):
    acc = _dot(a_ref[...].astype(BF16), w_ref[0])
    acc += _dot(b_ref[...].astype(BF16), w_ref[1])
    acc += _dot(c_ref[...].astype(BF16), w_ref[2])
    acc += _dot(d_ref[...].astype(BF16), w_ref[3])
    o_ref[...] = x_ref[...] + gate_ref[0] * acc


def outproj(mix, w4, x, gate, rows_per_mod):
    r, d = x.shape
    gw = w4.shape[1]
    tm = _tile(rows_per_mod, 512)
    tpb = rows_per_mod // tm
    mixspec = pl.BlockSpec((tm, gw), lambda i: (i, 0))
    return pl.pallas_call(
        _outproj_kernel,
        grid=(r // tm,),
        in_specs=[mixspec, mixspec, mixspec, mixspec,
                  pl.BlockSpec(w4.shape, lambda i: (0, 0, 0)),
                  pl.BlockSpec((tm, d), lambda i: (i, 0)),
                  pl.BlockSpec((1, 1, d), lambda i: (i // tpb, 0, 0))],
        out_specs=pl.BlockSpec((tm, d), lambda i: (i, 0)),
        out_shape=jax.ShapeDtypeStruct((r, d), F32),
        compiler_params=_params(("arbitrary",)),
        name="outproj",
    )(*mix, w4, x, gate)


def _mlp_kernel(x_ref, g_ref, sh_ref, sc_ref, gate_ref, w1_ref, w2_ref, o_ref, h_scr, acc_scr):
    j = pl.program_id(1)

    @pl.when(j == 0)
    def _():
        h_scr[...] = _norm_mod(x_ref[...], g_ref[...], sh_ref[0], sc_ref[0]).astype(BF16)
        acc_scr[...] = jnp.zeros_like(acc_scr)

    u = jnp.maximum(_dot(h_scr[...], w1_ref[...]), 0.0)
    acc_scr[...] += _dot((u * u).astype(BF16), w2_ref[...])

    @pl.when(j == pl.num_programs(1) - 1)
    def _():
        o_ref[...] = x_ref[...] + gate_ref[0] * acc_scr[...]


def mlp(x, g, shift, scale, gate, w1, w2, rows_per_mod):
    r, d = x.shape
    dff = w1.shape[1]
    tm = _tile(rows_per_mod, 1024)
    tf = _tile(dff, 512)
    tpb = rows_per_mod // tm
    modspec = pl.BlockSpec((1, 1, d), lambda i, j: (i // tpb, 0, 0))
    return pl.pallas_call(
        _mlp_kernel,
        grid=(r // tm, dff // tf),
        in_specs=[pl.BlockSpec((tm, d), lambda i, j: (i, 0)),
                  pl.BlockSpec((1, d), lambda i, j: (0, 0)),
                  modspec, modspec, modspec,
                  pl.BlockSpec((d, tf), lambda i, j: (0, j)),
                  pl.BlockSpec((tf, d), lambda i, j: (j, 0))],
        out_specs=pl.BlockSpec((tm, d), lambda i, j: (i, 0)),
        out_shape=jax.ShapeDtypeStruct((r, d), F32),
        scratch_shapes=[pltpu.VMEM((tm, d), BF16), pltpu.VMEM((tm, d), F32)],
        compiler_params=_params(("arbitrary", "arbitrary")),
        name="mlp",
    )(x, g, shift, scale, gate, w1, w2)


def _final_norm_kernel(x_ref, g_ref, o_ref):
    x = x_ref[...]
    ms = jnp.mean(x * x, axis=-1, keepdims=True)
    o_ref[...] = x * lax.rsqrt(ms + NORM_EPS) * g_ref[...]


def final_norm(x, g):
    r, d = x.shape
    tm = _tile(r, 1024)
    return pl.pallas_call(
        _final_norm_kernel,
        grid=(r // tm,),
        in_specs=[pl.BlockSpec((tm, d), lambda i: (i, 0)), pl.BlockSpec((1, d), lambda i: (0, 0))],
        out_specs=pl.BlockSpec((tm, d), lambda i: (i, 0)),
        out_shape=jax.ShapeDtypeStruct((r, d), F32),
        compiler_params=_params(("arbitrary",)),
        name="final_norm",
    )(x, g)


def _swap_halves(y):
    n = y.shape[-1]
    lane = lax.broadcasted_iota(jnp.int32, y.shape, 1)
    half = HEAD_DIM // 2
    return jnp.where(lane % HEAD_DIM < half, pltpu.roll(y, n - half, 1), pltpu.roll(y, half, 1))


def _qknorm_rope_kernel(p_ref, qg_ref, kg_ref, cos_ref, sin_ref, eq_ref, ek_ref, q_ref, k_ref, v_ref, *, rope):
    gq = q_ref.shape[1]
    gk = k_ref.shape[1]
    p = p_ref[...]

    def normed(x, gain, e):
        ms = _seg_sum(x * x, e) * (1.0 / HEAD_DIM)
        return x * lax.rsqrt(ms + NORM_EPS) * gain

    q = normed(p[:, :gq], qg_ref[...], eq_ref[...])
    k = normed(p[:, gq:gq + gk], kg_ref[...], ek_ref[...])
    if rope:
        cos = cos_ref[...]
        sin = sin_ref[...]
        q = q * cos + _swap_halves(q) * sin
        k = k * cos[:, :gk] + _swap_halves(k) * sin[:, :gk]
    q_ref[...] = (q * HEAD_DIM ** -0.5).astype(BF16)
    k_ref[...] = k.astype(BF16)
    v_ref[...] = p[:, gq + gk:].astype(BF16)


def qknorm_rope(pc, q_gain, k_gain, cos, sin, eq, ek, seq_len, rope):
    r, n = pc.shape
    gq = n // 2
    gk = n // 4
    tm = _tile(seq_len, 512)
    nseq = seq_len // tm
    return pl.pallas_call(
        functools.partial(_qknorm_rope_kernel, rope=rope),
        grid=(r // tm,),
        in_specs=[pl.BlockSpec((tm, n), lambda i: (i, 0)),
                  pl.BlockSpec((1, gq), lambda i: (0, 0)),
                  pl.BlockSpec((1, gk), lambda i: (0, 0)),
                  pl.BlockSpec((tm, gq), lambda i: (i % nseq, 0)),
                  pl.BlockSpec((tm, gq), lambda i: (i % nseq, 0)),
                  pl.BlockSpec(eq.shape, lambda i: (0, 0)),
                  pl.BlockSpec(ek.shape, lambda i: (0, 0))],
        out_specs=[pl.BlockSpec((tm, gq), lambda i: (i, 0)),
                   pl.BlockSpec((tm, gk), lambda i: (i, 0)),
                   pl.BlockSpec((tm, gk), lambda i: (i, 0))],
        out_shape=[jax.ShapeDtypeStruct((r, gq), BF16),
                   jax.ShapeDtypeStruct((r, gk), BF16),
                   jax.ShapeDtypeStruct((r, gk), BF16)],
        compiler_params=_params(("arbitrary",)),
        name="qknorm_rope",
    )(pc, q_gain, k_gain, cos, sin, eq, ek)


def _attn_kernel(q_ref, k_ref, v_ref, o_ref):
    nq, tq, d = q_ref.shape[1:]
    q = q_ref[0].reshape(nq * tq, d)
    s = _dot_nt(q, k_ref[0])
    m = jnp.max(s, axis=-1, keepdims=True)
    p = jnp.exp(s - m)
    l = jnp.sum(p, axis=-1, keepdims=True)
    o = _dot(p.astype(BF16), v_ref[0]) / l
    o_ref[0] = o.reshape(nq, tq, d)


def attention(q, k, v):
    g, nq, lq, d = q.shape
    lk = k.shape[1]
    tq = _tile(lq, 128)
    return pl.pallas_call(
        _attn_kernel,
        grid=(g, lq // tq),
        in_specs=[pl.BlockSpec((1, nq, tq, d), lambda i, j: (i, 0, j, 0)),
                  pl.BlockSpec((1, lk, d), lambda i, j: (i, 0, 0)),
                  pl.BlockSpec((1, lk, d), lambda i, j: (i, 0, 0))],
        out_specs=pl.BlockSpec((1, nq, tq, d), lambda i, j: (i, 0, j, 0)),
        out_shape=jax.ShapeDtypeStruct((g, nq, lq, d), F32),
        compiler_params=_params(("arbitrary", "arbitrary")),
        name="attention",
    )(q, k, v)


def _na_kernel(q_ref, k_ref, v_ref, kc_ref, vc_ref, bias_ref, o_ref, *, rows, kh):
    band = kh * GRID_W
    kc = kc_ref[0]
    vc = vc_ref[0]

    def one_row(r, carry):
        rs = jnp.clip(r - kh // 2, 0, rows - kh)
        q = q_ref[0, pl.ds(pl.multiple_of(r * GRID_W, GRID_W), GRID_W), :]
        k0 = pl.multiple_of(rs * GRID_W, GRID_W)
        s1 = _dot_nt(q, k_ref[0, pl.ds(k0, band), :]) + bias_ref[0, r - rs]
        s2 = _dot_nt(q, kc)
        m = jnp.maximum(jnp.max(s1, axis=-1, keepdims=True), jnp.max(s2, axis=-1, keepdims=True))
        p1 = jnp.exp(s1 - m)
        p2 = jnp.exp(s2 - m)
        l = jnp.sum(p1, axis=-1, keepdims=True) + jnp.sum(p2, axis=-1, keepdims=True)
        o = _dot(p1.astype(BF16), v_ref[0, pl.ds(k0, band), :]) + _dot(p2.astype(BF16), vc)
        o_ref[0, pl.ds(pl.multiple_of(r * GRID_W, GRID_W), GRID_W), :] = o / l
        return carry

    lax.fori_loop(0, rows, one_row, 0)


def neighbourhood_attention(q, k, v, kc, vc, bias, n_heads):
    g, l, d = q.shape
    lc = kc.shape[1]
    rows = l // GRID_W
    kh = min(NA_ROWS, rows)
    seq = pl.BlockSpec((1, l, d), lambda i: (i, 0, 0))
    cseq = pl.BlockSpec((1, lc, d), lambda i: (i, 0, 0))
    return pl.pallas_call(
        functools.partial(_na_kernel, rows=rows, kh=kh),
        grid=(g,),
        in_specs=[seq, seq, seq, cseq, cseq,
                  pl.BlockSpec((1,) + bias.shape[1:], lambda i: (i % n_heads, 0, 0, 0))],
        out_specs=seq,
        out_shape=jax.ShapeDtypeStruct((g, l, d), F32),
        compiler_params=_params(("arbitrary",)),
        name="neighbourhood_attention",
    )(q, k, v, kc, vc, bias)


def _na_bias_kernel(rpb_ref, o_ref, *, kh):
    h = pl.program_id(0)
    q = lax.broadcasted_iota(jnp.int32, (GRID_W, GRID_W), 0)
    k = lax.broadcasted_iota(jnp.int32, (GRID_W, GRID_W), 1)
    start = jnp.clip(q - NA_COLS // 2, 0, GRID_W - NA_COLS)
    in_win = (k >= start) & (k < start + NA_COLS)
    off = k - q + NA_COLS - 1
    neg = jnp.full((GRID_W, GRID_W), -jnp.inf, F32)
    blocks = []
    for ro in range(2 * NA_ROWS - 1):
        t = neg
        for c in range(2 * NA_COLS - 1):
            t = jnp.where(off == c, rpb_ref[h, ro, c], t)
        blocks.append(jnp.where(in_win, t, neg))
    for di in range(kh):
        for i in range(kh):
            o_ref[0, di, :, i * GRID_W:(i + 1) * GRID_W] = blocks[i - di + NA_ROWS - 1]


def na_bias_table(rpb, rows):
    kh = min(NA_ROWS, rows)
    nh = rpb.shape[0]
    return pl.pallas_call(
        functools.partial(_na_bias_kernel, kh=kh),
        grid=(nh,),
        in_specs=[pl.BlockSpec(memory_space=pltpu.SMEM)],
        out_specs=pl.BlockSpec((1, kh, GRID_W, kh * GRID_W), lambda i: (i, 0, 0, 0)),
        out_shape=jax.ShapeDtypeStruct((nh, kh, GRID_W, kh * GRID_W), F32),
        compiler_params=_params(("arbitrary",)),
        name="na_bias",
    )(rpb)


def _shift_rows(x, d, t):
    n = x.shape[0]
    y = pltpu.roll(x, d % n, 0)
    src = t - d
    return jnp.where((src >= 0) & (src < n), y, 0.0)


def _pool_kernel(x_ref, w_ref, scale_ref, o_ref):
    x = x_ref[0]
    n, g = x.shape
    pg = g // len(POOL_WINDOWS)
    t = lax.broadcasted_iota(jnp.int32, x.shape, 0)
    group = lax.broadcasted_iota(jnp.int32, x.shape, 1) // pg
    fwd = x
    bwd = x
    cur = 1
    total = jnp.zeros_like(x)
    count = jnp.ones_like(x)
    for j, w in enumerate(POOL_WINDOWS):
        half = w // 2
        while cur < half:
            fwd = fwd + _shift_rows(fwd, -cur, t)
            bwd = bwd + _shift_rows(bwd, cur, t)
            cur *= 2
        win = _shift_rows(bwd, 1, t) + fwd
        lo = jnp.clip(t - half, 0, n)
        hi = jnp.clip(t - half + w, 0, n)
        total = jnp.where(group == j, win, total)
        count = jnp.where(group == j, (hi - lo).astype(F32), count)
    diff = total / count - x
    o_ref[0] = _dot(diff.astype(BF16), w_ref[...]) * scale_ref[...]


def pool_mixer(p, w_bd, scale):
    b, n, g = p.shape
    return pl.pallas_call(
        _pool_kernel,
        grid=(b,),
        in_specs=[pl.BlockSpec((1, n, g), lambda i: (i, 0, 0)),
                  pl.BlockSpec((g, g), lambda i: (0, 0)),
                  pl.BlockSpec((1, g), lambda i: (0, 0))],
        out_specs=pl.BlockSpec((1, n, g), lambda i: (i, 0, 0)),
        out_shape=jax.ShapeDtypeStruct((b, n, g), F32),
        compiler_params=_params(("arbitrary",)),
        name="pool_mixer",
    )(p, w_bd, scale)


def _rwkv_prep_kernel(p_ref, w0_ref, a0_ref, w2_ref, a2_ref, kk_w_ref, ka_ref, e_ref,
                      kk_ref, decf_ref, bf_ref, kmf_ref, decb_ref, bb_ref, kmb_ref):
    g = kk_ref.shape[1]
    k = p_ref[:, g:2 * g]
    lr = p_ref[:, 3 * g:4 * g]
    kx = k * kk_w_ref[...]
    norm = jnp.sqrt(_seg_sum(kx * kx, e_ref[...]))
    kk = kx / jnp.maximum(norm, 1e-12)
    kk_ref[...] = kk
    lr_t = jnp.tanh(lr).astype(BF16)
    lr_b = lr.astype(BF16)
    for d, (dec_ref, b_ref, km_ref) in enumerate(((decf_ref, bf_ref, kmf_ref), (decb_ref, bb_ref, kmb_ref))):
        z = w0_ref[d:d + 1, :] + _dot(lr_t, w2_ref[d])
        softplus_neg = jnp.maximum(-z, 0.0) + jnp.log1p(jnp.exp(-jnp.abs(z)))
        w = -softplus_neg - 0.5
        dec_ref[...] = jnp.exp(-jnp.exp(w))
        a = jax.nn.sigmoid(a0_ref[d:d + 1, :] + _dot(lr_b, a2_ref[d]))
        b_ref[...] = kk * a
        km_ref[...] = k * (1.0 + (a - 1.0) * ka_ref[...])


def rwkv_prep(pa, w0, a0, w2p, a2p, k_k, k_a, eseg):
    r, n = pa.shape
    g = n // 4
    tm = _tile(r, 512)
    full2 = lambda shape: pl.BlockSpec(shape, lambda i: (0,) * len(shape))
    out = pl.BlockSpec((tm, g), lambda i: (i, 0))
    return pl.pallas_call(
        _rwkv_prep_kernel,
        grid=(r // tm,),
        in_specs=[pl.BlockSpec((tm, n), lambda i: (i, 0)),
                  full2(w0.shape), full2(a0.shape), full2(w2p.shape), full2(a2p.shape),
                  full2(k_k.shape), full2(k_a.shape), full2(eseg.shape)],
        out_specs=[out] * 7,
        out_shape=[jax.ShapeDtypeStruct((r, g), F32)] * 7,
        compiler_params=_params(("arbitrary",)),
        name="rwkv_prep",
    )(pa, w0, a0, w2p, a2p, k_k, k_a, eseg)


def _wkv_scan_kernel(rf_ref, rb_ref, kkf_ref, kkb_ref, decf_ref, bf_ref, kmf_ref, decb_ref, bb_ref, kmb_ref,
                     vtf_ref, vtb_ref, e_ref, s0_ref, yf_ref, yb_ref, send_ref, s_scr):
    nb, t_blk, g = rf_ref.shape
    step_id = pl.program_id(0)

    @pl.when(step_id == 0)
    def _():
        s_scr[...] = s0_ref[...]

    yf_ref[...] = jnp.zeros_like(yf_ref)
    yb_ref[...] = jnp.zeros_like(yb_ref)
    e = e_ref[...]
    lane = lax.broadcasted_iota(jnp.int32, (HEAD_DIM, g), 1) % SCAN_SUB
    n = nb * HEAD_DIM
    dirs = ((rf_ref, kkf_ref, decf_ref, bf_ref, kmf_ref, vtf_ref, yf_ref),
            (rb_ref, kkb_ref, decb_ref, bb_ref, kmb_ref, vtb_ref, yb_ref))

    def one_token(tt, carry):
        for d, (r_ref, kk_ref, dec_ref, b_ref, km_ref, vt_ref, y_ref) in enumerate(dirs):
            tl = tt if d == 0 else t_blk - 1 - tt
            sub = tl // SCAN_SUB
            hit = lane == tl % SCAN_SUB
            row = lambda ref, b: jnp.broadcast_to(ref[b, pl.ds(tl, 1), :], (HEAD_DIM, g))
            blk = lambda x, b: x[b * HEAD_DIM:(b + 1) * HEAD_DIM]
            s_old = [s_scr[d, b] for b in range(nb)]
            x = jnp.concatenate([s_old[b] * row(kk_ref, b) for b in range(nb)], axis=0)
            hi = x.astype(BF16)
            lo = (x - hi.astype(F32)).astype(BF16)
            vm = jnp.concatenate([jnp.where(hit, vt_ref[b, sub], 0.0).astype(BF16) for b in range(nb)], axis=0)
            out = _dot(jnp.concatenate([hi, lo, vm], axis=0), e)
            sa = out[:n] + out[n:2 * n]
            vcol = out[2 * n:]
            sr = []
            for b in range(nb):
                s = s_old[b] * row(dec_ref, b) - blk(sa, b) * row(b_ref, b) + blk(vcol, b) * row(km_ref, b)
                s_scr[d, b] = s
                sr.append(s * row(r_ref, b))
            y = _dot(jnp.concatenate(sr, axis=0).astype(BF16), e)
            for b in range(nb):
                y_ref[b, sub] = jnp.where(hit, blk(y, b), y_ref[b, sub])
        return carry

    lax.fori_loop(0, t_blk, one_token, 0)

    @pl.when(step_id == pl.num_programs(0) - 1)
    def _():
        send_ref[...] = s_scr[...]


def wkv_scan(pa3, kk, dec_f, b_f, km_f, dec_b, b_b, km_b, vt, eseg, s0):
    nb, l, g = kk.shape
    t_blk = SCAN_BLOCK
    nsub = t_blk // SCAN_SUB
    nblk = l // t_blk
    assert l % t_blk == 0
    seq_f = pl.BlockSpec((nb, t_blk, g), lambda i: (0, i, 0))
    seq_b = pl.BlockSpec((nb, t_blk, g), lambda i: (0, nblk - 1 - i, 0))
    vt_f = pl.BlockSpec((nb, nsub, HEAD_DIM, g), lambda i: (0, i, 0, 0))
    vt_b = pl.BlockSpec((nb, nsub, HEAD_DIM, g), lambda i: (0, nblk - 1 - i, 0, 0))
    state = pl.BlockSpec(s0.shape, lambda i: (0, 0, 0, 0))
    return pl.pallas_call(
        _wkv_scan_kernel,
        grid=(nblk,),
        in_specs=[seq_f, seq_b, seq_f, seq_b, seq_f, seq_f, seq_f, seq_b, seq_b, seq_b,
                  vt_f, vt_b, pl.BlockSpec(eseg.shape, lambda i: (0, 0)), state],
        out_specs=[vt_f, vt_b, state],
        out_shape=[jax.ShapeDtypeStruct(vt.shape, F32), jax.ShapeDtypeStruct(vt.shape, F32),
                   jax.ShapeDtypeStruct(s0.shape, F32)],
        scratch_shapes=[pltpu.VMEM(s0.shape, F32)],
        compiler_params=_params(("arbitrary",)),
        name="wkv_scan",
    )(pa3, pa3, kk, kk, dec_f, b_f, km_f, dec_b, b_b, km_b, vt, vt, eseg, s0)


def to_head_tiles(x):
    b, l, g = x.shape
    x = x.reshape(b, l // SCAN_SUB, SCAN_SUB, g // HEAD_DIM, HEAD_DIM)
    return x.transpose(0, 1, 4, 3, 2).reshape(b, l // SCAN_SUB, HEAD_DIM, g)


def from_head_tiles(y):
    b, nblk, _, g = y.shape
    y = y.reshape(b, nblk, HEAD_DIM, g // HEAD_DIM, SCAN_SUB)
    return y.transpose(0, 1, 4, 3, 2).reshape(b, nblk * SCAN_SUB, g)


def _rwkv_readout_kernel(yf_ref, yb_ref, p_ref, kmf_ref, kmb_ref, rk_ref, g2_ref, gnw_ref, gnb_ref, e_ref, o_ref):
    g = o_ref.shape[1]
    e = e_ref[...]
    r = p_ref[:, :g]
    v = p_ref[:, 2 * g:3 * g]
    lr = p_ref[:, 3 * g:4 * g]
    y = yf_ref[...] + yb_ref[...]
    mu = _seg_sum(y, e) * (1.0 / HEAD_DIM)
    yc = y - mu
    var = _seg_sum(yc * yc, e) * (1.0 / HEAD_DIM)
    yn = yc * lax.rsqrt(var + RWKV_GN_EPS) * gnw_ref[...] + gnb_ref[...]
    bonus = _seg_sum(r * (kmf_ref[...] + kmb_ref[...]) * rk_ref[...], e) * v
    gate = _dot(jax.nn.sigmoid(lr).astype(BF16), g2_ref[...])
    o_ref[...] = (yn + bonus) * gate


def rwkv_readout(yf, yb, pa, km_f, km_b, r_k, g2p, gn_w, gn_b, eseg):
    r, g = yf.shape
    tm = _tile(r, 512)
    row = pl.BlockSpec((tm, g), lambda i: (i, 0))
    full2 = lambda shape: pl.BlockSpec(shape, lambda i: (0,) * len(shape))
    return pl.pallas_call(
        _rwkv_readout_kernel,
        grid=(r // tm,),
        in_specs=[row, row, pl.BlockSpec((tm, 4 * g), lambda i: (i, 0)), row, row,
                  full2(r_k.shape), full2(g2p.shape), full2(gn_w.shape), full2(gn_b.shape), full2(eseg.shape)],
        out_specs=row,
        out_shape=jax.ShapeDtypeStruct((r, g), F32),
        compiler_params=_params(("arbitrary",)),
        name="rwkv_readout",
    )(yf, yb, pa, km_f, km_b, r_k, g2p, gn_w, gn_b, eseg)


def _block_ones(n, seg):
    idx = np.arange(n) // seg
    return jnp.asarray(idx[:, None] == idx[None, :], dtype=BF16)


def _padded_rows(w, offset, n):
    return jnp.zeros((n, w.shape[1]), w.dtype).at[offset:offset + w.shape[0]].set(w)


def _rope_tables(n_tokens, n_rep):
    t = jnp.arange(n_tokens)
    row = (t // GRID_W).astype(F32)
    col = (t % GRID_W).astype(F32)
    n_freq = HEAD_DIM // 4
    inv_freq = ROPE_THETA ** (-jnp.arange(n_freq, dtype=F32) / n_freq)
    ang = jnp.concatenate([row[:, None] * inv_freq, col[:, None] * inv_freq], axis=-1)
    cos, sin = jnp.cos(ang), jnp.sin(ang)
    cos_h = jnp.concatenate([cos, cos], axis=-1)
    sin_h = jnp.concatenate([-sin, sin], axis=-1)
    return jnp.tile(cos_h, (1, n_rep)), jnp.tile(sin_h, (1, n_rep))


def _heads_first(t, b, l):
    return t.reshape(b, l, -1, HEAD_DIM).transpose(0, 2, 1, 3)


def _heads_last(t):
    b, h, l, d = t.shape
    return t.transpose(0, 2, 1, 3).reshape(b * l, h * d)


def kernel(x, c, ctx, c_ctx, ada_w, ada_b, norm1_g, norm2_g, w_in, w_out, rwkv_w0, rwkv_w2, rwkv_a0, rwkv_a2,
           rwkv_k_k, rwkv_k_a, rwkv_r_k, rwkv_g2, rwkv_gn_w, rwkv_gn_b, na_rpb, gqa_q_gain, gqa_k_gain, pool_w,
           pool_scale, mlp_w1, mlp_w2, final_g):
    nb, l, d = x.shape
    lc = ctx.shape[1]
    depth = ada_w.shape[0]
    g = d // 4
    nh = g // HEAD_DIM
    nkv = nh // 2
    kvw = nkv * HEAD_DIM
    dr = rwkv_w2.shape[2]
    ir = rwkv_a2.shape[2]
    gr = rwkv_g2.shape[1]
    assert 2 * dr + 2 * ir + gr <= g and nb + 1 <= 8
    rows = l // GRID_W

    cpad = jnp.zeros((8, d), F32).at[:nb].set(c).at[nb].set(c_ctx)
    mods = adaln(cpad, ada_w, ada_b)

    eseg = _block_ones(g, HEAD_DIM)
    ekv = _block_ones(kvw, HEAD_DIM)
    cos_q, sin_q = _rope_tables(l, nh)
    ones_c, zeros_c = jnp.ones((lc, g), F32), jnp.zeros((lc, g), F32)

    splits = np.cumsum([0, g, g, g, dr, dr, ir, ir, gr, g, g, g, g, kvw, kvw, g])
    lowrank_w = splits[8] - splits[3]
    widths = (4 * g, 3 * g, 2 * g, g)

    xl = x.reshape(nb * l, d)
    xc = ctx.reshape(nb * lc, d)
    zero_state = jnp.zeros((2, nb, HEAD_DIM, g), F32)

    for i in range(depth):
        need_ctx_out = i < depth - 1
        mod_l = mods[i, :nb].reshape(nb, N_MOD, 1, d)
        mod_c = mods[i, nb].reshape(N_MOD, 1, 1, d)
        ml = [mod_l[:, k] for k in range(N_MOD)]
        mc = [mod_c[k] for k in range(N_MOD)]

        wi = w_in[i]
        w_inp = jnp.concatenate([wi[:, :splits[8]], jnp.zeros((d, g - lowrank_w), F32), wi[:, splits[8]:]],
                                axis=1).astype(BF16)
        g1 = norm1_g[i].reshape(1, d)
        g2 = norm2_g[i].reshape(1, d)
        pa_l, pb_l, pc_l, pd_l = inproj(xl, g1, ml[0], ml[1], w_inp, l, widths)
        pa_c, pb_c, pc_c, pd_c = inproj(xc, g1, mc[0], mc[1], w_inp, nb * lc, widths)

        w2p = jnp.stack([_padded_rows(rwkv_w2[i, 0], 0, g), _padded_rows(rwkv_w2[i, 1], dr, g)]).astype(BF16)
        a2p = jnp.stack([_padded_rows(rwkv_a2[i, 0], 2 * dr, g),
                         _padded_rows(rwkv_a2[i, 1], 2 * dr + ir, g)]).astype(BF16)
        g2p = _padded_rows(rwkv_g2[i], 2 * dr + 2 * ir, g).astype(BF16)
        k_k = rwkv_k_k[i].reshape(1, g)
        k_a = rwkv_k_a[i].reshape(1, g)
        r_k = rwkv_r_k[i].reshape(1, g)
        gn_w = rwkv_gn_w[i].reshape(1, g)
        gn_b = rwkv_gn_b[i].reshape(1, g)

        def rwkv_terms(pa, seq):
            terms = rwkv_prep(pa, rwkv_w0[i], rwkv_a0[i], w2p, a2p, k_k, k_a, eseg)
            terms3 = [t.reshape(nb, seq, g) for t in terms]
            vt = to_head_tiles(pa[:, 2 * g:3 * g].reshape(nb, seq, g))
            return terms, terms3, vt

        terms_c, terms3_c, vt_c = rwkv_terms(pa_c, lc)
        yf_c, yb_c, state_c = wkv_scan(pa_c.reshape(nb, lc, 4 * g), *terms3_c, vt_c, eseg, zero_state)
        terms_l, terms3_l, vt_l = rwkv_terms(pa_l, l)
        yf_l, yb_l, _ = wkv_scan(pa_l.reshape(nb, l, 4 * g), *terms3_l, vt_l, eseg, state_c)
        a_l = rwkv_readout(from_head_tiles(yf_l).reshape(nb * l, g), from_head_tiles(yb_l).reshape(nb * l, g),
                           pa_l, terms_l[3], terms_l[6], r_k, g2p, gn_w, gn_b, eseg)

        def heads_b(p, seq, scale):
            out = []
            for k in range(3):
                t = p[:, k * g:(k + 1) * g]
                if k == 0:
                    t = t * scale
                out.append(_heads_first(t.astype(BF16), nb, seq).reshape(nb * nh, seq, HEAD_DIM))
            return out

        qb_l, kb_l, vb_l = heads_b(pb_l, l, HEAD_DIM ** -0.5)
        qb_c, kb_c, vb_c = heads_b(pb_c, lc, HEAD_DIM ** -0.5)
        bias = na_bias_table(na_rpb[i], rows)
        b_l = neighbourhood_attention(qb_l, kb_l, vb_l, kb_c, vb_c, bias, nh)
        b_l = _heads_last(b_l.reshape(nb, nh, l, HEAD_DIM))

        q_gain = jnp.tile(gqa_q_gain[i], nh).reshape(1, g)
        k_gain = jnp.tile(gqa_k_gain[i], nkv).reshape(1, kvw)
        qc_l, kc_l, vc_l = qknorm_rope(pc_l, q_gain, k_gain, cos_q, sin_q, eseg, ekv, l, True)
        qc_c, kc_c, vc_c = qknorm_rope(pc_c, q_gain, k_gain, ones_c, zeros_c, eseg, ekv, lc, False)

        def kv_heads(t, seq):
            return _heads_first(t, nb, seq).reshape(nb * nkv, seq, HEAD_DIM)

        def q_heads(t, seq):
            return _heads_first(t, nb, seq).reshape(nb * nkv, nh // nkv, seq, HEAD_DIM)

        k_all = jnp.concatenate([kv_heads(kc_c, lc), kv_heads(kc_l, l)], axis=1)
        v_all = jnp.concatenate([kv_heads(vc_c, lc), kv_heads(vc_l, l)], axis=1)
        c_l = attention(q_heads(qc_l, l), k_all, v_all)
        c_l = _heads_last(c_l.reshape(nb, nh, l, HEAD_DIM))

        w_bd = jax.scipy.linalg.block_diag(*[pool_w[i, k] for k in range(len(POOL_WINDOWS))]).astype(BF16)
        p_scale = pool_scale[i].reshape(1, g)
        d_l = pool_mixer(pd_l.reshape(nb, l, g), w_bd, p_scale).reshape(nb * l, g)

        w_out4 = w_out[i].reshape(4, g, d).astype(BF16)
        w1 = mlp_w1[i].astype(BF16)
        w2 = mlp_w2[i].astype(BF16)
        xl = outproj((a_l, b_l, c_l, d_l), w_out4, xl, ml[2], l)
        xl = mlp(xl, g2, ml[3], ml[4], ml[5], w1, w2, l)

        if need_ctx_out:
            a_c = rwkv_readout(from_head_tiles(yf_c).reshape(nb * lc, g), from_head_tiles(yb_c).reshape(nb * lc, g),
                               pa_c, terms_c[3], terms_c[6], r_k, g2p, gn_w, gn_b, eseg)
            b_c = attention(qb_c.reshape(nb * nh, 1, lc, HEAD_DIM), kb_c, vb_c)
            b_c = _heads_last(b_c.reshape(nb, nh, lc, HEAD_DIM))
            c_c = attention(q_heads(qc_c, lc), kv_heads(kc_c, lc), kv_heads(vc_c, lc))
            c_c = _heads_last(c_c.reshape(nb, nh, lc, HEAD_DIM))
            d_c = pool_mixer(pd_c.reshape(nb, lc, g), w_bd, p_scale).reshape(nb * lc, g)
            xc = outproj((a_c, b_c, c_c, d_c), w_out4, xc, mc[2], nb * lc)
            xc = mlp(xc, g2, mc[3], mc[4], mc[5], w1, w2, nb * lc)

    return final_norm(xl, final_g.reshape(1, d)).reshape(nb, l, d)
```

```python
import functools

import jax
import jax.numpy as jnp
import numpy as np
from jax import lax
from jax.experimental import pallas as pl
from jax.experimental.pallas import tpu as pltpu

F32 = jnp.float32
BF16 = jnp.bfloat16

HEAD_DIM = 64
GRID_W = 64
NA_ROWS = 8
NA_COLS = 16
ROPE_THETA = 10000.0
POOL_WINDOWS = (2, 4, 8, 16)
NORM_EPS = 1e-6
RWKV_GN_EPS = 1e-5 * HEAD_DIM
N_MOD = 6
LANES = 128
SCAN_BLOCK = 128
SCAN_SUB = 64
VMEM_LIMIT = 52 * 1024 * 1024


def _params(sem):
    return pltpu.CompilerParams(dimension_semantics=sem, vmem_limit_bytes=VMEM_LIMIT)


def _tile(n, pref):
    t = min(n, pref)
    assert n % t == 0, (n, pref)
    return t


def _dot(a, b):
    return jnp.dot(a, b, preferred_element_type=F32)


def _dot_nt(a, b):
    return lax.dot_general(a, b, (((1,), (1,)), ((), ())), preferred_element_type=F32)


def _seg_sum(x, e):
    hi = x.astype(BF16)
    lo = (x - hi.astype(F32)).astype(BF16)
    return _dot(hi, e) + _dot(lo, e)


def _norm_mod(x, g, shift, scale):
    ms = jnp.mean(x * x, axis=-1, keepdims=True)
    h = x * lax.rsqrt(ms + NORM_EPS) * g
    return h * (1.0 + scale) + shift


def _adaln_kernel(c_ref, w_ref, b_ref, o_ref):
    c = c_ref[...]
    s = c * jax.nn.sigmoid(c)
    o_ref[0] = jnp.dot(s, w_ref[0], preferred_element_type=F32, precision=lax.Precision.HIGHEST) + b_ref[0]


def adaln(cpad, ada_w, ada_b):
    depth, d, n = ada_w.shape
    tn = _tile(n, 1536)
    return pl.pallas_call(
        _adaln_kernel,
        grid=(depth, n // tn),
        in_specs=[pl.BlockSpec((8, d), lambda i, j: (0, 0)),
                  pl.BlockSpec((1, d, tn), lambda i, j: (i, 0, j)),
                  pl.BlockSpec((1, 1, tn), lambda i, j: (i, 0, j))],
        out_specs=pl.BlockSpec((1, 8, tn), lambda i, j: (i, 0, j)),
        out_shape=jax.ShapeDtypeStruct((depth, 8, n), F32),
        compiler_params=_params(("arbitrary", "arbitrary")),
        name="adaln",
    )(cpad, ada_w, ada_b.reshape(depth, 1, n))


def _inproj_kernel(x_ref, g_ref, sh_ref, sc_ref, w_ref, oa_ref, ob_ref, oc_ref, od_ref):
    h = _norm_mod(x_ref[...], g_ref[...], sh_ref[0], sc_ref[0]).astype(BF16)
    off = 0
    for o_ref in (oa_ref, ob_ref, oc_ref, od_ref):
        n = o_ref.shape[1]
        o_ref[...] = _dot(h, w_ref[:, off:off + n]).astype(o_ref.dtype)
        off += n


def inproj(x, g, shift, scale, w, rows_per_mod, widths, dtypes):
    r, d = x.shape
    tm = _tile(rows_per_mod, 512)
    tpb = rows_per_mod // tm
    modspec = pl.BlockSpec((1, 1, d), lambda i: (i // tpb, 0, 0))
    return pl.pallas_call(
        _inproj_kernel,
        grid=(r // tm,),
        in_specs=[pl.BlockSpec((tm, d), lambda i: (i, 0)),
                  pl.BlockSpec((1, d), lambda i: (0, 0)),
                  modspec, modspec,
                  pl.BlockSpec(w.shape, lambda i: (0, 0))],
        out_specs=[pl.BlockSpec((tm, n), lambda i: (i, 0)) for n in widths],
        out_shape=[jax.ShapeDtypeStruct((r, n), dt) for n, dt in zip(widths, dtypes)],
        compiler_params=_params(("arbitrary",)),
        name="inproj",
    )(x, g, shift, scale, w)


def _outproj_kernel(a_ref, b_ref, c_ref, d_ref, w_ref, x_ref, gate_ref, o_ref):
    acc = _dot(a_ref[...].astype(BF16), w_ref[0])
    acc += _dot(b_ref[...].astype(BF16), w_ref[1])
    acc += _dot(c_ref[...].astype(BF16), w_ref[2])
    acc += _dot(d_ref[...].astype(BF16), w_ref[3])
    o_ref[...] = x_ref[...] + gate_ref[0] * acc


def outproj(mix, w4, x, gate, rows_per_mod):
    r, d = x.shape
    gw = w4.shape[1]
    tm = _tile(rows_per_mod, 512)
    tpb = rows_per_mod // tm
    mixspec = pl.BlockSpec((tm, gw), lambda i: (i, 0))
    return pl.pallas_call(
        _outproj_kernel,
        grid=(r // tm,),
        in_specs=[mixspec, mixspec, mixspec, mixspec,
                  pl.BlockSpec(w4.shape, lambda i: (0, 0, 0)),
                  pl.BlockSpec((tm, d), lambda i: (i, 0)),
                  pl.BlockSpec((1, 1, d), lambda i: (i // tpb, 0, 0))],
        out_specs=pl.BlockSpec((tm, d), lambda i: (i, 0)),
        out_shape=jax.ShapeDtypeStruct((r, d), F32),
        compiler_params=_params(("arbitrary",)),
        name="outproj",
    )(*mix, w4, x, gate)


def _mlp_kernel(x_ref, g_ref, sh_ref, sc_ref, gate_ref, w1_ref, w2_ref, o_ref, h_scr, acc_scr):
    j = pl.program_id(1)

    @pl.when(j == 0)
    def _():
        h_scr[...] = _norm_mod(x_ref[...], g_ref[...], sh_ref[0], sc_ref[0]).astype(BF16)
        acc_scr[...] = jnp.zeros_like(acc_scr)

    u = jnp.maximum(_dot(h_scr[...], w1_ref[...]), 0.0)
    acc_scr[...] += _dot((u * u).astype(BF16), w2_ref[...])

    @pl.when(j == pl.num_programs(1) - 1)
    def _():
        o_ref[...] = x_ref[...] + gate_ref[0] * acc_scr[...]


def mlp(x, g, shift, scale, gate, w1, w2, rows_per_mod):
    r, d = x.shape
    dff = w1.shape[1]
    tm = _tile(rows_per_mod, 1024)
    tf = _tile(dff, 512)
    tpb = rows_per_mod // tm
    modspec = pl.BlockSpec((1, 1, d), lambda i, j: (i // tpb, 0, 0))
    return pl.pallas_call(
        _mlp_kernel,
        grid=(r // tm, dff // tf),
        in_specs=[pl.BlockSpec((tm, d), lambda i, j: (i, 0)),
                  pl.BlockSpec((1, d), lambda i, j: (0, 0)),
                  modspec, modspec, modspec,
                  pl.BlockSpec((d, tf), lambda i, j: (0, j)),
                  pl.BlockSpec((tf, d), lambda i, j: (j, 0))],
        out_specs=pl.BlockSpec((tm, d), lambda i, j: (i, 0)),
        out_shape=jax.ShapeDtypeStruct((r, d), F32),
        scratch_shapes=[pltpu.VMEM((tm, d), BF16), pltpu.VMEM((tm, d), F32)],
        compiler_params=_params(("arbitrary", "arbitrary")),
        name="mlp",
    )(x, g, shift, scale, gate, w1, w2)


def _final_norm_kernel(x_ref, g_ref, o_ref):
    x = x_ref[...]
    ms = jnp.mean(x * x, axis=-1, keepdims=True)
    o_ref[...] = x * lax.rsqrt(ms + NORM_EPS) * g_ref[...]


def final_norm(x, g):
    r, d = x.shape
    tm = _tile(r, 1024)
    return pl.pallas_call(
        _final_norm_kernel,
        grid=(r // tm,),
        in_specs=[pl.BlockSpec((tm, d), lambda i: (i, 0)), pl.BlockSpec((1, d), lambda i: (0, 0))],
        out_specs=pl.BlockSpec((tm, d), lambda i: (i, 0)),
        out_shape=jax.ShapeDtypeStruct((r, d), F32),
        compiler_params=_params(("arbitrary",)),
        name="final_norm",
    )(x, g)


def _swap_halves(y):
    n = y.shape[-1]
    lane = lax.broadcasted_iota(jnp.int32, y.shape, 1)
    half = HEAD_DIM // 2
    return jnp.where(lane % HEAD_DIM < half, pltpu.roll(y, n - half, 1), pltpu.roll(y, half, 1))


def _qknorm_rope_kernel(p_ref, qg_ref, kg_ref, cos_ref, sin_ref, eq_ref, ek_ref, q_ref, k_ref, v_ref, *, rope):
    gq = q_ref.shape[1]
    gk = k_ref.shape[1]
    p = p_ref[...]

    def normed(x, gain, e):
        ms = _seg_sum(x * x, e) * (1.0 / HEAD_DIM)
        return x * lax.rsqrt(ms + NORM_EPS) * gain

    q = normed(p[:, :gq], qg_ref[...], eq_ref[...])
    k = normed(p[:, gq:gq + gk], kg_ref[...], ek_ref[...])
    if rope:
        cos = cos_ref[...]
        sin = sin_ref[...]
        q = q * cos + _swap_halves(q) * sin
        k = k * cos[:, :gk] + _swap_halves(k) * sin[:, :gk]
    q_ref[...] = (q * HEAD_DIM ** -0.5).astype(BF16)
    k_ref[...] = k.astype(BF16)
    v_ref[...] = p[:, gq + gk:].astype(BF16)


def qknorm_rope(pc, q_gain, k_gain, cos, sin, eq, ek, seq_len, rope):
    r, n = pc.shape
    gq = n // 2
    gk = n // 4
    tm = _tile(seq_len, 512)
    nseq = seq_len // tm
    return pl.pallas_call(
        functools.partial(_qknorm_rope_kernel, rope=rope),
        grid=(r // tm,),
        in_specs=[pl.BlockSpec((tm, n), lambda i: (i, 0)),
                  pl.BlockSpec((1, gq), lambda i: (0, 0)),
                  pl.BlockSpec((1, gk), lambda i: (0, 0)),
                  pl.BlockSpec((tm, gq), lambda i: (i % nseq, 0)),
                  pl.BlockSpec((tm, gq), lambda i: (i % nseq, 0)),
                  pl.BlockSpec(eq.shape, lambda i: (0, 0)),
                  pl.BlockSpec(ek.shape, lambda i: (0, 0))],
        out_specs=[pl.BlockSpec((tm, gq), lambda i: (i, 0)),
                   pl.BlockSpec((tm, gk), lambda i: (i, 0)),
                   pl.BlockSpec((tm, gk), lambda i: (i, 0))],
        out_shape=[jax.ShapeDtypeStruct((r, gq), BF16),
                   jax.ShapeDtypeStruct((r, gk), BF16),
                   jax.ShapeDtypeStruct((r, gk), BF16)],
        compiler_params=_params(("arbitrary",)),
        name="qknorm_rope",
    )(pc, q_gain, k_gain, cos, sin, eq, ek)


def _pad_heads(q, n_heads, kv_width, rep):
    assert kv_width % LANES == 0
    t = q.shape[0]
    lane = lax.broadcasted_iota(jnp.int32, (t, LANES), 1)
    lo_half = lane < HEAD_DIM
    zero = jnp.zeros((t, LANES), q.dtype)
    rows = []
    for h in range(n_heads):
        src = h * HEAD_DIM
        dst = (h // rep) * HEAD_DIM
        piece = q[:, src // LANES * LANES:(src // LANES + 1) * LANES]
        if src % LANES != dst % LANES:
            piece = pltpu.roll(piece.astype(F32), HEAD_DIM, 1).astype(q.dtype)
        piece = jnp.where(lo_half if dst % LANES == 0 else ~lo_half, piece, zero)
        blocks = [piece if j == dst // LANES else zero for j in range(kv_width // LANES)]
        rows.append(jnp.concatenate(blocks, axis=1) if len(blocks) > 1 else piece)
    return jnp.concatenate(rows, axis=0)


def _gather_heads(res, n_heads, rep):
    t = res.shape[0] // n_heads
    lane = lax.broadcasted_iota(jnp.int32, (t, LANES), 1)
    lo_half = lane < HEAD_DIM
    pieces = []
    for h in range(n_heads):
        src = (h // rep) * HEAD_DIM
        dst = h * HEAD_DIM
        piece = res[h * t:(h + 1) * t, src // LANES * LANES:(src // LANES + 1) * LANES]
        if src % LANES != dst % LANES:
            piece = pltpu.roll(piece, HEAD_DIM, 1)
        pieces.append(piece)
    blocks = [jnp.where(lo_half, pieces[2 * j], pieces[2 * j + 1]) for j in range(n_heads // 2)]
    return jnp.concatenate(blocks, axis=1)


def _attn_kernel(q_ref, k_ref, v_ref, o_ref, *, rep):
    n_heads = q_ref.shape[1] // HEAD_DIM
    q = _pad_heads(q_ref[...], n_heads, k_ref.shape[2], rep)
    s = _dot_nt(q, k_ref[0])
    m = jnp.max(s, axis=-1, keepdims=True)
    p = jnp.exp(s - m)
    l = jnp.sum(p, axis=-1, keepdims=True)
    res = _dot(p.astype(BF16), v_ref[0]) / l
    o_ref[...] = _gather_heads(res, n_heads, rep)


def attention(q, k, v, seq_len):
    r, gq = q.shape
    nb, lk, kw = k.shape
    rep = gq // kw
    tq = _tile(seq_len, 128)
    nq = seq_len // tq
    return pl.pallas_call(
        functools.partial(_attn_kernel, rep=rep),
        grid=(nb, nq),
        in_specs=[pl.BlockSpec((tq, gq), lambda i, j: (i * nq + j, 0)),
                  pl.BlockSpec((1, lk, kw), lambda i, j: (i, 0, 0)),
                  pl.BlockSpec((1, lk, kw), lambda i, j: (i, 0, 0))],
        out_specs=pl.BlockSpec((tq, gq), lambda i, j: (i * nq + j, 0)),
        out_shape=jax.ShapeDtypeStruct((r, gq), F32),
        compiler_params=_params(("arbitrary", "arbitrary")),
        name="attention",
    )(q, k, v)


def _na_kernel(p_ref, pc_ref, bias_ref, o_ref, *, rows, kh):
    g = o_ref.shape[2]
    n_heads = g // HEAD_DIM
    band = kh * GRID_W
    kc = pc_ref[0, :, g:2 * g]
    vc = pc_ref[0, :, 2 * g:3 * g]

    def one_row(r, carry):
        rs = jnp.clip(r - kh // 2, 0, rows - kh)
        q0 = pl.multiple_of(r * GRID_W, GRID_W)
        k0 = pl.multiple_of(rs * GRID_W, GRID_W)
        q = _pad_heads(p_ref[0, pl.ds(q0, GRID_W), 0:g], n_heads, g, 1)
        s1 = _dot_nt(q, p_ref[0, pl.ds(k0, band), g:2 * g]) + bias_ref[r - rs]
        s2 = _dot_nt(q, kc)
        m = jnp.maximum(jnp.max(s1, axis=-1, keepdims=True), jnp.max(s2, axis=-1, keepdims=True))
        p1 = jnp.exp(s1 - m)
        p2 = jnp.exp(s2 - m)
        l = jnp.sum(p1, axis=-1, keepdims=True) + jnp.sum(p2, axis=-1, keepdims=True)
        res = (_dot(p1.astype(BF16), p_ref[0, pl.ds(k0, band), 2 * g:3 * g]) + _dot(p2.astype(BF16), vc)) / l
        o_ref[0, pl.ds(q0, GRID_W), :] = _gather_heads(res, n_heads, 1)
        return carry

    lax.fori_loop(0, rows, one_row, 0)


def neighbourhood_attention(p, pc, bias):
    nb, l, g3 = p.shape
    g = g3 // 3
    lc = pc.shape[1]
    rows = l // GRID_W
    kh = min(NA_ROWS, rows)
    return pl.pallas_call(
        functools.partial(_na_kernel, rows=rows, kh=kh),
        grid=(nb,),
        in_specs=[pl.BlockSpec((1, l, g3), lambda i: (i, 0, 0)),
                  pl.BlockSpec((1, lc, g3), lambda i: (i, 0, 0)),
                  pl.BlockSpec(bias.shape, lambda i: (0, 0, 0))],
        out_specs=pl.BlockSpec((1, l, g), lambda i: (i, 0, 0)),
        out_shape=jax.ShapeDtypeStruct((nb, l, g), F32),
        compiler_params=_params(("arbitrary",)),
        name="neighbourhood_attention",
    )(p, pc, bias)


def _na_bias_kernel(rpb_ref, o_ref, *, kh):
    h = pl.program_id(0)
    q = lax.broadcasted_iota(jnp.int32, (GRID_W, GRID_W), 0)
    k = lax.broadcasted_iota(jnp.int32, (GRID_W, GRID_W), 1)
    start = jnp.clip(q - NA_COLS // 2, 0, GRID_W - NA_COLS)
    in_win = (k >= start) & (k < start + NA_COLS)
    off = k - q + NA_COLS - 1
    neg = jnp.full((GRID_W, GRID_W), -jnp.inf, F32)
    blocks = []
    for ro in range(2 * NA_ROWS - 1):
        t = neg
        for c in range(2 * NA_COLS - 1):
            t = jnp.where(off == c, rpb_ref[h, ro, c], t)
        blocks.append(jnp.where(in_win, t, neg))
    for di in range(kh):
        for i in range(kh):
            o_ref[di, :, i * GRID_W:(i + 1) * GRID_W] = blocks[i - di + NA_ROWS - 1]


def na_bias_table(rpb, rows):
    kh = min(NA_ROWS, rows)
    nh = rpb.shape[0]
    return pl.pallas_call(
        functools.partial(_na_bias_kernel, kh=kh),
        grid=(nh,),
        in_specs=[pl.BlockSpec(memory_space=pltpu.SMEM)],
        out_specs=pl.BlockSpec((kh, GRID_W, kh * GRID_W), lambda i: (0, i, 0)),
        out_shape=jax.ShapeDtypeStruct((kh, nh * GRID_W, kh * GRID_W), F32),
        compiler_params=_params(("arbitrary",)),
        name="na_bias",
    )(rpb)


def _shift_rows(x, d, t):
    n = x.shape[0]
    y = pltpu.roll(x, d % n, 0)
    src = t - d
    return jnp.where((src >= 0) & (src < n), y, 0.0)


def _pool_kernel(x_ref, w_ref, scale_ref, o_ref):
    x = x_ref[0]
    n, g = x.shape
    pg = g // len(POOL_WINDOWS)
    t = lax.broadcasted_iota(jnp.int32, x.shape, 0)
    group = lax.broadcasted_iota(jnp.int32, x.shape, 1) // pg
    fwd = x
    bwd = x
    cur = 1
    total = jnp.zeros_like(x)
    count = jnp.ones_like(x)
    for j, w in enumerate(POOL_WINDOWS):
        half = w // 2
        while cur < half:
            fwd = fwd + _shift_rows(fwd, -cur, t)
            bwd = bwd + _shift_rows(bwd, cur, t)
            cur *= 2
        win = _shift_rows(bwd, 1, t) + fwd
        lo = jnp.clip(t - half, 0, n)
        hi = jnp.clip(t - half + w, 0, n)
        total = jnp.where(group == j, win, total)
        count = jnp.where(group == j, (hi - lo).astype(F32), count)
    diff = total / count - x
    o_ref[0] = _dot(diff.astype(BF16), w_ref[...]) * scale_ref[...]


def pool_mixer(p, w_bd, scale):
    b, n, g = p.shape
    return pl.pallas_call(
        _pool_kernel,
        grid=(b,),
        in_specs=[pl.BlockSpec((1, n, g), lambda i: (i, 0, 0)),
                  pl.BlockSpec((g, g), lambda i: (0, 0)),
                  pl.BlockSpec((1, g), lambda i: (0, 0))],
        out_specs=pl.BlockSpec((1, n, g), lambda i: (i, 0, 0)),
        out_shape=jax.ShapeDtypeStruct((b, n, g), F32),
        compiler_params=_params(("arbitrary",)),
        name="pool_mixer",
    )(p, w_bd, scale)


def _rwkv_prep_kernel(p_ref, w0_ref, a0_ref, w2_ref, a2_ref, kk_w_ref, ka_ref, e_ref,
                      kk_ref, decf_ref, bf_ref, kmf_ref, decb_ref, bb_ref, kmb_ref):
    g = kk_ref.shape[1]
    k = p_ref[:, g:2 * g]
    lr = p_ref[:, 3 * g:4 * g]
    kx = k * kk_w_ref[...]
    norm = jnp.sqrt(_seg_sum(kx * kx, e_ref[...]))
    kk = kx / jnp.maximum(norm, 1e-12)
    kk_ref[...] = kk
    lr_t = jnp.tanh(lr).astype(BF16)
    lr_b = lr.astype(BF16)
    for d, (dec_ref, b_ref, km_ref) in enumerate(((decf_ref, bf_ref, kmf_ref), (decb_ref, bb_ref, kmb_ref))):
        z = w0_ref[d:d + 1, :] + _dot(lr_t, w2_ref[d])
        softplus_neg = jnp.maximum(-z, 0.0) + jnp.log1p(jnp.exp(-jnp.abs(z)))
        w = -softplus_neg - 0.5
        dec_ref[...] = jnp.exp(-jnp.exp(w))
        a = jax.nn.sigmoid(a0_ref[d:d + 1, :] + _dot(lr_b, a2_ref[d]))
        b_ref[...] = kk * a
        km_ref[...] = k * (1.0 + (a - 1.0) * ka_ref[...])


def rwkv_prep(pa, w0, a0, w2p, a2p, k_k, k_a, eseg):
    r, n = pa.shape
    g = n // 4
    tm = _tile(r, 512)
    full2 = lambda shape: pl.BlockSpec(shape, lambda i: (0,) * len(shape))
    out = pl.BlockSpec((tm, g), lambda i: (i, 0))
    return pl.pallas_call(
        _rwkv_prep_kernel,
        grid=(r // tm,),
        in_specs=[pl.BlockSpec((tm, n), lambda i: (i, 0)),
                  full2(w0.shape), full2(a0.shape), full2(w2p.shape), full2(a2p.shape),
                  full2(k_k.shape), full2(k_a.shape), full2(eseg.shape)],
        out_specs=[out] * 7,
        out_shape=[jax.ShapeDtypeStruct((r, g), F32)] * 7,
        compiler_params=_params(("arbitrary",)),
        name="rwkv_prep",
    )(pa, w0, a0, w2p, a2p, k_k, k_a, eseg)


def _wkv_scan_kernel(rf_ref, rb_ref, kkf_ref, kkb_ref, decf_ref, bf_ref, kmf_ref, decb_ref, bb_ref, kmb_ref,
                     vtf_ref, vtb_ref, e_ref, s0_ref, yf_ref, yb_ref, send_ref, s_scr, sr_scr):
    nb, t_blk, g = rf_ref.shape
    step_id = pl.program_id(0)

    @pl.when(step_id == 0)
    def _():
        s_scr[...] = s0_ref[...]

    yf_ref[...] = jnp.zeros_like(yf_ref)
    yb_ref[...] = jnp.zeros_like(yb_ref)
    sr_scr[...] = jnp.zeros_like(sr_scr)
    e = e_ref[...]
    lane = lax.broadcasted_iota(jnp.int32, (HEAD_DIM, g), 1) % SCAN_SUB
    lane128 = lax.broadcasted_iota(jnp.int32, (HEAD_DIM, LANES), 1)
    n = nb * HEAD_DIM
    dirs = ((rf_ref, kkf_ref, decf_ref, bf_ref, kmf_ref, vtf_ref, yf_ref),
            (rb_ref, kkb_ref, decb_ref, bb_ref, kmb_ref, vtb_ref, yb_ref))

    def write_y(d, y, tt_done, valid):
        y_ref = dirs[d][-1]
        tl = tt_done if d == 0 else t_blk - 1 - tt_done
        tl = jnp.clip(tl, 0, t_blk - 1)
        hit = (lane == tl % SCAN_SUB) & valid
        for b in range(nb):
            y_ref[b, tl // SCAN_SUB] = jnp.where(hit, y[b * HEAD_DIM:(b + 1) * HEAD_DIM], y_ref[b, tl // SCAN_SUB])

    def one_token(tt, carry):
        for d, (r_ref, kk_ref, dec_ref, b_ref, km_ref, vt_ref, y_ref) in enumerate(dirs):
            tl = tt if d == 0 else t_blk - 1 - tt
            row = lambda ref, b: jnp.broadcast_to(ref[b, pl.ds(tl, 1), :], (HEAD_DIM, g))
            blk = lambda x, b: x[b * HEAD_DIM:(b + 1) * HEAD_DIM]
            s_old = [s_scr[d, b] for b in range(nb)]
            x = jnp.concatenate([s_old[b] * row(kk_ref, b) for b in range(nb)], axis=0)
            idx = (lane128 // SCAN_SUB) * SCAN_SUB + tl % SCAN_SUB
            vcols = [jnp.concatenate([jnp.take_along_axis(vt_ref[b, tl // SCAN_SUB][:, j * LANES:(j + 1) * LANES], idx, axis=1)
                                      for j in range(g // LANES)], axis=1) for b in range(nb)]
            out = _dot(jnp.concatenate([x.astype(BF16), sr_scr[d]], axis=0), e)
            sa = out[:n]
            vcol = jnp.concatenate(vcols, axis=0)
            write_y(d, out[n:], tt - 1, tt > 0)
            for b in range(nb):
                s = s_old[b] * row(dec_ref, b) - blk(sa, b) * row(b_ref, b) + blk(vcol, b) * row(km_ref, b)
                s_scr[d, b] = s
                sr_scr[d, b * HEAD_DIM:(b + 1) * HEAD_DIM, :] = (s * row(r_ref, b)).astype(BF16)
        return carry

    lax.fori_loop(0, t_blk, one_token, 0, unroll=2)
    for d in range(2):
        write_y(d, _dot(sr_scr[d], e), t_blk - 1, True)

    @pl.when(step_id == pl.num_programs(0) - 1)
    def _():
        send_ref[...] = s_scr[...]


def wkv_scan(pa3, kk, dec_f, b_f, km_f, dec_b, b_b, km_b, vt, eseg, s0):
    nb, l, g = kk.shape
    t_blk = SCAN_BLOCK
    nsub = t_blk // SCAN_SUB
    nblk = l // t_blk
    assert l % t_blk == 0
    seq_f = pl.BlockSpec((nb, t_blk, g), lambda i: (0, i, 0))
    seq_b = pl.BlockSpec((nb, t_blk, g), lambda i: (0, nblk - 1 - i, 0))
    vt_f = pl.BlockSpec((nb, nsub, HEAD_DIM, g), lambda i: (0, i, 0, 0))
    vt_b = pl.BlockSpec((nb, nsub, HEAD_DIM, g), lambda i: (0, nblk - 1 - i, 0, 0))
    state = pl.BlockSpec(s0.shape, lambda i: (0, 0, 0, 0))
    return pl.pallas_call(
        _wkv_scan_kernel,
        grid=(nblk,),
        in_specs=[seq_f, seq_b, seq_f, seq_b, seq_f, seq_f, seq_f, seq_b, seq_b, seq_b,
                  vt_f, vt_b, pl.BlockSpec(eseg.shape, lambda i: (0, 0)), state],
        out_specs=[vt_f, vt_b, state],
        out_shape=[jax.ShapeDtypeStruct(vt.shape, F32), jax.ShapeDtypeStruct(vt.shape, F32),
                   jax.ShapeDtypeStruct(s0.shape, F32)],
        scratch_shapes=[pltpu.VMEM(s0.shape, F32), pltpu.VMEM((2, nb * HEAD_DIM, g), BF16)],
        compiler_params=_params(("arbitrary",)),
        name="wkv_scan",
    )(pa3, pa3, kk, kk, dec_f, b_f, km_f, dec_b, b_b, km_b, vt, vt, eseg, s0)


def to_head_tiles(x):
    b, l, g = x.shape
    x = x.reshape(b, l // SCAN_SUB, SCAN_SUB, g // HEAD_DIM, HEAD_DIM)
    return x.transpose(0, 1, 4, 3, 2).reshape(b, l // SCAN_SUB, HEAD_DIM, g)


def from_head_tiles(y):
    b, nblk, _, g = y.shape
    y = y.reshape(b, nblk, HEAD_DIM, g // HEAD_DIM, SCAN_SUB)
    return y.transpose(0, 1, 4, 3, 2).reshape(b, nblk * SCAN_SUB, g)


def _rwkv_readout_kernel(yf_ref, yb_ref, p_ref, kmf_ref, kmb_ref, rk_ref, g2_ref, gnw_ref, gnb_ref, e_ref, o_ref):
    g = o_ref.shape[1]
    e = e_ref[...]
    r = p_ref[:, :g]
    v = p_ref[:, 2 * g:3 * g]
    lr = p_ref[:, 3 * g:4 * g]
    y = yf_ref[...] + yb_ref[...]
    mu = _seg_sum(y, e) * (1.0 / HEAD_DIM)
    yc = y - mu
    var = _seg_sum(yc * yc, e) * (1.0 / HEAD_DIM)
    yn = yc * lax.rsqrt(var + RWKV_GN_EPS) * gnw_ref[...] + gnb_ref[...]
    bonus = _seg_sum(r * (kmf_ref[...] + kmb_ref[...]) * rk_ref[...], e) * v
    gate = _dot(jax.nn.sigmoid(lr).astype(BF16), g2_ref[...])
    o_ref[...] = (yn + bonus) * gate


def rwkv_readout(yf, yb, pa, km_f, km_b, r_k, g2p, gn_w, gn_b, eseg):
    r, g = yf.shape
    tm = _tile(r, 512)
    row = pl.BlockSpec((tm, g), lambda i: (i, 0))
    full2 = lambda shape: pl.BlockSpec(shape, lambda i: (0,) * len(shape))
    return pl.pallas_call(
        _rwkv_readout_kernel,
        grid=(r // tm,),
        in_specs=[row, row, pl.BlockSpec((tm, 4 * g), lambda i: (i, 0)), row, row,
                  full2(r_k.shape), full2(g2p.shape), full2(gn_w.shape), full2(gn_b.shape), full2(eseg.shape)],
        out_specs=row,
        out_shape=jax.ShapeDtypeStruct((r, g), F32),
        compiler_params=_params(("arbitrary",)),
        name="rwkv_readout",
    )(yf, yb, pa, km_f, km_b, r_k, g2p, gn_w, gn_b, eseg)


def _block_ones(n, seg):
    idx = np.arange(n) // seg
    return jnp.asarray(idx[:, None] == idx[None, :], dtype=BF16)


def _padded_rows(w, offset, n):
    return jnp.zeros((n, w.shape[1]), w.dtype).at[offset:offset + w.shape[0]].set(w)


def _rope_tables(n_tokens, n_rep):
    t = jnp.arange(n_tokens)
    row = (t // GRID_W).astype(F32)
    col = (t % GRID_W).astype(F32)
    n_freq = HEAD_DIM // 4
    inv_freq = ROPE_THETA ** (-jnp.arange(n_freq, dtype=F32) / n_freq)
    ang = jnp.concatenate([row[:, None] * inv_freq, col[:, None] * inv_freq], axis=-1)
    cos, sin = jnp.cos(ang), jnp.sin(ang)
    cos_h = jnp.concatenate([cos, cos], axis=-1)
    sin_h = jnp.concatenate([-sin, sin], axis=-1)
    return jnp.tile(cos_h, (1, n_rep)), jnp.tile(sin_h, (1, n_rep))


def kernel(x, c, ctx, c_ctx, ada_w, ada_b, norm1_g, norm2_g, w_in, w_out, rwkv_w0, rwkv_w2, rwkv_a0, rwkv_a2,
           rwkv_k_k, rwkv_k_a, rwkv_r_k, rwkv_g2, rwkv_gn_w, rwkv_gn_b, na_rpb, gqa_q_gain, gqa_k_gain, pool_w,
           pool_scale, mlp_w1, mlp_w2, final_g):
    nb, l, d = x.shape
    lc = ctx.shape[1]
    depth = ada_w.shape[0]
    g = d // 4
    nh = g // HEAD_DIM
    nkv = nh // 2
    kvw = nkv * HEAD_DIM
    dr = rwkv_w2.shape[2]
    ir = rwkv_a2.shape[2]
    gr = rwkv_g2.shape[1]
    assert 2 * dr + 2 * ir + gr <= g and nb + 1 <= 8
    rows = l // GRID_W

    cpad = jnp.zeros((8, d), F32).at[:nb].set(c).at[nb].set(c_ctx)
    mods = adaln(cpad, ada_w, ada_b)

    eseg = _block_ones(g, HEAD_DIM)
    ekv = _block_ones(kvw, HEAD_DIM)
    cos_q, sin_q = _rope_tables(l, nh)
    ones_c, zeros_c = jnp.ones((lc, g), F32), jnp.zeros((lc, g), F32)

    splits = np.cumsum([0, g, g, g, dr, dr, ir, ir, gr, g, g, g, g, kvw, kvw, g])
    lowrank_w = splits[8] - splits[3]
    widths = (4 * g, 3 * g, 2 * g, g)
    proj_dtypes = (F32, BF16, F32, F32)

    xl = x.reshape(nb * l, d)
    xc = ctx.reshape(nb * lc, d)
    zero_state = jnp.zeros((2, nb, HEAD_DIM, g), F32)

    for i in range(depth):
        need_ctx_out = i < depth - 1
        mod_l = mods[i, :nb].reshape(nb, N_MOD, 1, d)
        mod_c = mods[i, nb].reshape(N_MOD, 1, 1, d)
        ml = [mod_l[:, k] for k in range(N_MOD)]
        mc = [mod_c[k] for k in range(N_MOD)]

        wi = w_in[i]
        w_inp = jnp.concatenate([wi[:, :splits[8]], jnp.zeros((d, g - lowrank_w), F32),
                                 wi[:, splits[8]:splits[9]] * HEAD_DIM ** -0.5, wi[:, splits[9]:]], axis=1).astype(BF16)
        g1 = norm1_g[i].reshape(1, d)
        g2 = norm2_g[i].reshape(1, d)
        pa_l, pb_l, pc_l, pd_l = inproj(xl, g1, ml[0], ml[1], w_inp, l, widths, proj_dtypes)
        pa_c, pb_c, pc_c, pd_c = inproj(xc, g1, mc[0], mc[1], w_inp, nb * lc, widths, proj_dtypes)

        w2p = jnp.stack([_padded_rows(rwkv_w2[i, 0], 0, g), _padded_rows(rwkv_w2[i, 1], dr, g)]).astype(BF16)
        a2p = jnp.stack([_padded_rows(rwkv_a2[i, 0], 2 * dr, g),
                         _padded_rows(rwkv_a2[i, 1], 2 * dr + ir, g)]).astype(BF16)
        g2p = _padded_rows(rwkv_g2[i], 2 * dr + 2 * ir, g).astype(BF16)
        k_k = rwkv_k_k[i].reshape(1, g)
        k_a = rwkv_k_a[i].reshape(1, g)
        r_k = rwkv_r_k[i].reshape(1, g)
        gn_w = rwkv_gn_w[i].reshape(1, g)
        gn_b = rwkv_gn_b[i].reshape(1, g)

        def rwkv_terms(pa, seq):
            terms = rwkv_prep(pa, rwkv_w0[i], rwkv_a0[i], w2p, a2p, k_k, k_a, eseg)
            terms3 = [t.reshape(nb, seq, g) for t in terms]
            vt = to_head_tiles(pa[:, 2 * g:3 * g].reshape(nb, seq, g))
            return terms, terms3, vt

        terms_c, terms3_c, vt_c = rwkv_terms(pa_c, lc)
        yf_c, yb_c, state_c = wkv_scan(pa_c.reshape(nb, lc, 4 * g), *terms3_c, vt_c, eseg, zero_state)
        terms_l, terms3_l, vt_l = rwkv_terms(pa_l, l)
        yf_l, yb_l, _ = wkv_scan(pa_l.reshape(nb, l, 4 * g), *terms3_l, vt_l, eseg, state_c)
        a_l = rwkv_readout(from_head_tiles(yf_l).reshape(nb * l, g), from_head_tiles(yb_l).reshape(nb * l, g),
                           pa_l, terms_l[3], terms_l[6], r_k, g2p, gn_w, gn_b, eseg)

        bias = na_bias_table(na_rpb[i], rows)
        b_l = neighbourhood_attention(pb_l.reshape(nb, l, 3 * g), pb_c.reshape(nb, lc, 3 * g), bias).reshape(nb * l, g)

        q_gain = jnp.tile(gqa_q_gain[i], nh).reshape(1, g)
        k_gain = jnp.tile(gqa_k_gain[i], nkv).reshape(1, kvw)
        qc_l, kc_l, vc_l = qknorm_rope(pc_l, q_gain, k_gain, cos_q, sin_q, eseg, ekv, l, True)
        qc_c, kc_c, vc_c = qknorm_rope(pc_c, q_gain, k_gain, ones_c, zeros_c, eseg, ekv, lc, False)
        kc_c, vc_c = kc_c.reshape(nb, lc, kvw), vc_c.reshape(nb, lc, kvw)
        k_all = jnp.concatenate([kc_c, kc_l.reshape(nb, l, kvw)], axis=1)
        v_all = jnp.concatenate([vc_c, vc_l.reshape(nb, l, kvw)], axis=1)
        c_l = attention(qc_l, k_all, v_all, l)

        w_bd = jax.scipy.linalg.block_diag(*[pool_w[i, k] for k in range(len(POOL_WINDOWS))]).astype(BF16)
        p_scale = pool_scale[i].reshape(1, g)
        d_l = pool_mixer(pd_l.reshape(nb, l, g), w_bd, p_scale).reshape(nb * l, g)

        w_out4 = w_out[i].reshape(4, g, d).astype(BF16)
        w1 = mlp_w1[i].astype(BF16)
        w2 = mlp_w2[i].astype(BF16)
        xl = outproj((a_l, b_l, c_l, d_l), w_out4, xl, ml[2], l)
        xl = mlp(xl, g2, ml[3], ml[4], ml[5], w1, w2, l)

        if need_ctx_out:
            a_c = rwkv_readout(from_head_tiles(yf_c).reshape(nb * lc, g), from_head_tiles(yb_c).reshape(nb * lc, g),
                               pa_c, terms_c[3], terms_c[6], r_k, g2p, gn_w, gn_b, eseg)
            b_c = attention(pb_c[:, :g], pb_c[:, g:2 * g].reshape(nb, lc, g), pb_c[:, 2 * g:].reshape(nb, lc, g), lc)
            c_c = attention(qc_c, kc_c, vc_c, lc)
            d_c = pool_mixer(pd_c.reshape(nb, lc, g), w_bd, p_scale).reshape(nb * lc, g)
            xc = outproj((a_c, b_c, c_c, d_c), w_out4, xc, mc[2], nb * lc)
            xc = mlp(xc, g2, mc[3], mc[4], mc[5], w1, w2, nb * lc)

    return final_norm(xl, final_g.reshape(1, d)).reshape(nb, l, d)
```

```python
import functools

import jax
import jax.numpy as jnp
import numpy as np
from jax import lax
from jax.experimental import pallas as pl
from jax.experimental.pallas import tpu as pltpu

F32 = jnp.float32
BF16 = jnp.bfloat16

HEAD_DIM = 64
GRID_W = 64
NA_ROWS = 8
NA_COLS = 16
ROPE_THETA = 10000.0
POOL_WINDOWS = (2, 4, 8, 16)
NORM_EPS = 1e-6
RWKV_GN_EPS = 1e-5 * HEAD_DIM
N_MOD = 6
LANES = 128
SCAN_BLOCK = 128
SCAN_SUB = 64
ATTN_CHUNK = 1024
VMEM_LIMIT = 52 * 1024 * 1024


def _params(sem):
    return pltpu.CompilerParams(dimension_semantics=sem, vmem_limit_bytes=VMEM_LIMIT)


def _tile(n, pref):
    t = min(n, pref)
    assert n % t == 0, (n, pref)
    return t


def _dot(a, b):
    return jnp.dot(a, b, preferred_element_type=F32)


def _dot_nt(a, b):
    return lax.dot_general(a, b, (((1,), (1,)), ((), ())), preferred_element_type=F32)


def _seg_sum(x, e):
    hi = x.astype(BF16)
    lo = (x - hi.astype(F32)).astype(BF16)
    return _dot(hi, e) + _dot(lo, e)


def _norm_mod(x, g, shift, scale):
    ms = jnp.mean(x * x, axis=-1, keepdims=True)
    h = x * lax.rsqrt(ms + NORM_EPS) * g
    return h * (1.0 + scale) + shift


def _adaln_kernel(c_ref, w_ref, b_ref, o_ref):
    c = c_ref[...]
    s = c * jax.nn.sigmoid(c)
    o_ref[0] = jnp.dot(s, w_ref[0], preferred_element_type=F32, precision=lax.Precision.HIGHEST) + b_ref[0]


def adaln(cpad, ada_w, ada_b):
    depth, d, n = ada_w.shape
    tn = _tile(n, 1536)
    return pl.pallas_call(
        _adaln_kernel,
        grid=(depth, n // tn),
        in_specs=[pl.BlockSpec((8, d), lambda i, j: (0, 0)),
                  pl.BlockSpec((1, d, tn), lambda i, j: (i, 0, j)),
                  pl.BlockSpec((1, 1, tn), lambda i, j: (i, 0, j))],
        out_specs=pl.BlockSpec((1, 8, tn), lambda i, j: (i, 0, j)),
        out_shape=jax.ShapeDtypeStruct((depth, 8, n), F32),
        compiler_params=_params(("arbitrary", "arbitrary")),
        name="adaln",
    )(cpad, ada_w, ada_b.reshape(depth, 1, n))


def _inproj_kernel(x_ref, g_ref, sh_ref, sc_ref, w_ref, oa_ref, ob_ref, oc_ref, od_ref):
    h = _norm_mod(x_ref[...], g_ref[...], sh_ref[0], sc_ref[0]).astype(BF16)
    off = 0
    for o_ref in (oa_ref, ob_ref, oc_ref, od_ref):
        n = o_ref.shape[1]
        o_ref[...] = _dot(h, w_ref[:, off:off + n]).astype(o_ref.dtype)
        off += n


def inproj(x, g, shift, scale, w, rows_per_mod, widths, dtypes):
    r, d = x.shape
    tm = _tile(rows_per_mod, 512)
    tpb = rows_per_mod // tm
    modspec = pl.BlockSpec((1, 1, d), lambda i: (i // tpb, 0, 0))
    return pl.pallas_call(
        _inproj_kernel,
        grid=(r // tm,),
        in_specs=[pl.BlockSpec((tm, d), lambda i: (i, 0)),
                  pl.BlockSpec((1, d), lambda i: (0, 0)),
                  modspec, modspec,
                  pl.BlockSpec(w.shape, lambda i: (0, 0))],
        out_specs=[pl.BlockSpec((tm, n), lambda i: (i, 0)) for n in widths],
        out_shape=[jax.ShapeDtypeStruct((r, n), dt) for n, dt in zip(widths, dtypes)],
        compiler_params=_params(("arbitrary",)),
        name="inproj",
    )(x, g, shift, scale, w)


def _outproj_kernel(a_ref, b_ref, c_ref, d_ref, w_ref, x_ref, gate_ref, o_ref):
    acc = _dot(a_ref[...].astype(BF16), w_ref[0])
    acc += _dot(b_ref[...].astype(BF16), w_ref[1])
    acc += _dot(c_ref[...].astype(BF16), w_ref[2])
    acc += _dot(d_ref[...].astype(BF16), w_ref[3])
    o_ref[...] = x_ref[...] + gate_ref[0] * acc


def outproj(mix, w4, x, gate, rows_per_mod):
    r, d = x.shape
    gw = w4.shape[1]
    tm = _tile(rows_per_mod, 512)
    tpb = rows_per_mod // tm
    mixspec = pl.BlockSpec((tm, gw), lambda i: (i, 0))
    return pl.pallas_call(
        _outproj_kernel,
        grid=(r // tm,),
        in_specs=[mixspec, mixspec, mixspec, mixspec,
                  pl.BlockSpec(w4.shape, lambda i: (0, 0, 0)),
                  pl.BlockSpec((tm, d), lambda i: (i, 0)),
                  pl.BlockSpec((1, 1, d), lambda i: (i // tpb, 0, 0))],
        out_specs=pl.BlockSpec((tm, d), lambda i: (i, 0)),
        out_shape=jax.ShapeDtypeStruct((r, d), F32),
        compiler_params=_params(("arbitrary",)),
        name="outproj",
    )(*mix, w4, x, gate)


def _mlp_kernel(x_ref, g_ref, sh_ref, sc_ref, gate_ref, w1_ref, w2_ref, o_ref, h_scr, acc_scr):
    j = pl.program_id(1)

    @pl.when(j == 0)
    def _():
        h_scr[...] = _norm_mod(x_ref[...], g_ref[...], sh_ref[0], sc_ref[0]).astype(BF16)
        acc_scr[...] = jnp.zeros_like(acc_scr)

    u = jnp.maximum(_dot(h_scr[...], w1_ref[...]), 0.0)
    acc_scr[...] += _dot((u * u).astype(BF16), w2_ref[...])

    @pl.when(j == pl.num_programs(1) - 1)
    def _():
        o_ref[...] = x_ref[...] + gate_ref[0] * acc_scr[...]


def mlp(x, g, shift, scale, gate, w1, w2, rows_per_mod):
    r, d = x.shape
    dff = w1.shape[1]
    tm = _tile(rows_per_mod, 1024)
    tf = _tile(dff, 1024)
    tpb = rows_per_mod // tm
    modspec = pl.BlockSpec((1, 1, d), lambda i, j: (i // tpb, 0, 0))
    return pl.pallas_call(
        _mlp_kernel,
        grid=(r // tm, dff // tf),
        in_specs=[pl.BlockSpec((tm, d), lambda i, j: (i, 0)),
                  pl.BlockSpec((1, d), lambda i, j: (0, 0)),
                  modspec, modspec, modspec,
                  pl.BlockSpec((d, tf), lambda i, j: (0, j)),
                  pl.BlockSpec((tf, d), lambda i, j: (j, 0))],
        out_specs=pl.BlockSpec((tm, d), lambda i, j: (i, 0)),
        out_shape=jax.ShapeDtypeStruct((r, d), F32),
        scratch_shapes=[pltpu.VMEM((tm, d), BF16), pltpu.VMEM((tm, d), F32)],
        compiler_params=_params(("arbitrary", "arbitrary")),
        name="mlp",
    )(x, g, shift, scale, gate, w1, w2)


def _final_norm_kernel(x_ref, g_ref, o_ref):
    x = x_ref[...]
    ms = jnp.mean(x * x, axis=-1, keepdims=True)
    o_ref[...] = x * lax.rsqrt(ms + NORM_EPS) * g_ref[...]


def final_norm(x, g):
    r, d = x.shape
    tm = _tile(r, 1024)
    return pl.pallas_call(
        _final_norm_kernel,
        grid=(r // tm,),
        in_specs=[pl.BlockSpec((tm, d), lambda i: (i, 0)), pl.BlockSpec((1, d), lambda i: (0, 0))],
        out_specs=pl.BlockSpec((tm, d), lambda i: (i, 0)),
        out_shape=jax.ShapeDtypeStruct((r, d), F32),
        compiler_params=_params(("arbitrary",)),
        name="final_norm",
    )(x, g)


def _swap_halves(y):
    n = y.shape[-1]
    lane = lax.broadcasted_iota(jnp.int32, y.shape, 1)
    half = HEAD_DIM // 2
    return jnp.where(lane % HEAD_DIM < half, pltpu.roll(y, n - half, 1), pltpu.roll(y, half, 1))


def _qknorm_rope_kernel(p_ref, qg_ref, kg_ref, cos_ref, sin_ref, eq_ref, ek_ref, q_ref, k_ref, v_ref, *, rope):
    gq = q_ref.shape[1]
    gk = k_ref.shape[1]
    p = p_ref[...]

    def normed(x, gain, e):
        ms = _seg_sum(x * x, e) * (1.0 / HEAD_DIM)
        return x * lax.rsqrt(ms + NORM_EPS) * gain

    q = normed(p[:, :gq], qg_ref[...], eq_ref[...])
    k = normed(p[:, gq:gq + gk], kg_ref[...], ek_ref[...])
    if rope:
        cos = cos_ref[...]
        sin = sin_ref[...]
        q = q * cos + _swap_halves(q) * sin
        k = k * cos[:, :gk] + _swap_halves(k) * sin[:, :gk]
    q_ref[...] = (q * HEAD_DIM ** -0.5).astype(BF16)
    k_ref[...] = k.astype(BF16)
    v_ref[...] = p[:, gq + gk:].astype(BF16)


def qknorm_rope(pc, q_gain, k_gain, cos, sin, eq, ek, seq_len, rope):
    r, n = pc.shape
    gq = n // 2
    gk = n // 4
    tm = _tile(seq_len, 512)
    nseq = seq_len // tm
    return pl.pallas_call(
        functools.partial(_qknorm_rope_kernel, rope=rope),
        grid=(r // tm,),
        in_specs=[pl.BlockSpec((tm, n), lambda i: (i, 0)),
                  pl.BlockSpec((1, gq), lambda i: (0, 0)),
                  pl.BlockSpec((1, gk), lambda i: (0, 0)),
                  pl.BlockSpec((tm, gq), lambda i: (i % nseq, 0)),
                  pl.BlockSpec((tm, gq), lambda i: (i % nseq, 0)),
                  pl.BlockSpec(eq.shape, lambda i: (0, 0)),
                  pl.BlockSpec(ek.shape, lambda i: (0, 0))],
        out_specs=[pl.BlockSpec((tm, gq), lambda i: (i, 0)),
                   pl.BlockSpec((tm, gk), lambda i: (i, 0)),
                   pl.BlockSpec((tm, gk), lambda i: (i, 0))],
        out_shape=[jax.ShapeDtypeStruct((r, gq), BF16),
                   jax.ShapeDtypeStruct((r, gk), BF16),
                   jax.ShapeDtypeStruct((r, gk), BF16)],
        compiler_params=_params(("arbitrary",)),
        name="qknorm_rope",
    )(pc, q_gain, k_gain, cos, sin, eq, ek)


def _pad_heads(q, n_heads, kv_width, rep):
    assert kv_width % LANES == 0
    t = q.shape[0]
    lane = lax.broadcasted_iota(jnp.int32, (t, LANES), 1)
    lo_half = lane < HEAD_DIM
    zero = jnp.zeros((t, LANES), q.dtype)
    rows = []
    for h in range(n_heads):
        src = h * HEAD_DIM
        dst = (h // rep) * HEAD_DIM
        piece = q[:, src // LANES * LANES:(src // LANES + 1) * LANES]
        if src % LANES != dst % LANES:
            piece = pltpu.roll(piece.astype(F32), HEAD_DIM, 1).astype(q.dtype)
        piece = jnp.where(lo_half if dst % LANES == 0 else ~lo_half, piece, zero)
        blocks = [piece if j == dst // LANES else zero for j in range(kv_width // LANES)]
        rows.append(jnp.concatenate(blocks, axis=1) if len(blocks) > 1 else piece)
    return jnp.concatenate(rows, axis=0)


def _gather_heads(res, n_heads, rep):
    t = res.shape[0] // n_heads
    lane = lax.broadcasted_iota(jnp.int32, (t, LANES), 1)
    lo_half = lane < HEAD_DIM
    pieces = []
    for h in range(n_heads):
        src = (h // rep) * HEAD_DIM
        dst = h * HEAD_DIM
        piece = res[h * t:(h + 1) * t, src // LANES * LANES:(src // LANES + 1) * LANES]
        if src % LANES != dst % LANES:
            piece = pltpu.roll(piece, HEAD_DIM, 1)
        pieces.append(piece)
    blocks = [jnp.where(lo_half, pieces[2 * j], pieces[2 * j + 1]) for j in range(n_heads // 2)]
    return jnp.concatenate(blocks, axis=1)


def _attn_kernel(q_ref, k_ref, v_ref, o_ref, *, rep):
    n_heads = q_ref.shape[1] // HEAD_DIM
    lk = k_ref.shape[1]
    q = _pad_heads(q_ref[...], n_heads, k_ref.shape[2], rep)
    edges = [0] + list(range(lk % ATTN_CHUNK, lk + 1, ATTN_CHUNK))
    if edges[1] == 0:
        edges = edges[1:]
    m = l = acc = None
    for k0, k1 in zip(edges[:-1], edges[1:]):
        s = _dot_nt(q, k_ref[0, k0:k1, :])
        m_new = jnp.max(s, axis=-1, keepdims=True)
        if m is not None:
            m_new = jnp.maximum(m, m_new)
        p = jnp.exp(s - m_new)
        l_new = jnp.sum(p, axis=-1, keepdims=True)
        acc_new = _dot(p.astype(BF16), v_ref[0, k0:k1, :])
        if m is not None:
            alpha = jnp.exp(m - m_new)
            l_new = alpha * l + l_new
            acc_new = alpha * acc + acc_new
        m, l, acc = m_new, l_new, acc_new
    o_ref[...] = _gather_heads(acc / l, n_heads, rep)


def attention(q, k, v, seq_len):
    r, gq = q.shape
    nb, lk, kw = k.shape
    rep = gq // kw
    tq = _tile(seq_len, 128)
    nq = seq_len // tq
    return pl.pallas_call(
        functools.partial(_attn_kernel, rep=rep),
        grid=(nb, nq),
        in_specs=[pl.BlockSpec((tq, gq), lambda i, j: (i * nq + j, 0)),
                  pl.BlockSpec((1, lk, kw), lambda i, j: (i, 0, 0)),
                  pl.BlockSpec((1, lk, kw), lambda i, j: (i, 0, 0))],
        out_specs=pl.BlockSpec((tq, gq), lambda i, j: (i * nq + j, 0)),
        out_shape=jax.ShapeDtypeStruct((r, gq), F32),
        compiler_params=_params(("arbitrary", "arbitrary")),
        name="attention",
    )(q, k, v)


def _na_kernel(p_ref, pc_ref, bias_ref, o_ref, *, rows, kh):
    g = o_ref.shape[2]
    n_heads = g // HEAD_DIM
    band = kh * GRID_W
    kc = pc_ref[0, :, g:2 * g]
    vc = pc_ref[0, :, 2 * g:3 * g]

    def one_row(r, carry):
        rs = jnp.clip(r - kh // 2, 0, rows - kh)
        q0 = pl.multiple_of(r * GRID_W, GRID_W)
        k0 = pl.multiple_of(rs * GRID_W, GRID_W)
        q = _pad_heads(p_ref[0, pl.ds(q0, GRID_W), 0:g], n_heads, g, 1)
        s1 = _dot_nt(q, p_ref[0, pl.ds(k0, band), g:2 * g]) + bias_ref[r - rs]
        s2 = _dot_nt(q, kc)
        m = jnp.maximum(jnp.max(s1, axis=-1, keepdims=True), jnp.max(s2, axis=-1, keepdims=True))
        p1 = jnp.exp(s1 - m)
        p2 = jnp.exp(s2 - m)
        l = jnp.sum(p1, axis=-1, keepdims=True) + jnp.sum(p2, axis=-1, keepdims=True)
        res = (_dot(p1.astype(BF16), p_ref[0, pl.ds(k0, band), 2 * g:3 * g]) + _dot(p2.astype(BF16), vc)) / l
        o_ref[0, pl.ds(q0, GRID_W), :] = _gather_heads(res, n_heads, 1)
        return carry

    lax.fori_loop(0, rows, one_row, 0, unroll=4)


def neighbourhood_attention(p, pc, bias):
    nb, l, g3 = p.shape
    g = g3 // 3
    lc = pc.shape[1]
    rows = l // GRID_W
    kh = min(NA_ROWS, rows)
    return pl.pallas_call(
        functools.partial(_na_kernel, rows=rows, kh=kh),
        grid=(nb,),
        in_specs=[pl.BlockSpec((1, l, g3), lambda i: (i, 0, 0)),
                  pl.BlockSpec((1, lc, g3), lambda i: (i, 0, 0)),
                  pl.BlockSpec(bias.shape, lambda i: (0, 0, 0))],
        out_specs=pl.BlockSpec((1, l, g), lambda i: (i, 0, 0)),
        out_shape=jax.ShapeDtypeStruct((nb, l, g), F32),
        compiler_params=_params(("arbitrary",)),
        name="neighbourhood_attention",
    )(p, pc, bias)


def _na_bias_kernel(rpb_ref, o_ref, *, kh):
    h = pl.program_id(0)
    q = lax.broadcasted_iota(jnp.int32, (GRID_W, GRID_W), 0)
    k = lax.broadcasted_iota(jnp.int32, (GRID_W, GRID_W), 1)
    start = jnp.clip(q - NA_COLS // 2, 0, GRID_W - NA_COLS)
    in_win = (k >= start) & (k < start + NA_COLS)
    off = k - q + NA_COLS - 1
    neg = jnp.full((GRID_W, GRID_W), -jnp.inf, F32)
    blocks = []
    for ro in range(2 * NA_ROWS - 1):
        t = neg
        for c in range(2 * NA_COLS - 1):
            t = jnp.where(off == c, rpb_ref[h, ro, c], t)
        blocks.append(jnp.where(in_win, t, neg))
    for di in range(kh):
        for i in range(kh):
            o_ref[di, :, i * GRID_W:(i + 1) * GRID_W] = blocks[i - di + NA_ROWS - 1]


def na_bias_table(rpb, rows):
    kh = min(NA_ROWS, rows)
    nh = rpb.shape[0]
    return pl.pallas_call(
        functools.partial(_na_bias_kernel, kh=kh),
        grid=(nh,),
        in_specs=[pl.BlockSpec(memory_space=pltpu.SMEM)],
        out_specs=pl.BlockSpec((kh, GRID_W, kh * GRID_W), lambda i: (0, i, 0)),
        out_shape=jax.ShapeDtypeStruct((kh, nh * GRID_W, kh * GRID_W), F32),
        compiler_params=_params(("arbitrary",)),
        name="na_bias",
    )(rpb)


def _shift_rows(x, d, t):
    n = x.shape[0]
    y = pltpu.roll(x, d % n, 0)
    src = t - d
    return jnp.where((src >= 0) & (src < n), y, 0.0)


def _pool_kernel(x_ref, w_ref, scale_ref, o_ref):
    x = x_ref[0]
    n, g = x.shape
    pg = g // len(POOL_WINDOWS)
    t = lax.broadcasted_iota(jnp.int32, x.shape, 0)
    group = lax.broadcasted_iota(jnp.int32, x.shape, 1) // pg
    fwd = x
    bwd = x
    cur = 1
    total = jnp.zeros_like(x)
    count = jnp.ones_like(x)
    for j, w in enumerate(POOL_WINDOWS):
        half = w // 2
        while cur < half:
            fwd = fwd + _shift_rows(fwd, -cur, t)
            bwd = bwd + _shift_rows(bwd, cur, t)
            cur *= 2
        win = _shift_rows(bwd, 1, t) + fwd
        lo = jnp.clip(t - half, 0, n)
        hi = jnp.clip(t - half + w, 0, n)
        total = jnp.where(group == j, win, total)
        count = jnp.where(group == j, (hi - lo).astype(F32), count)
    diff = total / count - x
    o_ref[0] = _dot(diff.astype(BF16), w_ref[...]) * scale_ref[...]


def pool_mixer(p, w_bd, scale):
    b, n, g = p.shape
    return pl.pallas_call(
        _pool_kernel,
        grid=(b,),
        in_specs=[pl.BlockSpec((1, n, g), lambda i: (i, 0, 0)),
                  pl.BlockSpec((g, g), lambda i: (0, 0)),
                  pl.BlockSpec((1, g), lambda i: (0, 0))],
        out_specs=pl.BlockSpec((1, n, g), lambda i: (i, 0, 0)),
        out_shape=jax.ShapeDtypeStruct((b, n, g), F32),
        compiler_params=_params(("arbitrary",)),
        name="pool_mixer",
    )(p, w_bd, scale)


def _rwkv_prep_kernel(p_ref, w0_ref, a0_ref, w2_ref, a2_ref, kk_w_ref, ka_ref, e_ref,
                      kk_ref, decf_ref, bf_ref, kmf_ref, decb_ref, bb_ref, kmb_ref):
    g = kk_ref.shape[1]
    k = p_ref[:, g:2 * g]
    lr = p_ref[:, 3 * g:4 * g]
    kx = k * kk_w_ref[...]
    norm = jnp.sqrt(_seg_sum(kx * kx, e_ref[...]))
    kk = kx / jnp.maximum(norm, 1e-12)
    kk_ref[...] = kk
    lr_t = jnp.tanh(lr).astype(BF16)
    lr_b = lr.astype(BF16)
    for d, (dec_ref, b_ref, km_ref) in enumerate(((decf_ref, bf_ref, kmf_ref), (decb_ref, bb_ref, kmb_ref))):
        z = w0_ref[d:d + 1, :] + _dot(lr_t, w2_ref[d])
        softplus_neg = jnp.maximum(-z, 0.0) + jnp.log1p(jnp.exp(-jnp.abs(z)))
        w = -softplus_neg - 0.5
        dec_ref[...] = jnp.exp(-jnp.exp(w))
        a = jax.nn.sigmoid(a0_ref[d:d + 1, :] + _dot(lr_b, a2_ref[d]))
        b_ref[...] = kk * a
        km_ref[...] = k * (1.0 + (a - 1.0) * ka_ref[...])


def rwkv_prep(pa, w0, a0, w2p, a2p, k_k, k_a, eseg):
    r, n = pa.shape
    g = n // 4
    tm = _tile(r, 512)
    full2 = lambda shape: pl.BlockSpec(shape, lambda i: (0,) * len(shape))
    out = pl.BlockSpec((tm, g), lambda i: (i, 0))
    return pl.pallas_call(
        _rwkv_prep_kernel,
        grid=(r // tm,),
        in_specs=[pl.BlockSpec((tm, n), lambda i: (i, 0)),
                  full2(w0.shape), full2(a0.shape), full2(w2p.shape), full2(a2p.shape),
                  full2(k_k.shape), full2(k_a.shape), full2(eseg.shape)],
        out_specs=[out] * 7,
        out_shape=[jax.ShapeDtypeStruct((r, g), F32)] * 7,
        compiler_params=_params(("arbitrary",)),
        name="rwkv_prep",
    )(pa, w0, a0, w2p, a2p, k_k, k_a, eseg)


def _wkv_scan_kernel(rf_ref, rb_ref, kkf_ref, kkb_ref, decf_ref, bf_ref, kmf_ref, decb_ref, bb_ref, kmb_ref,
                     vtf_ref, vtb_ref, e_ref, s0_ref, yf_ref, yb_ref, send_ref, s_scr, sr_scr):
    nb, t_blk, g = rf_ref.shape
    step_id = pl.program_id(0)

    @pl.when(step_id == 0)
    def _():
        s_scr[...] = s0_ref[...]

    yf_ref[...] = jnp.zeros_like(yf_ref)
    yb_ref[...] = jnp.zeros_like(yb_ref)
    sr_scr[...] = jnp.zeros_like(sr_scr)
    e = e_ref[...]
    lane = lax.broadcasted_iota(jnp.int32, (HEAD_DIM, g), 1) % SCAN_SUB
    lane128 = lax.broadcasted_iota(jnp.int32, (HEAD_DIM, LANES), 1)
    n = nb * HEAD_DIM
    dirs = ((rf_ref, kkf_ref, decf_ref, bf_ref, kmf_ref, vtf_ref, yf_ref),
            (rb_ref, kkb_ref, decb_ref, bb_ref, kmb_ref, vtb_ref, yb_ref))

    def write_y(d, y, tt_done, valid):
        y_ref = dirs[d][-1]
        tl = tt_done if d == 0 else t_blk - 1 - tt_done
        tl = jnp.clip(tl, 0, t_blk - 1)
        hit = (lane == tl % SCAN_SUB) & valid
        for b in range(nb):
            pltpu.store(y_ref.at[b, tl // SCAN_SUB], y[b * HEAD_DIM:(b + 1) * HEAD_DIM], mask=hit)

    def one_token(tt, carry):
        for d, (r_ref, kk_ref, dec_ref, b_ref, km_ref, vt_ref, y_ref) in enumerate(dirs):
            tl = tt if d == 0 else t_blk - 1 - tt
            row = lambda ref, b: jnp.broadcast_to(ref[b, pl.ds(tl, 1), :], (HEAD_DIM, g))
            blk = lambda x, b: x[b * HEAD_DIM:(b + 1) * HEAD_DIM]
            s_old = [s_scr[d, b] for b in range(nb)]
            x = jnp.concatenate([s_old[b] * row(kk_ref, b) for b in range(nb)], axis=0)
            idx = (lane128 // SCAN_SUB) * SCAN_SUB + tl % SCAN_SUB
            vcols = [jnp.concatenate([jnp.take_along_axis(vt_ref[b, tl // SCAN_SUB][:, j * LANES:(j + 1) * LANES], idx, axis=1)
                                      for j in range(g // LANES)], axis=1) for b in range(nb)]
            out = _dot(jnp.concatenate([x.astype(BF16), sr_scr[d]], axis=0), e)
            sa = out[:n]
            vcol = jnp.concatenate(vcols, axis=0)
            write_y(d, out[n:], tt - 1, tt > 0)
            for b in range(nb):
                s = s_old[b] * row(dec_ref, b) - blk(sa, b) * row(b_ref, b) + blk(vcol, b) * row(km_ref, b)
                s_scr[d, b] = s
                sr_scr[d, b * HEAD_DIM:(b + 1) * HEAD_DIM, :] = (s * row(r_ref, b)).astype(BF16)
        return carry

    lax.fori_loop(0, t_blk, one_token, 0, unroll=2)
    for d in range(2):
        write_y(d, _dot(sr_scr[d], e), t_blk - 1, True)

    @pl.when(step_id == pl.num_programs(0) - 1)
    def _():
        send_ref[...] = s_scr[...]


def wkv_scan(pa3, kk, dec_f, b_f, km_f, dec_b, b_b, km_b, vt, eseg, s0):
    nb, l, g = kk.shape
    t_blk = SCAN_BLOCK
    nsub = t_blk // SCAN_SUB
    nblk = l // t_blk
    assert l % t_blk == 0
    seq_f = pl.BlockSpec((nb, t_blk, g), lambda i: (0, i, 0))
    seq_b = pl.BlockSpec((nb, t_blk, g), lambda i: (0, nblk - 1 - i, 0))
    vt_f = pl.BlockSpec((nb, nsub, HEAD_DIM, g), lambda i: (0, i, 0, 0))
    vt_b = pl.BlockSpec((nb, nsub, HEAD_DIM, g), lambda i: (0, nblk - 1 - i, 0, 0))
    state = pl.BlockSpec(s0.shape, lambda i: (0, 0, 0, 0))
    return pl.pallas_call(
        _wkv_scan_kernel,
        grid=(nblk,),
        in_specs=[seq_f, seq_b, seq_f, seq_b, seq_f, seq_f, seq_f, seq_b, seq_b, seq_b,
                  vt_f, vt_b, pl.BlockSpec(eseg.shape, lambda i: (0, 0)), state],
        out_specs=[vt_f, vt_b, state],
        out_shape=[jax.ShapeDtypeStruct(vt.shape, F32), jax.ShapeDtypeStruct(vt.shape, F32),
                   jax.ShapeDtypeStruct(s0.shape, F32)],
        scratch_shapes=[pltpu.VMEM(s0.shape, F32), pltpu.VMEM((2, nb * HEAD_DIM, g), BF16)],
        compiler_params=_params(("arbitrary",)),
        name="wkv_scan",
    )(pa3, pa3, kk, kk, dec_f, b_f, km_f, dec_b, b_b, km_b, vt, vt, eseg, s0)


def to_head_tiles(x):
    b, l, g = x.shape
    x = x.reshape(b, l // SCAN_SUB, SCAN_SUB, g // HEAD_DIM, HEAD_DIM)
    return x.transpose(0, 1, 4, 3, 2).reshape(b, l // SCAN_SUB, HEAD_DIM, g)


def from_head_tiles(y):
    b, nblk, _, g = y.shape
    y = y.reshape(b, nblk, HEAD_DIM, g // HEAD_DIM, SCAN_SUB)
    return y.transpose(0, 1, 4, 3, 2).reshape(b, nblk * SCAN_SUB, g)


def _rwkv_readout_kernel(yf_ref, yb_ref, p_ref, kmf_ref, kmb_ref, rk_ref, g2_ref, gnw_ref, gnb_ref, e_ref, o_ref):
    g = o_ref.shape[1]
    e = e_ref[...]
    r = p_ref[:, :g]
    v = p_ref[:, 2 * g:3 * g]
    lr = p_ref[:, 3 * g:4 * g]
    y = yf_ref[...] + yb_ref[...]
    mu = _seg_sum(y, e) * (1.0 / HEAD_DIM)
    yc = y - mu
    var = _seg_sum(yc * yc, e) * (1.0 / HEAD_DIM)
    yn = yc * lax.rsqrt(var + RWKV_GN_EPS) * gnw_ref[...] + gnb_ref[...]
    bonus = _seg_sum(r * (kmf_ref[...] + kmb_ref[...]) * rk_ref[...], e) * v
    gate = _dot(jax.nn.sigmoid(lr).astype(BF16), g2_ref[...])
    o_ref[...] = (yn + bonus) * gate


def rwkv_readout(yf, yb, pa, km_f, km_b, r_k, g2p, gn_w, gn_b, eseg):
    r, g = yf.shape
    tm = _tile(r, 512)
    row = pl.BlockSpec((tm, g), lambda i: (i, 0))
    full2 = lambda shape: pl.BlockSpec(shape, lambda i: (0,) * len(shape))
    return pl.pallas_call(
        _rwkv_readout_kernel,
        grid=(r // tm,),
        in_specs=[row, row, pl.BlockSpec((tm, 4 * g), lambda i: (i, 0)), row, row,
                  full2(r_k.shape), full2(g2p.shape), full2(gn_w.shape), full2(gn_b.shape), full2(eseg.shape)],
        out_specs=row,
        out_shape=jax.ShapeDtypeStruct((r, g), F32),
        compiler_params=_params(("arbitrary",)),
        name="rwkv_readout",
    )(yf, yb, pa, km_f, km_b, r_k, g2p, gn_w, gn_b, eseg)


def _block_ones(n, seg):
    idx = np.arange(n) // seg
    return jnp.asarray(idx[:, None] == idx[None, :], dtype=BF16)


def _padded_rows(w, offset, n):
    return jnp.zeros((n, w.shape[1]), w.dtype).at[offset:offset + w.shape[0]].set(w)


def _rope_tables(n_tokens, n_rep):
    t = jnp.arange(n_tokens)
    row = (t // GRID_W).astype(F32)
    col = (t % GRID_W).astype(F32)
    n_freq = HEAD_DIM // 4
    inv_freq = ROPE_THETA ** (-jnp.arange(n_freq, dtype=F32) / n_freq)
    ang = jnp.concatenate([row[:, None] * inv_freq, col[:, None] * inv_freq], axis=-1)
    cos, sin = jnp.cos(ang), jnp.sin(ang)
    cos_h = jnp.concatenate([cos, cos], axis=-1)
    sin_h = jnp.concatenate([-sin, sin], axis=-1)
    return jnp.tile(cos_h, (1, n_rep)), jnp.tile(sin_h, (1, n_rep))


def kernel(x, c, ctx, c_ctx, ada_w, ada_b, norm1_g, norm2_g, w_in, w_out, rwkv_w0, rwkv_w2, rwkv_a0, rwkv_a2,
           rwkv_k_k, rwkv_k_a, rwkv_r_k, rwkv_g2, rwkv_gn_w, rwkv_gn_b, na_rpb, gqa_q_gain, gqa_k_gain, pool_w,
           pool_scale, mlp_w1, mlp_w2, final_g):
    nb, l, d = x.shape
    lc = ctx.shape[1]
    depth = ada_w.shape[0]
    g = d // 4
    nh = g // HEAD_DIM
    nkv = nh // 2
    kvw = nkv * HEAD_DIM
    dr = rwkv_w2.shape[2]
    ir = rwkv_a2.shape[2]
    gr = rwkv_g2.shape[1]
    assert 2 * dr + 2 * ir + gr <= g and nb + 1 <= 8
    rows = l // GRID_W

    cpad = jnp.zeros((8, d), F32).at[:nb].set(c).at[nb].set(c_ctx)
    mods = adaln(cpad, ada_w, ada_b)

    eseg = _block_ones(g, HEAD_DIM)
    ekv = _block_ones(kvw, HEAD_DIM)
    cos_q, sin_q = _rope_tables(l, nh)
    ones_c, zeros_c = jnp.ones((lc, g), F32), jnp.zeros((lc, g), F32)

    splits = np.cumsum([0, g, g, g, dr, dr, ir, ir, gr, g, g, g, g, kvw, kvw, g])
    lowrank_w = splits[8] - splits[3]
    widths = (4 * g, 3 * g, 2 * g, g)
    proj_dtypes = (F32, BF16, F32, F32)

    xl = x.reshape(nb * l, d)
    xc = ctx.reshape(nb * lc, d)
    zero_state = jnp.zeros((2, nb, HEAD_DIM, g), F32)

    for i in range(depth):
        need_ctx_out = i < depth - 1
        mod_l = mods[i, :nb].reshape(nb, N_MOD, 1, d)
        mod_c = mods[i, nb].reshape(N_MOD, 1, 1, d)
        ml = [mod_l[:, k] for k in range(N_MOD)]
        mc = [mod_c[k] for k in range(N_MOD)]

        wi = w_in[i]
        w_inp = jnp.concatenate([wi[:, :splits[8]], jnp.zeros((d, g - lowrank_w), F32),
                                 wi[:, splits[8]:splits[9]] * HEAD_DIM ** -0.5, wi[:, splits[9]:]], axis=1).astype(BF16)
        g1 = norm1_g[i].reshape(1, d)
        g2 = norm2_g[i].reshape(1, d)
        pa_l, pb_l, pc_l, pd_l = inproj(xl, g1, ml[0], ml[1], w_inp, l, widths, proj_dtypes)
        pa_c, pb_c, pc_c, pd_c = inproj(xc, g1, mc[0], mc[1], w_inp, nb * lc, widths, proj_dtypes)

        w2p = jnp.stack([_padded_rows(rwkv_w2[i, 0], 0, g), _padded_rows(rwkv_w2[i, 1], dr, g)]).astype(BF16)
        a2p = jnp.stack([_padded_rows(rwkv_a2[i, 0], 2 * dr, g),
                         _padded_rows(rwkv_a2[i, 1], 2 * dr + ir, g)]).astype(BF16)
        g2p = _padded_rows(rwkv_g2[i], 2 * dr + 2 * ir, g).astype(BF16)
        k_k = rwkv_k_k[i].reshape(1, g)
        k_a = rwkv_k_a[i].reshape(1, g)
        r_k = rwkv_r_k[i].reshape(1, g)
        gn_w = rwkv_gn_w[i].reshape(1, g)
        gn_b = rwkv_gn_b[i].reshape(1, g)

        def rwkv_terms(pa, seq):
            terms = rwkv_prep(pa, rwkv_w0[i], rwkv_a0[i], w2p, a2p, k_k, k_a, eseg)
            terms3 = [t.reshape(nb, seq, g) for t in terms]
            vt = to_head_tiles(pa[:, 2 * g:3 * g].reshape(nb, seq, g))
            return terms, terms3, vt

        terms_c, terms3_c, vt_c = rwkv_terms(pa_c, lc)
        yf_c, yb_c, state_c = wkv_scan(pa_c.reshape(nb, lc, 4 * g), *terms3_c, vt_c, eseg, zero_state)
        terms_l, terms3_l, vt_l = rwkv_terms(pa_l, l)
        yf_l, yb_l, _ = wkv_scan(pa_l.reshape(nb, l, 4 * g), *terms3_l, vt_l, eseg, state_c)
        a_l = rwkv_readout(from_head_tiles(yf_l).reshape(nb * l, g), from_head_tiles(yb_l).reshape(nb * l, g),
                           pa_l, terms_l[3], terms_l[6], r_k, g2p, gn_w, gn_b, eseg)

        bias = na_bias_table(na_rpb[i], rows)
        b_l = neighbourhood_attention(pb_l.reshape(nb, l, 3 * g), pb_c.reshape(nb, lc, 3 * g), bias).reshape(nb * l, g)

        q_gain = jnp.tile(gqa_q_gain[i], nh).reshape(1, g)
        k_gain = jnp.tile(gqa_k_gain[i], nkv).reshape(1, kvw)
        qc_l, kc_l, vc_l = qknorm_rope(pc_l, q_gain, k_gain, cos_q, sin_q, eseg, ekv, l, True)
        qc_c, kc_c, vc_c = qknorm_rope(pc_c, q_gain, k_gain, ones_c, zeros_c, eseg, ekv, lc, False)
        kc_c, vc_c = kc_c.reshape(nb, lc, kvw), vc_c.reshape(nb, lc, kvw)
        k_all = jnp.concatenate([kc_c, kc_l.reshape(nb, l, kvw)], axis=1)
        v_all = jnp.concatenate([vc_c, vc_l.reshape(nb, l, kvw)], axis=1)
        c_l = attention(qc_l, k_all, v_all, l)

        w_bd = jax.scipy.linalg.block_diag(*[pool_w[i, k] for k in range(len(POOL_WINDOWS))]).astype(BF16)
        p_scale = pool_scale[i].reshape(1, g)
        d_l = pool_mixer(pd_l.reshape(nb, l, g), w_bd, p_scale).reshape(nb * l, g)

        w_out4 = w_out[i].reshape(4, g, d).astype(BF16)
        w1 = mlp_w1[i].astype(BF16)
        w2 = mlp_w2[i].astype(BF16)
        xl = outproj((a_l, b_l, c_l, d_l), w_out4, xl, ml[2], l)
        xl = mlp(xl, g2, ml[3], ml[4], ml[5], w1, w2, l)

        if need_ctx_out:
            a_c = rwkv_readout(from_head_tiles(yf_c).reshape(nb * lc, g), from_head_tiles(yb_c).reshape(nb * lc, g),
                               pa_c, terms_c[3], terms_c[6], r_k, g2p, gn_w, gn_b, eseg)
            b_c = attention(pb_c[:, :g], pb_c[:, g:2 * g].reshape(nb, lc, g), pb_c[:, 2 * g:].reshape(nb, lc, g), lc)
            c_c = attention(qc_c, kc_c, vc_c, lc)
            d_c = pool_mixer(pd_c.reshape(nb, lc, g), w_bd, p_scale).reshape(nb * lc, g)
            xc = outproj((a_c, b_c, c_c, d_c), w_out4, xc, mc[2], nb * lc)
            xc = mlp(xc, g2, mc[3], mc[4], mc[5], w1, w2, nb * lc)

    return final_norm(xl, final_g.reshape(1, d)).reshape(nb, l, d)
```

```python
import functools

import jax
import jax.numpy as jnp
import numpy as np
from jax import lax
from jax.experimental import pallas as pl
from jax.experimental.pallas import tpu as pltpu

F32 = jnp.float32
BF16 = jnp.bfloat16

HEAD_DIM = 64
GRID_W = 64
NA_ROWS = 8
NA_COLS = 16
ROPE_THETA = 10000.0
POOL_WINDOWS = (2, 4, 8, 16)
NORM_EPS = 1e-6
RWKV_GN_EPS = 1e-5 * HEAD_DIM
N_MOD = 6
LANES = 128
SCAN_BLOCK = 128
SCAN_SUB = 64
ATTN_CHUNK = 1024
VMEM_LIMIT = 52 * 1024 * 1024


def _params(sem):
    return pltpu.CompilerParams(dimension_semantics=sem, vmem_limit_bytes=VMEM_LIMIT)


def _tile(n, pref):
    t = min(n, pref)
    assert n % t == 0, (n, pref)
    return t


def _dot(a, b):
    return jnp.dot(a, b, preferred_element_type=F32)


def _dot_nt(a, b):
    return lax.dot_general(a, b, (((1,), (1,)), ((), ())), preferred_element_type=F32)


def _seg_sum(x, e):
    hi = x.astype(BF16)
    lo = (x - hi.astype(F32)).astype(BF16)
    return _dot(hi, e) + _dot(lo, e)


def _norm_mod(x, g, shift, scale):
    ms = jnp.mean(x * x, axis=-1, keepdims=True)
    h = x * lax.rsqrt(ms + NORM_EPS) * g
    return h * (1.0 + scale) + shift


def _adaln_kernel(c_ref, w_ref, b_ref, o_ref):
    c = c_ref[...]
    s = c * jax.nn.sigmoid(c)
    o_ref[0] = jnp.dot(s, w_ref[0], preferred_element_type=F32, precision=lax.Precision.HIGHEST) + b_ref[0]


def adaln(cpad, ada_w, ada_b):
    depth, d, n = ada_w.shape
    tn = _tile(n, 1536)
    return pl.pallas_call(
        _adaln_kernel,
        grid=(depth, n // tn),
        in_specs=[pl.BlockSpec((8, d), lambda i, j: (0, 0)),
                  pl.BlockSpec((1, d, tn), lambda i, j: (i, 0, j)),
                  pl.BlockSpec((1, 1, tn), lambda i, j: (i, 0, j))],
        out_specs=pl.BlockSpec((1, 8, tn), lambda i, j: (i, 0, j)),
        out_shape=jax.ShapeDtypeStruct((depth, 8, n), F32),
        compiler_params=_params(("arbitrary", "arbitrary")),
        name="adaln",
    )(cpad, ada_w, ada_b.reshape(depth, 1, n))


def _inproj_kernel(x_ref, g_ref, sh_ref, sc_ref, w_ref, oa_ref, ob_ref, oc_ref, od_ref):
    h = _norm_mod(x_ref[...], g_ref[...], sh_ref[0], sc_ref[0]).astype(BF16)
    off = 0
    for o_ref in (oa_ref, ob_ref, oc_ref, od_ref):
        n = o_ref.shape[1]
        o_ref[...] = _dot(h, w_ref[:, off:off + n]).astype(o_ref.dtype)
        off += n


def inproj(x, g, shift, scale, w, rows_per_mod, widths, dtypes):
    r, d = x.shape
    tm = _tile(rows_per_mod, 512)
    tpb = rows_per_mod // tm
    modspec = pl.BlockSpec((1, 1, d), lambda i: (i // tpb, 0, 0))
    return pl.pallas_call(
        _inproj_kernel,
        grid=(r // tm,),
        in_specs=[pl.BlockSpec((tm, d), lambda i: (i, 0)),
                  pl.BlockSpec((1, d), lambda i: (0, 0)),
                  modspec, modspec,
                  pl.BlockSpec(w.shape, lambda i: (0, 0))],
        out_specs=[pl.BlockSpec((tm, n), lambda i: (i, 0)) for n in widths],
        out_shape=[jax.ShapeDtypeStruct((r, n), dt) for n, dt in zip(widths, dtypes)],
        compiler_params=_params(("arbitrary",)),
        name="inproj",
    )(x, g, shift, scale, w)


def _outproj_kernel(a_ref, b_ref, c_ref, d_ref, w_ref, x_ref, gate_ref, o_ref):
    acc = _dot(a_ref[...].astype(BF16), w_ref[0])
    acc += _dot(b_ref[...].astype(BF16), w_ref[1])
    acc += _dot(c_ref[...].astype(BF16), w_ref[2])
    acc += _dot(d_ref[...].astype(BF16), w_ref[3])
    o_ref[...] = x_ref[...] + gate_ref[0] * acc


def outproj(mix, w4, x, gate, rows_per_mod):
    r, d = x.shape
    gw = w4.shape[1]
    tm = _tile(rows_per_mod, 512)
    tpb = rows_per_mod // tm
    mixspec = pl.BlockSpec((tm, gw), lambda i: (i, 0))
    return pl.pallas_call(
        _outproj_kernel,
        grid=(r // tm,),
        in_specs=[mixspec, mixspec, mixspec, mixspec,
                  pl.BlockSpec(w4.shape, lambda i: (0, 0, 0)),
                  pl.BlockSpec((tm, d), lambda i: (i, 0)),
                  pl.BlockSpec((1, 1, d), lambda i: (i // tpb, 0, 0))],
        out_specs=pl.BlockSpec((tm, d), lambda i: (i, 0)),
        out_shape=jax.ShapeDtypeStruct((r, d), F32),
        compiler_params=_params(("arbitrary",)),
        name="outproj",
    )(*mix, w4, x, gate)


def _mlp_kernel(x_ref, g_ref, sh_ref, sc_ref, gate_ref, w1_ref, w2_ref, o_ref, h_scr, acc_scr):
    j = pl.program_id(1)

    @pl.when(j == 0)
    def _():
        h_scr[...] = _norm_mod(x_ref[...], g_ref[...], sh_ref[0], sc_ref[0]).astype(BF16)
        acc_scr[...] = jnp.zeros_like(acc_scr)

    u = jnp.maximum(_dot(h_scr[...], w1_ref[...]), 0.0)
    acc_scr[...] += _dot((u * u).astype(BF16), w2_ref[...])

    @pl.when(j == pl.num_programs(1) - 1)
    def _():
        o_ref[...] = x_ref[...] + gate_ref[0] * acc_scr[...]


def mlp(x, g, shift, scale, gate, w1, w2, rows_per_mod):
    r, d = x.shape
    dff = w1.shape[1]
    tm = _tile(rows_per_mod, 1024)
    tf = _tile(dff, 1024)
    tpb = rows_per_mod // tm
    modspec = pl.BlockSpec((1, 1, d), lambda i, j: (i // tpb, 0, 0))
    return pl.pallas_call(
        _mlp_kernel,
        grid=(r // tm, dff // tf),
        in_specs=[pl.BlockSpec((tm, d), lambda i, j: (i, 0)),
                  pl.BlockSpec((1, d), lambda i, j: (0, 0)),
                  modspec, modspec, modspec,
                  pl.BlockSpec((d, tf), lambda i, j: (0, j)),
                  pl.BlockSpec((tf, d), lambda i, j: (j, 0))],
        out_specs=pl.BlockSpec((tm, d), lambda i, j: (i, 0)),
        out_shape=jax.ShapeDtypeStruct((r, d), F32),
        scratch_shapes=[pltpu.VMEM((tm, d), BF16), pltpu.VMEM((tm, d), F32)],
        compiler_params=_params(("arbitrary", "arbitrary")),
        name="mlp",
    )(x, g, shift, scale, gate, w1, w2)


def _final_norm_kernel(x_ref, g_ref, o_ref):
    x = x_ref[...]
    ms = jnp.mean(x * x, axis=-1, keepdims=True)
    o_ref[...] = x * lax.rsqrt(ms + NORM_EPS) * g_ref[...]


def final_norm(x, g):
    r, d = x.shape
    tm = _tile(r, 1024)
    return pl.pallas_call(
        _final_norm_kernel,
        grid=(r // tm,),
        in_specs=[pl.BlockSpec((tm, d), lambda i: (i, 0)), pl.BlockSpec((1, d), lambda i: (0, 0))],
        out_specs=pl.BlockSpec((tm, d), lambda i: (i, 0)),
        out_shape=jax.ShapeDtypeStruct((r, d), F32),
        compiler_params=_params(("arbitrary",)),
        name="final_norm",
    )(x, g)


def _swap_halves(y):
    n = y.shape[-1]
    lane = lax.broadcasted_iota(jnp.int32, y.shape, 1)
    half = HEAD_DIM // 2
    return jnp.where(lane % HEAD_DIM < half, pltpu.roll(y, n - half, 1), pltpu.roll(y, half, 1))


def _qknorm_rope_kernel(p_ref, qg_ref, kg_ref, cos_ref, sin_ref, eq_ref, ek_ref, q_ref, k_ref, v_ref, *, rope):
    gq = q_ref.shape[1]
    gk = k_ref.shape[1]
    p = p_ref[...]

    def normed(x, gain, e):
        ms = _seg_sum(x * x, e) * (1.0 / HEAD_DIM)
        return x * lax.rsqrt(ms + NORM_EPS) * gain

    q = normed(p[:, :gq], qg_ref[...], eq_ref[...])
    k = normed(p[:, gq:gq + gk], kg_ref[...], ek_ref[...])
    if rope:
        cos = cos_ref[...]
        sin = sin_ref[...]
        q = q * cos + _swap_halves(q) * sin
        k = k * cos[:, :gk] + _swap_halves(k) * sin[:, :gk]
    q_ref[...] = (q * HEAD_DIM ** -0.5).astype(BF16)
    k_ref[...] = k.astype(BF16)
    v_ref[...] = p[:, gq + gk:].astype(BF16)


def qknorm_rope(pc, q_gain, k_gain, cos, sin, eq, ek, seq_len, rope):
    r, n = pc.shape
    gq = n // 2
    gk = n // 4
    tm = _tile(seq_len, 512)
    nseq = seq_len // tm
    return pl.pallas_call(
        functools.partial(_qknorm_rope_kernel, rope=rope),
        grid=(r // tm,),
        in_specs=[pl.BlockSpec((tm, n), lambda i: (i, 0)),
                  pl.BlockSpec((1, gq), lambda i: (0, 0)),
                  pl.BlockSpec((1, gk), lambda i: (0, 0)),
                  pl.BlockSpec((tm, gq), lambda i: (i % nseq, 0)),
                  pl.BlockSpec((tm, gq), lambda i: (i % nseq, 0)),
                  pl.BlockSpec(eq.shape, lambda i: (0, 0)),
                  pl.BlockSpec(ek.shape, lambda i: (0, 0))],
        out_specs=[pl.BlockSpec((tm, gq), lambda i: (i, 0)),
                   pl.BlockSpec((tm, gk), lambda i: (i, 0)),
                   pl.BlockSpec((tm, gk), lambda i: (i, 0))],
        out_shape=[jax.ShapeDtypeStruct((r, gq), BF16),
                   jax.ShapeDtypeStruct((r, gk), BF16),
                   jax.ShapeDtypeStruct((r, gk), BF16)],
        compiler_params=_params(("arbitrary",)),
        name="qknorm_rope",
    )(pc, q_gain, k_gain, cos, sin, eq, ek)


def _pad_heads(q, n_heads, kv_width, rep):
    assert kv_width % LANES == 0
    t = q.shape[0]
    lane = lax.broadcasted_iota(jnp.int32, (t, LANES), 1)
    lo_half = lane < HEAD_DIM
    zero = jnp.zeros((t, LANES), q.dtype)
    rows = []
    for h in range(n_heads):
        src = h * HEAD_DIM
        dst = (h // rep) * HEAD_DIM
        piece = q[:, src // LANES * LANES:(src // LANES + 1) * LANES]
        if src % LANES != dst % LANES:
            piece = pltpu.roll(piece.astype(F32), HEAD_DIM, 1).astype(q.dtype)
        piece = jnp.where(lo_half if dst % LANES == 0 else ~lo_half, piece, zero)
        blocks = [piece if j == dst // LANES else zero for j in range(kv_width // LANES)]
        rows.append(jnp.concatenate(blocks, axis=1) if len(blocks) > 1 else piece)
    return jnp.concatenate(rows, axis=0)


def _gather_heads(res, n_heads, rep):
    t = res.shape[0] // n_heads
    lane = lax.broadcasted_iota(jnp.int32, (t, LANES), 1)
    lo_half = lane < HEAD_DIM
    pieces = []
    for h in range(n_heads):
        src = (h // rep) * HEAD_DIM
        dst = h * HEAD_DIM
        piece = res[h * t:(h + 1) * t, src // LANES * LANES:(src // LANES + 1) * LANES]
        if src % LANES != dst % LANES:
            piece = pltpu.roll(piece, HEAD_DIM, 1)
        pieces.append(piece)
    blocks = [jnp.where(lo_half, pieces[2 * j], pieces[2 * j + 1]) for j in range(n_heads // 2)]
    return jnp.concatenate(blocks, axis=1)


def _attn_kernel(q_ref, k_ref, v_ref, o_ref, *, rep):
    n_heads = q_ref.shape[1] // HEAD_DIM
    lk = k_ref.shape[1]
    q = _pad_heads(q_ref[...], n_heads, k_ref.shape[2], rep)
    edges = list(range(0, lk, ATTN_CHUNK)) + [lk]
    m = l = acc = None
    for k0, k1 in zip(edges[:-1], edges[1:]):
        s = _dot_nt(q, k_ref[0, k0:k1, :])
        m_new = jnp.max(s, axis=-1, keepdims=True)
        if m is not None:
            m_new = jnp.maximum(m, m_new)
        p = jnp.exp(s - m_new)
        l_new = jnp.sum(p, axis=-1, keepdims=True)
        acc_new = _dot(p.astype(BF16), v_ref[0, k0:k1, :])
        if m is not None:
            alpha = jnp.exp(m - m_new)
            l_new = alpha * l + l_new
            acc_new = alpha * acc + acc_new
        m, l, acc = m_new, l_new, acc_new
    o_ref[...] = _gather_heads(acc / l, n_heads, rep)


def attention(q, k, v, seq_len):
    r, gq = q.shape
    nb, lk, kw = k.shape
    rep = gq // kw
    tq = _tile(seq_len, 128)
    nq = seq_len // tq
    return pl.pallas_call(
        functools.partial(_attn_kernel, rep=rep),
        grid=(nb, nq),
        in_specs=[pl.BlockSpec((tq, gq), lambda i, j: (i * nq + j, 0)),
                  pl.BlockSpec((1, lk, kw), lambda i, j: (i, 0, 0)),
                  pl.BlockSpec((1, lk, kw), lambda i, j: (i, 0, 0))],
        out_specs=pl.BlockSpec((tq, gq), lambda i, j: (i * nq + j, 0)),
        out_shape=jax.ShapeDtypeStruct((r, gq), F32),
        compiler_params=_params(("arbitrary", "arbitrary")),
        name="attention",
    )(q, k, v)


def _na_kernel(p_ref, pc_ref, bias_ref, o_ref, *, rows, kh):
    g = o_ref.shape[2]
    n_heads = g // HEAD_DIM
    band = kh * GRID_W
    kc = pc_ref[0, :, g:2 * g]
    vc = pc_ref[0, :, 2 * g:3 * g]

    def one_row(r, carry):
        rs = jnp.clip(r - kh // 2, 0, rows - kh)
        q0 = pl.multiple_of(r * GRID_W, GRID_W)
        k0 = pl.multiple_of(rs * GRID_W, GRID_W)
        q = _pad_heads(p_ref[0, pl.ds(q0, GRID_W), 0:g], n_heads, g, 1)
        s1 = _dot_nt(q, p_ref[0, pl.ds(k0, band), g:2 * g]) + bias_ref[r - rs]
        s2 = _dot_nt(q, kc)
        m = jnp.maximum(jnp.max(s1, axis=-1, keepdims=True), jnp.max(s2, axis=-1, keepdims=True))
        p1 = jnp.exp(s1 - m)
        p2 = jnp.exp(s2 - m)
        l = jnp.sum(p1, axis=-1, keepdims=True) + jnp.sum(p2, axis=-1, keepdims=True)
        res = (_dot(p1.astype(BF16), p_ref[0, pl.ds(k0, band), 2 * g:3 * g]) + _dot(p2.astype(BF16), vc)) / l
        o_ref[0, pl.ds(q0, GRID_W), :] = _gather_heads(res, n_heads, 1)
        return carry

    lax.fori_loop(0, rows, one_row, 0, unroll=4)


def neighbourhood_attention(p, pc, bias):
    nb, l, g3 = p.shape
    g = g3 // 3
    lc = pc.shape[1]
    rows = l // GRID_W
    kh = min(NA_ROWS, rows)
    return pl.pallas_call(
        functools.partial(_na_kernel, rows=rows, kh=kh),
        grid=(nb,),
        in_specs=[pl.BlockSpec((1, l, g3), lambda i: (i, 0, 0)),
                  pl.BlockSpec((1, lc, g3), lambda i: (i, 0, 0)),
                  pl.BlockSpec(bias.shape, lambda i: (0, 0, 0))],
        out_specs=pl.BlockSpec((1, l, g), lambda i: (i, 0, 0)),
        out_shape=jax.ShapeDtypeStruct((nb, l, g), F32),
        compiler_params=_params(("arbitrary",)),
        name="neighbourhood_attention",
    )(p, pc, bias)


def _na_bias_kernel(rpb_ref, o_ref, *, kh):
    h = pl.program_id(0)
    q = lax.broadcasted_iota(jnp.int32, (GRID_W, GRID_W), 0)
    k = lax.broadcasted_iota(jnp.int32, (GRID_W, GRID_W), 1)
    start = jnp.clip(q - NA_COLS // 2, 0, GRID_W - NA_COLS)
    in_win = (k >= start) & (k < start + NA_COLS)
    off = k - q + NA_COLS - 1
    neg = jnp.full((GRID_W, GRID_W), -jnp.inf, F32)
    blocks = []
    for ro in range(2 * NA_ROWS - 1):
        t = neg
        for c in range(2 * NA_COLS - 1):
            t = jnp.where(off == c, rpb_ref[h, ro, c], t)
        blocks.append(jnp.where(in_win, t, neg))
    for di in range(kh):
        for i in range(kh):
            o_ref[di, :, i * GRID_W:(i + 1) * GRID_W] = blocks[i - di + NA_ROWS - 1]


def na_bias_table(rpb, rows):
    kh = min(NA_ROWS, rows)
    nh = rpb.shape[0]
    return pl.pallas_call(
        functools.partial(_na_bias_kernel, kh=kh),
        grid=(nh,),
        in_specs=[pl.BlockSpec(memory_space=pltpu.SMEM)],
        out_specs=pl.BlockSpec((kh, GRID_W, kh * GRID_W), lambda i: (0, i, 0)),
        out_shape=jax.ShapeDtypeStruct((kh, nh * GRID_W, kh * GRID_W), F32),
        compiler_params=_params(("arbitrary",)),
        name="na_bias",
    )(rpb)


def _shift_rows(x, d, t):
    n = x.shape[0]
    y = pltpu.roll(x, d % n, 0)
    src = t - d
    return jnp.where((src >= 0) & (src < n), y, 0.0)


def _pool_kernel(x_ref, w_ref, scale_ref, o_ref):
    x = x_ref[0]
    n, g = x.shape
    pg = g // len(POOL_WINDOWS)
    t = lax.broadcasted_iota(jnp.int32, x.shape, 0)
    group = lax.broadcasted_iota(jnp.int32, x.shape, 1) // pg
    fwd = x
    bwd = x
    cur = 1
    total = jnp.zeros_like(x)
    count = jnp.ones_like(x)
    for j, w in enumerate(POOL_WINDOWS):
        half = w // 2
        while cur < half:
            fwd = fwd + _shift_rows(fwd, -cur, t)
            bwd = bwd + _shift_rows(bwd, cur, t)
            cur *= 2
        win = _shift_rows(bwd, 1, t) + fwd
        lo = jnp.clip(t - half, 0, n)
        hi = jnp.clip(t - half + w, 0, n)
        total = jnp.where(group == j, win, total)
        count = jnp.where(group == j, (hi - lo).astype(F32), count)
    diff = total / count - x
    o_ref[0] = _dot(diff.astype(BF16), w_ref[...]) * scale_ref[...]


def pool_mixer(p, w_bd, scale):
    b, n, g = p.shape
    return pl.pallas_call(
        _pool_kernel,
        grid=(b,),
        in_specs=[pl.BlockSpec((1, n, g), lambda i: (i, 0, 0)),
                  pl.BlockSpec((g, g), lambda i: (0, 0)),
                  pl.BlockSpec((1, g), lambda i: (0, 0))],
        out_specs=pl.BlockSpec((1, n, g), lambda i: (i, 0, 0)),
        out_shape=jax.ShapeDtypeStruct((b, n, g), F32),
        compiler_params=_params(("arbitrary",)),
        name="pool_mixer",
    )(p, w_bd, scale)


def _to_tiles(x):
    g = x.shape[1]
    tiles = []
    for s in range(x.shape[0] // SCAN_SUB):
        blk = x[s * SCAN_SUB:(s + 1) * SCAN_SUB, :]
        tiles.append(jnp.concatenate([blk[:, h * HEAD_DIM:(h + 1) * HEAD_DIM].T for h in range(g // HEAD_DIM)], axis=1))
    return tiles


def _from_tiles(tiles):
    g = tiles[0].shape[1]
    return jnp.concatenate(
        [jnp.concatenate([t[:, h * SCAN_SUB:(h + 1) * SCAN_SUB].T for h in range(g // HEAD_DIM)], axis=1) for t in tiles],
        axis=0)


def _rwkv_prep_kernel(p_ref, w0_ref, a0_ref, w2_ref, a2_ref, kk_w_ref, ka_ref, e_ref,
                      kk_ref, decf_ref, bf_ref, kmf_ref, decb_ref, bb_ref, kmb_ref, vt_ref):
    g = kk_ref.shape[1]
    k = p_ref[:, g:2 * g]
    lr = p_ref[:, 3 * g:4 * g]
    for s, tile in enumerate(_to_tiles(p_ref[:, 2 * g:3 * g])):
        vt_ref[s] = tile
    kx = k * kk_w_ref[...]
    norm = jnp.sqrt(_seg_sum(kx * kx, e_ref[...]))
    kk = kx / jnp.maximum(norm, 1e-12)
    kk_ref[...] = kk
    lr_t = jnp.tanh(lr).astype(BF16)
    lr_b = lr.astype(BF16)
    for d, (dec_ref, b_ref, km_ref) in enumerate(((decf_ref, bf_ref, kmf_ref), (decb_ref, bb_ref, kmb_ref))):
        z = w0_ref[d:d + 1, :] + _dot(lr_t, w2_ref[d])
        softplus_neg = jnp.maximum(-z, 0.0) + jnp.log1p(jnp.exp(-jnp.abs(z)))
        w = -softplus_neg - 0.5
        dec_ref[...] = jnp.exp(-jnp.exp(w))
        a = jax.nn.sigmoid(a0_ref[d:d + 1, :] + _dot(lr_b, a2_ref[d]))
        b_ref[...] = kk * a
        km_ref[...] = k * (1.0 + (a - 1.0) * ka_ref[...])


def rwkv_prep(pa, w0, a0, w2p, a2p, k_k, k_a, eseg):
    r, n = pa.shape
    g = n // 4
    tm = _tile(r, 512)
    full2 = lambda shape: pl.BlockSpec(shape, lambda i: (0,) * len(shape))
    out = pl.BlockSpec((tm, g), lambda i: (i, 0))
    return pl.pallas_call(
        _rwkv_prep_kernel,
        grid=(r // tm,),
        in_specs=[pl.BlockSpec((tm, n), lambda i: (i, 0)),
                  full2(w0.shape), full2(a0.shape), full2(w2p.shape), full2(a2p.shape),
                  full2(k_k.shape), full2(k_a.shape), full2(eseg.shape)],
        out_specs=[out] * 7 + [pl.BlockSpec((tm // SCAN_SUB, HEAD_DIM, g), lambda i: (i, 0, 0))],
        out_shape=[jax.ShapeDtypeStruct((r, g), F32)] * 7 + [jax.ShapeDtypeStruct((r // SCAN_SUB, HEAD_DIM, g), F32)],
        compiler_params=_params(("arbitrary",)),
        name="rwkv_prep",
    )(pa, w0, a0, w2p, a2p, k_k, k_a, eseg)


def _wkv_scan_kernel(rf_ref, rb_ref, kkf_ref, kkb_ref, decf_ref, bf_ref, kmf_ref, decb_ref, bb_ref, kmb_ref,
                     vtf_ref, vtb_ref, e_ref, s0_ref, yf_ref, yb_ref, send_ref, s_scr, sr_scr):
    nb, t_blk, g = rf_ref.shape
    step_id = pl.program_id(0)

    @pl.when(step_id == 0)
    def _():
        s_scr[...] = s0_ref[...]

    yf_ref[...] = jnp.zeros_like(yf_ref)
    yb_ref[...] = jnp.zeros_like(yb_ref)
    sr_scr[...] = jnp.zeros_like(sr_scr)
    e = e_ref[...]
    lane = lax.broadcasted_iota(jnp.int32, (HEAD_DIM, g), 1) % SCAN_SUB
    lane128 = lax.broadcasted_iota(jnp.int32, (HEAD_DIM, LANES), 1)
    n = nb * HEAD_DIM
    dirs = ((rf_ref, kkf_ref, decf_ref, bf_ref, kmf_ref, vtf_ref, yf_ref),
            (rb_ref, kkb_ref, decb_ref, bb_ref, kmb_ref, vtb_ref, yb_ref))

    def write_y(d, bs, y, tt_done, valid):
        y_ref = dirs[d][-1]
        tl = tt_done if d == 0 else t_blk - 1 - tt_done
        tl = jnp.clip(tl, 0, t_blk - 1)
        hit = (lane == tl % SCAN_SUB) & valid
        for i, b in enumerate(bs):
            pltpu.store(y_ref.at[b, tl // SCAN_SUB], y[i * HEAD_DIM:(i + 1) * HEAD_DIM], mask=hit)

    groups = [tuple(range(i, min(i + 2, nb))) for i in range(0, nb, 2)]

    def one_token(tt, carry):
        for d, (r_ref, kk_ref, dec_ref, b_ref, km_ref, vt_ref, y_ref) in enumerate(dirs):
            tl = tt if d == 0 else t_blk - 1 - tt
            row = lambda ref, b: jnp.broadcast_to(ref[b, pl.ds(tl, 1), :], (HEAD_DIM, g))
            blk = lambda x, i: x[i * HEAD_DIM:(i + 1) * HEAD_DIM]
            idx = (lane128 // SCAN_SUB) * SCAN_SUB + tl % SCAN_SUB
            for bs in groups:
                m = len(bs) * HEAD_DIM
                s_old = [s_scr[d, b] for b in bs]
                x = jnp.concatenate([s_old[i] * row(kk_ref, b) for i, b in enumerate(bs)], axis=0)
                out = _dot(jnp.concatenate([x.astype(BF16), sr_scr[d, bs[0] * HEAD_DIM:(bs[-1] + 1) * HEAD_DIM, :]], axis=0), e)
                write_y(d, bs, out[m:], tt - 1, tt > 0)
                for i, b in enumerate(bs):
                    vcol = jnp.concatenate([jnp.take_along_axis(vt_ref[b, tl // SCAN_SUB][:, j * LANES:(j + 1) * LANES], idx, axis=1)
                                            for j in range(g // LANES)], axis=1)
                    s = s_old[i] * row(dec_ref, b) - blk(out, i) * row(b_ref, b) + vcol * row(km_ref, b)
                    s_scr[d, b] = s
                    sr_scr[d, b * HEAD_DIM:(b + 1) * HEAD_DIM, :] = (s * row(r_ref, b)).astype(BF16)
        return carry

    lax.fori_loop(0, t_blk, one_token, 0, unroll=4)
    for d in range(2):
        write_y(d, tuple(range(nb)), _dot(sr_scr[d], e), t_blk - 1, True)

    @pl.when(step_id == pl.num_programs(0) - 1)
    def _():
        send_ref[...] = s_scr[...]


def wkv_scan(pa3, kk, dec_f, b_f, km_f, dec_b, b_b, km_b, vt, eseg, s0):
    nb, l, g = kk.shape
    t_blk = SCAN_BLOCK
    nsub = t_blk // SCAN_SUB
    nblk = l // t_blk
    assert l % t_blk == 0
    seq_f = pl.BlockSpec((nb, t_blk, g), lambda i: (0, i, 0))
    seq_b = pl.BlockSpec((nb, t_blk, g), lambda i: (0, nblk - 1 - i, 0))
    vt_f = pl.BlockSpec((nb, nsub, HEAD_DIM, g), lambda i: (0, i, 0, 0))
    vt_b = pl.BlockSpec((nb, nsub, HEAD_DIM, g), lambda i: (0, nblk - 1 - i, 0, 0))
    state = pl.BlockSpec(s0.shape, lambda i: (0, 0, 0, 0))
    return pl.pallas_call(
        _wkv_scan_kernel,
        grid=(nblk,),
        in_specs=[seq_f, seq_b, seq_f, seq_b, seq_f, seq_f, seq_f, seq_b, seq_b, seq_b,
                  vt_f, vt_b, pl.BlockSpec(eseg.shape, lambda i: (0, 0)), state],
        out_specs=[vt_f, vt_b, state],
        out_shape=[jax.ShapeDtypeStruct(vt.shape, F32), jax.ShapeDtypeStruct(vt.shape, F32),
                   jax.ShapeDtypeStruct(s0.shape, F32)],
        scratch_shapes=[pltpu.VMEM(s0.shape, F32), pltpu.VMEM((2, nb * HEAD_DIM, g), BF16)],
        compiler_params=_params(("arbitrary",)),
        name="wkv_scan",
    )(pa3, pa3, kk, kk, dec_f, b_f, km_f, dec_b, b_b, km_b, vt, vt, eseg, s0)


def _rwkv_readout_kernel(yf_ref, yb_ref, p_ref, kmf_ref, kmb_ref, rk_ref, g2_ref, gnw_ref, gnb_ref, e_ref, o_ref):
    g = o_ref.shape[1]
    e = e_ref[...]
    r = p_ref[:, :g]
    v = p_ref[:, 2 * g:3 * g]
    lr = p_ref[:, 3 * g:4 * g]
    y = _from_tiles([yf_ref[s] + yb_ref[s] for s in range(yf_ref.shape[0])])
    mu = _seg_sum(y, e) * (1.0 / HEAD_DIM)
    yc = y - mu
    var = _seg_sum(yc * yc, e) * (1.0 / HEAD_DIM)
    yn = yc * lax.rsqrt(var + RWKV_GN_EPS) * gnw_ref[...] + gnb_ref[...]
    bonus = _seg_sum(r * (kmf_ref[...] + kmb_ref[...]) * rk_ref[...], e) * v
    gate = _dot(jax.nn.sigmoid(lr).astype(BF16), g2_ref[...])
    o_ref[...] = (yn + bonus) * gate


def rwkv_readout(yf, yb, pa, km_f, km_b, r_k, g2p, gn_w, gn_b, eseg):
    r, g = km_f.shape
    tm = _tile(r, 512)
    row = pl.BlockSpec((tm, g), lambda i: (i, 0))
    tiles = pl.BlockSpec((tm // SCAN_SUB, HEAD_DIM, g), lambda i: (i, 0, 0))
    full2 = lambda shape: pl.BlockSpec(shape, lambda i: (0,) * len(shape))
    return pl.pallas_call(
        _rwkv_readout_kernel,
        grid=(r // tm,),
        in_specs=[tiles, tiles, pl.BlockSpec((tm, 4 * g), lambda i: (i, 0)), row, row,
                  full2(r_k.shape), full2(g2p.shape), full2(gn_w.shape), full2(gn_b.shape), full2(eseg.shape)],
        out_specs=row,
        out_shape=jax.ShapeDtypeStruct((r, g), F32),
        compiler_params=_params(("arbitrary",)),
        name="rwkv_readout",
    )(yf, yb, pa, km_f, km_b, r_k, g2p, gn_w, gn_b, eseg)


def _block_ones(n, seg):
    idx = np.arange(n) // seg
    return jnp.asarray(idx[:, None] == idx[None, :], dtype=BF16)


def _padded_rows(w, offset, n):
    return jnp.zeros((n, w.shape[1]), w.dtype).at[offset:offset + w.shape[0]].set(w)


def _rope_tables(n_tokens, n_rep):
    t = jnp.arange(n_tokens)
    row = (t // GRID_W).astype(F32)
    col = (t % GRID_W).astype(F32)
    n_freq = HEAD_DIM // 4
    inv_freq = ROPE_THETA ** (-jnp.arange(n_freq, dtype=F32) / n_freq)
    ang = jnp.concatenate([row[:, None] * inv_freq, col[:, None] * inv_freq], axis=-1)
    cos, sin = jnp.cos(ang), jnp.sin(ang)
    cos_h = jnp.concatenate([cos, cos], axis=-1)
    sin_h = jnp.concatenate([-sin, sin], axis=-1)
    return jnp.tile(cos_h, (1, n_rep)), jnp.tile(sin_h, (1, n_rep))


def kernel(x, c, ctx, c_ctx, ada_w, ada_b, norm1_g, norm2_g, w_in, w_out, rwkv_w0, rwkv_w2, rwkv_a0, rwkv_a2,
           rwkv_k_k, rwkv_k_a, rwkv_r_k, rwkv_g2, rwkv_gn_w, rwkv_gn_b, na_rpb, gqa_q_gain, gqa_k_gain, pool_w,
           pool_scale, mlp_w1, mlp_w2, final_g):
    nb, l, d = x.shape
    lc = ctx.shape[1]
    depth = ada_w.shape[0]
    g = d // 4
    nh = g // HEAD_DIM
    nkv = nh // 2
    kvw = nkv * HEAD_DIM
    dr = rwkv_w2.shape[2]
    ir = rwkv_a2.shape[2]
    gr = rwkv_g2.shape[1]
    assert 2 * dr + 2 * ir + gr <= g and nb + 1 <= 8
    rows = l // GRID_W

    cpad = jnp.zeros((8, d), F32).at[:nb].set(c).at[nb].set(c_ctx)
    mods = adaln(cpad, ada_w, ada_b)

    eseg = _block_ones(g, HEAD_DIM)
    ekv = _block_ones(kvw, HEAD_DIM)
    cos_q, sin_q = _rope_tables(l, nh)
    ones_c, zeros_c = jnp.ones((lc, g), F32), jnp.zeros((lc, g), F32)

    splits = np.cumsum([0, g, g, g, dr, dr, ir, ir, gr, g, g, g, g, kvw, kvw, g])
    lowrank_w = splits[8] - splits[3]
    widths = (4 * g, 3 * g, 2 * g, g)
    proj_dtypes = (F32, BF16, F32, F32)

    xl = x.reshape(nb * l, d)
    xc = ctx.reshape(nb * lc, d)
    zero_state = jnp.zeros((2, nb, HEAD_DIM, g), F32)

    for i in range(depth):
        need_ctx_out = i < depth - 1
        mod_l = mods[i, :nb].reshape(nb, N_MOD, 1, d)
        mod_c = mods[i, nb].reshape(N_MOD, 1, 1, d)
        ml = [mod_l[:, k] for k in range(N_MOD)]
        mc = [mod_c[k] for k in range(N_MOD)]

        wi = w_in[i]
        w_inp = jnp.concatenate([wi[:, :splits[8]], jnp.zeros((d, g - lowrank_w), F32),
                                 wi[:, splits[8]:splits[9]] * HEAD_DIM ** -0.5, wi[:, splits[9]:]], axis=1).astype(BF16)
        g1 = norm1_g[i].reshape(1, d)
        g2 = norm2_g[i].reshape(1, d)
        pa_l, pb_l, pc_l, pd_l = inproj(xl, g1, ml[0], ml[1], w_inp, l, widths, proj_dtypes)
        pa_c, pb_c, pc_c, pd_c = inproj(xc, g1, mc[0], mc[1], w_inp, nb * lc, widths, proj_dtypes)

        w2p = jnp.stack([_padded_rows(rwkv_w2[i, 0], 0, g), _padded_rows(rwkv_w2[i, 1], dr, g)]).astype(BF16)
        a2p = jnp.stack([_padded_rows(rwkv_a2[i, 0], 2 * dr, g),
                         _padded_rows(rwkv_a2[i, 1], 2 * dr + ir, g)]).astype(BF16)
        g2p = _padded_rows(rwkv_g2[i], 2 * dr + 2 * ir, g).astype(BF16)
        k_k = rwkv_k_k[i].reshape(1, g)
        k_a = rwkv_k_a[i].reshape(1, g)
        r_k = rwkv_r_k[i].reshape(1, g)
        gn_w = rwkv_gn_w[i].reshape(1, g)
        gn_b = rwkv_gn_b[i].reshape(1, g)

        def rwkv_terms(pa, seq):
            terms = rwkv_prep(pa, rwkv_w0[i], rwkv_a0[i], w2p, a2p, k_k, k_a, eseg)
            terms3 = [t.reshape(nb, seq, g) for t in terms[:7]]
            vt = terms[7].reshape(nb, seq // SCAN_SUB, HEAD_DIM, g)
            return terms, terms3, vt

        terms_c, terms3_c, vt_c = rwkv_terms(pa_c, lc)
        yf_c, yb_c, state_c = wkv_scan(pa_c.reshape(nb, lc, 4 * g), *terms3_c, vt_c, eseg, zero_state)
        terms_l, terms3_l, vt_l = rwkv_terms(pa_l, l)
        yf_l, yb_l, _ = wkv_scan(pa_l.reshape(nb, l, 4 * g), *terms3_l, vt_l, eseg, state_c)
        a_l = rwkv_readout(yf_l.reshape(-1, HEAD_DIM, g), yb_l.reshape(-1, HEAD_DIM, g),
                           pa_l, terms_l[3], terms_l[6], r_k, g2p, gn_w, gn_b, eseg)

        bias = na_bias_table(na_rpb[i], rows)
        b_l = neighbourhood_attention(pb_l.reshape(nb, l, 3 * g), pb_c.reshape(nb, lc, 3 * g), bias).reshape(nb * l, g)

        q_gain = jnp.tile(gqa_q_gain[i], nh).reshape(1, g)
        k_gain = jnp.tile(gqa_k_gain[i], nkv).reshape(1, kvw)
        qc_l, kc_l, vc_l = qknorm_rope(pc_l, q_gain, k_gain, cos_q, sin_q, eseg, ekv, l, True)
        qc_c, kc_c, vc_c = qknorm_rope(pc_c, q_gain, k_gain, ones_c, zeros_c, eseg, ekv, lc, False)
        kc_c, vc_c = kc_c.reshape(nb, lc, kvw), vc_c.reshape(nb, lc, kvw)
        k_all = jnp.concatenate([kc_c, kc_l.reshape(nb, l, kvw)], axis=1)
        v_all = jnp.concatenate([vc_c, vc_l.reshape(nb, l, kvw)], axis=1)
        c_l = attention(qc_l, k_all, v_all, l)

        w_bd = jax.scipy.linalg.block_diag(*[pool_w[i, k] for k in range(len(POOL_WINDOWS))]).astype(BF16)
        p_scale = pool_scale[i].reshape(1, g)
        d_l = pool_mixer(pd_l.reshape(nb, l, g), w_bd, p_scale).reshape(nb * l, g)

        w_out4 = w_out[i].reshape(4, g, d).astype(BF16)
        w1 = mlp_w1[i].astype(BF16)
        w2 = mlp_w2[i].astype(BF16)
        xl = outproj((a_l, b_l, c_l, d_l), w_out4, xl, ml[2], l)
        xl = mlp(xl, g2, ml[3], ml[4], ml[5], w1, w2, l)

        if need_ctx_out:
            a_c = rwkv_readout(yf_c.reshape(-1, HEAD_DIM, g), yb_c.reshape(-1, HEAD_DIM, g),
                               pa_c, terms_c[3], terms_c[6], r_k, g2p, gn_w, gn_b, eseg)
            b_c = attention(pb_c[:, :g], pb_c[:, g:2 * g].reshape(nb, lc, g), pb_c[:, 2 * g:].reshape(nb, lc, g), lc)
            c_c = attention(qc_c, kc_c, vc_c, lc)
            d_c = pool_mixer(pd_c.reshape(nb, lc, g), w_bd, p_scale).reshape(nb * lc, g)
            xc = outproj((a_c, b_c, c_c, d_c), w_out4, xc, mc[2], nb * lc)
            xc = mlp(xc, g2, mc[3], mc[4], mc[5], w1, w2, nb * lc)

    return final_norm(xl, final_g.reshape(1, d)).reshape(nb, l, d)
```

```python
import functools

import jax
import jax.numpy as jnp
import numpy as np
from jax import lax
from jax.experimental import pallas as pl
from jax.experimental.pallas import tpu as pltpu

F32 = jnp.float32
BF16 = jnp.bfloat16

HEAD_DIM = 64
GRID_W = 64
NA_ROWS = 8
NA_COLS = 16
ROPE_THETA = 10000.0
POOL_WINDOWS = (2, 4, 8, 16)
NORM_EPS = 1e-6
RWKV_GN_EPS = 1e-5 * HEAD_DIM
N_MOD = 6
LANES = 128
SCAN_BLOCK = 128
SCAN_SUB = 64
ATTN_CHUNK = 1024
VMEM_LIMIT = 52 * 1024 * 1024


def _params(sem):
    return pltpu.CompilerParams(dimension_semantics=sem, vmem_limit_bytes=VMEM_LIMIT)


def _tile(n, pref):
    t = min(n, pref)
    assert n % t == 0, (n, pref)
    return t


def _dot(a, b):
    return jnp.dot(a, b, preferred_element_type=F32)


def _dot_nt(a, b):
    return lax.dot_general(a, b, (((1,), (1,)), ((), ())), preferred_element_type=F32)


def _seg_sum(x, e):
    hi = x.astype(BF16)
    lo = (x - hi.astype(F32)).astype(BF16)
    return _dot(hi, e) + _dot(lo, e)


def _norm_mod(x, g, shift, scale):
    ms = jnp.mean(x * x, axis=-1, keepdims=True)
    h = x * lax.rsqrt(ms + NORM_EPS) * g
    return h * (1.0 + scale) + shift


def _adaln_kernel(c_ref, w_ref, b_ref, o_ref):
    c = c_ref[...]
    s = c * jax.nn.sigmoid(c)
    o_ref[0] = jnp.dot(s, w_ref[0], preferred_element_type=F32, precision=lax.Precision.HIGHEST) + b_ref[0]


def adaln(cpad, ada_w, ada_b):
    depth, d, n = ada_w.shape
    tn = _tile(n, 1536)
    return pl.pallas_call(
        _adaln_kernel,
        grid=(depth, n // tn),
        in_specs=[pl.BlockSpec((8, d), lambda i, j: (0, 0)),
                  pl.BlockSpec((1, d, tn), lambda i, j: (i, 0, j)),
                  pl.BlockSpec((1, 1, tn), lambda i, j: (i, 0, j))],
        out_specs=pl.BlockSpec((1, 8, tn), lambda i, j: (i, 0, j)),
        out_shape=jax.ShapeDtypeStruct((depth, 8, n), F32),
        compiler_params=_params(("arbitrary", "arbitrary")),
        name="adaln",
    )(cpad, ada_w, ada_b.reshape(depth, 1, n))


def _inproj_kernel(x_ref, g_ref, sh_ref, sc_ref, w_ref, oa_ref, ob_ref, oc_ref, od_ref):
    h = _norm_mod(x_ref[...], g_ref[...], sh_ref[0], sc_ref[0]).astype(BF16)
    off = 0
    for o_ref in (oa_ref, ob_ref, oc_ref, od_ref):
        n = o_ref.shape[1]
        o_ref[...] = _dot(h, w_ref[:, off:off + n]).astype(o_ref.dtype)
        off += n


def inproj(x, g, shift, scale, w, rows_per_mod, widths, dtypes):
    r, d = x.shape
    tm = _tile(rows_per_mod, 512)
    tpb = rows_per_mod // tm
    modspec = pl.BlockSpec((1, 1, d), lambda i: (i // tpb, 0, 0))
    return pl.pallas_call(
        _inproj_kernel,
        grid=(r // tm,),
        in_specs=[pl.BlockSpec((tm, d), lambda i: (i, 0)),
                  pl.BlockSpec((1, d), lambda i: (0, 0)),
                  modspec, modspec,
                  pl.BlockSpec(w.shape, lambda i: (0, 0))],
        out_specs=[pl.BlockSpec((tm, n), lambda i: (i, 0)) for n in widths],
        out_shape=[jax.ShapeDtypeStruct((r, n), dt) for n, dt in zip(widths, dtypes)],
        compiler_params=_params(("arbitrary",)),
        name="inproj",
    )(x, g, shift, scale, w)


def _outproj_kernel(a_ref, b_ref, c_ref, d_ref, w_ref, x_ref, gate_ref, o_ref):
    acc = _dot(a_ref[...].astype(BF16), w_ref[0])
    acc += _dot(b_ref[...].astype(BF16), w_ref[1])
    acc += _dot(c_ref[...].astype(BF16), w_ref[2])
    acc += _dot(d_ref[...].astype(BF16), w_ref[3])
    o_ref[...] = x_ref[...] + gate_ref[0] * acc


def outproj(mix, w4, x, gate, rows_per_mod):
    r, d = x.shape
    gw = w4.shape[1]
    tm = _tile(rows_per_mod, 512)
    tpb = rows_per_mod // tm
    mixspec = pl.BlockSpec((tm, gw), lambda i: (i, 0))
    return pl.pallas_call(
        _outproj_kernel,
        grid=(r // tm,),
        in_specs=[mixspec, mixspec, mixspec, mixspec,
                  pl.BlockSpec(w4.shape, lambda i: (0, 0, 0)),
                  pl.BlockSpec((tm, d), lambda i: (i, 0)),
                  pl.BlockSpec((1, 1, d), lambda i: (i // tpb, 0, 0))],
        out_specs=pl.BlockSpec((tm, d), lambda i: (i, 0)),
        out_shape=jax.ShapeDtypeStruct((r, d), F32),
        compiler_params=_params(("arbitrary",)),
        name="outproj",
    )(*mix, w4, x, gate)


def _mlp_kernel(x_ref, g_ref, sh_ref, sc_ref, gate_ref, w1_ref, w2_ref, o_ref, h_scr, acc_scr):
    j = pl.program_id(1)

    @pl.when(j == 0)
    def _():
        h_scr[...] = _norm_mod(x_ref[...], g_ref[...], sh_ref[0], sc_ref[0]).astype(BF16)
        acc_scr[...] = jnp.zeros_like(acc_scr)

    u = jnp.maximum(_dot(h_scr[...], w1_ref[...]), 0.0)
    acc_scr[...] += _dot((u * u).astype(BF16), w2_ref[...])

    @pl.when(j == pl.num_programs(1) - 1)
    def _():
        o_ref[...] = x_ref[...] + gate_ref[0] * acc_scr[...]


def mlp(x, g, shift, scale, gate, w1, w2, rows_per_mod):
    r, d = x.shape
    dff = w1.shape[1]
    tm = _tile(rows_per_mod, 1024)
    tf = _tile(dff, 1024)
    tpb = rows_per_mod // tm
    modspec = pl.BlockSpec((1, 1, d), lambda i, j: (i // tpb, 0, 0))
    return pl.pallas_call(
        _mlp_kernel,
        grid=(r // tm, dff // tf),
        in_specs=[pl.BlockSpec((tm, d), lambda i, j: (i, 0)),
                  pl.BlockSpec((1, d), lambda i, j: (0, 0)),
                  modspec, modspec, modspec,
                  pl.BlockSpec((d, tf), lambda i, j: (0, j)),
                  pl.BlockSpec((tf, d), lambda i, j: (j, 0))],
        out_specs=pl.BlockSpec((tm, d), lambda i, j: (i, 0)),
        out_shape=jax.ShapeDtypeStruct((r, d), F32),
        scratch_shapes=[pltpu.VMEM((tm, d), BF16), pltpu.VMEM((tm, d), F32)],
        compiler_params=_params(("arbitrary", "arbitrary")),
        name="mlp",
    )(x, g, shift, scale, gate, w1, w2)


def _final_norm_kernel(x_ref, g_ref, o_ref):
    x = x_ref[...]
    ms = jnp.mean(x * x, axis=-1, keepdims=True)
    o_ref[...] = x * lax.rsqrt(ms + NORM_EPS) * g_ref[...]


def final_norm(x, g):
    r, d = x.shape
    tm = _tile(r, 1024)
    return pl.pallas_call(
        _final_norm_kernel,
        grid=(r // tm,),
        in_specs=[pl.BlockSpec((tm, d), lambda i: (i, 0)), pl.BlockSpec((1, d), lambda i: (0, 0))],
        out_specs=pl.BlockSpec((tm, d), lambda i: (i, 0)),
        out_shape=jax.ShapeDtypeStruct((r, d), F32),
        compiler_params=_params(("arbitrary",)),
        name="final_norm",
    )(x, g)


def _swap_halves(y):
    n = y.shape[-1]
    lane = lax.broadcasted_iota(jnp.int32, y.shape, 1)
    half = HEAD_DIM // 2
    return jnp.where(lane % HEAD_DIM < half, pltpu.roll(y, n - half, 1), pltpu.roll(y, half, 1))


def _qknorm_rope_kernel(p_ref, qg_ref, kg_ref, cos_ref, sin_ref, eq_ref, ek_ref, q_ref, k_ref, v_ref, *, rope):
    gq = q_ref.shape[1]
    gk = k_ref.shape[1]
    p = p_ref[...]

    def normed(x, gain, e):
        ms = _seg_sum(x * x, e) * (1.0 / HEAD_DIM)
        return x * lax.rsqrt(ms + NORM_EPS) * gain

    q = normed(p[:, :gq], qg_ref[...], eq_ref[...])
    k = normed(p[:, gq:gq + gk], kg_ref[...], ek_ref[...])
    if rope:
        cos = cos_ref[...]
        sin = sin_ref[...]
        q = q * cos + _swap_halves(q) * sin
        k = k * cos[:, :gk] + _swap_halves(k) * sin[:, :gk]
    q_ref[...] = (q * HEAD_DIM ** -0.5).astype(BF16)
    k_ref[...] = k.astype(BF16)
    v_ref[...] = p[:, gq + gk:].astype(BF16)


def qknorm_rope(pc, q_gain, k_gain, cos, sin, eq, ek, seq_len, rope):
    r, n = pc.shape
    gq = n // 2
    gk = n // 4
    tm = _tile(seq_len, 512)
    nseq = seq_len // tm
    return pl.pallas_call(
        functools.partial(_qknorm_rope_kernel, rope=rope),
        grid=(r // tm,),
        in_specs=[pl.BlockSpec((tm, n), lambda i: (i, 0)),
                  pl.BlockSpec((1, gq), lambda i: (0, 0)),
                  pl.BlockSpec((1, gk), lambda i: (0, 0)),
                  pl.BlockSpec((tm, gq), lambda i: (i % nseq, 0)),
                  pl.BlockSpec((tm, gq), lambda i: (i % nseq, 0)),
                  pl.BlockSpec(eq.shape, lambda i: (0, 0)),
                  pl.BlockSpec(ek.shape, lambda i: (0, 0))],
        out_specs=[pl.BlockSpec((tm, gq), lambda i: (i, 0)),
                   pl.BlockSpec((tm, gk), lambda i: (i, 0)),
                   pl.BlockSpec((tm, gk), lambda i: (i, 0))],
        out_shape=[jax.ShapeDtypeStruct((r, gq), BF16),
                   jax.ShapeDtypeStruct((r, gk), BF16),
                   jax.ShapeDtypeStruct((r, gk), BF16)],
        compiler_params=_params(("arbitrary",)),
        name="qknorm_rope",
    )(pc, q_gain, k_gain, cos, sin, eq, ek)


def _pad_heads(q, n_heads, kv_width, rep):
    assert kv_width % LANES == 0
    t = q.shape[0]
    lane = lax.broadcasted_iota(jnp.int32, (t, LANES), 1)
    lo_half = lane < HEAD_DIM
    zero = jnp.zeros((t, LANES), q.dtype)
    rows = []
    for h in range(n_heads):
        src = h * HEAD_DIM
        dst = (h // rep) * HEAD_DIM
        piece = q[:, src // LANES * LANES:(src // LANES + 1) * LANES]
        if src % LANES != dst % LANES:
            piece = pltpu.roll(piece.astype(F32), HEAD_DIM, 1).astype(q.dtype)
        piece = jnp.where(lo_half if dst % LANES == 0 else ~lo_half, piece, zero)
        blocks = [piece if j == dst // LANES else zero for j in range(kv_width // LANES)]
        rows.append(jnp.concatenate(blocks, axis=1) if len(blocks) > 1 else piece)
    return jnp.concatenate(rows, axis=0)


def _gather_heads(res, n_heads, rep):
    t = res.shape[0] // n_heads
    lane = lax.broadcasted_iota(jnp.int32, (t, LANES), 1)
    lo_half = lane < HEAD_DIM
    pieces = []
    for h in range(n_heads):
        src = (h // rep) * HEAD_DIM
        dst = h * HEAD_DIM
        piece = res[h * t:(h + 1) * t, src // LANES * LANES:(src // LANES + 1) * LANES]
        if src % LANES != dst % LANES:
            piece = pltpu.roll(piece, HEAD_DIM, 1)
        pieces.append(piece)
    blocks = [jnp.where(lo_half, pieces[2 * j], pieces[2 * j + 1]) for j in range(n_heads // 2)]
    return jnp.concatenate(blocks, axis=1)


def _attn_kernel(q_ref, k_ref, v_ref, o_ref, *, rep):
    n_heads = q_ref.shape[1] // HEAD_DIM
    lk = k_ref.shape[1]
    q = _pad_heads(q_ref[...], n_heads, k_ref.shape[2], rep)
    edges = list(range(0, lk, ATTN_CHUNK)) + [lk]
    m = l = acc = None
    for k0, k1 in zip(edges[:-1], edges[1:]):
        s = _dot_nt(q, k_ref[0, k0:k1, :])
        m_new = jnp.max(s, axis=-1, keepdims=True)
        if m is not None:
            m_new = jnp.maximum(m, m_new)
        p = jnp.exp(s - m_new)
        l_new = jnp.sum(p, axis=-1, keepdims=True)
        acc_new = _dot(p.astype(BF16), v_ref[0, k0:k1, :])
        if m is not None:
            alpha = jnp.exp(m - m_new)
            l_new = alpha * l + l_new
            acc_new = alpha * acc + acc_new
        m, l, acc = m_new, l_new, acc_new
    o_ref[...] = _gather_heads(acc / l, n_heads, rep)


def attention(q, k, v, seq_len):
    r, gq = q.shape
    nb, lk, kw = k.shape
    rep = gq // kw
    tq = _tile(seq_len, 256)
    nq = seq_len // tq
    return pl.pallas_call(
        functools.partial(_attn_kernel, rep=rep),
        grid=(nb, nq),
        in_specs=[pl.BlockSpec((tq, gq), lambda i, j: (i * nq + j, 0)),
                  pl.BlockSpec((1, lk, kw), lambda i, j: (i, 0, 0)),
                  pl.BlockSpec((1, lk, kw), lambda i, j: (i, 0, 0))],
        out_specs=pl.BlockSpec((tq, gq), lambda i, j: (i * nq + j, 0)),
        out_shape=jax.ShapeDtypeStruct((r, gq), F32),
        compiler_params=_params(("arbitrary", "arbitrary")),
        name="attention",
    )(q, k, v)


def _na_kernel(p_ref, pc_ref, bias_ref, o_ref, *, rows, kh):
    g = o_ref.shape[2]
    n_heads = g // HEAD_DIM
    band = kh * GRID_W
    kc = pc_ref[0, :, g:2 * g]
    vc = pc_ref[0, :, 2 * g:3 * g]

    def one_row(r, carry):
        rs = jnp.clip(r - kh // 2, 0, rows - kh)
        q0 = pl.multiple_of(r * GRID_W, GRID_W)
        k0 = pl.multiple_of(rs * GRID_W, GRID_W)
        q = _pad_heads(p_ref[0, pl.ds(q0, GRID_W), 0:g], n_heads, g, 1)
        s1 = _dot_nt(q, p_ref[0, pl.ds(k0, band), g:2 * g]) + bias_ref[r - rs]
        s2 = _dot_nt(q, kc)
        m = jnp.maximum(jnp.max(s1, axis=-1, keepdims=True), jnp.max(s2, axis=-1, keepdims=True))
        p1 = jnp.exp(s1 - m)
        p2 = jnp.exp(s2 - m)
        l = jnp.sum(p1, axis=-1, keepdims=True) + jnp.sum(p2, axis=-1, keepdims=True)
        res = (_dot(p1.astype(BF16), p_ref[0, pl.ds(k0, band), 2 * g:3 * g]) + _dot(p2.astype(BF16), vc)) / l
        o_ref[0, pl.ds(q0, GRID_W), :] = _gather_heads(res, n_heads, 1)
        return carry

    lax.fori_loop(0, rows, one_row, 0, unroll=4)


def neighbourhood_attention(p, pc, bias):
    nb, l, g3 = p.shape
    g = g3 // 3
    lc = pc.shape[1]
    rows = l // GRID_W
    kh = min(NA_ROWS, rows)
    return pl.pallas_call(
        functools.partial(_na_kernel, rows=rows, kh=kh),
        grid=(nb,),
        in_specs=[pl.BlockSpec((1, l, g3), lambda i: (i, 0, 0)),
                  pl.BlockSpec((1, lc, g3), lambda i: (i, 0, 0)),
                  pl.BlockSpec(bias.shape, lambda i: (0, 0, 0))],
        out_specs=pl.BlockSpec((1, l, g), lambda i: (i, 0, 0)),
        out_shape=jax.ShapeDtypeStruct((nb, l, g), F32),
        compiler_params=_params(("arbitrary",)),
        name="neighbourhood_attention",
    )(p, pc, bias)


def _na_bias_kernel(rpb_ref, o_ref, *, kh):
    h = pl.program_id(0)
    q = lax.broadcasted_iota(jnp.int32, (GRID_W, GRID_W), 0)
    k = lax.broadcasted_iota(jnp.int32, (GRID_W, GRID_W), 1)
    start = jnp.clip(q - NA_COLS // 2, 0, GRID_W - NA_COLS)
    in_win = (k >= start) & (k < start + NA_COLS)
    off = k - q + NA_COLS - 1
    neg = jnp.full((GRID_W, GRID_W), -jnp.inf, F32)
    blocks = []
    for ro in range(2 * NA_ROWS - 1):
        t = neg
        for c in range(2 * NA_COLS - 1):
            t = jnp.where(off == c, rpb_ref[h, ro, c], t)
        blocks.append(jnp.where(in_win, t, neg))
    for di in range(kh):
        for i in range(kh):
            o_ref[di, :, i * GRID_W:(i + 1) * GRID_W] = blocks[i - di + NA_ROWS - 1]


def na_bias_table(rpb, rows):
    kh = min(NA_ROWS, rows)
    nh = rpb.shape[0]
    return pl.pallas_call(
        functools.partial(_na_bias_kernel, kh=kh),
        grid=(nh,),
        in_specs=[pl.BlockSpec(memory_space=pltpu.SMEM)],
        out_specs=pl.BlockSpec((kh, GRID_W, kh * GRID_W), lambda i: (0, i, 0)),
        out_shape=jax.ShapeDtypeStruct((kh, nh * GRID_W, kh * GRID_W), F32),
        compiler_params=_params(("arbitrary",)),
        name="na_bias",
    )(rpb)


def _shift_rows(x, d, t):
    n = x.shape[0]
    y = pltpu.roll(x, d % n, 0)
    src = t - d
    return jnp.where((src >= 0) & (src < n), y, 0.0)


def _pool_kernel(x_ref, w_ref, scale_ref, o_ref):
    x = x_ref[0]
    n, g = x.shape
    pg = g // len(POOL_WINDOWS)
    t = lax.broadcasted_iota(jnp.int32, x.shape, 0)
    group = lax.broadcasted_iota(jnp.int32, x.shape, 1) // pg
    fwd = x
    bwd = x
    cur = 1
    total = jnp.zeros_like(x)
    count = jnp.ones_like(x)
    for j, w in enumerate(POOL_WINDOWS):
        half = w // 2
        while cur < half:
            fwd = fwd + _shift_rows(fwd, -cur, t)
            bwd = bwd + _shift_rows(bwd, cur, t)
            cur *= 2
        win = _shift_rows(bwd, 1, t) + fwd
        lo = jnp.clip(t - half, 0, n)
        hi = jnp.clip(t - half + w, 0, n)
        total = jnp.where(group == j, win, total)
        count = jnp.where(group == j, (hi - lo).astype(F32), count)
    diff = total / count - x
    o_ref[0] = _dot(diff.astype(BF16), w_ref[...]) * scale_ref[...]


def pool_mixer(p, w_bd, scale):
    b, n, g = p.shape
    return pl.pallas_call(
        _pool_kernel,
        grid=(b,),
        in_specs=[pl.BlockSpec((1, n, g), lambda i: (i, 0, 0)),
                  pl.BlockSpec((g, g), lambda i: (0, 0)),
                  pl.BlockSpec((1, g), lambda i: (0, 0))],
        out_specs=pl.BlockSpec((1, n, g), lambda i: (i, 0, 0)),
        out_shape=jax.ShapeDtypeStruct((b, n, g), F32),
        compiler_params=_params(("arbitrary",)),
        name="pool_mixer",
    )(p, w_bd, scale)


def _to_tiles(x):
    g = x.shape[1]
    tiles = []
    for s in range(x.shape[0] // SCAN_SUB):
        blk = x[s * SCAN_SUB:(s + 1) * SCAN_SUB, :]
        tiles.append(jnp.concatenate([blk[:, h * HEAD_DIM:(h + 1) * HEAD_DIM].T for h in range(g // HEAD_DIM)], axis=1))
    return tiles


def _from_tiles(tiles):
    g = tiles[0].shape[1]
    return jnp.concatenate(
        [jnp.concatenate([t[:, h * SCAN_SUB:(h + 1) * SCAN_SUB].T for h in range(g // HEAD_DIM)], axis=1) for t in tiles],
        axis=0)


def _rwkv_prep_kernel(p_ref, w0_ref, a0_ref, w2_ref, a2_ref, kk_w_ref, ka_ref, e_ref,
                      kk_ref, decf_ref, bf_ref, kmf_ref, decb_ref, bb_ref, kmb_ref, vt_ref):
    g = kk_ref.shape[1]
    k = p_ref[:, g:2 * g]
    lr = p_ref[:, 3 * g:4 * g]
    for s, tile in enumerate(_to_tiles(p_ref[:, 2 * g:3 * g])):
        vt_ref[s] = tile
    kx = k * kk_w_ref[...]
    norm = jnp.sqrt(_seg_sum(kx * kx, e_ref[...]))
    kk = kx / jnp.maximum(norm, 1e-12)
    kk_ref[...] = kk
    lr_t = jnp.tanh(lr).astype(BF16)
    lr_b = lr.astype(BF16)
    for d, (dec_ref, b_ref, km_ref) in enumerate(((decf_ref, bf_ref, kmf_ref), (decb_ref, bb_ref, kmb_ref))):
        z = w0_ref[d:d + 1, :] + _dot(lr_t, w2_ref[d])
        softplus_neg = jnp.maximum(-z, 0.0) + jnp.log1p(jnp.exp(-jnp.abs(z)))
        w = -softplus_neg - 0.5
        dec_ref[...] = jnp.exp(-jnp.exp(w))
        a = jax.nn.sigmoid(a0_ref[d:d + 1, :] + _dot(lr_b, a2_ref[d]))
        b_ref[...] = kk * a
        km_ref[...] = k * (1.0 + (a - 1.0) * ka_ref[...])


def rwkv_prep(pa, w0, a0, w2p, a2p, k_k, k_a, eseg):
    r, n = pa.shape
    g = n // 4
    tm = _tile(r, 512)
    full2 = lambda shape: pl.BlockSpec(shape, lambda i: (0,) * len(shape))
    out = pl.BlockSpec((tm, g), lambda i: (i, 0))
    return pl.pallas_call(
        _rwkv_prep_kernel,
        grid=(r // tm,),
        in_specs=[pl.BlockSpec((tm, n), lambda i: (i, 0)),
                  full2(w0.shape), full2(a0.shape), full2(w2p.shape), full2(a2p.shape),
                  full2(k_k.shape), full2(k_a.shape), full2(eseg.shape)],
        out_specs=[out] * 7 + [pl.BlockSpec((tm // SCAN_SUB, HEAD_DIM, g), lambda i: (i, 0, 0))],
        out_shape=[jax.ShapeDtypeStruct((r, g), F32)] * 7 + [jax.ShapeDtypeStruct((r // SCAN_SUB, HEAD_DIM, g), F32)],
        compiler_params=_params(("arbitrary",)),
        name="rwkv_prep",
    )(pa, w0, a0, w2p, a2p, k_k, k_a, eseg)


def _wkv_scan_kernel(rf_ref, rb_ref, kkf_ref, kkb_ref, decf_ref, bf_ref, kmf_ref, decb_ref, bb_ref, kmb_ref,
                     vtf_ref, vtb_ref, e_ref, s0_ref, yf_ref, yb_ref, send_ref, s_scr, sr_scr):
    nb, t_blk, g = rf_ref.shape
    step_id = pl.program_id(0)

    @pl.when(step_id == 0)
    def _():
        s_scr[...] = s0_ref[...]

    yf_ref[...] = jnp.zeros_like(yf_ref)
    yb_ref[...] = jnp.zeros_like(yb_ref)
    sr_scr[...] = jnp.zeros_like(sr_scr)
    e = e_ref[...]
    lane = lax.broadcasted_iota(jnp.int32, (HEAD_DIM, g), 1) % SCAN_SUB
    lane128 = lax.broadcasted_iota(jnp.int32, (HEAD_DIM, LANES), 1)
    n = nb * HEAD_DIM
    dirs = ((rf_ref, kkf_ref, decf_ref, bf_ref, kmf_ref, vtf_ref, yf_ref),
            (rb_ref, kkb_ref, decb_ref, bb_ref, kmb_ref, vtb_ref, yb_ref))

    def write_y(d, bs, y, tt_done, valid):
        y_ref = dirs[d][-1]
        tl = tt_done if d == 0 else t_blk - 1 - tt_done
        tl = jnp.clip(tl, 0, t_blk - 1)
        hit = (lane == tl % SCAN_SUB) & valid
        for i, b in enumerate(bs):
            pltpu.store(y_ref.at[b, tl // SCAN_SUB], y[i * HEAD_DIM:(i + 1) * HEAD_DIM], mask=hit)

    groups = [tuple(range(i, min(i + 2, nb))) for i in range(0, nb, 2)]

    def one_token(tt, carry):
        for d, (r_ref, kk_ref, dec_ref, b_ref, km_ref, vt_ref, y_ref) in enumerate(dirs):
            tl = tt if d == 0 else t_blk - 1 - tt
            row = lambda ref, b: jnp.broadcast_to(ref[b, pl.ds(tl, 1), :], (HEAD_DIM, g))
            blk = lambda x, i: x[i * HEAD_DIM:(i + 1) * HEAD_DIM]
            idx = (lane128 // SCAN_SUB) * SCAN_SUB + tl % SCAN_SUB
            for bs in groups:
                m = len(bs) * HEAD_DIM
                s_old = [s_scr[d, b] for b in bs]
                x = jnp.concatenate([s_old[i].astype(BF16) * row(kk_ref, b).astype(BF16) for i, b in enumerate(bs)], axis=0)
                out = _dot(jnp.concatenate([x, sr_scr[d, bs[0] * HEAD_DIM:(bs[-1] + 1) * HEAD_DIM, :]], axis=0), e)
                write_y(d, bs, out[m:], tt - 1, tt > 0)
                for i, b in enumerate(bs):
                    vcol = jnp.concatenate([jnp.take_along_axis(vt_ref[b, tl // SCAN_SUB][:, j * LANES:(j + 1) * LANES], idx, axis=1)
                                            for j in range(g // LANES)], axis=1)
                    s = s_old[i] * row(dec_ref, b) - blk(out, i) * row(b_ref, b) + vcol * row(km_ref, b)
                    s_scr[d, b] = s
                    sr_scr[d, b * HEAD_DIM:(b + 1) * HEAD_DIM, :] = s.astype(BF16) * row(r_ref, b).astype(BF16)
        return carry

    lax.fori_loop(0, t_blk, one_token, 0, unroll=8)
    for d in range(2):
        write_y(d, tuple(range(nb)), _dot(sr_scr[d], e), t_blk - 1, True)

    @pl.when(step_id == pl.num_programs(0) - 1)
    def _():
        send_ref[...] = s_scr[...]


def wkv_scan(pa3, kk, dec_f, b_f, km_f, dec_b, b_b, km_b, vt, eseg, s0):
    nb, l, g = kk.shape
    t_blk = SCAN_BLOCK
    nsub = t_blk // SCAN_SUB
    nblk = l // t_blk
    assert l % t_blk == 0
    seq_f = pl.BlockSpec((nb, t_blk, g), lambda i: (0, i, 0))
    seq_b = pl.BlockSpec((nb, t_blk, g), lambda i: (0, nblk - 1 - i, 0))
    vt_f = pl.BlockSpec((nb, nsub, HEAD_DIM, g), lambda i: (0, i, 0, 0))
    vt_b = pl.BlockSpec((nb, nsub, HEAD_DIM, g), lambda i: (0, nblk - 1 - i, 0, 0))
    state = pl.BlockSpec(s0.shape, lambda i: (0, 0, 0, 0))
    return pl.pallas_call(
        _wkv_scan_kernel,
        grid=(nblk,),
        in_specs=[seq_f, seq_b, seq_f, seq_b, seq_f, seq_f, seq_f, seq_b, seq_b, seq_b,
                  vt_f, vt_b, pl.BlockSpec(eseg.shape, lambda i: (0, 0)), state],
        out_specs=[vt_f, vt_b, state],
        out_shape=[jax.ShapeDtypeStruct(vt.shape, F32), jax.ShapeDtypeStruct(vt.shape, F32),
                   jax.ShapeDtypeStruct(s0.shape, F32)],
        scratch_shapes=[pltpu.VMEM(s0.shape, F32), pltpu.VMEM((2, nb * HEAD_DIM, g), BF16)],
        compiler_params=_params(("arbitrary",)),
        name="wkv_scan",
    )(pa3, pa3, kk, kk, dec_f, b_f, km_f, dec_b, b_b, km_b, vt, vt, eseg, s0)


def _rwkv_readout_kernel(yf_ref, yb_ref, p_ref, kmf_ref, kmb_ref, rk_ref, g2_ref, gnw_ref, gnb_ref, e_ref, o_ref):
    g = o_ref.shape[1]
    e = e_ref[...]
    r = p_ref[:, :g]
    v = p_ref[:, 2 * g:3 * g]
    lr = p_ref[:, 3 * g:4 * g]
    y = _from_tiles([yf_ref[s] + yb_ref[s] for s in range(yf_ref.shape[0])])
    mu = _seg_sum(y, e) * (1.0 / HEAD_DIM)
    yc = y - mu
    var = _seg_sum(yc * yc, e) * (1.0 / HEAD_DIM)
    yn = yc * lax.rsqrt(var + RWKV_GN_EPS) * gnw_ref[...] + gnb_ref[...]
    bonus = _seg_sum(r * (kmf_ref[...] + kmb_ref[...]) * rk_ref[...], e) * v
    gate = _dot(jax.nn.sigmoid(lr).astype(BF16), g2_ref[...])
    o_ref[...] = (yn + bonus) * gate


def rwkv_readout(yf, yb, pa, km_f, km_b, r_k, g2p, gn_w, gn_b, eseg):
    r, g = km_f.shape
    tm = _tile(r, 512)
    row = pl.BlockSpec((tm, g), lambda i: (i, 0))
    tiles = pl.BlockSpec((tm // SCAN_SUB, HEAD_DIM, g), lambda i: (i, 0, 0))
    full2 = lambda shape: pl.BlockSpec(shape, lambda i: (0,) * len(shape))
    return pl.pallas_call(
        _rwkv_readout_kernel,
        grid=(r // tm,),
        in_specs=[tiles, tiles, pl.BlockSpec((tm, 4 * g), lambda i: (i, 0)), row, row,
                  full2(r_k.shape), full2(g2p.shape), full2(gn_w.shape), full2(gn_b.shape), full2(eseg.shape)],
        out_specs=row,
        out_shape=jax.ShapeDtypeStruct((r, g), F32),
        compiler_params=_params(("arbitrary",)),
        name="rwkv_readout",
    )(yf, yb, pa, km_f, km_b, r_k, g2p, gn_w, gn_b, eseg)


def _block_ones(n, seg):
    idx = np.arange(n) // seg
    return jnp.asarray(idx[:, None] == idx[None, :], dtype=BF16)


def _padded_rows(w, offset, n):
    return jnp.zeros((n, w.shape[1]), w.dtype).at[offset:offset + w.shape[0]].set(w)


def _rope_tables(n_tokens, n_rep):
    t = jnp.arange(n_tokens)
    row = (t // GRID_W).astype(F32)
    col = (t % GRID_W).astype(F32)
    n_freq = HEAD_DIM // 4
    inv_freq = ROPE_THETA ** (-jnp.arange(n_freq, dtype=F32) / n_freq)
    ang = jnp.concatenate([row[:, None] * inv_freq, col[:, None] * inv_freq], axis=-1)
    cos, sin = jnp.cos(ang), jnp.sin(ang)
    cos_h = jnp.concatenate([cos, cos], axis=-1)
    sin_h = jnp.concatenate([-sin, sin], axis=-1)
    return jnp.tile(cos_h, (1, n_rep)), jnp.tile(sin_h, (1, n_rep))


def kernel(x, c, ctx, c_ctx, ada_w, ada_b, norm1_g, norm2_g, w_in, w_out, rwkv_w0, rwkv_w2, rwkv_a0, rwkv_a2,
           rwkv_k_k, rwkv_k_a, rwkv_r_k, rwkv_g2, rwkv_gn_w, rwkv_gn_b, na_rpb, gqa_q_gain, gqa_k_gain, pool_w,
           pool_scale, mlp_w1, mlp_w2, final_g):
    nb, l, d = x.shape
    lc = ctx.shape[1]
    depth = ada_w.shape[0]
    g = d // 4
    nh = g // HEAD_DIM
    nkv = nh // 2
    kvw = nkv * HEAD_DIM
    dr = rwkv_w2.shape[2]
    ir = rwkv_a2.shape[2]
    gr = rwkv_g2.shape[1]
    assert 2 * dr + 2 * ir + gr <= g and nb + 1 <= 8
    rows = l // GRID_W

    cpad = jnp.zeros((8, d), F32).at[:nb].set(c).at[nb].set(c_ctx)
    mods = adaln(cpad, ada_w, ada_b)

    eseg = _block_ones(g, HEAD_DIM)
    ekv = _block_ones(kvw, HEAD_DIM)
    cos_q, sin_q = _rope_tables(l, nh)
    ones_c, zeros_c = jnp.ones((lc, g), F32), jnp.zeros((lc, g), F32)

    splits = np.cumsum([0, g, g, g, dr, dr, ir, ir, gr, g, g, g, g, kvw, kvw, g])
    lowrank_w = splits[8] - splits[3]
    widths = (4 * g, 3 * g, 2 * g, g)
    proj_dtypes = (F32, BF16, F32, F32)

    xl = x.reshape(nb * l, d)
    xc = ctx.reshape(nb * lc, d)
    zero_state = jnp.zeros((2, nb, HEAD_DIM, g), F32)

    for i in range(depth):
        need_ctx_out = i < depth - 1
        mod_l = mods[i, :nb].reshape(nb, N_MOD, 1, d)
        mod_c = mods[i, nb].reshape(N_MOD, 1, 1, d)
        ml = [mod_l[:, k] for k in range(N_MOD)]
        mc = [mod_c[k] for k in range(N_MOD)]

        wi = w_in[i]
        w_inp = jnp.concatenate([wi[:, :splits[8]], jnp.zeros((d, g - lowrank_w), F32),
                                 wi[:, splits[8]:splits[9]] * HEAD_DIM ** -0.5, wi[:, splits[9]:]], axis=1).astype(BF16)
        g1 = norm1_g[i].reshape(1, d)
        g2 = norm2_g[i].reshape(1, d)
        pa_l, pb_l, pc_l, pd_l = inproj(xl, g1, ml[0], ml[1], w_inp, l, widths, proj_dtypes)
        pa_c, pb_c, pc_c, pd_c = inproj(xc, g1, mc[0], mc[1], w_inp, nb * lc, widths, proj_dtypes)

        w2p = jnp.stack([_padded_rows(rwkv_w2[i, 0], 0, g), _padded_rows(rwkv_w2[i, 1], dr, g)]).astype(BF16)
        a2p = jnp.stack([_padded_rows(rwkv_a2[i, 0], 2 * dr, g),
                         _padded_rows(rwkv_a2[i, 1], 2 * dr + ir, g)]).astype(BF16)
        g2p = _padded_rows(rwkv_g2[i], 2 * dr + 2 * ir, g).astype(BF16)
        k_k = rwkv_k_k[i].reshape(1, g)
        k_a = rwkv_k_a[i].reshape(1, g)
        r_k = rwkv_r_k[i].reshape(1, g)
        gn_w = rwkv_gn_w[i].reshape(1, g)
        gn_b = rwkv_gn_b[i].reshape(1, g)

        def rwkv_terms(pa, seq):
            terms = rwkv_prep(pa, rwkv_w0[i], rwkv_a0[i], w2p, a2p, k_k, k_a, eseg)
            terms3 = [t.reshape(nb, seq, g) for t in terms[:7]]
            vt = terms[7].reshape(nb, seq // SCAN_SUB, HEAD_DIM, g)
            return terms, terms3, vt

        terms_c, terms3_c, vt_c = rwkv_terms(pa_c, lc)
        yf_c, yb_c, state_c = wkv_scan(pa_c.reshape(nb, lc, 4 * g), *terms3_c, vt_c, eseg, zero_state)
        terms_l, terms3_l, vt_l = rwkv_terms(pa_l, l)
        yf_l, yb_l, _ = wkv_scan(pa_l.reshape(nb, l, 4 * g), *terms3_l, vt_l, eseg, state_c)
        a_l = rwkv_readout(yf_l.reshape(-1, HEAD_DIM, g), yb_l.reshape(-1, HEAD_DIM, g),
                           pa_l, terms_l[3], terms_l[6], r_k, g2p, gn_w, gn_b, eseg)

        bias = na_bias_table(na_rpb[i], rows)
        b_l = neighbourhood_attention(pb_l.reshape(nb, l, 3 * g), pb_c.reshape(nb, lc, 3 * g), bias).reshape(nb * l, g)

        q_gain = jnp.tile(gqa_q_gain[i], nh).reshape(1, g)
        k_gain = jnp.tile(gqa_k_gain[i], nkv).reshape(1, kvw)
        qc_l, kc_l, vc_l = qknorm_rope(pc_l, q_gain, k_gain, cos_q, sin_q, eseg, ekv, l, True)
        qc_c, kc_c, vc_c = qknorm_rope(pc_c, q_gain, k_gain, ones_c, zeros_c, eseg, ekv, lc, False)
        kc_c, vc_c = kc_c.reshape(nb, lc, kvw), vc_c.reshape(nb, lc, kvw)
        k_all = jnp.concatenate([kc_c, kc_l.reshape(nb, l, kvw)], axis=1)
        v_all = jnp.concatenate([vc_c, vc_l.reshape(nb, l, kvw)], axis=1)
        c_l = attention(qc_l, k_all, v_all, l)

        w_bd = jax.scipy.linalg.block_diag(*[pool_w[i, k] for k in range(len(POOL_WINDOWS))]).astype(BF16)
        p_scale = pool_scale[i].reshape(1, g)
        d_l = pool_mixer(pd_l.reshape(nb, l, g), w_bd, p_scale).reshape(nb * l, g)

        w_out4 = w_out[i].reshape(4, g, d).astype(BF16)
        w1 = mlp_w1[i].astype(BF16)
        w2 = mlp_w2[i].astype(BF16)
        xl = outproj((a_l, b_l, c_l, d_l), w_out4, xl, ml[2], l)
        xl = mlp(xl, g2, ml[3], ml[4], ml[5], w1, w2, l)

        if need_ctx_out:
            a_c = rwkv_readout(yf_c.reshape(-1, HEAD_DIM, g), yb_c.reshape(-1, HEAD_DIM, g),
                               pa_c, terms_c[3], terms_c[6], r_k, g2p, gn_w, gn_b, eseg)
            b_c = attention(pb_c[:, :g], pb_c[:, g:2 * g].reshape(nb, lc, g), pb_c[:, 2 * g:].reshape(nb, lc, g), lc)
            c_c = attention(qc_c, kc_c, vc_c, lc)
            d_c = pool_mixer(pd_c.reshape(nb, lc, g), w_bd, p_scale).reshape(nb * lc, g)
            xc = outproj((a_c, b_c, c_c, d_c), w_out4, xc, mc[2], nb * lc)
            xc = mlp(xc, g2, mc[3], mc[4], mc[5], w1, w2, nb * lc)

    return final_norm(xl, final_g.reshape(1, d)).reshape(nb, l, d)
```

```python
import functools

import jax
import jax.numpy as jnp
import numpy as np
from jax import lax
from jax.experimental import pallas as pl
from jax.experimental.pallas import tpu as pltpu

F32 = jnp.float32
BF16 = jnp.bfloat16

HEAD_DIM = 64
GRID_W = 64
NA_ROWS = 8
NA_COLS = 16
ROPE_THETA = 10000.0
POOL_WINDOWS = (2, 4, 8, 16)
NORM_EPS = 1e-6
RWKV_GN_EPS = 1e-5 * HEAD_DIM
N_MOD = 6
LANES = 128
SCAN_BLOCK = 128
SCAN_SUB = 64
ATTN_CHUNK = 2048
VMEM_LIMIT = 52 * 1024 * 1024


def _params(sem):
    return pltpu.CompilerParams(dimension_semantics=sem, vmem_limit_bytes=VMEM_LIMIT)


def _tile(n, pref):
    t = min(n, pref)
    assert n % t == 0, (n, pref)
    return t


def _dot(a, b):
    return jnp.dot(a, b, preferred_element_type=F32)


def _dot_nt(a, b):
    return lax.dot_general(a, b, (((1,), (1,)), ((), ())), preferred_element_type=F32)


def _seg_sum(x, e):
    hi = x.astype(BF16)
    lo = (x - hi.astype(F32)).astype(BF16)
    return _dot(hi, e) + _dot(lo, e)


def _norm_mod(x, g, shift, scale):
    ms = jnp.mean(x * x, axis=-1, keepdims=True)
    h = x * lax.rsqrt(ms + NORM_EPS) * g
    return h * (1.0 + scale) + shift


def _adaln_kernel(c_ref, w_ref, b_ref, o_ref):
    c = c_ref[...]
    s = c * jax.nn.sigmoid(c)
    o_ref[0] = jnp.dot(s, w_ref[0], preferred_element_type=F32, precision=lax.Precision.HIGHEST) + b_ref[0]


def adaln(cpad, ada_w, ada_b):
    depth, d, n = ada_w.shape
    tn = _tile(n, 1536)
    return pl.pallas_call(
        _adaln_kernel,
        grid=(depth, n // tn),
        in_specs=[pl.BlockSpec((8, d), lambda i, j: (0, 0)),
                  pl.BlockSpec((1, d, tn), lambda i, j: (i, 0, j)),
                  pl.BlockSpec((1, 1, tn), lambda i, j: (i, 0, j))],
        out_specs=pl.BlockSpec((1, 8, tn), lambda i, j: (i, 0, j)),
        out_shape=jax.ShapeDtypeStruct((depth, 8, n), F32),
        compiler_params=_params(("arbitrary", "arbitrary")),
        name="adaln",
    )(cpad, ada_w, ada_b.reshape(depth, 1, n))


def _inproj_kernel(x_ref, g_ref, sh_ref, sc_ref, w_ref, oa_ref, ob_ref, oc_ref, od_ref):
    h = _norm_mod(x_ref[...], g_ref[...], sh_ref[0], sc_ref[0]).astype(BF16)
    off = 0
    for o_ref in (oa_ref, ob_ref, oc_ref, od_ref):
        n = o_ref.shape[1]
        o_ref[...] = _dot(h, w_ref[:, off:off + n]).astype(o_ref.dtype)
        off += n


def inproj(x, g, shift, scale, w, rows_per_mod, widths, dtypes):
    r, d = x.shape
    tm = _tile(rows_per_mod, 512)
    tpb = rows_per_mod // tm
    modspec = pl.BlockSpec((1, 1, d), lambda i: (i // tpb, 0, 0))
    return pl.pallas_call(
        _inproj_kernel,
        grid=(r // tm,),
        in_specs=[pl.BlockSpec((tm, d), lambda i: (i, 0)),
                  pl.BlockSpec((1, d), lambda i: (0, 0)),
                  modspec, modspec,
                  pl.BlockSpec(w.shape, lambda i: (0, 0))],
        out_specs=[pl.BlockSpec((tm, n), lambda i: (i, 0)) for n in widths],
        out_shape=[jax.ShapeDtypeStruct((r, n), dt) for n, dt in zip(widths, dtypes)],
        compiler_params=_params(("arbitrary",)),
        name="inproj",
    )(x, g, shift, scale, w)


def _outproj_kernel(a_ref, b_ref, c_ref, d_ref, w_ref, x_ref, gate_ref, o_ref):
    acc = _dot(a_ref[...].astype(BF16), w_ref[0])
    acc += _dot(b_ref[...].astype(BF16), w_ref[1])
    acc += _dot(c_ref[...].astype(BF16), w_ref[2])
    acc += _dot(d_ref[...].astype(BF16), w_ref[3])
    o_ref[...] = x_ref[...] + gate_ref[0] * acc


def outproj(mix, w4, x, gate, rows_per_mod):
    r, d = x.shape
    gw = w4.shape[1]
    tm = _tile(rows_per_mod, 512)
    tpb = rows_per_mod // tm
    mixspec = pl.BlockSpec((tm, gw), lambda i: (i, 0))
    return pl.pallas_call(
        _outproj_kernel,
        grid=(r // tm,),
        in_specs=[mixspec, mixspec, mixspec, mixspec,
                  pl.BlockSpec(w4.shape, lambda i: (0, 0, 0)),
                  pl.BlockSpec((tm, d), lambda i: (i, 0)),
                  pl.BlockSpec((1, 1, d), lambda i: (i // tpb, 0, 0))],
        out_specs=pl.BlockSpec((tm, d), lambda i: (i, 0)),
        out_shape=jax.ShapeDtypeStruct((r, d), F32),
        compiler_params=_params(("arbitrary",)),
        name="outproj",
    )(*mix, w4, x, gate)


def _mlp_kernel(x_ref, g_ref, sh_ref, sc_ref, gate_ref, w1_ref, w2_ref, o_ref, h_scr, acc_scr):
    j = pl.program_id(1)

    @pl.when(j == 0)
    def _():
        h_scr[...] = _norm_mod(x_ref[...], g_ref[...], sh_ref[0], sc_ref[0]).astype(BF16)
        acc_scr[...] = jnp.zeros_like(acc_scr)

    u = jnp.maximum(_dot(h_scr[...], w1_ref[...]), 0.0)
    acc_scr[...] += _dot((u * u).astype(BF16), w2_ref[...])

    @pl.when(j == pl.num_programs(1) - 1)
    def _():
        o_ref[...] = x_ref[...] + gate_ref[0] * acc_scr[...]


def mlp(x, g, shift, scale, gate, w1, w2, rows_per_mod):
    r, d = x.shape
    dff = w1.shape[1]
    tm = _tile(rows_per_mod, 1024)
    tf = _tile(dff, 1024)
    tpb = rows_per_mod // tm
    modspec = pl.BlockSpec((1, 1, d), lambda i, j: (i // tpb, 0, 0))
    return pl.pallas_call(
        _mlp_kernel,
        grid=(r // tm, dff // tf),
        in_specs=[pl.BlockSpec((tm, d), lambda i, j: (i, 0)),
                  pl.BlockSpec((1, d), lambda i, j: (0, 0)),
                  modspec, modspec, modspec,
                  pl.BlockSpec((d, tf), lambda i, j: (0, j)),
                  pl.BlockSpec((tf, d), lambda i, j: (j, 0))],
        out_specs=pl.BlockSpec((tm, d), lambda i, j: (i, 0)),
        out_shape=jax.ShapeDtypeStruct((r, d), F32),
        scratch_shapes=[pltpu.VMEM((tm, d), BF16), pltpu.VMEM((tm, d), F32)],
        compiler_params=_params(("arbitrary", "arbitrary")),
        name="mlp",
    )(x, g, shift, scale, gate, w1, w2)


def _final_norm_kernel(x_ref, g_ref, o_ref):
    x = x_ref[...]
    ms = jnp.mean(x * x, axis=-1, keepdims=True)
    o_ref[...] = x * lax.rsqrt(ms + NORM_EPS) * g_ref[...]


def final_norm(x, g):
    r, d = x.shape
    tm = _tile(r, 1024)
    return pl.pallas_call(
        _final_norm_kernel,
        grid=(r // tm,),
        in_specs=[pl.BlockSpec((tm, d), lambda i: (i, 0)), pl.BlockSpec((1, d), lambda i: (0, 0))],
        out_specs=pl.BlockSpec((tm, d), lambda i: (i, 0)),
        out_shape=jax.ShapeDtypeStruct((r, d), F32),
        compiler_params=_params(("arbitrary",)),
        name="final_norm",
    )(x, g)


def _swap_halves(y):
    n = y.shape[-1]
    lane = lax.broadcasted_iota(jnp.int32, y.shape, 1)
    half = HEAD_DIM // 2
    return jnp.where(lane % HEAD_DIM < half, pltpu.roll(y, n - half, 1), pltpu.roll(y, half, 1))


def _qknorm_rope_kernel(p_ref, qg_ref, kg_ref, cos_ref, sin_ref, eq_ref, ek_ref, q_ref, k_ref, v_ref, *, rope):
    gq = q_ref.shape[1]
    gk = k_ref.shape[1]
    p = p_ref[...]

    def normed(x, gain, e):
        ms = _seg_sum(x * x, e) * (1.0 / HEAD_DIM)
        return x * lax.rsqrt(ms + NORM_EPS) * gain

    q = normed(p[:, :gq], qg_ref[...], eq_ref[...])
    k = normed(p[:, gq:gq + gk], kg_ref[...], ek_ref[...])
    if rope:
        cos = cos_ref[...]
        sin = sin_ref[...]
        q = q * cos + _swap_halves(q) * sin
        k = k * cos[:, :gk] + _swap_halves(k) * sin[:, :gk]
    q_ref[...] = (q * HEAD_DIM ** -0.5).astype(BF16)
    k_ref[...] = k.astype(BF16)
    v_ref[...] = p[:, gq + gk:].astype(BF16)


def qknorm_rope(pc, q_gain, k_gain, cos, sin, eq, ek, seq_len, rope):
    r, n = pc.shape
    gq = n // 2
    gk = n // 4
    tm = _tile(seq_len, 512)
    nseq = seq_len // tm
    return pl.pallas_call(
        functools.partial(_qknorm_rope_kernel, rope=rope),
        grid=(r // tm,),
        in_specs=[pl.BlockSpec((tm, n), lambda i: (i, 0)),
                  pl.BlockSpec((1, gq), lambda i: (0, 0)),
                  pl.BlockSpec((1, gk), lambda i: (0, 0)),
                  pl.BlockSpec((tm, gq), lambda i: (i % nseq, 0)),
                  pl.BlockSpec((tm, gq), lambda i: (i % nseq, 0)),
                  pl.BlockSpec(eq.shape, lambda i: (0, 0)),
                  pl.BlockSpec(ek.shape, lambda i: (0, 0))],
        out_specs=[pl.BlockSpec((tm, gq), lambda i: (i, 0)),
                   pl.BlockSpec((tm, gk), lambda i: (i, 0)),
                   pl.BlockSpec((tm, gk), lambda i: (i, 0))],
        out_shape=[jax.ShapeDtypeStruct((r, gq), BF16),
                   jax.ShapeDtypeStruct((r, gk), BF16),
                   jax.ShapeDtypeStruct((r, gk), BF16)],
        compiler_params=_params(("arbitrary",)),
        name="qknorm_rope",
    )(pc, q_gain, k_gain, cos, sin, eq, ek)


def _pad_heads(q, n_heads, kv_width, rep):
    assert kv_width % LANES == 0
    t = q.shape[0]
    lane = lax.broadcasted_iota(jnp.int32, (t, LANES), 1)
    lo_half = lane < HEAD_DIM
    zero = jnp.zeros((t, LANES), q.dtype)
    rows = []
    for h in range(n_heads):
        src = h * HEAD_DIM
        dst = (h // rep) * HEAD_DIM
        piece = q[:, src // LANES * LANES:(src // LANES + 1) * LANES]
        if src % LANES != dst % LANES:
            piece = pltpu.roll(piece.astype(F32), HEAD_DIM, 1).astype(q.dtype)
        piece = jnp.where(lo_half if dst % LANES == 0 else ~lo_half, piece, zero)
        blocks = [piece if j == dst // LANES else zero for j in range(kv_width // LANES)]
        rows.append(jnp.concatenate(blocks, axis=1) if len(blocks) > 1 else piece)
    return jnp.concatenate(rows, axis=0)


def _gather_heads(res, n_heads, rep):
    t = res.shape[0] // n_heads
    lane = lax.broadcasted_iota(jnp.int32, (t, LANES), 1)
    lo_half = lane < HEAD_DIM
    pieces = []
    for h in range(n_heads):
        src = (h // rep) * HEAD_DIM
        dst = h * HEAD_DIM
        piece = res[h * t:(h + 1) * t, src // LANES * LANES:(src // LANES + 1) * LANES]
        if src % LANES != dst % LANES:
            piece = pltpu.roll(piece, HEAD_DIM, 1)
        pieces.append(piece)
    blocks = [jnp.where(lo_half, pieces[2 * j], pieces[2 * j + 1]) for j in range(n_heads // 2)]
    return jnp.concatenate(blocks, axis=1)


def _attn_kernel(q_ref, k_ref, v_ref, o_ref, *, rep):
    n_heads = q_ref.shape[1] // HEAD_DIM
    lk = k_ref.shape[1]
    q = _pad_heads(q_ref[...], n_heads, k_ref.shape[2], rep)
    edges = list(range(0, lk, ATTN_CHUNK)) + [lk]
    m = l = acc = None
    for k0, k1 in zip(edges[:-1], edges[1:]):
        s = _dot_nt(q, k_ref[0, k0:k1, :])
        m_new = jnp.max(s, axis=-1, keepdims=True)
        if m is not None:
            m_new = jnp.maximum(m, m_new)
        p = jnp.exp(s - m_new)
        l_new = jnp.sum(p, axis=-1, keepdims=True)
        acc_new = _dot(p.astype(BF16), v_ref[0, k0:k1, :])
        if m is not None:
            alpha = jnp.exp(m - m_new)
            l_new = alpha * l + l_new
            acc_new = alpha * acc + acc_new
        m, l, acc = m_new, l_new, acc_new
    o_ref[...] = _gather_heads(acc / l, n_heads, rep)


def attention(q, k, v, seq_len):
    r, gq = q.shape
    nb, lk, kw = k.shape
    rep = gq // kw
    tq = _tile(seq_len, 128)
    nq = seq_len // tq
    return pl.pallas_call(
        functools.partial(_attn_kernel, rep=rep),
        grid=(nb, nq),
        in_specs=[pl.BlockSpec((tq, gq), lambda i, j: (i * nq + j, 0)),
                  pl.BlockSpec((1, lk, kw), lambda i, j: (i, 0, 0)),
                  pl.BlockSpec((1, lk, kw), lambda i, j: (i, 0, 0))],
        out_specs=pl.BlockSpec((tq, gq), lambda i, j: (i * nq + j, 0)),
        out_shape=jax.ShapeDtypeStruct((r, gq), F32),
        compiler_params=_params(("arbitrary", "arbitrary")),
        name="attention",
    )(q, k, v)


def _na_kernel(p_ref, pc_ref, bias_ref, o_ref, *, rows, kh):
    g = o_ref.shape[2]
    n_heads = g // HEAD_DIM
    band = kh * GRID_W
    kc = pc_ref[0, :, g:2 * g]
    vc = pc_ref[0, :, 2 * g:3 * g]

    def one_row(r, carry):
        rs = jnp.clip(r - kh // 2, 0, rows - kh)
        q0 = pl.multiple_of(r * GRID_W, GRID_W)
        k0 = pl.multiple_of(rs * GRID_W, GRID_W)
        q = _pad_heads(p_ref[0, pl.ds(q0, GRID_W), 0:g], n_heads, g, 1)
        s1 = _dot_nt(q, p_ref[0, pl.ds(k0, band), g:2 * g]) + bias_ref[r - rs]
        s2 = _dot_nt(q, kc)
        m = jnp.maximum(jnp.max(s1, axis=-1, keepdims=True), jnp.max(s2, axis=-1, keepdims=True))
        p1 = jnp.exp(s1 - m)
        p2 = jnp.exp(s2 - m)
        l = jnp.sum(p1, axis=-1, keepdims=True) + jnp.sum(p2, axis=-1, keepdims=True)
        res = (_dot(p1.astype(BF16), p_ref[0, pl.ds(k0, band), 2 * g:3 * g]) + _dot(p2.astype(BF16), vc)) / l
        o_ref[0, pl.ds(q0, GRID_W), :] = _gather_heads(res, n_heads, 1)
        return carry

    lax.fori_loop(0, rows, one_row, 0, unroll=8)


def neighbourhood_attention(p, pc, bias):
    nb, l, g3 = p.shape
    g = g3 // 3
    lc = pc.shape[1]
    rows = l // GRID_W
    kh = min(NA_ROWS, rows)
    return pl.pallas_call(
        functools.partial(_na_kernel, rows=rows, kh=kh),
        grid=(nb,),
        in_specs=[pl.BlockSpec((1, l, g3), lambda i: (i, 0, 0)),
                  pl.BlockSpec((1, lc, g3), lambda i: (i, 0, 0)),
                  pl.BlockSpec(bias.shape, lambda i: (0, 0, 0))],
        out_specs=pl.BlockSpec((1, l, g), lambda i: (i, 0, 0)),
        out_shape=jax.ShapeDtypeStruct((nb, l, g), F32),
        compiler_params=_params(("arbitrary",)),
        name="neighbourhood_attention",
    )(p, pc, bias)


def _na_bias_kernel(rpb_ref, o_ref, *, kh):
    h = pl.program_id(0)
    q = lax.broadcasted_iota(jnp.int32, (GRID_W, GRID_W), 0)
    k = lax.broadcasted_iota(jnp.int32, (GRID_W, GRID_W), 1)
    start = jnp.clip(q - NA_COLS // 2, 0, GRID_W - NA_COLS)
    in_win = (k >= start) & (k < start + NA_COLS)
    off = k - q + NA_COLS - 1
    neg = jnp.full((GRID_W, GRID_W), -jnp.inf, F32)
    blocks = []
    for ro in range(2 * NA_ROWS - 1):
        t = neg
        for c in range(2 * NA_COLS - 1):
            t = jnp.where(off == c, rpb_ref[h, ro, c], t)
        blocks.append(jnp.where(in_win, t, neg))
    for di in range(kh):
        for i in range(kh):
            o_ref[di, :, i * GRID_W:(i + 1) * GRID_W] = blocks[i - di + NA_ROWS - 1]


def na_bias_table(rpb, rows):
    kh = min(NA_ROWS, rows)
    nh = rpb.shape[0]
    return pl.pallas_call(
        functools.partial(_na_bias_kernel, kh=kh),
        grid=(nh,),
        in_specs=[pl.BlockSpec(memory_space=pltpu.SMEM)],
        out_specs=pl.BlockSpec((kh, GRID_W, kh * GRID_W), lambda i: (0, i, 0)),
        out_shape=jax.ShapeDtypeStruct((kh, nh * GRID_W, kh * GRID_W), F32),
        compiler_params=_params(("arbitrary",)),
        name="na_bias",
    )(rpb)


def _shift_rows(x, d, t):
    n = x.shape[0]
    y = pltpu.roll(x, d % n, 0)
    src = t - d
    return jnp.where((src >= 0) & (src < n), y, 0.0)


def _pool_kernel(x_ref, w_ref, scale_ref, o_ref):
    x = x_ref[0]
    n, g = x.shape
    pg = g // len(POOL_WINDOWS)
    t = lax.broadcasted_iota(jnp.int32, x.shape, 0)
    group = lax.broadcasted_iota(jnp.int32, x.shape, 1) // pg
    fwd = x
    bwd = x
    cur = 1
    total = jnp.zeros_like(x)
    count = jnp.ones_like(x)
    for j, w in enumerate(POOL_WINDOWS):
        half = w // 2
        while cur < half:
            fwd = fwd + _shift_rows(fwd, -cur, t)
            bwd = bwd + _shift_rows(bwd, cur, t)
            cur *= 2
        win = _shift_rows(bwd, 1, t) + fwd
        lo = jnp.clip(t - half, 0, n)
        hi = jnp.clip(t - half + w, 0, n)
        total = jnp.where(group == j, win, total)
        count = jnp.where(group == j, (hi - lo).astype(F32), count)
    diff = total / count - x
    o_ref[0] = _dot(diff.astype(BF16), w_ref[...]) * scale_ref[...]


def pool_mixer(p, w_bd, scale):
    b, n, g = p.shape
    return pl.pallas_call(
        _pool_kernel,
        grid=(b,),
        in_specs=[pl.BlockSpec((1, n, g), lambda i: (i, 0, 0)),
                  pl.BlockSpec((g, g), lambda i: (0, 0)),
                  pl.BlockSpec((1, g), lambda i: (0, 0))],
        out_specs=pl.BlockSpec((1, n, g), lambda i: (i, 0, 0)),
        out_shape=jax.ShapeDtypeStruct((b, n, g), F32),
        compiler_params=_params(("arbitrary",)),
        name="pool_mixer",
    )(p, w_bd, scale)


def _to_tiles(x):
    g = x.shape[1]
    tiles = []
    for s in range(x.shape[0] // SCAN_SUB):
        blk = x[s * SCAN_SUB:(s + 1) * SCAN_SUB, :]
        tiles.append(jnp.concatenate([blk[:, h * HEAD_DIM:(h + 1) * HEAD_DIM].T for h in range(g // HEAD_DIM)], axis=1))
    return tiles


def _from_tiles(tiles):
    g = tiles[0].shape[1]
    return jnp.concatenate(
        [jnp.concatenate([t[:, h * SCAN_SUB:(h + 1) * SCAN_SUB].T for h in range(g // HEAD_DIM)], axis=1) for t in tiles],
        axis=0)


def _rwkv_prep_kernel(p_ref, w0_ref, a0_ref, w2_ref, a2_ref, kk_w_ref, ka_ref, e_ref,
                      kk_ref, decf_ref, bf_ref, kmf_ref, decb_ref, bb_ref, kmb_ref, vt_ref):
    g = kk_ref.shape[1]
    k = p_ref[:, g:2 * g]
    lr = p_ref[:, 3 * g:4 * g]
    for s, tile in enumerate(_to_tiles(p_ref[:, 2 * g:3 * g])):
        vt_ref[s] = tile
    kx = k * kk_w_ref[...]
    norm = jnp.sqrt(_seg_sum(kx * kx, e_ref[...]))
    kk = kx / jnp.maximum(norm, 1e-12)
    kk_ref[...] = kk
    lr_t = jnp.tanh(lr).astype(BF16)
    lr_b = lr.astype(BF16)
    for d, (dec_ref, b_ref, km_ref) in enumerate(((decf_ref, bf_ref, kmf_ref), (decb_ref, bb_ref, kmb_ref))):
        z = w0_ref[d:d + 1, :] + _dot(lr_t, w2_ref[d])
        softplus_neg = jnp.maximum(-z, 0.0) + jnp.log1p(jnp.exp(-jnp.abs(z)))
        w = -softplus_neg - 0.5
        dec_ref[...] = jnp.exp(-jnp.exp(w))
        a = jax.nn.sigmoid(a0_ref[d:d + 1, :] + _dot(lr_b, a2_ref[d]))
        b_ref[...] = kk * a
        km_ref[...] = k * (1.0 + (a - 1.0) * ka_ref[...])


def rwkv_prep(pa, w0, a0, w2p, a2p, k_k, k_a, eseg):
    r, n = pa.shape
    g = n // 4
    tm = _tile(r, 512)
    full2 = lambda shape: pl.BlockSpec(shape, lambda i: (0,) * len(shape))
    out = pl.BlockSpec((tm, g), lambda i: (i, 0))
    return pl.pallas_call(
        _rwkv_prep_kernel,
        grid=(r // tm,),
        in_specs=[pl.BlockSpec((tm, n), lambda i: (i, 0)),
                  full2(w0.shape), full2(a0.shape), full2(w2p.shape), full2(a2p.shape),
                  full2(k_k.shape), full2(k_a.shape), full2(eseg.shape)],
        out_specs=[out] * 7 + [pl.BlockSpec((tm // SCAN_SUB, HEAD_DIM, g), lambda i: (i, 0, 0))],
        out_shape=[jax.ShapeDtypeStruct((r, g), F32)] * 7 + [jax.ShapeDtypeStruct((r // SCAN_SUB, HEAD_DIM, g), F32)],
        compiler_params=_params(("arbitrary",)),
        name="rwkv_prep",
    )(pa, w0, a0, w2p, a2p, k_k, k_a, eseg)


def _wkv_scan_kernel(rf_ref, rb_ref, kkf_ref, kkb_ref, decf_ref, bf_ref, kmf_ref, decb_ref, bb_ref, kmb_ref,
                     vtf_ref, vtb_ref, e_ref, s0_ref, yf_ref, yb_ref, send_ref, s_scr, sr_scr):
    nb, t_blk, g = rf_ref.shape
    step_id = pl.program_id(0)

    @pl.when(step_id == 0)
    def _():
        s_scr[...] = s0_ref[...]

    yf_ref[...] = jnp.zeros_like(yf_ref)
    yb_ref[...] = jnp.zeros_like(yb_ref)
    sr_scr[...] = jnp.zeros_like(sr_scr)
    e = e_ref[...]
    lane = lax.broadcasted_iota(jnp.int32, (HEAD_DIM, g), 1) % SCAN_SUB
    lane128 = lax.broadcasted_iota(jnp.int32, (HEAD_DIM, LANES), 1)
    n = nb * HEAD_DIM
    dirs = ((rf_ref, kkf_ref, decf_ref, bf_ref, kmf_ref, vtf_ref, yf_ref),
            (rb_ref, kkb_ref, decb_ref, bb_ref, kmb_ref, vtb_ref, yb_ref))

    def write_y(d, bs, y, tt_done, valid):
        y_ref = dirs[d][-1]
        tl = tt_done if d == 0 else t_blk - 1 - tt_done
        tl = jnp.clip(tl, 0, t_blk - 1)
        hit = (lane == tl % SCAN_SUB) & valid
        for i, b in enumerate(bs):
            pltpu.store(y_ref.at[b, tl // SCAN_SUB], y[i * HEAD_DIM:(i + 1) * HEAD_DIM], mask=hit)

    groups = [tuple(range(i, min(i + 2, nb))) for i in range(0, nb, 2)]

    def one_token(tt, carry):
        for d, (r_ref, kk_ref, dec_ref, b_ref, km_ref, vt_ref, y_ref) in enumerate(dirs):
            tl = tt if d == 0 else t_blk - 1 - tt
            row = lambda ref, b: jnp.broadcast_to(ref[b, pl.ds(tl, 1), :], (HEAD_DIM, g))
            blk = lambda x, i: x[i * HEAD_DIM:(i + 1) * HEAD_DIM]
            idx = (lane128 // SCAN_SUB) * SCAN_SUB + tl % SCAN_SUB
            for bs in groups:
                m = len(bs) * HEAD_DIM
                s_old = [s_scr[d, b] for b in bs]
                x = jnp.concatenate([s_old[i].astype(BF16) * row(kk_ref, b).astype(BF16) for i, b in enumerate(bs)], axis=0)
                out = _dot(jnp.concatenate([x, sr_scr[d, bs[0] * HEAD_DIM:(bs[-1] + 1) * HEAD_DIM, :]], axis=0), e)
                write_y(d, bs, out[m:], tt - 1, tt > 0)
                for i, b in enumerate(bs):
                    vcol = jnp.concatenate([jnp.take_along_axis(vt_ref[b, tl // SCAN_SUB][:, j * LANES:(j + 1) * LANES], idx, axis=1)
                                            for j in range(g // LANES)], axis=1)
                    s = s_old[i] * row(dec_ref, b) - blk(out, i) * row(b_ref, b) + vcol * row(km_ref, b)
                    s_scr[d, b] = s
                    sr_scr[d, b * HEAD_DIM:(b + 1) * HEAD_DIM, :] = s.astype(BF16) * row(r_ref, b).astype(BF16)
        return carry

    lax.fori_loop(0, t_blk, one_token, 0, unroll=8)
    for d in range(2):
        write_y(d, tuple(range(nb)), _dot(sr_scr[d], e), t_blk - 1, True)

    @pl.when(step_id == pl.num_programs(0) - 1)
    def _():
        send_ref[...] = s_scr[...]


def wkv_scan(pa3, kk, dec_f, b_f, km_f, dec_b, b_b, km_b, vt, eseg, s0):
    nb, l, g = kk.shape
    t_blk = SCAN_BLOCK
    nsub = t_blk // SCAN_SUB
    nblk = l // t_blk
    assert l % t_blk == 0
    seq_f = pl.BlockSpec((nb, t_blk, g), lambda i: (0, i, 0))
    seq_b = pl.BlockSpec((nb, t_blk, g), lambda i: (0, nblk - 1 - i, 0))
    vt_f = pl.BlockSpec((nb, nsub, HEAD_DIM, g), lambda i: (0, i, 0, 0))
    vt_b = pl.BlockSpec((nb, nsub, HEAD_DIM, g), lambda i: (0, nblk - 1 - i, 0, 0))
    state = pl.BlockSpec(s0.shape, lambda i: (0, 0, 0, 0))
    return pl.pallas_call(
        _wkv_scan_kernel,
        grid=(nblk,),
        in_specs=[seq_f, seq_b, seq_f, seq_b, seq_f, seq_f, seq_f, seq_b, seq_b, seq_b,
                  vt_f, vt_b, pl.BlockSpec(eseg.shape, lambda i: (0, 0)), state],
        out_specs=[vt_f, vt_b, state],
        out_shape=[jax.ShapeDtypeStruct(vt.shape, F32), jax.ShapeDtypeStruct(vt.shape, F32),
                   jax.ShapeDtypeStruct(s0.shape, F32)],
        scratch_shapes=[pltpu.VMEM(s0.shape, F32), pltpu.VMEM((2, nb * HEAD_DIM, g), BF16)],
        compiler_params=_params(("arbitrary",)),
        name="wkv_scan",
    )(pa3, pa3, kk, kk, dec_f, b_f, km_f, dec_b, b_b, km_b, vt, vt, eseg, s0)


def _rwkv_readout_kernel(yf_ref, yb_ref, p_ref, kmf_ref, kmb_ref, rk_ref, g2_ref, gnw_ref, gnb_ref, e_ref, o_ref):
    g = o_ref.shape[1]
    e = e_ref[...]
    r = p_ref[:, :g]
    v = p_ref[:, 2 * g:3 * g]
    lr = p_ref[:, 3 * g:4 * g]
    y = _from_tiles([yf_ref[s] + yb_ref[s] for s in range(yf_ref.shape[0])])
    mu = _seg_sum(y, e) * (1.0 / HEAD_DIM)
    yc = y - mu
    var = _seg_sum(yc * yc, e) * (1.0 / HEAD_DIM)
    yn = yc * lax.rsqrt(var + RWKV_GN_EPS) * gnw_ref[...] + gnb_ref[...]
    bonus = _seg_sum(r * (kmf_ref[...] + kmb_ref[...]) * rk_ref[...], e) * v
    gate = _dot(jax.nn.sigmoid(lr).astype(BF16), g2_ref[...])
    o_ref[...] = (yn + bonus) * gate


def rwkv_readout(yf, yb, pa, km_f, km_b, r_k, g2p, gn_w, gn_b, eseg):
    r, g = km_f.shape
    tm = _tile(r, 512)
    row = pl.BlockSpec((tm, g), lambda i: (i, 0))
    tiles = pl.BlockSpec((tm // SCAN_SUB, HEAD_DIM, g), lambda i: (i, 0, 0))
    full2 = lambda shape: pl.BlockSpec(shape, lambda i: (0,) * len(shape))
    return pl.pallas_call(
        _rwkv_readout_kernel,
        grid=(r // tm,),
        in_specs=[tiles, tiles, pl.BlockSpec((tm, 4 * g), lambda i: (i, 0)), row, row,
                  full2(r_k.shape), full2(g2p.shape), full2(gn_w.shape), full2(gn_b.shape), full2(eseg.shape)],
        out_specs=row,
        out_shape=jax.ShapeDtypeStruct((r, g), F32),
        compiler_params=_params(("arbitrary",)),
        name="rwkv_readout",
    )(yf, yb, pa, km_f, km_b, r_k, g2p, gn_w, gn_b, eseg)


def _block_ones(n, seg):
    idx = np.arange(n) // seg
    return jnp.asarray(idx[:, None] == idx[None, :], dtype=BF16)


def _padded_rows(w, offset, n):
    return jnp.zeros((n, w.shape[1]), w.dtype).at[offset:offset + w.shape[0]].set(w)


def _rope_tables(n_tokens, n_rep):
    t = jnp.arange(n_tokens)
    row = (t // GRID_W).astype(F32)
    col = (t % GRID_W).astype(F32)
    n_freq = HEAD_DIM // 4
    inv_freq = ROPE_THETA ** (-jnp.arange(n_freq, dtype=F32) / n_freq)
    ang = jnp.concatenate([row[:, None] * inv_freq, col[:, None] * inv_freq], axis=-1)
    cos, sin = jnp.cos(ang), jnp.sin(ang)
    cos_h = jnp.concatenate([cos, cos], axis=-1)
    sin_h = jnp.concatenate([-sin, sin], axis=-1)
    return jnp.tile(cos_h, (1, n_rep)), jnp.tile(sin_h, (1, n_rep))


def kernel(x, c, ctx, c_ctx, ada_w, ada_b, norm1_g, norm2_g, w_in, w_out, rwkv_w0, rwkv_w2, rwkv_a0, rwkv_a2,
           rwkv_k_k, rwkv_k_a, rwkv_r_k, rwkv_g2, rwkv_gn_w, rwkv_gn_b, na_rpb, gqa_q_gain, gqa_k_gain, pool_w,
           pool_scale, mlp_w1, mlp_w2, final_g):
    nb, l, d = x.shape
    lc = ctx.shape[1]
    depth = ada_w.shape[0]
    g = d // 4
    nh = g // HEAD_DIM
    nkv = nh // 2
    kvw = nkv * HEAD_DIM
    dr = rwkv_w2.shape[2]
    ir = rwkv_a2.shape[2]
    gr = rwkv_g2.shape[1]
    assert 2 * dr + 2 * ir + gr <= g and nb + 1 <= 8
    rows = l // GRID_W

    cpad = jnp.zeros((8, d), F32).at[:nb].set(c).at[nb].set(c_ctx)
    mods = adaln(cpad, ada_w, ada_b)

    eseg = _block_ones(g, HEAD_DIM)
    ekv = _block_ones(kvw, HEAD_DIM)
    cos_q, sin_q = _rope_tables(l, nh)
    ones_c, zeros_c = jnp.ones((lc, g), F32), jnp.zeros((lc, g), F32)

    splits = np.cumsum([0, g, g, g, dr, dr, ir, ir, gr, g, g, g, g, kvw, kvw, g])
    lowrank_w = splits[8] - splits[3]
    widths = (4 * g, 3 * g, 2 * g, g)
    proj_dtypes = (F32, BF16, F32, F32)

    xl = x.reshape(nb * l, d)
    xc = ctx.reshape(nb * lc, d)
    zero_state = jnp.zeros((2, nb, HEAD_DIM, g), F32)

    for i in range(depth):
        need_ctx_out = i < depth - 1
        mod_l = mods[i, :nb].reshape(nb, N_MOD, 1, d)
        mod_c = mods[i, nb].reshape(N_MOD, 1, 1, d)
        ml = [mod_l[:, k] for k in range(N_MOD)]
        mc = [mod_c[k] for k in range(N_MOD)]

        wi = w_in[i]
        w_inp = jnp.concatenate([wi[:, :splits[8]], jnp.zeros((d, g - lowrank_w), F32),
                                 wi[:, splits[8]:splits[9]] * HEAD_DIM ** -0.5, wi[:, splits[9]:]], axis=1).astype(BF16)
        g1 = norm1_g[i].reshape(1, d)
        g2 = norm2_g[i].reshape(1, d)
        pa_l, pb_l, pc_l, pd_l = inproj(xl, g1, ml[0], ml[1], w_inp, l, widths, proj_dtypes)
        pa_c, pb_c, pc_c, pd_c = inproj(xc, g1, mc[0], mc[1], w_inp, nb * lc, widths, proj_dtypes)

        w2p = jnp.stack([_padded_rows(rwkv_w2[i, 0], 0, g), _padded_rows(rwkv_w2[i, 1], dr, g)]).astype(BF16)
        a2p = jnp.stack([_padded_rows(rwkv_a2[i, 0], 2 * dr, g),
                         _padded_rows(rwkv_a2[i, 1], 2 * dr + ir, g)]).astype(BF16)
        g2p = _padded_rows(rwkv_g2[i], 2 * dr + 2 * ir, g).astype(BF16)
        k_k = rwkv_k_k[i].reshape(1, g)
        k_a = rwkv_k_a[i].reshape(1, g)
        r_k = rwkv_r_k[i].reshape(1, g)
        gn_w = rwkv_gn_w[i].reshape(1, g)
        gn_b = rwkv_gn_b[i].reshape(1, g)

        def rwkv_terms(pa, seq):
            terms = rwkv_prep(pa, rwkv_w0[i], rwkv_a0[i], w2p, a2p, k_k, k_a, eseg)
            terms3 = [t.reshape(nb, seq, g) for t in terms[:7]]
            vt = terms[7].reshape(nb, seq // SCAN_SUB, HEAD_DIM, g)
            return terms, terms3, vt

        terms_c, terms3_c, vt_c = rwkv_terms(pa_c, lc)
        yf_c, yb_c, state_c = wkv_scan(pa_c.reshape(nb, lc, 4 * g), *terms3_c, vt_c, eseg, zero_state)
        terms_l, terms3_l, vt_l = rwkv_terms(pa_l, l)
        yf_l, yb_l, _ = wkv_scan(pa_l.reshape(nb, l, 4 * g), *terms3_l, vt_l, eseg, state_c)
        a_l = rwkv_readout(yf_l.reshape(-1, HEAD_DIM, g), yb_l.reshape(-1, HEAD_DIM, g),
                           pa_l, terms_l[3], terms_l[6], r_k, g2p, gn_w, gn_b, eseg)

        bias = na_bias_table(na_rpb[i], rows)
        b_l = neighbourhood_attention(pb_l.reshape(nb, l, 3 * g), pb_c.reshape(nb, lc, 3 * g), bias).reshape(nb * l, g)

        q_gain = jnp.tile(gqa_q_gain[i], nh).reshape(1, g)
        k_gain = jnp.tile(gqa_k_gain[i], nkv).reshape(1, kvw)
        qc_l, kc_l, vc_l = qknorm_rope(pc_l, q_gain, k_gain, cos_q, sin_q, eseg, ekv, l, True)
        qc_c, kc_c, vc_c = qknorm_rope(pc_c, q_gain, k_gain, ones_c, zeros_c, eseg, ekv, lc, False)
        kc_c, vc_c = kc_c.reshape(nb, lc, kvw), vc_c.reshape(nb, lc, kvw)
        k_all = jnp.concatenate([kc_c, kc_l.reshape(nb, l, kvw)], axis=1)
        v_all = jnp.concatenate([vc_c, vc_l.reshape(nb, l, kvw)], axis=1)
        c_l = attention(qc_l, k_all, v_all, l)

        w_bd = jax.scipy.linalg.block_diag(*[pool_w[i, k] for k in range(len(POOL_WINDOWS))]).astype(BF16)
        p_scale = pool_scale[i].reshape(1, g)
        d_l = pool_mixer(pd_l.reshape(nb, l, g), w_bd, p_scale).reshape(nb * l, g)

        w_out4 = w_out[i].reshape(4, g, d).astype(BF16)
        w1 = mlp_w1[i].astype(BF16)
        w2 = mlp_w2[i].astype(BF16)
        xl = outproj((a_l, b_l, c_l, d_l), w_out4, xl, ml[2], l)
        xl = mlp(xl, g2, ml[3], ml[4], ml[5], w1, w2, l)

        if need_ctx_out:
            a_c = rwkv_readout(yf_c.reshape(-1, HEAD_DIM, g), yb_c.reshape(-1, HEAD_DIM, g),
                               pa_c, terms_c[3], terms_c[6], r_k, g2p, gn_w, gn_b, eseg)
            b_c = attention(pb_c[:, :g], pb_c[:, g:2 * g].reshape(nb, lc, g), pb_c[:, 2 * g:].reshape(nb, lc, g), lc)
            c_c = attention(qc_c, kc_c, vc_c, lc)
            d_c = pool_mixer(pd_c.reshape(nb, lc, g), w_bd, p_scale).reshape(nb * lc, g)
            xc = outproj((a_c, b_c, c_c, d_c), w_out4, xc, mc[2], nb * lc)
            xc = mlp(xc, g2, mc[3], mc[4], mc[5], w1, w2, nb * lc)

    return final_norm(xl, final_g.reshape(1, d)).reshape(nb, l, d)
```

```python
import functools

import jax
import jax.numpy as jnp
import numpy as np
from jax import lax
from jax.experimental import pallas as pl
from jax.experimental.pallas import tpu as pltpu

F32 = jnp.float32
BF16 = jnp.bfloat16

HEAD_DIM = 64
GRID_W = 64
NA_ROWS = 8
NA_COLS = 16
ROPE_THETA = 10000.0
POOL_WINDOWS = (2, 4, 8, 16)
NORM_EPS = 1e-6
RWKV_GN_EPS = 1e-5 * HEAD_DIM
N_MOD = 6
LANES = 128
SCAN_BLOCK = 128
SCAN_SUB = 64
ATTN_CHUNK = 2048
VMEM_LIMIT = 52 * 1024 * 1024


def _params(sem):
    return pltpu.CompilerParams(dimension_semantics=sem, vmem_limit_bytes=VMEM_LIMIT)


def _tile(n, pref):
    t = min(n, pref)
    assert n % t == 0, (n, pref)
    return t


def _dot(a, b):
    return jnp.dot(a, b, preferred_element_type=F32)


def _dot_nt(a, b):
    return lax.dot_general(a, b, (((1,), (1,)), ((), ())), preferred_element_type=F32)


def _seg_sum(x, e):
    hi = x.astype(BF16)
    lo = (x - hi.astype(F32)).astype(BF16)
    return _dot(hi, e) + _dot(lo, e)


def _norm_mod(x, g, shift, scale):
    ms = jnp.mean(x * x, axis=-1, keepdims=True)
    h = x * lax.rsqrt(ms + NORM_EPS) * g
    return h * (1.0 + scale) + shift


def _adaln_kernel(c_ref, w_ref, b_ref, o_ref):
    c = c_ref[...]
    s = c * jax.nn.sigmoid(c)
    o_ref[0] = jnp.dot(s, w_ref[0], preferred_element_type=F32, precision=lax.Precision.HIGHEST) + b_ref[0]


def adaln(cpad, ada_w, ada_b):
    depth, d, n = ada_w.shape
    tn = _tile(n, 1536)
    return pl.pallas_call(
        _adaln_kernel,
        grid=(depth, n // tn),
        in_specs=[pl.BlockSpec((8, d), lambda i, j: (0, 0)),
                  pl.BlockSpec((1, d, tn), lambda i, j: (i, 0, j)),
                  pl.BlockSpec((1, 1, tn), lambda i, j: (i, 0, j))],
        out_specs=pl.BlockSpec((1, 8, tn), lambda i, j: (i, 0, j)),
        out_shape=jax.ShapeDtypeStruct((depth, 8, n), F32),
        compiler_params=_params(("arbitrary", "arbitrary")),
        name="adaln",
    )(cpad, ada_w, ada_b.reshape(depth, 1, n))


def _swap_halves(y):
    n = y.shape[-1]
    lane = lax.broadcasted_iota(jnp.int32, y.shape, 1)
    half = HEAD_DIM // 2
    return jnp.where(lane % HEAD_DIM < half, pltpu.roll(y, n - half, 1), pltpu.roll(y, half, 1))


def _to_tiles(x):
    g = x.shape[1]
    tiles = []
    for s in range(x.shape[0] // SCAN_SUB):
        blk = x[s * SCAN_SUB:(s + 1) * SCAN_SUB, :]
        tiles.append(jnp.concatenate([blk[:, h * HEAD_DIM:(h + 1) * HEAD_DIM].T for h in range(g // HEAD_DIM)], axis=1))
    return tiles


def _from_tiles(tiles):
    g = tiles[0].shape[1]
    return jnp.concatenate(
        [jnp.concatenate([t[:, h * SCAN_SUB:(h + 1) * SCAN_SUB].T for h in range(g // HEAD_DIM)], axis=1) for t in tiles],
        axis=0)


def _inproj_kernel(x_ref, g_ref, sh_ref, sc_ref, w_ref,
                   w0_ref, a0_ref, w2_ref, a2_ref, kk_w_ref, ka_ref, e_ref,
                   qg_ref, kg_ref, cos_ref, sin_ref, ek_ref,
                   oa_ref, ob_ref, od_ref,
                   kk_ref, decf_ref, bf_ref, kmf_ref, decb_ref, bb_ref, kmb_ref, vt_ref,
                   q_ref, kc_ref, vc_ref, *, rope):
    g = od_ref.shape[1]
    gk = kc_ref.shape[1]
    h = _norm_mod(x_ref[...], g_ref[...], sh_ref[0], sc_ref[0]).astype(BF16)
    e = e_ref[...]

    pa = _dot(h, w_ref[:, 0:4 * g])
    pc = _dot(h, w_ref[:, 7 * g:9 * g])

    oa_ref[...] = pa
    k = pa[:, g:2 * g]
    lr = pa[:, 3 * g:4 * g]
    for s, tile in enumerate(_to_tiles(pa[:, 2 * g:3 * g])):
        vt_ref[s] = tile
    kx = k * kk_w_ref[...]
    norm = jnp.sqrt(_seg_sum(kx * kx, e))
    kk = kx / jnp.maximum(norm, 1e-12)
    kk_ref[...] = kk
    lr_t = jnp.tanh(lr).astype(BF16)
    lr_b = lr.astype(BF16)
    for d, (dec_ref, b_ref, km_ref) in enumerate(((decf_ref, bf_ref, kmf_ref), (decb_ref, bb_ref, kmb_ref))):
        z = w0_ref[d:d + 1, :] + _dot(lr_t, w2_ref[d])
        softplus_neg = jnp.maximum(-z, 0.0) + jnp.log1p(jnp.exp(-jnp.abs(z)))
        w = -softplus_neg - 0.5
        dec_ref[...] = jnp.exp(-jnp.exp(w))
        a = jax.nn.sigmoid(a0_ref[d:d + 1, :] + _dot(lr_b, a2_ref[d]))
        b_ref[...] = kk * a
        km_ref[...] = k * (1.0 + (a - 1.0) * ka_ref[...])

    def normed(x, gain, seg):
        ms = _seg_sum(x * x, seg) * (1.0 / HEAD_DIM)
        return x * lax.rsqrt(ms + NORM_EPS) * gain

    q = normed(pc[:, :g], qg_ref[...], e)
    kc = normed(pc[:, g:g + gk], kg_ref[...], ek_ref[...])
    if rope:
        cos = cos_ref[...]
        sin = sin_ref[...]
        q = q * cos + _swap_halves(q) * sin
        kc = kc * cos[:, :gk] + _swap_halves(kc) * sin[:, :gk]
    q_ref[...] = (q * HEAD_DIM ** -0.5).astype(BF16)
    kc_ref[...] = kc.astype(BF16)
    vc_ref[...] = pc[:, g + gk:].astype(BF16)

    ob_ref[...] = _dot(h, w_ref[:, 4 * g:7 * g]).astype(ob_ref.dtype)

    od_ref[...] = _dot(h, w_ref[:, 9 * g:10 * g])


def inproj(x, gain, shift, scale, w, rows_per_mod, rwkv_params, gqa_params, seq_len, rope):
    r, d = x.shape
    g = w.shape[1] // 10
    gk = g // 2
    tm = _tile(min(rows_per_mod, seq_len), 512)
    tpb = rows_per_mod // tm
    nseq = seq_len // tm
    modspec = pl.BlockSpec((1, 1, d), lambda i: (i // tpb, 0, 0))
    full2 = lambda a: pl.BlockSpec(a.shape, lambda i: (0,) * a.ndim)
    rows = lambda n: pl.BlockSpec((tm, n), lambda i: (i, 0))
    q_gain, k_gain, cos, sin, ekv = gqa_params
    table = pl.BlockSpec((tm, g), lambda i: (i % nseq, 0))
    out_specs = ([rows(4 * g), rows(3 * g), rows(g)] + [rows(g)] * 7
                 + [pl.BlockSpec((tm // SCAN_SUB, HEAD_DIM, g), lambda i: (i, 0, 0))] + [rows(g), rows(gk), rows(gk)])
    out_shape = ([jax.ShapeDtypeStruct((r, 4 * g), F32), jax.ShapeDtypeStruct((r, 3 * g), BF16),
                  jax.ShapeDtypeStruct((r, g), F32)] + [jax.ShapeDtypeStruct((r, g), F32)] * 7
                 + [jax.ShapeDtypeStruct((r // SCAN_SUB, HEAD_DIM, g), F32), jax.ShapeDtypeStruct((r, g), BF16),
                    jax.ShapeDtypeStruct((r, gk), BF16), jax.ShapeDtypeStruct((r, gk), BF16)])
    return pl.pallas_call(
        functools.partial(_inproj_kernel, rope=rope),
        grid=(r // tm,),
        in_specs=[rows(d), pl.BlockSpec((1, d), lambda i: (0, 0)), modspec, modspec, full2(w)]
                 + [full2(p) for p in rwkv_params] + [full2(q_gain), full2(k_gain), table, table, full2(ekv)],
        out_specs=out_specs,
        out_shape=out_shape,
        compiler_params=_params(("arbitrary",)),
        name="inproj",
    )(x, gain, shift, scale, w, *rwkv_params, q_gain, k_gain, cos, sin, ekv)


def _outproj_kernel(yf_ref, yb_ref, p_ref, kmf_ref, kmb_ref, rk_ref, g2_ref, gnw_ref, gnb_ref, e_ref,
                    b_ref, c_ref, d_ref, w_ref, x_ref, gate_ref, o_ref):
    g = b_ref.shape[1]
    e = e_ref[...]
    r = p_ref[:, :g]
    v = p_ref[:, 2 * g:3 * g]
    lr = p_ref[:, 3 * g:4 * g]
    y = _from_tiles([yf_ref[s] + yb_ref[s] for s in range(yf_ref.shape[0])])
    mu = _seg_sum(y, e) * (1.0 / HEAD_DIM)
    yc = y - mu
    var = _seg_sum(yc * yc, e) * (1.0 / HEAD_DIM)
    yn = yc * lax.rsqrt(var + RWKV_GN_EPS) * gnw_ref[...] + gnb_ref[...]
    bonus = _seg_sum(r * (kmf_ref[...] + kmb_ref[...]) * rk_ref[...], e) * v
    a = (yn + bonus) * _dot(jax.nn.sigmoid(lr).astype(BF16), g2_ref[...])
    acc = _dot(b_ref[...].astype(BF16), w_ref[1])
    acc += _dot(c_ref[...].astype(BF16), w_ref[2])
    acc += _dot(d_ref[...].astype(BF16), w_ref[3])
    acc += _dot(a.astype(BF16), w_ref[0])
    o_ref[...] = x_ref[...] + gate_ref[0] * acc


def outproj(yf, yb, pa, km_f, km_b, readout_params, mix_bcd, w4, x, gate, rows_per_mod):
    r, d = x.shape
    g = w4.shape[1]
    tm = _tile(rows_per_mod, 512)
    tpb = rows_per_mod // tm
    rows = lambda n: pl.BlockSpec((tm, n), lambda i: (i, 0))
    tiles = pl.BlockSpec((tm // SCAN_SUB, HEAD_DIM, g), lambda i: (i, 0, 0))
    full2 = lambda a: pl.BlockSpec(a.shape, lambda i: (0,) * a.ndim)
    return pl.pallas_call(
        _outproj_kernel,
        grid=(r // tm,),
        in_specs=[tiles, tiles, rows(4 * g), rows(g), rows(g)] + [full2(p) for p in readout_params]
                 + [rows(g), rows(g), rows(g), full2(w4), rows(d), pl.BlockSpec((1, 1, d), lambda i: (i // tpb, 0, 0))],
        out_specs=rows(d),
        out_shape=jax.ShapeDtypeStruct((r, d), F32),
        compiler_params=_params(("arbitrary",)),
        name="outproj",
    )(yf, yb, pa, km_f, km_b, *readout_params, *mix_bcd, w4, x, gate)


def _mlp_kernel(x_ref, g_ref, sh_ref, sc_ref, gate_ref, w1_ref, w2_ref, o_ref, h_scr, acc_scr):
    j = pl.program_id(1)

    @pl.when(j == 0)
    def _():
        h_scr[...] = _norm_mod(x_ref[...], g_ref[...], sh_ref[0], sc_ref[0]).astype(BF16)
        acc_scr[...] = jnp.zeros_like(acc_scr)

    u = jnp.maximum(_dot(h_scr[...], w1_ref[...]), 0.0)
    acc_scr[...] += _dot((u * u).astype(BF16), w2_ref[...])

    @pl.when(j == pl.num_programs(1) - 1)
    def _():
        o_ref[...] = x_ref[...] + gate_ref[0] * acc_scr[...]


def mlp(x, g, shift, scale, gate, w1, w2, rows_per_mod):
    r, d = x.shape
    dff = w1.shape[1]
    tm = _tile(rows_per_mod, 1024)
    tf = _tile(dff, 1024)
    tpb = rows_per_mod // tm
    modspec = pl.BlockSpec((1, 1, d), lambda i, j: (i // tpb, 0, 0))
    return pl.pallas_call(
        _mlp_kernel,
        grid=(r // tm, dff // tf),
        in_specs=[pl.BlockSpec((tm, d), lambda i, j: (i, 0)),
                  pl.BlockSpec((1, d), lambda i, j: (0, 0)),
                  modspec, modspec, modspec,
                  pl.BlockSpec((d, tf), lambda i, j: (0, j)),
                  pl.BlockSpec((tf, d), lambda i, j: (j, 0))],
        out_specs=pl.BlockSpec((tm, d), lambda i, j: (i, 0)),
        out_shape=jax.ShapeDtypeStruct((r, d), F32),
        scratch_shapes=[pltpu.VMEM((tm, d), BF16), pltpu.VMEM((tm, d), F32)],
        compiler_params=_params(("arbitrary", "arbitrary")),
        name="mlp",
    )(x, g, shift, scale, gate, w1, w2)


def _final_norm_kernel(x_ref, g_ref, o_ref):
    x = x_ref[...]
    ms = jnp.mean(x * x, axis=-1, keepdims=True)
    o_ref[...] = x * lax.rsqrt(ms + NORM_EPS) * g_ref[...]


def final_norm(x, g):
    r, d = x.shape
    tm = _tile(r, 1024)
    return pl.pallas_call(
        _final_norm_kernel,
        grid=(r // tm,),
        in_specs=[pl.BlockSpec((tm, d), lambda i: (i, 0)), pl.BlockSpec((1, d), lambda i: (0, 0))],
        out_specs=pl.BlockSpec((tm, d), lambda i: (i, 0)),
        out_shape=jax.ShapeDtypeStruct((r, d), F32),
        compiler_params=_params(("arbitrary",)),
        name="final_norm",
    )(x, g)


def _pad_heads(q, n_heads, kv_width, rep):
    assert kv_width % LANES == 0
    t = q.shape[0]
    lane = lax.broadcasted_iota(jnp.int32, (t, LANES), 1)
    lo_half = lane < HEAD_DIM
    zero = jnp.zeros((t, LANES), q.dtype)
    rows = []
    for h in range(n_heads):
        src = h * HEAD_DIM
        dst = (h // rep) * HEAD_DIM
        piece = q[:, src // LANES * LANES:(src // LANES + 1) * LANES]
        if src % LANES != dst % LANES:
            piece = pltpu.roll(piece.astype(F32), HEAD_DIM, 1).astype(q.dtype)
        piece = jnp.where(lo_half if dst % LANES == 0 else ~lo_half, piece, zero)
        blocks = [piece if j == dst // LANES else zero for j in range(kv_width // LANES)]
        rows.append(jnp.concatenate(blocks, axis=1) if len(blocks) > 1 else piece)
    return jnp.concatenate(rows, axis=0)


def _gather_heads(res, n_heads, rep):
    t = res.shape[0] // n_heads
    lane = lax.broadcasted_iota(jnp.int32, (t, LANES), 1)
    lo_half = lane < HEAD_DIM
    pieces = []
    for h in range(n_heads):
        src = (h // rep) * HEAD_DIM
        dst = h * HEAD_DIM
        piece = res[h * t:(h + 1) * t, src // LANES * LANES:(src // LANES + 1) * LANES]
        if src % LANES != dst % LANES:
            piece = pltpu.roll(piece, HEAD_DIM, 1)
        pieces.append(piece)
    blocks = [jnp.where(lo_half, pieces[2 * j], pieces[2 * j + 1]) for j in range(n_heads // 2)]
    return jnp.concatenate(blocks, axis=1)


def _attn_kernel(q_ref, k_ref, v_ref, o_ref, *, rep):
    n_heads = q_ref.shape[1] // HEAD_DIM
    lk = k_ref.shape[1]
    q = _pad_heads(q_ref[...], n_heads, k_ref.shape[2], rep)
    edges = list(range(0, lk, ATTN_CHUNK)) + [lk]
    m = l = acc = None
    for k0, k1 in zip(edges[:-1], edges[1:]):
        s = _dot_nt(q, k_ref[0, k0:k1, :])
        m_new = jnp.max(s, axis=-1, keepdims=True)
        if m is not None:
            m_new = jnp.maximum(m, m_new)
        p = jnp.exp(s - m_new)
        l_new = jnp.sum(p, axis=-1, keepdims=True)
        acc_new = _dot(p.astype(BF16), v_ref[0, k0:k1, :])
        if m is not None:
            alpha = jnp.exp(m - m_new)
            l_new = alpha * l + l_new
            acc_new = alpha * acc + acc_new
        m, l, acc = m_new, l_new, acc_new
    o_ref[...] = _gather_heads(acc / l, n_heads, rep)


def attention(q, k, v, seq_len):
    r, gq = q.shape
    nb, lk, kw = k.shape
    rep = gq // kw
    tq = _tile(seq_len, 128)
    nq = seq_len // tq
    return pl.pallas_call(
        functools.partial(_attn_kernel, rep=rep),
        grid=(nb, nq),
        in_specs=[pl.BlockSpec((tq, gq), lambda i, j: (i * nq + j, 0)),
                  pl.BlockSpec((1, lk, kw), lambda i, j: (i, 0, 0)),
                  pl.BlockSpec((1, lk, kw), lambda i, j: (i, 0, 0))],
        out_specs=pl.BlockSpec((tq, gq), lambda i, j: (i * nq + j, 0)),
        out_shape=jax.ShapeDtypeStruct((r, gq), F32),
        compiler_params=_params(("arbitrary", "arbitrary")),
        name="attention",
    )(q, k, v)


def _na_kernel(p_ref, pc_ref, bias_ref, o_ref, *, rows, kh):
    g = o_ref.shape[2]
    n_heads = g // HEAD_DIM
    band = kh * GRID_W
    kc = pc_ref[0, :, g:2 * g]
    vc = pc_ref[0, :, 2 * g:3 * g]

    def one_row(r, carry):
        rs = jnp.clip(r - kh // 2, 0, rows - kh)
        q0 = pl.multiple_of(r * GRID_W, GRID_W)
        k0 = pl.multiple_of(rs * GRID_W, GRID_W)
        q = _pad_heads(p_ref[0, pl.ds(q0, GRID_W), 0:g], n_heads, g, 1)
        s1 = _dot_nt(q, p_ref[0, pl.ds(k0, band), g:2 * g]) + bias_ref[r - rs]
        s2 = _dot_nt(q, kc)
        m = jnp.maximum(jnp.max(s1, axis=-1, keepdims=True), jnp.max(s2, axis=-1, keepdims=True))
        p1 = jnp.exp(s1 - m)
        p2 = jnp.exp(s2 - m)
        l = jnp.sum(p1, axis=-1, keepdims=True) + jnp.sum(p2, axis=-1, keepdims=True)
        res = (_dot(p1.astype(BF16), p_ref[0, pl.ds(k0, band), 2 * g:3 * g]) + _dot(p2.astype(BF16), vc)) / l
        o_ref[0, pl.ds(q0, GRID_W), :] = _gather_heads(res, n_heads, 1)
        return carry

    lax.fori_loop(0, rows, one_row, 0, unroll=8)


def neighbourhood_attention(p, pc, bias):
    nb, l, g3 = p.shape
    g = g3 // 3
    lc = pc.shape[1]
    rows = l // GRID_W
    kh = min(NA_ROWS, rows)
    return pl.pallas_call(
        functools.partial(_na_kernel, rows=rows, kh=kh),
        grid=(nb,),
        in_specs=[pl.BlockSpec((1, l, g3), lambda i: (i, 0, 0)),
                  pl.BlockSpec((1, lc, g3), lambda i: (i, 0, 0)),
                  pl.BlockSpec(bias.shape, lambda i: (0, 0, 0))],
        out_specs=pl.BlockSpec((1, l, g), lambda i: (i, 0, 0)),
        out_shape=jax.ShapeDtypeStruct((nb, l, g), F32),
        compiler_params=_params(("arbitrary",)),
        name="neighbourhood_attention",
    )(p, pc, bias)


def _na_bias_kernel(rpb_ref, o_ref, *, kh):
    h = pl.program_id(0)
    q = lax.broadcasted_iota(jnp.int32, (GRID_W, GRID_W), 0)
    k = lax.broadcasted_iota(jnp.int32, (GRID_W, GRID_W), 1)
    start = jnp.clip(q - NA_COLS // 2, 0, GRID_W - NA_COLS)
    in_win = (k >= start) & (k < start + NA_COLS)
    off = k - q + NA_COLS - 1
    neg = jnp.full((GRID_W, GRID_W), -jnp.inf, F32)
    blocks = []
    for ro in range(2 * NA_ROWS - 1):
        t = neg
        for c in range(2 * NA_COLS - 1):
            t = jnp.where(off == c, rpb_ref[h, ro, c], t)
        blocks.append(jnp.where(in_win, t, neg))
    for di in range(kh):
        for i in range(kh):
            o_ref[di, :, i * GRID_W:(i + 1) * GRID_W] = blocks[i - di + NA_ROWS - 1]


def na_bias_table(rpb, rows):
    kh = min(NA_ROWS, rows)
    nh = rpb.shape[0]
    return pl.pallas_call(
        functools.partial(_na_bias_kernel, kh=kh),
        grid=(nh,),
        in_specs=[pl.BlockSpec(memory_space=pltpu.SMEM)],
        out_specs=pl.BlockSpec((kh, GRID_W, kh * GRID_W), lambda i: (0, i, 0)),
        out_shape=jax.ShapeDtypeStruct((kh, nh * GRID_W, kh * GRID_W), F32),
        compiler_params=_params(("arbitrary",)),
        name="na_bias",
    )(rpb)


def _shift_rows(x, d, t):
    n = x.shape[0]
    y = pltpu.roll(x, d % n, 0)
    src = t - d
    return jnp.where((src >= 0) & (src < n), y, 0.0)


def _pool_kernel(x_ref, w_ref, scale_ref, o_ref):
    x = x_ref[0]
    n, g = x.shape
    pg = g // len(POOL_WINDOWS)
    t = lax.broadcasted_iota(jnp.int32, x.shape, 0)
    group = lax.broadcasted_iota(jnp.int32, x.shape, 1) // pg
    fwd = x
    bwd = x
    cur = 1
    total = jnp.zeros_like(x)
    count = jnp.ones_like(x)
    for j, w in enumerate(POOL_WINDOWS):
        half = w // 2
        while cur < half:
            fwd = fwd + _shift_rows(fwd, -cur, t)
            bwd = bwd + _shift_rows(bwd, cur, t)
            cur *= 2
        win = _shift_rows(bwd, 1, t) + fwd
        lo = jnp.clip(t - half, 0, n)
        hi = jnp.clip(t - half + w, 0, n)
        total = jnp.where(group == j, win, total)
        count = jnp.where(group == j, (hi - lo).astype(F32), count)
    diff = total / count - x
    o_ref[0] = _dot(diff.astype(BF16), w_ref[...]) * scale_ref[...]


def pool_mixer(p, w_bd, scale):
    b, n, g = p.shape
    return pl.pallas_call(
        _pool_kernel,
        grid=(b,),
        in_specs=[pl.BlockSpec((1, n, g), lambda i: (i, 0, 0)),
                  pl.BlockSpec((g, g), lambda i: (0, 0)),
                  pl.BlockSpec((1, g), lambda i: (0, 0))],
        out_specs=pl.BlockSpec((1, n, g), lambda i: (i, 0, 0)),
        out_shape=jax.ShapeDtypeStruct((b, n, g), F32),
        compiler_params=_params(("arbitrary",)),
        name="pool_mixer",
    )(p, w_bd, scale)


def _wkv_scan_kernel(rf_ref, rb_ref, kkf_ref, kkb_ref, decf_ref, bf_ref, kmf_ref, decb_ref, bb_ref, kmb_ref,
                     vtf_ref, vtb_ref, e_ref, s0_ref, yf_ref, yb_ref, send_ref, s_scr, sr_scr):
    nb, t_blk, g = rf_ref.shape
    step_id = pl.program_id(0)

    @pl.when(step_id == 0)
    def _():
        s_scr[...] = s0_ref[...]

    yf_ref[...] = jnp.zeros_like(yf_ref)
    yb_ref[...] = jnp.zeros_like(yb_ref)
    sr_scr[...] = jnp.zeros_like(sr_scr)
    e = e_ref[...]
    lane = lax.broadcasted_iota(jnp.int32, (HEAD_DIM, g), 1) % SCAN_SUB
    lane128 = lax.broadcasted_iota(jnp.int32, (HEAD_DIM, LANES), 1)
    dirs = ((rf_ref, kkf_ref, decf_ref, bf_ref, kmf_ref, vtf_ref, yf_ref),
            (rb_ref, kkb_ref, decb_ref, bb_ref, kmb_ref, vtb_ref, yb_ref))

    def write_y(d, bs, y, tt_done, valid):
        y_ref = dirs[d][-1]
        tl = tt_done if d == 0 else t_blk - 1 - tt_done
        tl = jnp.clip(tl, 0, t_blk - 1)
        hit = (lane == tl % SCAN_SUB) & valid
        for i, b in enumerate(bs):
            pltpu.store(y_ref.at[b, tl // SCAN_SUB], y[i * HEAD_DIM:(i + 1) * HEAD_DIM], mask=hit)

    groups = [tuple(range(i, min(i + 2, nb))) for i in range(0, nb, 2)]

    def one_token(tt, carry):
        for d, (r_ref, kk_ref, dec_ref, b_ref, km_ref, vt_ref, y_ref) in enumerate(dirs):
            tl = tt if d == 0 else t_blk - 1 - tt
            row = lambda ref, b: jnp.broadcast_to(ref[b, pl.ds(tl, 1), :], (HEAD_DIM, g))
            blk = lambda x, i: x[i * HEAD_DIM:(i + 1) * HEAD_DIM]
            idx = (lane128 // SCAN_SUB) * SCAN_SUB + tl % SCAN_SUB
            for bs in groups:
                m = len(bs) * HEAD_DIM
                s_old = [s_scr[d, b] for b in bs]
                x = jnp.concatenate([s_old[i].astype(BF16) * row(kk_ref, b).astype(BF16) for i, b in enumerate(bs)], axis=0)
                out = _dot(jnp.concatenate([x, sr_scr[d, bs[0] * HEAD_DIM:(bs[-1] + 1) * HEAD_DIM, :]], axis=0), e)
                write_y(d, bs, out[m:], tt - 1, tt > 0)
                for i, b in enumerate(bs):
                    vcol = jnp.concatenate([jnp.take_along_axis(vt_ref[b, tl // SCAN_SUB][:, j * LANES:(j + 1) * LANES], idx, axis=1)
                                            for j in range(g // LANES)], axis=1)
                    s = s_old[i] * row(dec_ref, b) - blk(out, i) * row(b_ref, b) + vcol * row(km_ref, b)
                    s_scr[d, b] = s
                    sr_scr[d, b * HEAD_DIM:(b + 1) * HEAD_DIM, :] = s.astype(BF16) * row(r_ref, b).astype(BF16)
        return carry

    lax.fori_loop(0, t_blk, one_token, 0, unroll=8)
    for d in range(2):
        write_y(d, tuple(range(nb)), _dot(sr_scr[d], e), t_blk - 1, True)

    @pl.when(step_id == pl.num_programs(0) - 1)
    def _():
        send_ref[...] = s_scr[...]


def wkv_scan(pa3, kk, dec_f, b_f, km_f, dec_b, b_b, km_b, vt, eseg, s0):
    nb, l, g = kk.shape
    t_blk = SCAN_BLOCK
    nsub = t_blk // SCAN_SUB
    nblk = l // t_blk
    assert l % t_blk == 0
    seq_f = pl.BlockSpec((nb, t_blk, g), lambda i: (0, i, 0))
    seq_b = pl.BlockSpec((nb, t_blk, g), lambda i: (0, nblk - 1 - i, 0))
    vt_f = pl.BlockSpec((nb, nsub, HEAD_DIM, g), lambda i: (0, i, 0, 0))
    vt_b = pl.BlockSpec((nb, nsub, HEAD_DIM, g), lambda i: (0, nblk - 1 - i, 0, 0))
    state = pl.BlockSpec(s0.shape, lambda i: (0, 0, 0, 0))
    return pl.pallas_call(
        _wkv_scan_kernel,
        grid=(nblk,),
        in_specs=[seq_f, seq_b, seq_f, seq_b, seq_f, seq_f, seq_f, seq_b, seq_b, seq_b,
                  vt_f, vt_b, pl.BlockSpec(eseg.shape, lambda i: (0, 0)), state],
        out_specs=[vt_f, vt_b, state],
        out_shape=[jax.ShapeDtypeStruct(vt.shape, F32), jax.ShapeDtypeStruct(vt.shape, F32),
                   jax.ShapeDtypeStruct(s0.shape, F32)],
        scratch_shapes=[pltpu.VMEM(s0.shape, F32), pltpu.VMEM((2, nb * HEAD_DIM, g), BF16)],
        compiler_params=_params(("arbitrary",)),
        name="wkv_scan",
    )(pa3, pa3, kk, kk, dec_f, b_f, km_f, dec_b, b_b, km_b, vt, vt, eseg, s0)


def _block_ones(n, seg):
    idx = np.arange(n) // seg
    return jnp.asarray(idx[:, None] == idx[None, :], dtype=BF16)


def _padded_rows(w, offset, n):
    return jnp.zeros((n, w.shape[1]), w.dtype).at[offset:offset + w.shape[0]].set(w)


def _rope_tables(n_tokens, n_rep):
    t = jnp.arange(n_tokens)
    row = (t // GRID_W).astype(F32)
    col = (t % GRID_W).astype(F32)
    n_freq = HEAD_DIM // 4
    inv_freq = ROPE_THETA ** (-jnp.arange(n_freq, dtype=F32) / n_freq)
    ang = jnp.concatenate([row[:, None] * inv_freq, col[:, None] * inv_freq], axis=-1)
    cos, sin = jnp.cos(ang), jnp.sin(ang)
    cos_h = jnp.concatenate([cos, cos], axis=-1)
    sin_h = jnp.concatenate([-sin, sin], axis=-1)
    return jnp.tile(cos_h, (1, n_rep)), jnp.tile(sin_h, (1, n_rep))


def kernel(x, c, ctx, c_ctx, ada_w, ada_b, norm1_g, norm2_g, w_in, w_out, rwkv_w0, rwkv_w2, rwkv_a0, rwkv_a2,
           rwkv_k_k, rwkv_k_a, rwkv_r_k, rwkv_g2, rwkv_gn_w, rwkv_gn_b, na_rpb, gqa_q_gain, gqa_k_gain, pool_w,
           pool_scale, mlp_w1, mlp_w2, final_g):
    nb, l, d = x.shape
    lc = ctx.shape[1]
    depth = ada_w.shape[0]
    g = d // 4
    nh = g // HEAD_DIM
    nkv = nh // 2
    kvw = nkv * HEAD_DIM
    dr = rwkv_w2.shape[2]
    ir = rwkv_a2.shape[2]
    gr = rwkv_g2.shape[1]
    assert 2 * dr + 2 * ir + gr <= g and nb + 1 <= 8
    rows = l // GRID_W

    cpad = jnp.zeros((8, d), F32).at[:nb].set(c).at[nb].set(c_ctx)
    mods = adaln(cpad, ada_w, ada_b)

    eseg = _block_ones(g, HEAD_DIM)
    ekv = _block_ones(kvw, HEAD_DIM)
    cos_q, sin_q = _rope_tables(l, nh)
    ones_c, zeros_c = jnp.ones((lc, g), F32), jnp.zeros((lc, g), F32)

    splits = np.cumsum([0, g, g, g, dr, dr, ir, ir, gr, g, g, g, g, kvw, kvw, g])
    lowrank_w = splits[8] - splits[3]

    xl = x.reshape(nb * l, d)
    xc = ctx.reshape(nb * lc, d)
    zero_state = jnp.zeros((2, nb, HEAD_DIM, g), F32)

    for i in range(depth):
        need_ctx_out = i < depth - 1
        mod_l = mods[i, :nb].reshape(nb, N_MOD, 1, d)
        mod_c = mods[i, nb].reshape(N_MOD, 1, 1, d)
        ml = [mod_l[:, k] for k in range(N_MOD)]
        mc = [mod_c[k] for k in range(N_MOD)]

        wi = w_in[i]
        w_inp = jnp.concatenate([wi[:, :splits[8]], jnp.zeros((d, g - lowrank_w), F32),
                                 wi[:, splits[8]:splits[9]] * HEAD_DIM ** -0.5, wi[:, splits[9]:]], axis=1).astype(BF16)
        g1 = norm1_g[i].reshape(1, d)
        g2 = norm2_g[i].reshape(1, d)

        w2p = jnp.stack([_padded_rows(rwkv_w2[i, 0], 0, g), _padded_rows(rwkv_w2[i, 1], dr, g)]).astype(BF16)
        a2p = jnp.stack([_padded_rows(rwkv_a2[i, 0], 2 * dr, g),
                         _padded_rows(rwkv_a2[i, 1], 2 * dr + ir, g)]).astype(BF16)
        g2p = _padded_rows(rwkv_g2[i], 2 * dr + 2 * ir, g).astype(BF16)
        r_k = rwkv_r_k[i].reshape(1, g)
        gn_w = rwkv_gn_w[i].reshape(1, g)
        gn_b = rwkv_gn_b[i].reshape(1, g)
        rwkv_params = (rwkv_w0[i], rwkv_a0[i], w2p, a2p, rwkv_k_k[i].reshape(1, g), rwkv_k_a[i].reshape(1, g), eseg)
        q_gain = jnp.tile(gqa_q_gain[i], nh).reshape(1, g)
        k_gain = jnp.tile(gqa_k_gain[i], nkv).reshape(1, kvw)

        proj_l = inproj(xl, g1, ml[0], ml[1], w_inp, l, rwkv_params, (q_gain, k_gain, cos_q, sin_q, ekv), l, True)
        proj_c = inproj(xc, g1, mc[0], mc[1], w_inp, nb * lc, rwkv_params, (q_gain, k_gain, ones_c, zeros_c, ekv), lc, False)
        pa_l, pb_l, pd_l = proj_l[:3]
        pa_c, pb_c, pd_c = proj_c[:3]

        def scan_inputs(proj, seq):
            terms3 = [t.reshape(nb, seq, g) for t in proj[3:10]]
            return terms3, proj[10].reshape(nb, seq // SCAN_SUB, HEAD_DIM, g)

        terms3_c, vt_c = scan_inputs(proj_c, lc)
        yf_c, yb_c, state_c = wkv_scan(pa_c.reshape(nb, lc, 4 * g), *terms3_c, vt_c, eseg, zero_state)
        terms3_l, vt_l = scan_inputs(proj_l, l)
        yf_l, yb_l, _ = wkv_scan(pa_l.reshape(nb, l, 4 * g), *terms3_l, vt_l, eseg, state_c)

        bias = na_bias_table(na_rpb[i], rows)
        b_l = neighbourhood_attention(pb_l.reshape(nb, l, 3 * g), pb_c.reshape(nb, lc, 3 * g), bias).reshape(nb * l, g)

        qc_l, kc_l, vc_l = proj_l[11:14]
        qc_c, kc_c, vc_c = proj_c[11:14]
        kc_c, vc_c = kc_c.reshape(nb, lc, kvw), vc_c.reshape(nb, lc, kvw)
        k_all = jnp.concatenate([kc_c, kc_l.reshape(nb, l, kvw)], axis=1)
        v_all = jnp.concatenate([vc_c, vc_l.reshape(nb, l, kvw)], axis=1)
        c_l = attention(qc_l, k_all, v_all, l)

        w_bd = jax.scipy.linalg.block_diag(*[pool_w[i, k] for k in range(len(POOL_WINDOWS))]).astype(BF16)
        p_scale = pool_scale[i].reshape(1, g)
        d_l = pool_mixer(pd_l.reshape(nb, l, g), w_bd, p_scale).reshape(nb * l, g)

        w_out4 = w_out[i].reshape(4, g, d).astype(BF16)
        w1 = mlp_w1[i].astype(BF16)
        w2 = mlp_w2[i].astype(BF16)
        readout_params = (r_k, g2p, gn_w, gn_b, eseg)
        xl = outproj(yf_l.reshape(-1, HEAD_DIM, g), yb_l.reshape(-1, HEAD_DIM, g), pa_l, proj_l[6], proj_l[9],
                     readout_params, (b_l, c_l, d_l), w_out4, xl, ml[2], l)
        xl = mlp(xl, g2, ml[3], ml[4], ml[5], w1, w2, l)

        if need_ctx_out:
            b_c = attention(pb_c[:, :g], pb_c[:, g:2 * g].reshape(nb, lc, g), pb_c[:, 2 * g:].reshape(nb, lc, g), lc)
            c_c = attention(qc_c, kc_c, vc_c, lc)
            d_c = pool_mixer(pd_c.reshape(nb, lc, g), w_bd, p_scale).reshape(nb * lc, g)
            xc = outproj(yf_c.reshape(-1, HEAD_DIM, g), yb_c.reshape(-1, HEAD_DIM, g), pa_c, proj_c[6], proj_c[9],
                         readout_params, (b_c, c_c, d_c), w_out4, xc, mc[2], nb * lc)
            xc = mlp(xc, g2, mc[3], mc[4], mc[5], w1, w2, nb * lc)

    return final_norm(xl, final_g.reshape(1, d)).reshape(nb, l, d)
```

```python
import functools

import jax
import jax.numpy as jnp
import numpy as np
from jax import lax
from jax.experimental import pallas as pl
from jax.experimental.pallas import tpu as pltpu

F32 = jnp.float32
BF16 = jnp.bfloat16

HEAD_DIM = 64
GRID_W = 64
NA_ROWS = 8
NA_COLS = 16
ROPE_THETA = 10000.0
POOL_WINDOWS = (2, 4, 8, 16)
NORM_EPS = 1e-6
RWKV_GN_EPS = 1e-5 * HEAD_DIM
N_MOD = 6
LANES = 128
SCAN_BLOCK = 128
SCAN_SUB = 64
ATTN_CHUNK = 2048
VMEM_LIMIT = 52 * 1024 * 1024


def _params(sem):
    return pltpu.CompilerParams(dimension_semantics=sem, vmem_limit_bytes=VMEM_LIMIT)


def _tile(n, pref):
    t = min(n, pref)
    assert n % t == 0, (n, pref)
    return t


def _dot(a, b):
    return jnp.dot(a, b, preferred_element_type=F32)


def _dot_nt(a, b):
    return lax.dot_general(a, b, (((1,), (1,)), ((), ())), preferred_element_type=F32)


def _seg_sum(x, e):
    hi = x.astype(BF16)
    lo = (x - hi.astype(F32)).astype(BF16)
    return _dot(hi, e) + _dot(lo, e)


def _norm_mod(x, g, shift, scale):
    ms = jnp.mean(x * x, axis=-1, keepdims=True)
    h = x * lax.rsqrt(ms + NORM_EPS) * g
    return h * (1.0 + scale) + shift


def _adaln_kernel(c_ref, w_ref, b_ref, o_ref):
    c = c_ref[...]
    s = c * jax.nn.sigmoid(c)
    o_ref[0] = jnp.dot(s, w_ref[0], preferred_element_type=F32, precision=lax.Precision.HIGHEST) + b_ref[0]


def adaln(cpad, ada_w, ada_b):
    depth, d, n = ada_w.shape
    tn = _tile(n, 1536)
    return pl.pallas_call(
        _adaln_kernel,
        grid=(depth, n // tn),
        in_specs=[pl.BlockSpec((8, d), lambda i, j: (0, 0)),
                  pl.BlockSpec((1, d, tn), lambda i, j: (i, 0, j)),
                  pl.BlockSpec((1, 1, tn), lambda i, j: (i, 0, j))],
        out_specs=pl.BlockSpec((1, 8, tn), lambda i, j: (i, 0, j)),
        out_shape=jax.ShapeDtypeStruct((depth, 8, n), F32),
        compiler_params=_params(("arbitrary", "arbitrary")),
        name="adaln",
    )(cpad, ada_w, ada_b.reshape(depth, 1, n))


def _swap_halves(y):
    n = y.shape[-1]
    lane = lax.broadcasted_iota(jnp.int32, y.shape, 1)
    half = HEAD_DIM // 2
    return jnp.where(lane % HEAD_DIM < half, pltpu.roll(y, n - half, 1), pltpu.roll(y, half, 1))


def _to_tiles(x):
    g = x.shape[1]
    tiles = []
    for s in range(x.shape[0] // SCAN_SUB):
        blk = x[s * SCAN_SUB:(s + 1) * SCAN_SUB, :]
        tiles.append(jnp.concatenate([blk[:, h * HEAD_DIM:(h + 1) * HEAD_DIM].T for h in range(g // HEAD_DIM)], axis=1))
    return tiles


def _from_tiles(tiles):
    g = tiles[0].shape[1]
    return jnp.concatenate(
        [jnp.concatenate([t[:, h * SCAN_SUB:(h + 1) * SCAN_SUB].T for h in range(g // HEAD_DIM)], axis=1) for t in tiles],
        axis=0)


def _inproj_kernel(x_ref, g_ref, sh_ref, sc_ref, w_ref,
                   w0_ref, a0_ref, w2_ref, a2_ref, kk_w_ref, ka_ref, e_ref,
                   qg_ref, kg_ref, cos_ref, sin_ref, ek_ref,
                   oa_ref, ob_ref, od_ref,
                   kk_ref, decf_ref, bf_ref, kmf_ref, decb_ref, bb_ref, kmb_ref, vt_ref,
                   q_ref, kc_ref, vc_ref, *, rope):
    g = od_ref.shape[1]
    gk = kc_ref.shape[1]
    h = _norm_mod(x_ref[...], g_ref[...], sh_ref[0], sc_ref[0]).astype(BF16)
    e = e_ref[...]

    pa = _dot(h, w_ref[:, 0:4 * g])
    pc = _dot(h, w_ref[:, 7 * g:9 * g])

    oa_ref[...] = pa
    k = pa[:, g:2 * g]
    lr = pa[:, 3 * g:4 * g]
    for s, tile in enumerate(_to_tiles(pa[:, 2 * g:3 * g])):
        vt_ref[s] = tile
    kx = k * kk_w_ref[...]
    norm = jnp.sqrt(_seg_sum(kx * kx, e))
    kk = kx / jnp.maximum(norm, 1e-12)
    kk_ref[...] = kk
    lr_t = jnp.tanh(lr).astype(BF16)
    lr_b = lr.astype(BF16)
    for d, (dec_ref, b_ref, km_ref) in enumerate(((decf_ref, bf_ref, kmf_ref), (decb_ref, bb_ref, kmb_ref))):
        z = w0_ref[d:d + 1, :] + _dot(lr_t, w2_ref[d])
        softplus_neg = jnp.maximum(-z, 0.0) + jnp.log1p(jnp.exp(-jnp.abs(z)))
        w = -softplus_neg - 0.5
        dec_ref[...] = jnp.exp(-jnp.exp(w))
        a = jax.nn.sigmoid(a0_ref[d:d + 1, :] + _dot(lr_b, a2_ref[d]))
        b_ref[...] = kk * a
        km_ref[...] = k * (1.0 + (a - 1.0) * ka_ref[...])

    def normed(x, gain, seg):
        ms = _seg_sum(x * x, seg) * (1.0 / HEAD_DIM)
        return x * lax.rsqrt(ms + NORM_EPS) * gain

    q = normed(pc[:, :g], qg_ref[...], e)
    kc = normed(pc[:, g:g + gk], kg_ref[...], ek_ref[...])
    if rope:
        cos = cos_ref[...]
        sin = sin_ref[...]
        q = q * cos + _swap_halves(q) * sin
        kc = kc * cos[:, :gk] + _swap_halves(kc) * sin[:, :gk]
    q_ref[...] = (q * HEAD_DIM ** -0.5).astype(BF16)
    kc_ref[...] = kc.astype(BF16)
    vc_ref[...] = pc[:, g + gk:].astype(BF16)

    ob_ref[...] = _dot(h, w_ref[:, 4 * g:7 * g]).astype(ob_ref.dtype)

    od_ref[...] = _dot(h, w_ref[:, 9 * g:10 * g])


def inproj(x, gain, shift, scale, w, rows_per_mod, rwkv_params, gqa_params, seq_len, rope):
    r, d = x.shape
    g = w.shape[1] // 10
    gk = g // 2
    tm = _tile(min(rows_per_mod, seq_len), 512)
    tpb = rows_per_mod // tm
    nseq = seq_len // tm
    modspec = pl.BlockSpec((1, 1, d), lambda i: (i // tpb, 0, 0))
    full2 = lambda a: pl.BlockSpec(a.shape, lambda i: (0,) * a.ndim)
    rows = lambda n: pl.BlockSpec((tm, n), lambda i: (i, 0))
    q_gain, k_gain, cos, sin, ekv = gqa_params
    table = pl.BlockSpec((tm, g), lambda i: (i % nseq, 0))
    out_specs = ([rows(4 * g), rows(3 * g), rows(g)] + [rows(g)] * 7
                 + [pl.BlockSpec((tm // SCAN_SUB, HEAD_DIM, g), lambda i: (i, 0, 0))] + [rows(g), rows(gk), rows(gk)])
    out_shape = ([jax.ShapeDtypeStruct((r, 4 * g), F32), jax.ShapeDtypeStruct((r, 3 * g), BF16),
                  jax.ShapeDtypeStruct((r, g), F32)] + [jax.ShapeDtypeStruct((r, g), F32)] * 7
                 + [jax.ShapeDtypeStruct((r // SCAN_SUB, HEAD_DIM, g), F32), jax.ShapeDtypeStruct((r, g), BF16),
                    jax.ShapeDtypeStruct((r, gk), BF16), jax.ShapeDtypeStruct((r, gk), BF16)])
    return pl.pallas_call(
        functools.partial(_inproj_kernel, rope=rope),
        grid=(r // tm,),
        in_specs=[rows(d), pl.BlockSpec((1, d), lambda i: (0, 0)), modspec, modspec, full2(w)]
                 + [full2(p) for p in rwkv_params] + [full2(q_gain), full2(k_gain), table, table, full2(ekv)],
        out_specs=out_specs,
        out_shape=out_shape,
        compiler_params=_params(("arbitrary",)),
        name="inproj",
    )(x, gain, shift, scale, w, *rwkv_params, q_gain, k_gain, cos, sin, ekv)


def _outproj_kernel(yf_ref, yb_ref, p_ref, kmf_ref, kmb_ref, rk_ref, g2_ref, gnw_ref, gnb_ref, e_ref,
                    b_ref, c_ref, d_ref, w_ref, x_ref, gate_ref, o_ref):
    g = b_ref.shape[1]
    e = e_ref[...]
    r = p_ref[:, :g]
    v = p_ref[:, 2 * g:3 * g]
    lr = p_ref[:, 3 * g:4 * g]
    y = _from_tiles([yf_ref[s] + yb_ref[s] for s in range(yf_ref.shape[0])])
    mu = _seg_sum(y, e) * (1.0 / HEAD_DIM)
    yc = y - mu
    var = _seg_sum(yc * yc, e) * (1.0 / HEAD_DIM)
    yn = yc * lax.rsqrt(var + RWKV_GN_EPS) * gnw_ref[...] + gnb_ref[...]
    bonus = _seg_sum(r * (kmf_ref[...] + kmb_ref[...]) * rk_ref[...], e) * v
    a = (yn + bonus) * _dot(jax.nn.sigmoid(lr).astype(BF16), g2_ref[...])
    acc = _dot(b_ref[...].astype(BF16), w_ref[1])
    acc += _dot(c_ref[...].astype(BF16), w_ref[2])
    acc += _dot(d_ref[...].astype(BF16), w_ref[3])
    acc += _dot(a.astype(BF16), w_ref[0])
    o_ref[...] = x_ref[...] + gate_ref[0] * acc


def outproj(yf, yb, pa, km_f, km_b, readout_params, mix_bcd, w4, x, gate, rows_per_mod):
    r, d = x.shape
    g = w4.shape[1]
    tm = _tile(rows_per_mod, 512)
    tpb = rows_per_mod // tm
    rows = lambda n: pl.BlockSpec((tm, n), lambda i: (i, 0))
    tiles = pl.BlockSpec((tm // SCAN_SUB, HEAD_DIM, g), lambda i: (i, 0, 0))
    full2 = lambda a: pl.BlockSpec(a.shape, lambda i: (0,) * a.ndim)
    return pl.pallas_call(
        _outproj_kernel,
        grid=(r // tm,),
        in_specs=[tiles, tiles, rows(4 * g), rows(g), rows(g)] + [full2(p) for p in readout_params]
                 + [rows(g), rows(g), rows(g), full2(w4), rows(d), pl.BlockSpec((1, 1, d), lambda i: (i // tpb, 0, 0))],
        out_specs=rows(d),
        out_shape=jax.ShapeDtypeStruct((r, d), F32),
        compiler_params=_params(("arbitrary",)),
        name="outproj",
    )(yf, yb, pa, km_f, km_b, *readout_params, *mix_bcd, w4, x, gate)


def _mlp_kernel(x_ref, g_ref, sh_ref, sc_ref, gate_ref, w1_ref, w2_ref, o_ref, h_scr, acc_scr):
    j = pl.program_id(1)

    @pl.when(j == 0)
    def _():
        h_scr[...] = _norm_mod(x_ref[...], g_ref[...], sh_ref[0], sc_ref[0]).astype(BF16)
        acc_scr[...] = jnp.zeros_like(acc_scr)

    u = jnp.maximum(_dot(h_scr[...], w1_ref[...]), 0.0)
    acc_scr[...] += _dot((u * u).astype(BF16), w2_ref[...])

    @pl.when(j == pl.num_programs(1) - 1)
    def _():
        o_ref[...] = x_ref[...] + gate_ref[0] * acc_scr[...]


def mlp(x, g, shift, scale, gate, w1, w2, rows_per_mod):
    r, d = x.shape
    dff = w1.shape[1]
    tm = _tile(rows_per_mod, 1024)
    tf = _tile(dff, 1024)
    tpb = rows_per_mod // tm
    modspec = pl.BlockSpec((1, 1, d), lambda i, j: (i // tpb, 0, 0))
    return pl.pallas_call(
        _mlp_kernel,
        grid=(r // tm, dff // tf),
        in_specs=[pl.BlockSpec((tm, d), lambda i, j: (i, 0)),
                  pl.BlockSpec((1, d), lambda i, j: (0, 0)),
                  modspec, modspec, modspec,
                  pl.BlockSpec((d, tf), lambda i, j: (0, j)),
                  pl.BlockSpec((tf, d), lambda i, j: (j, 0))],
        out_specs=pl.BlockSpec((tm, d), lambda i, j: (i, 0)),
        out_shape=jax.ShapeDtypeStruct((r, d), F32),
        scratch_shapes=[pltpu.VMEM((tm, d), BF16), pltpu.VMEM((tm, d), F32)],
        compiler_params=_params(("arbitrary", "arbitrary")),
        name="mlp",
    )(x, g, shift, scale, gate, w1, w2)


def _final_norm_kernel(x_ref, g_ref, o_ref):
    x = x_ref[...]
    ms = jnp.mean(x * x, axis=-1, keepdims=True)
    o_ref[...] = x * lax.rsqrt(ms + NORM_EPS) * g_ref[...]


def final_norm(x, g):
    r, d = x.shape
    tm = _tile(r, 1024)
    return pl.pallas_call(
        _final_norm_kernel,
        grid=(r // tm,),
        in_specs=[pl.BlockSpec((tm, d), lambda i: (i, 0)), pl.BlockSpec((1, d), lambda i: (0, 0))],
        out_specs=pl.BlockSpec((tm, d), lambda i: (i, 0)),
        out_shape=jax.ShapeDtypeStruct((r, d), F32),
        compiler_params=_params(("arbitrary",)),
        name="final_norm",
    )(x, g)


def _pad_heads(q, n_heads, kv_width, rep):
    assert kv_width % LANES == 0
    t = q.shape[0]
    lane = lax.broadcasted_iota(jnp.int32, (t, LANES), 1)
    lo_half = lane < HEAD_DIM
    zero = jnp.zeros((t, LANES), q.dtype)
    rows = []
    for h in range(n_heads):
        src = h * HEAD_DIM
        dst = (h // rep) * HEAD_DIM
        piece = q[:, src // LANES * LANES:(src // LANES + 1) * LANES]
        if src % LANES != dst % LANES:
            piece = pltpu.roll(piece.astype(F32), HEAD_DIM, 1).astype(q.dtype)
        piece = jnp.where(lo_half if dst % LANES == 0 else ~lo_half, piece, zero)
        blocks = [piece if j == dst // LANES else zero for j in range(kv_width // LANES)]
        rows.append(jnp.concatenate(blocks, axis=1) if len(blocks) > 1 else piece)
    return jnp.concatenate(rows, axis=0)


def _gather_heads(res, n_heads, rep):
    t = res.shape[0] // n_heads
    lane = lax.broadcasted_iota(jnp.int32, (t, LANES), 1)
    lo_half = lane < HEAD_DIM
    pieces = []
    for h in range(n_heads):
        src = (h // rep) * HEAD_DIM
        dst = h * HEAD_DIM
        piece = res[h * t:(h + 1) * t, src // LANES * LANES:(src // LANES + 1) * LANES]
        if src % LANES != dst % LANES:
            piece = pltpu.roll(piece, HEAD_DIM, 1)
        pieces.append(piece)
    blocks = [jnp.where(lo_half, pieces[2 * j], pieces[2 * j + 1]) for j in range(n_heads // 2)]
    return jnp.concatenate(blocks, axis=1)


def _attn_kernel(q_ref, k_ref, v_ref, o_ref, *, rep):
    n_heads = q_ref.shape[1] // HEAD_DIM
    lk = k_ref.shape[1]
    q = _pad_heads(q_ref[...], n_heads, k_ref.shape[2], rep)
    edges = list(range(0, lk, ATTN_CHUNK)) + [lk]
    m = l = acc = None
    for k0, k1 in zip(edges[:-1], edges[1:]):
        s = _dot_nt(q, k_ref[0, k0:k1, :])
        m_new = jnp.max(s, axis=-1, keepdims=True)
        if m is not None:
            m_new = jnp.maximum(m, m_new)
        p = jnp.exp(s - m_new)
        l_new = jnp.sum(p, axis=-1, keepdims=True)
        acc_new = _dot(p.astype(BF16), v_ref[0, k0:k1, :])
        if m is not None:
            alpha = jnp.exp(m - m_new)
            l_new = alpha * l + l_new
            acc_new = alpha * acc + acc_new
        m, l, acc = m_new, l_new, acc_new
    o_ref[...] = _gather_heads(acc / l, n_heads, rep)


def attention(q, k, v, seq_len):
    r, gq = q.shape
    nb, lk, kw = k.shape
    rep = gq // kw
    tq = _tile(seq_len, 128)
    nq = seq_len // tq
    return pl.pallas_call(
        functools.partial(_attn_kernel, rep=rep),
        grid=(nb, nq),
        in_specs=[pl.BlockSpec((tq, gq), lambda i, j: (i * nq + j, 0)),
                  pl.BlockSpec((1, lk, kw), lambda i, j: (i, 0, 0)),
                  pl.BlockSpec((1, lk, kw), lambda i, j: (i, 0, 0))],
        out_specs=pl.BlockSpec((tq, gq), lambda i, j: (i * nq + j, 0)),
        out_shape=jax.ShapeDtypeStruct((r, gq), F32),
        compiler_params=_params(("arbitrary", "arbitrary")),
        name="attention",
    )(q, k, v)


def _na_kernel(p_ref, pc_ref, bias_ref, o_ref, *, rows, kh):
    g = o_ref.shape[2]
    n_heads = g // HEAD_DIM
    band = kh * GRID_W
    kc = pc_ref[0, :, g:2 * g]
    vc = pc_ref[0, :, 2 * g:3 * g]

    def one_row(r, carry):
        rs = jnp.clip(r - kh // 2, 0, rows - kh)
        q0 = pl.multiple_of(r * GRID_W, GRID_W)
        k0 = pl.multiple_of(rs * GRID_W, GRID_W)
        q = _pad_heads(p_ref[0, pl.ds(q0, GRID_W), 0:g], n_heads, g, 1)
        s1 = _dot_nt(q, p_ref[0, pl.ds(k0, band), g:2 * g]) + bias_ref[r - rs]
        s2 = _dot_nt(q, kc)
        m = jnp.maximum(jnp.max(s1, axis=-1, keepdims=True), jnp.max(s2, axis=-1, keepdims=True))
        p1 = jnp.exp(s1 - m)
        p2 = jnp.exp(s2 - m)
        l = jnp.sum(p1, axis=-1, keepdims=True) + jnp.sum(p2, axis=-1, keepdims=True)
        res = (_dot(p1.astype(BF16), p_ref[0, pl.ds(k0, band), 2 * g:3 * g]) + _dot(p2.astype(BF16), vc)) / l
        o_ref[0, pl.ds(q0, GRID_W), :] = _gather_heads(res, n_heads, 1)
        return carry

    lax.fori_loop(0, rows, one_row, 0, unroll=8)


def neighbourhood_attention(p, pc, bias):
    nb, l, g3 = p.shape
    g = g3 // 3
    lc = pc.shape[1]
    rows = l // GRID_W
    kh = min(NA_ROWS, rows)
    return pl.pallas_call(
        functools.partial(_na_kernel, rows=rows, kh=kh),
        grid=(nb,),
        in_specs=[pl.BlockSpec((1, l, g3), lambda i: (i, 0, 0)),
                  pl.BlockSpec((1, lc, g3), lambda i: (i, 0, 0)),
                  pl.BlockSpec(bias.shape, lambda i: (0, 0, 0))],
        out_specs=pl.BlockSpec((1, l, g), lambda i: (i, 0, 0)),
        out_shape=jax.ShapeDtypeStruct((nb, l, g), F32),
        compiler_params=_params(("arbitrary",)),
        name="neighbourhood_attention",
    )(p, pc, bias)


def _na_bias_kernel(rpb_ref, o_ref, *, kh):
    h = pl.program_id(0)
    q = lax.broadcasted_iota(jnp.int32, (GRID_W, GRID_W), 0)
    k = lax.broadcasted_iota(jnp.int32, (GRID_W, GRID_W), 1)
    start = jnp.clip(q - NA_COLS // 2, 0, GRID_W - NA_COLS)
    in_win = (k >= start) & (k < start + NA_COLS)
    off = k - q + NA_COLS - 1
    neg = jnp.full((GRID_W, GRID_W), -jnp.inf, F32)
    blocks = []
    for ro in range(2 * NA_ROWS - 1):
        t = neg
        for c in range(2 * NA_COLS - 1):
            t = jnp.where(off == c, rpb_ref[h, ro, c], t)
        blocks.append(jnp.where(in_win, t, neg))
    for di in range(kh):
        for i in range(kh):
            o_ref[di, :, i * GRID_W:(i + 1) * GRID_W] = blocks[i - di + NA_ROWS - 1]


def na_bias_table(rpb, rows):
    kh = min(NA_ROWS, rows)
    nh = rpb.shape[0]
    return pl.pallas_call(
        functools.partial(_na_bias_kernel, kh=kh),
        grid=(nh,),
        in_specs=[pl.BlockSpec(memory_space=pltpu.SMEM)],
        out_specs=pl.BlockSpec((kh, GRID_W, kh * GRID_W), lambda i: (0, i, 0)),
        out_shape=jax.ShapeDtypeStruct((kh, nh * GRID_W, kh * GRID_W), F32),
        compiler_params=_params(("arbitrary",)),
        name="na_bias",
    )(rpb)


def _shift_rows(x, d, t):
    n = x.shape[0]
    y = pltpu.roll(x, d % n, 0)
    src = t - d
    return jnp.where((src >= 0) & (src < n), y, 0.0)


def _pool_kernel(x_ref, w_ref, scale_ref, o_ref):
    x = x_ref[0]
    n, g = x.shape
    pg = g // len(POOL_WINDOWS)
    t = lax.broadcasted_iota(jnp.int32, x.shape, 0)
    group = lax.broadcasted_iota(jnp.int32, x.shape, 1) // pg
    fwd = x
    bwd = x
    cur = 1
    total = jnp.zeros_like(x)
    count = jnp.ones_like(x)
    for j, w in enumerate(POOL_WINDOWS):
        half = w // 2
        while cur < half:
            fwd = fwd + _shift_rows(fwd, -cur, t)
            bwd = bwd + _shift_rows(bwd, cur, t)
            cur *= 2
        win = _shift_rows(bwd, 1, t) + fwd
        lo = jnp.clip(t - half, 0, n)
        hi = jnp.clip(t - half + w, 0, n)
        total = jnp.where(group == j, win, total)
        count = jnp.where(group == j, (hi - lo).astype(F32), count)
    diff = total / count - x
    o_ref[0] = _dot(diff.astype(BF16), w_ref[...]) * scale_ref[...]


def pool_mixer(p, w_bd, scale):
    b, n, g = p.shape
    return pl.pallas_call(
        _pool_kernel,
        grid=(b,),
        in_specs=[pl.BlockSpec((1, n, g), lambda i: (i, 0, 0)),
                  pl.BlockSpec((g, g), lambda i: (0, 0)),
                  pl.BlockSpec((1, g), lambda i: (0, 0))],
        out_specs=pl.BlockSpec((1, n, g), lambda i: (i, 0, 0)),
        out_shape=jax.ShapeDtypeStruct((b, n, g), F32),
        compiler_params=_params(("arbitrary",)),
        name="pool_mixer",
    )(p, w_bd, scale)


def _wkv_scan_kernel(rf_ref, rb_ref, kkf_ref, kkb_ref, decf_ref, bf_ref, kmf_ref, decb_ref, bb_ref, kmb_ref,
                     vtf_ref, vtb_ref, e_ref, s0_ref, yf_ref, yb_ref, send_ref, s_scr):
    nb, t_blk, g = rf_ref.shape
    step_id = pl.program_id(0)

    @pl.when(step_id == 0)
    def _():
        s_scr[...] = s0_ref[...]

    yf_ref[...] = jnp.zeros_like(yf_ref)
    yb_ref[...] = jnp.zeros_like(yb_ref)
    lane = lax.broadcasted_iota(jnp.int32, (HEAD_DIM, g), 1) % SCAN_SUB
    lane128 = lax.broadcasted_iota(jnp.int32, (HEAD_DIM, LANES), 1)
    dirs = ((rf_ref, kkf_ref, decf_ref, bf_ref, kmf_ref, vtf_ref, yf_ref),
            (rb_ref, kkb_ref, decb_ref, bb_ref, kmb_ref, vtb_ref, yb_ref))
    ents = HEAD_DIM // 4
    for d in range(2):
        pltpu.matmul_push_rhs(e_ref[...], 0, d)
        pltpu.matmul_acc_lhs(0, jnp.zeros((16, g), BF16), d, load_staged_rhs=0)
        pltpu.matmul_pop(0, (16, g), F32, d)

    def write_y(d, b, y, tt_done, valid):
        y_ref = dirs[d][-1]
        tl = tt_done if d == 0 else t_blk - 1 - tt_done
        tl = jnp.clip(tl, 0, t_blk - 1)
        hit = (lane == tl % SCAN_SUB) & valid
        pltpu.store(y_ref.at[b, tl // SCAN_SUB], y, mask=hit)

    def one_token(tt, carry):
        for d, (r_ref, kk_ref, dec_ref, b_ref, km_ref, vt_ref, y_ref) in enumerate(dirs):
            tl = tt if d == 0 else t_blk - 1 - tt
            row = lambda ref, b: jnp.broadcast_to(ref[b, pl.ds(tl, 1), :], (HEAD_DIM, g))
            idx = (lane128 // SCAN_SUB) * SCAN_SUB + tl % SCAN_SUB
            s_old = [s_scr[d, b] for b in range(nb)]
            for b in range(nb):
                pltpu.matmul_acc_lhs(b * ents, s_old[b].astype(BF16) * row(kk_ref, b).astype(BF16), d)
            for b in range(nb):
                write_y(d, b, pltpu.matmul_pop((nb + b) * ents, (HEAD_DIM, g), F32, d), tt - 1, tt > 0)
            for b in range(nb):
                sa = pltpu.matmul_pop(b * ents, (HEAD_DIM, g), F32, d)
                vcol = jnp.concatenate([jnp.take_along_axis(vt_ref[b, tl // SCAN_SUB][:, j * LANES:(j + 1) * LANES], idx, axis=1)
                                        for j in range(g // LANES)], axis=1)
                s = s_old[b] * row(dec_ref, b) - sa * row(b_ref, b) + vcol * row(km_ref, b)
                s_scr[d, b] = s
                pltpu.matmul_acc_lhs((nb + b) * ents, s.astype(BF16) * row(r_ref, b).astype(BF16), d)
        return carry

    lax.fori_loop(0, t_blk, one_token, 0, unroll=4)
    for d in range(2):
        for b in range(nb):
            write_y(d, b, pltpu.matmul_pop((nb + b) * ents, (HEAD_DIM, g), F32, d), t_blk - 1, True)

    @pl.when(step_id == pl.num_programs(0) - 1)
    def _():
        send_ref[...] = s_scr[...]


def wkv_scan(pa3, kk, dec_f, b_f, km_f, dec_b, b_b, km_b, vt, eseg, s0):
    nb, l, g = kk.shape
    t_blk = SCAN_BLOCK
    nsub = t_blk // SCAN_SUB
    nblk = l // t_blk
    assert l % t_blk == 0
    seq_f = pl.BlockSpec((nb, t_blk, g), lambda i: (0, i, 0))
    seq_b = pl.BlockSpec((nb, t_blk, g), lambda i: (0, nblk - 1 - i, 0))
    vt_f = pl.BlockSpec((nb, nsub, HEAD_DIM, g), lambda i: (0, i, 0, 0))
    vt_b = pl.BlockSpec((nb, nsub, HEAD_DIM, g), lambda i: (0, nblk - 1 - i, 0, 0))
    state = pl.BlockSpec(s0.shape, lambda i: (0, 0, 0, 0))
    return pl.pallas_call(
        _wkv_scan_kernel,
        grid=(nblk,),
        in_specs=[seq_f, seq_b, seq_f, seq_b, seq_f, seq_f, seq_f, seq_b, seq_b, seq_b,
                  vt_f, vt_b, pl.BlockSpec(eseg.shape, lambda i: (0, 0)), state],
        out_specs=[vt_f, vt_b, state],
        out_shape=[jax.ShapeDtypeStruct(vt.shape, F32), jax.ShapeDtypeStruct(vt.shape, F32),
                   jax.ShapeDtypeStruct(s0.shape, F32)],
        scratch_shapes=[pltpu.VMEM(s0.shape, F32)],
        compiler_params=_params(("arbitrary",)),
        name="wkv_scan",
    )(pa3, pa3, kk, kk, dec_f, b_f, km_f, dec_b, b_b, km_b, vt, vt, eseg, s0)


def _block_ones(n, seg):
    idx = np.arange(n) // seg
    return jnp.asarray(idx[:, None] == idx[None, :], dtype=BF16)


def _padded_rows(w, offset, n):
    return jnp.zeros((n, w.shape[1]), w.dtype).at[offset:offset + w.shape[0]].set(w)


def _rope_tables(n_tokens, n_rep):
    t = jnp.arange(n_tokens)
    row = (t // GRID_W).astype(F32)
    col = (t % GRID_W).astype(F32)
    n_freq = HEAD_DIM // 4
    inv_freq = ROPE_THETA ** (-jnp.arange(n_freq, dtype=F32) / n_freq)
    ang = jnp.concatenate([row[:, None] * inv_freq, col[:, None] * inv_freq], axis=-1)
    cos, sin = jnp.cos(ang), jnp.sin(ang)
    cos_h = jnp.concatenate([cos, cos], axis=-1)
    sin_h = jnp.concatenate([-sin, sin], axis=-1)
    return jnp.tile(cos_h, (1, n_rep)), jnp.tile(sin_h, (1, n_rep))


def kernel(x, c, ctx, c_ctx, ada_w, ada_b, norm1_g, norm2_g, w_in, w_out, rwkv_w0, rwkv_w2, rwkv_a0, rwkv_a2,
           rwkv_k_k, rwkv_k_a, rwkv_r_k, rwkv_g2, rwkv_gn_w, rwkv_gn_b, na_rpb, gqa_q_gain, gqa_k_gain, pool_w,
           pool_scale, mlp_w1, mlp_w2, final_g):
    nb, l, d = x.shape
    lc = ctx.shape[1]
    depth = ada_w.shape[0]
    g = d // 4
    nh = g // HEAD_DIM
    nkv = nh // 2
    kvw = nkv * HEAD_DIM
    dr = rwkv_w2.shape[2]
    ir = rwkv_a2.shape[2]
    gr = rwkv_g2.shape[1]
    assert 2 * dr + 2 * ir + gr <= g and nb + 1 <= 8
    rows = l // GRID_W

    cpad = jnp.zeros((8, d), F32).at[:nb].set(c).at[nb].set(c_ctx)
    mods = adaln(cpad, ada_w, ada_b)

    eseg = _block_ones(g, HEAD_DIM)
    ekv = _block_ones(kvw, HEAD_DIM)
    cos_q, sin_q = _rope_tables(l, nh)
    ones_c, zeros_c = jnp.ones((lc, g), F32), jnp.zeros((lc, g), F32)

    splits = np.cumsum([0, g, g, g, dr, dr, ir, ir, gr, g, g, g, g, kvw, kvw, g])
    lowrank_w = splits[8] - splits[3]

    xl = x.reshape(nb * l, d)
    xc = ctx.reshape(nb * lc, d)
    zero_state = jnp.zeros((2, nb, HEAD_DIM, g), F32)

    for i in range(depth):
        need_ctx_out = i < depth - 1
        mod_l = mods[i, :nb].reshape(nb, N_MOD, 1, d)
        mod_c = mods[i, nb].reshape(N_MOD, 1, 1, d)
        ml = [mod_l[:, k] for k in range(N_MOD)]
        mc = [mod_c[k] for k in range(N_MOD)]

        wi = w_in[i]
        w_inp = jnp.concatenate([wi[:, :splits[8]], jnp.zeros((d, g - lowrank_w), F32),
                                 wi[:, splits[8]:splits[9]] * HEAD_DIM ** -0.5, wi[:, splits[9]:]], axis=1).astype(BF16)
        g1 = norm1_g[i].reshape(1, d)
        g2 = norm2_g[i].reshape(1, d)

        w2p = jnp.stack([_padded_rows(rwkv_w2[i, 0], 0, g), _padded_rows(rwkv_w2[i, 1], dr, g)]).astype(BF16)
        a2p = jnp.stack([_padded_rows(rwkv_a2[i, 0], 2 * dr, g),
                         _padded_rows(rwkv_a2[i, 1], 2 * dr + ir, g)]).astype(BF16)
        g2p = _padded_rows(rwkv_g2[i], 2 * dr + 2 * ir, g).astype(BF16)
        r_k = rwkv_r_k[i].reshape(1, g)
        gn_w = rwkv_gn_w[i].reshape(1, g)
        gn_b = rwkv_gn_b[i].reshape(1, g)
        rwkv_params = (rwkv_w0[i], rwkv_a0[i], w2p, a2p, rwkv_k_k[i].reshape(1, g), rwkv_k_a[i].reshape(1, g), eseg)
        q_gain = jnp.tile(gqa_q_gain[i], nh).reshape(1, g)
        k_gain = jnp.tile(gqa_k_gain[i], nkv).reshape(1, kvw)

        proj_l = inproj(xl, g1, ml[0], ml[1], w_inp, l, rwkv_params, (q_gain, k_gain, cos_q, sin_q, ekv), l, True)
        proj_c = inproj(xc, g1, mc[0], mc[1], w_inp, nb * lc, rwkv_params, (q_gain, k_gain, ones_c, zeros_c, ekv), lc, False)
        pa_l, pb_l, pd_l = proj_l[:3]
        pa_c, pb_c, pd_c = proj_c[:3]

        def scan_inputs(proj, seq):
            terms3 = [t.reshape(nb, seq, g) for t in proj[3:10]]
            return terms3, proj[10].reshape(nb, seq // SCAN_SUB, HEAD_DIM, g)

        terms3_c, vt_c = scan_inputs(proj_c, lc)
        yf_c, yb_c, state_c = wkv_scan(pa_c.reshape(nb, lc, 4 * g), *terms3_c, vt_c, eseg, zero_state)
        terms3_l, vt_l = scan_inputs(proj_l, l)
        yf_l, yb_l, _ = wkv_scan(pa_l.reshape(nb, l, 4 * g), *terms3_l, vt_l, eseg, state_c)

        bias = na_bias_table(na_rpb[i], rows)
        b_l = neighbourhood_attention(pb_l.reshape(nb, l, 3 * g), pb_c.reshape(nb, lc, 3 * g), bias).reshape(nb * l, g)

        qc_l, kc_l, vc_l = proj_l[11:14]
        qc_c, kc_c, vc_c = proj_c[11:14]
        kc_c, vc_c = kc_c.reshape(nb, lc, kvw), vc_c.reshape(nb, lc, kvw)
        k_all = jnp.concatenate([kc_c, kc_l.reshape(nb, l, kvw)], axis=1)
        v_all = jnp.concatenate([vc_c, vc_l.reshape(nb, l, kvw)], axis=1)
        c_l = attention(qc_l, k_all, v_all, l)

        w_bd = jax.scipy.linalg.block_diag(*[pool_w[i, k] for k in range(len(POOL_WINDOWS))]).astype(BF16)
        p_scale = pool_scale[i].reshape(1, g)
        d_l = pool_mixer(pd_l.reshape(nb, l, g), w_bd, p_scale).reshape(nb * l, g)

        w_out4 = w_out[i].reshape(4, g, d).astype(BF16)
        w1 = mlp_w1[i].astype(BF16)
        w2 = mlp_w2[i].astype(BF16)
        readout_params = (r_k, g2p, gn_w, gn_b, eseg)
        xl = outproj(yf_l.reshape(-1, HEAD_DIM, g), yb_l.reshape(-1, HEAD_DIM, g), pa_l, proj_l[6], proj_l[9],
                     readout_params, (b_l, c_l, d_l), w_out4, xl, ml[2], l)
        xl = mlp(xl, g2, ml[3], ml[4], ml[5], w1, w2, l)

        if need_ctx_out:
            b_c = attention(pb_c[:, :g], pb_c[:, g:2 * g].reshape(nb, lc, g), pb_c[:, 2 * g:].reshape(nb, lc, g), lc)
            c_c = attention(qc_c, kc_c, vc_c, lc)
            d_c = pool_mixer(pd_c.reshape(nb, lc, g), w_bd, p_scale).reshape(nb * lc, g)
            xc = outproj(yf_c.reshape(-1, HEAD_DIM, g), yb_c.reshape(-1, HEAD_DIM, g), pa_c, proj_c[6], proj_c[9],
                         readout_params, (b_c, c_c, d_c), w_out4, xc, mc[2], nb * lc)
            xc = mlp(xc, g2, mc[3], mc[4], mc[5], w1, w2, nb * lc)

    return final_norm(xl, final_g.reshape(1, d)).reshape(nb, l, d)
```

```python
import functools

import jax
import jax.numpy as jnp
import numpy as np
from jax import lax
from jax.experimental import pallas as pl
from jax.experimental.pallas import tpu as pltpu

F32 = jnp.float32
BF16 = jnp.bfloat16

HEAD_DIM = 64
GRID_W = 64
NA_ROWS = 8
NA_COLS = 16
ROPE_THETA = 10000.0
POOL_WINDOWS = (2, 4, 8, 16)
NORM_EPS = 1e-6
RWKV_GN_EPS = 1e-5 * HEAD_DIM
N_MOD = 6
LANES = 128
SCAN_BLOCK = 128
SCAN_SUB = 64
ATTN_CHUNK = 2048
VMEM_LIMIT = 52 * 1024 * 1024


def _params(sem):
    return pltpu.CompilerParams(dimension_semantics=sem, vmem_limit_bytes=VMEM_LIMIT)


def _tile(n, pref):
    t = min(n, pref)
    assert n % t == 0, (n, pref)
    return t


def _dot(a, b):
    return jnp.dot(a, b, preferred_element_type=F32)


def _dot_nt(a, b):
    return lax.dot_general(a, b, (((1,), (1,)), ((), ())), preferred_element_type=F32)


def _seg_sum(x, e):
    hi = x.astype(BF16)
    lo = (x - hi.astype(F32)).astype(BF16)
    return _dot(hi, e) + _dot(lo, e)


def _norm_mod(x, g, shift, scale):
    ms = jnp.mean(x * x, axis=-1, keepdims=True)
    h = x * lax.rsqrt(ms + NORM_EPS) * g
    return h * (1.0 + scale) + shift


def _adaln_kernel(c_ref, w_ref, b_ref, o_ref):
    c = c_ref[...]
    s = c * jax.nn.sigmoid(c)
    o_ref[0] = jnp.dot(s, w_ref[0], preferred_element_type=F32, precision=lax.Precision.HIGHEST) + b_ref[0]


def adaln(cpad, ada_w, ada_b):
    depth, d, n = ada_w.shape
    tn = _tile(n, 1536)
    return pl.pallas_call(
        _adaln_kernel,
        grid=(depth, n // tn),
        in_specs=[pl.BlockSpec((8, d), lambda i, j: (0, 0)),
                  pl.BlockSpec((1, d, tn), lambda i, j: (i, 0, j)),
                  pl.BlockSpec((1, 1, tn), lambda i, j: (i, 0, j))],
        out_specs=pl.BlockSpec((1, 8, tn), lambda i, j: (i, 0, j)),
        out_shape=jax.ShapeDtypeStruct((depth, 8, n), F32),
        compiler_params=_params(("arbitrary", "arbitrary")),
        name="adaln",
    )(cpad, ada_w, ada_b.reshape(depth, 1, n))


def _swap_halves(y):
    n = y.shape[-1]
    lane = lax.broadcasted_iota(jnp.int32, y.shape, 1)
    half = HEAD_DIM // 2
    return jnp.where(lane % HEAD_DIM < half, pltpu.roll(y, n - half, 1), pltpu.roll(y, half, 1))


def _to_tiles(x):
    g = x.shape[1]
    tiles = []
    for s in range(x.shape[0] // SCAN_SUB):
        blk = x[s * SCAN_SUB:(s + 1) * SCAN_SUB, :]
        tiles.append(jnp.concatenate([blk[:, h * HEAD_DIM:(h + 1) * HEAD_DIM].T for h in range(g // HEAD_DIM)], axis=1))
    return tiles


def _from_tiles(tiles):
    g = tiles[0].shape[1]
    return jnp.concatenate(
        [jnp.concatenate([t[:, h * SCAN_SUB:(h + 1) * SCAN_SUB].T for h in range(g // HEAD_DIM)], axis=1) for t in tiles],
        axis=0)


def _inproj_kernel(x_ref, g_ref, sh_ref, sc_ref, w_ref,
                   w0_ref, a0_ref, w2_ref, a2_ref, kk_w_ref, ka_ref, e_ref,
                   qg_ref, kg_ref, cos_ref, sin_ref, ek_ref,
                   oa_ref, ob_ref, od_ref,
                   kk_ref, decf_ref, bf_ref, kmf_ref, decb_ref, bb_ref, kmb_ref, vt_ref,
                   q_ref, kc_ref, vc_ref, *, rope):
    g = od_ref.shape[1]
    gk = kc_ref.shape[1]
    h = _norm_mod(x_ref[...], g_ref[...], sh_ref[0], sc_ref[0]).astype(BF16)
    e = e_ref[...]

    pa = _dot(h, w_ref[:, 0:4 * g])
    pc = _dot(h, w_ref[:, 7 * g:9 * g])

    oa_ref[...] = pa
    k = pa[:, g:2 * g]
    lr = pa[:, 3 * g:4 * g]
    for s, tile in enumerate(_to_tiles(pa[:, 2 * g:3 * g])):
        vt_ref[s] = tile
    kx = k * kk_w_ref[...]
    norm = jnp.sqrt(_seg_sum(kx * kx, e))
    kk = kx / jnp.maximum(norm, 1e-12)
    kk_ref[...] = kk
    lr_t = jnp.tanh(lr).astype(BF16)
    lr_b = lr.astype(BF16)
    for d, (dec_ref, b_ref, km_ref) in enumerate(((decf_ref, bf_ref, kmf_ref), (decb_ref, bb_ref, kmb_ref))):
        z = w0_ref[d:d + 1, :] + _dot(lr_t, w2_ref[d])
        softplus_neg = jnp.maximum(-z, 0.0) + jnp.log1p(jnp.exp(-jnp.abs(z)))
        w = -softplus_neg - 0.5
        dec_ref[...] = jnp.exp(-jnp.exp(w))
        a = jax.nn.sigmoid(a0_ref[d:d + 1, :] + _dot(lr_b, a2_ref[d]))
        b_ref[...] = kk * a
        km_ref[...] = k * (1.0 + (a - 1.0) * ka_ref[...])

    def normed(x, gain, seg):
        ms = _seg_sum(x * x, seg) * (1.0 / HEAD_DIM)
        return x * lax.rsqrt(ms + NORM_EPS) * gain

    q = normed(pc[:, :g], qg_ref[...], e)
    kc = normed(pc[:, g:g + gk], kg_ref[...], ek_ref[...])
    if rope:
        cos = cos_ref[...]
        sin = sin_ref[...]
        q = q * cos + _swap_halves(q) * sin
        kc = kc * cos[:, :gk] + _swap_halves(kc) * sin[:, :gk]
    q_ref[...] = (q * HEAD_DIM ** -0.5).astype(BF16)
    kc_ref[...] = kc.astype(BF16)
    vc_ref[...] = pc[:, g + gk:].astype(BF16)

    ob_ref[...] = _dot(h, w_ref[:, 4 * g:7 * g]).astype(ob_ref.dtype)

    od_ref[...] = _dot(h, w_ref[:, 9 * g:10 * g])


def inproj(x, gain, shift, scale, w, rows_per_mod, rwkv_params, gqa_params, seq_len, rope):
    r, d = x.shape
    g = w.shape[1] // 10
    gk = g // 2
    tm = _tile(min(rows_per_mod, seq_len), 512)
    tpb = rows_per_mod // tm
    nseq = seq_len // tm
    modspec = pl.BlockSpec((1, 1, d), lambda i: (i // tpb, 0, 0))
    full2 = lambda a: pl.BlockSpec(a.shape, lambda i: (0,) * a.ndim)
    rows = lambda n: pl.BlockSpec((tm, n), lambda i: (i, 0))
    q_gain, k_gain, cos, sin, ekv = gqa_params
    table = pl.BlockSpec((tm, g), lambda i: (i % nseq, 0))
    out_specs = ([rows(4 * g), rows(3 * g), rows(g)] + [rows(g)] * 7
                 + [pl.BlockSpec((tm // SCAN_SUB, HEAD_DIM, g), lambda i: (i, 0, 0))] + [rows(g), rows(gk), rows(gk)])
    out_shape = ([jax.ShapeDtypeStruct((r, 4 * g), F32), jax.ShapeDtypeStruct((r, 3 * g), BF16),
                  jax.ShapeDtypeStruct((r, g), F32)] + [jax.ShapeDtypeStruct((r, g), F32)] * 7
                 + [jax.ShapeDtypeStruct((r // SCAN_SUB, HEAD_DIM, g), F32), jax.ShapeDtypeStruct((r, g), BF16),
                    jax.ShapeDtypeStruct((r, gk), BF16), jax.ShapeDtypeStruct((r, gk), BF16)])
    return pl.pallas_call(
        functools.partial(_inproj_kernel, rope=rope),
        grid=(r // tm,),
        in_specs=[rows(d), pl.BlockSpec((1, d), lambda i: (0, 0)), modspec, modspec, full2(w)]
                 + [full2(p) for p in rwkv_params] + [full2(q_gain), full2(k_gain), table, table, full2(ekv)],
        out_specs=out_specs,
        out_shape=out_shape,
        compiler_params=_params(("arbitrary",)),
        name="inproj",
    )(x, gain, shift, scale, w, *rwkv_params, q_gain, k_gain, cos, sin, ekv)


def _outproj_kernel(yf_ref, yb_ref, p_ref, kmf_ref, kmb_ref, rk_ref, g2_ref, gnw_ref, gnb_ref, e_ref,
                    b_ref, c_ref, d_ref, w_ref, x_ref, gate_ref, o_ref):
    g = b_ref.shape[1]
    e = e_ref[...]
    r = p_ref[:, :g]
    v = p_ref[:, 2 * g:3 * g]
    lr = p_ref[:, 3 * g:4 * g]
    y = _from_tiles([yf_ref[s] + yb_ref[s] for s in range(yf_ref.shape[0])])
    mu = _seg_sum(y, e) * (1.0 / HEAD_DIM)
    yc = y - mu
    var = _seg_sum(yc * yc, e) * (1.0 / HEAD_DIM)
    yn = yc * lax.rsqrt(var + RWKV_GN_EPS) * gnw_ref[...] + gnb_ref[...]
    bonus = _seg_sum(r * (kmf_ref[...] + kmb_ref[...]) * rk_ref[...], e) * v
    a = (yn + bonus) * _dot(jax.nn.sigmoid(lr).astype(BF16), g2_ref[...])
    acc = _dot(b_ref[...].astype(BF16), w_ref[1])
    acc += _dot(c_ref[...].astype(BF16), w_ref[2])
    acc += _dot(d_ref[...].astype(BF16), w_ref[3])
    acc += _dot(a.astype(BF16), w_ref[0])
    o_ref[...] = x_ref[...] + gate_ref[0] * acc


def outproj(yf, yb, pa, km_f, km_b, readout_params, mix_bcd, w4, x, gate, rows_per_mod):
    r, d = x.shape
    g = w4.shape[1]
    tm = _tile(rows_per_mod, 512)
    tpb = rows_per_mod // tm
    rows = lambda n: pl.BlockSpec((tm, n), lambda i: (i, 0))
    tiles = pl.BlockSpec((tm // SCAN_SUB, HEAD_DIM, g), lambda i: (i, 0, 0))
    full2 = lambda a: pl.BlockSpec(a.shape, lambda i: (0,) * a.ndim)
    return pl.pallas_call(
        _outproj_kernel,
        grid=(r // tm,),
        in_specs=[tiles, tiles, rows(4 * g), rows(g), rows(g)] + [full2(p) for p in readout_params]
                 + [rows(g), rows(g), rows(g), full2(w4), rows(d), pl.BlockSpec((1, 1, d), lambda i: (i // tpb, 0, 0))],
        out_specs=rows(d),
        out_shape=jax.ShapeDtypeStruct((r, d), F32),
        compiler_params=_params(("arbitrary",)),
        name="outproj",
    )(yf, yb, pa, km_f, km_b, *readout_params, *mix_bcd, w4, x, gate)


def _mlp_kernel(x_ref, g_ref, sh_ref, sc_ref, gate_ref, w1_ref, w2_ref, o_ref, h_scr, acc_scr):
    j = pl.program_id(1)

    @pl.when(j == 0)
    def _():
        h_scr[...] = _norm_mod(x_ref[...], g_ref[...], sh_ref[0], sc_ref[0]).astype(BF16)
        acc_scr[...] = jnp.zeros_like(acc_scr)

    u = jnp.maximum(_dot(h_scr[...], w1_ref[...]), 0.0)
    acc_scr[...] += _dot((u * u).astype(BF16), w2_ref[...])

    @pl.when(j == pl.num_programs(1) - 1)
    def _():
        o_ref[...] = x_ref[...] + gate_ref[0] * acc_scr[...]


def mlp(x, g, shift, scale, gate, w1, w2, rows_per_mod):
    r, d = x.shape
    dff = w1.shape[1]
    tm = _tile(rows_per_mod, 1024)
    tf = _tile(dff, 1024)
    tpb = rows_per_mod // tm
    modspec = pl.BlockSpec((1, 1, d), lambda i, j: (i // tpb, 0, 0))
    return pl.pallas_call(
        _mlp_kernel,
        grid=(r // tm, dff // tf),
        in_specs=[pl.BlockSpec((tm, d), lambda i, j: (i, 0)),
                  pl.BlockSpec((1, d), lambda i, j: (0, 0)),
                  modspec, modspec, modspec,
                  pl.BlockSpec((d, tf), lambda i, j: (0, j)),
                  pl.BlockSpec((tf, d), lambda i, j: (j, 0))],
        out_specs=pl.BlockSpec((tm, d), lambda i, j: (i, 0)),
        out_shape=jax.ShapeDtypeStruct((r, d), F32),
        scratch_shapes=[pltpu.VMEM((tm, d), BF16), pltpu.VMEM((tm, d), F32)],
        compiler_params=_params(("arbitrary", "arbitrary")),
        name="mlp",
    )(x, g, shift, scale, gate, w1, w2)


def _final_norm_kernel(x_ref, g_ref, o_ref):
    x = x_ref[...]
    ms = jnp.mean(x * x, axis=-1, keepdims=True)
    o_ref[...] = x * lax.rsqrt(ms + NORM_EPS) * g_ref[...]


def final_norm(x, g):
    r, d = x.shape
    tm = _tile(r, 1024)
    return pl.pallas_call(
        _final_norm_kernel,
        grid=(r // tm,),
        in_specs=[pl.BlockSpec((tm, d), lambda i: (i, 0)), pl.BlockSpec((1, d), lambda i: (0, 0))],
        out_specs=pl.BlockSpec((tm, d), lambda i: (i, 0)),
        out_shape=jax.ShapeDtypeStruct((r, d), F32),
        compiler_params=_params(("arbitrary",)),
        name="final_norm",
    )(x, g)


def _pad_heads(q, n_heads, kv_width, rep):
    assert kv_width % LANES == 0
    t = q.shape[0]
    lane = lax.broadcasted_iota(jnp.int32, (t, LANES), 1)
    lo_half = lane < HEAD_DIM
    zero = jnp.zeros((t, LANES), q.dtype)
    rows = []
    for h in range(n_heads):
        src = h * HEAD_DIM
        dst = (h // rep) * HEAD_DIM
        piece = q[:, src // LANES * LANES:(src // LANES + 1) * LANES]
        if src % LANES != dst % LANES:
            piece = pltpu.roll(piece.astype(F32), HEAD_DIM, 1).astype(q.dtype)
        piece = jnp.where(lo_half if dst % LANES == 0 else ~lo_half, piece, zero)
        blocks = [piece if j == dst // LANES else zero for j in range(kv_width // LANES)]
        rows.append(jnp.concatenate(blocks, axis=1) if len(blocks) > 1 else piece)
    return jnp.concatenate(rows, axis=0)


def _gather_heads(res, n_heads, rep):
    t = res.shape[0] // n_heads
    lane = lax.broadcasted_iota(jnp.int32, (t, LANES), 1)
    lo_half = lane < HEAD_DIM
    pieces = []
    for h in range(n_heads):
        src = (h // rep) * HEAD_DIM
        dst = h * HEAD_DIM
        piece = res[h * t:(h + 1) * t, src // LANES * LANES:(src // LANES + 1) * LANES]
        if src % LANES != dst % LANES:
            piece = pltpu.roll(piece, HEAD_DIM, 1)
        pieces.append(piece)
    blocks = [jnp.where(lo_half, pieces[2 * j], pieces[2 * j + 1]) for j in range(n_heads // 2)]
    return jnp.concatenate(blocks, axis=1)


def _attn_kernel(q_ref, k_ref, v_ref, o_ref, *, rep):
    n_heads = q_ref.shape[1] // HEAD_DIM
    lk = k_ref.shape[1]
    q = _pad_heads(q_ref[...], n_heads, k_ref.shape[2], rep)
    edges = list(range(0, lk, ATTN_CHUNK)) + [lk]
    m = l = acc = None
    for k0, k1 in zip(edges[:-1], edges[1:]):
        s = _dot_nt(q, k_ref[0, k0:k1, :])
        m_new = jnp.max(s, axis=-1, keepdims=True)
        if m is not None:
            m_new = jnp.maximum(m, m_new)
        p = jnp.exp(s - m_new)
        l_new = jnp.sum(p, axis=-1, keepdims=True)
        acc_new = _dot(p.astype(BF16), v_ref[0, k0:k1, :])
        if m is not None:
            alpha = jnp.exp(m - m_new)
            l_new = alpha * l + l_new
            acc_new = alpha * acc + acc_new
        m, l, acc = m_new, l_new, acc_new
    o_ref[...] = _gather_heads(acc / l, n_heads, rep)


def attention(q, k, v, seq_len):
    r, gq = q.shape
    nb, lk, kw = k.shape
    rep = gq // kw
    tq = _tile(seq_len, 128)
    nq = seq_len // tq
    return pl.pallas_call(
        functools.partial(_attn_kernel, rep=rep),
        grid=(nb, nq),
        in_specs=[pl.BlockSpec((tq, gq), lambda i, j: (i * nq + j, 0)),
                  pl.BlockSpec((1, lk, kw), lambda i, j: (i, 0, 0)),
                  pl.BlockSpec((1, lk, kw), lambda i, j: (i, 0, 0))],
        out_specs=pl.BlockSpec((tq, gq), lambda i, j: (i * nq + j, 0)),
        out_shape=jax.ShapeDtypeStruct((r, gq), F32),
        compiler_params=_params(("arbitrary", "arbitrary")),
        name="attention",
    )(q, k, v)


def _na_kernel(p_ref, pc_ref, bias_ref, o_ref, *, rows, kh):
    g = o_ref.shape[2]
    n_heads = g // HEAD_DIM
    band = kh * GRID_W
    kc = pc_ref[0, :, g:2 * g]
    vc = pc_ref[0, :, 2 * g:3 * g]

    def one_row(r, carry):
        rs = jnp.clip(r - kh // 2, 0, rows - kh)
        q0 = pl.multiple_of(r * GRID_W, GRID_W)
        k0 = pl.multiple_of(rs * GRID_W, GRID_W)
        q = _pad_heads(p_ref[0, pl.ds(q0, GRID_W), 0:g], n_heads, g, 1)
        s1 = _dot_nt(q, p_ref[0, pl.ds(k0, band), g:2 * g]) + bias_ref[r - rs]
        s2 = _dot_nt(q, kc)
        m = jnp.maximum(jnp.max(s1, axis=-1, keepdims=True), jnp.max(s2, axis=-1, keepdims=True))
        p1 = jnp.exp(s1 - m)
        p2 = jnp.exp(s2 - m)
        l = jnp.sum(p1, axis=-1, keepdims=True) + jnp.sum(p2, axis=-1, keepdims=True)
        res = (_dot(p1.astype(BF16), p_ref[0, pl.ds(k0, band), 2 * g:3 * g]) + _dot(p2.astype(BF16), vc)) / l
        o_ref[0, pl.ds(q0, GRID_W), :] = _gather_heads(res, n_heads, 1)
        return carry

    lax.fori_loop(0, rows, one_row, 0, unroll=8)


def neighbourhood_attention(p, pc, bias):
    nb, l, g3 = p.shape
    g = g3 // 3
    lc = pc.shape[1]
    rows = l // GRID_W
    kh = min(NA_ROWS, rows)
    return pl.pallas_call(
        functools.partial(_na_kernel, rows=rows, kh=kh),
        grid=(nb,),
        in_specs=[pl.BlockSpec((1, l, g3), lambda i: (i, 0, 0)),
                  pl.BlockSpec((1, lc, g3), lambda i: (i, 0, 0)),
                  pl.BlockSpec(bias.shape, lambda i: (0, 0, 0))],
        out_specs=pl.BlockSpec((1, l, g), lambda i: (i, 0, 0)),
        out_shape=jax.ShapeDtypeStruct((nb, l, g), F32),
        compiler_params=_params(("arbitrary",)),
        name="neighbourhood_attention",
    )(p, pc, bias)


def _na_bias_kernel(rpb_ref, o_ref, *, kh):
    h = pl.program_id(0)
    q = lax.broadcasted_iota(jnp.int32, (GRID_W, GRID_W), 0)
    k = lax.broadcasted_iota(jnp.int32, (GRID_W, GRID_W), 1)
    start = jnp.clip(q - NA_COLS // 2, 0, GRID_W - NA_COLS)
    in_win = (k >= start) & (k < start + NA_COLS)
    off = k - q + NA_COLS - 1
    neg = jnp.full((GRID_W, GRID_W), -jnp.inf, F32)
    blocks = []
    for ro in range(2 * NA_ROWS - 1):
        t = neg
        for c in range(2 * NA_COLS - 1):
            t = jnp.where(off == c, rpb_ref[h, ro, c], t)
        blocks.append(jnp.where(in_win, t, neg))
    for di in range(kh):
        for i in range(kh):
            o_ref[di, :, i * GRID_W:(i + 1) * GRID_W] = blocks[i - di + NA_ROWS - 1]


def na_bias_table(rpb, rows):
    kh = min(NA_ROWS, rows)
    nh = rpb.shape[0]
    return pl.pallas_call(
        functools.partial(_na_bias_kernel, kh=kh),
        grid=(nh,),
        in_specs=[pl.BlockSpec(memory_space=pltpu.SMEM)],
        out_specs=pl.BlockSpec((kh, GRID_W, kh * GRID_W), lambda i: (0, i, 0)),
        out_shape=jax.ShapeDtypeStruct((kh, nh * GRID_W, kh * GRID_W), F32),
        compiler_params=_params(("arbitrary",)),
        name="na_bias",
    )(rpb)


def _shift_rows(x, d, t):
    n = x.shape[0]
    y = pltpu.roll(x, d % n, 0)
    src = t - d
    return jnp.where((src >= 0) & (src < n), y, 0.0)


def _pool_kernel(x_ref, w_ref, scale_ref, o_ref):
    x = x_ref[0]
    n, g = x.shape
    pg = g // len(POOL_WINDOWS)
    t = lax.broadcasted_iota(jnp.int32, x.shape, 0)
    group = lax.broadcasted_iota(jnp.int32, x.shape, 1) // pg
    fwd = x
    bwd = x
    cur = 1
    total = jnp.zeros_like(x)
    count = jnp.ones_like(x)
    for j, w in enumerate(POOL_WINDOWS):
        half = w // 2
        while cur < half:
            fwd = fwd + _shift_rows(fwd, -cur, t)
            bwd = bwd + _shift_rows(bwd, cur, t)
            cur *= 2
        win = _shift_rows(bwd, 1, t) + fwd
        lo = jnp.clip(t - half, 0, n)
        hi = jnp.clip(t - half + w, 0, n)
        total = jnp.where(group == j, win, total)
        count = jnp.where(group == j, (hi - lo).astype(F32), count)
    diff = total / count - x
    o_ref[0] = _dot(diff.astype(BF16), w_ref[...]) * scale_ref[...]


def pool_mixer(p, w_bd, scale):
    b, n, g = p.shape
    return pl.pallas_call(
        _pool_kernel,
        grid=(b,),
        in_specs=[pl.BlockSpec((1, n, g), lambda i: (i, 0, 0)),
                  pl.BlockSpec((g, g), lambda i: (0, 0)),
                  pl.BlockSpec((1, g), lambda i: (0, 0))],
        out_specs=pl.BlockSpec((1, n, g), lambda i: (i, 0, 0)),
        out_shape=jax.ShapeDtypeStruct((b, n, g), F32),
        compiler_params=_params(("arbitrary",)),
        name="pool_mixer",
    )(p, w_bd, scale)


def _wkv_scan_kernel(rf_ref, rb_ref, kkf_ref, kkb_ref, decf_ref, bf_ref, kmf_ref, decb_ref, bb_ref, kmb_ref,
                     vtf_ref, vtb_ref, e_ref, s0_ref, yf_ref, yb_ref, send_ref, s_scr):
    nb, t_blk, g = rf_ref.shape
    step_id = pl.program_id(0)

    @pl.when(step_id == 0)
    def _():
        s_scr[...] = s0_ref[...]

    yf_ref[...] = jnp.zeros_like(yf_ref)
    yb_ref[...] = jnp.zeros_like(yb_ref)
    lane = lax.broadcasted_iota(jnp.int32, (HEAD_DIM, g), 1) % SCAN_SUB
    lane128 = lax.broadcasted_iota(jnp.int32, (HEAD_DIM, LANES), 1)
    dirs = ((rf_ref, kkf_ref, decf_ref, bf_ref, kmf_ref, vtf_ref, yf_ref),
            (rb_ref, kkb_ref, decb_ref, bb_ref, kmb_ref, vtb_ref, yb_ref))
    ents = HEAD_DIM // 4
    for d in range(2):
        pltpu.matmul_push_rhs(e_ref[...], 0, d)
        pltpu.matmul_acc_lhs(0, jnp.zeros((16, g), BF16), d, load_staged_rhs=0)
        pltpu.matmul_pop(0, (16, g), F32, d)

    def write_y(d, b, y, tt_done, valid):
        y_ref = dirs[d][-1]
        tl = tt_done if d == 0 else t_blk - 1 - tt_done
        tl = jnp.clip(tl, 0, t_blk - 1)
        hit = (lane == tl % SCAN_SUB) & valid
        pltpu.store(y_ref.at[b, tl // SCAN_SUB], y, mask=hit)

    def rows_at(d, tt):
        tl = jnp.clip(tt if d == 0 else t_blk - 1 - tt, 0, t_blk - 1)
        return lambda ref, b: jnp.broadcast_to(ref[b, pl.ds(tl, 1), :], (HEAD_DIM, g))

    for d, (r_ref, kk_ref, dec_ref, b_ref, km_ref, vt_ref, y_ref) in enumerate(dirs):
        row0 = rows_at(d, 0)
        for b in range(nb):
            pltpu.matmul_acc_lhs(b * ents, s_scr[d, b].astype(BF16) * row0(kk_ref, b).astype(BF16), d)

    def one_token(tt, carry):
        for b in range(nb):
            for d, (r_ref, kk_ref, dec_ref, b_ref, km_ref, vt_ref, y_ref) in enumerate(dirs):
                tl = tt if d == 0 else t_blk - 1 - tt
                row = rows_at(d, tt)
                nxt = rows_at(d, tt + 1)
                idx = (lane128 // SCAN_SUB) * SCAN_SUB + tl % SCAN_SUB
                sa = pltpu.matmul_pop(b * ents, (HEAD_DIM, g), F32, d)
                write_y(d, b, pltpu.matmul_pop((nb + b) * ents, (HEAD_DIM, g), F32, d), tt - 1, tt > 0)
                vcol = jnp.concatenate([jnp.take_along_axis(vt_ref[b, tl // SCAN_SUB][:, j * LANES:(j + 1) * LANES], idx, axis=1)
                                        for j in range(g // LANES)], axis=1)
                s = s_scr[d, b] * row(dec_ref, b) - sa * row(b_ref, b) + vcol * row(km_ref, b)
                s_scr[d, b] = s
                s_bf = s.astype(BF16)
                pltpu.matmul_acc_lhs((nb + b) * ents, s_bf * row(r_ref, b).astype(BF16), d)
                pltpu.matmul_acc_lhs(b * ents, s_bf * nxt(kk_ref, b).astype(BF16), d)
        return carry

    lax.fori_loop(0, t_blk, one_token, 0, unroll=4)
    for d in range(2):
        for b in range(nb):
            write_y(d, b, pltpu.matmul_pop((nb + b) * ents, (HEAD_DIM, g), F32, d), t_blk - 1, True)
            pltpu.matmul_pop(b * ents, (HEAD_DIM, g), F32, d)

    @pl.when(step_id == pl.num_programs(0) - 1)
    def _():
        send_ref[...] = s_scr[...]


def wkv_scan(pa3, kk, dec_f, b_f, km_f, dec_b, b_b, km_b, vt, eseg, s0):
    nb, l, g = kk.shape
    t_blk = SCAN_BLOCK
    nsub = t_blk // SCAN_SUB
    nblk = l // t_blk
    assert l % t_blk == 0
    seq_f = pl.BlockSpec((nb, t_blk, g), lambda i: (0, i, 0))
    seq_b = pl.BlockSpec((nb, t_blk, g), lambda i: (0, nblk - 1 - i, 0))
    vt_f = pl.BlockSpec((nb, nsub, HEAD_DIM, g), lambda i: (0, i, 0, 0))
    vt_b = pl.BlockSpec((nb, nsub, HEAD_DIM, g), lambda i: (0, nblk - 1 - i, 0, 0))
    state = pl.BlockSpec(s0.shape, lambda i: (0, 0, 0, 0))
    return pl.pallas_call(
        _wkv_scan_kernel,
        grid=(nblk,),
        in_specs=[seq_f, seq_b, seq_f, seq_b, seq_f, seq_f, seq_f, seq_b, seq_b, seq_b,
                  vt_f, vt_b, pl.BlockSpec(eseg.shape, lambda i: (0, 0)), state],
        out_specs=[vt_f, vt_b, state],
        out_shape=[jax.ShapeDtypeStruct(vt.shape, F32), jax.ShapeDtypeStruct(vt.shape, F32),
                   jax.ShapeDtypeStruct(s0.shape, F32)],
        scratch_shapes=[pltpu.VMEM(s0.shape, F32)],
        compiler_params=_params(("arbitrary",)),
        name="wkv_scan",
    )(pa3, pa3, kk, kk, dec_f, b_f, km_f, dec_b, b_b, km_b, vt, vt, eseg, s0)


def _block_ones(n, seg):
    idx = np.arange(n) // seg
    return jnp.asarray(idx[:, None] == idx[None, :], dtype=BF16)


def _padded_rows(w, offset, n):
    return jnp.zeros((n, w.shape[1]), w.dtype).at[offset:offset + w.shape[0]].set(w)


def _rope_tables(n_tokens, n_rep):
    t = jnp.arange(n_tokens)
    row = (t // GRID_W).astype(F32)
    col = (t % GRID_W).astype(F32)
    n_freq = HEAD_DIM // 4
    inv_freq = ROPE_THETA ** (-jnp.arange(n_freq, dtype=F32) / n_freq)
    ang = jnp.concatenate([row[:, None] * inv_freq, col[:, None] * inv_freq], axis=-1)
    cos, sin = jnp.cos(ang), jnp.sin(ang)
    cos_h = jnp.concatenate([cos, cos], axis=-1)
    sin_h = jnp.concatenate([-sin, sin], axis=-1)
    return jnp.tile(cos_h, (1, n_rep)), jnp.tile(sin_h, (1, n_rep))


def kernel(x, c, ctx, c_ctx, ada_w, ada_b, norm1_g, norm2_g, w_in, w_out, rwkv_w0, rwkv_w2, rwkv_a0, rwkv_a2,
           rwkv_k_k, rwkv_k_a, rwkv_r_k, rwkv_g2, rwkv_gn_w, rwkv_gn_b, na_rpb, gqa_q_gain, gqa_k_gain, pool_w,
           pool_scale, mlp_w1, mlp_w2, final_g):
    nb, l, d = x.shape
    lc = ctx.shape[1]
    depth = ada_w.shape[0]
    g = d // 4
    nh = g // HEAD_DIM
    nkv = nh // 2
    kvw = nkv * HEAD_DIM
    dr = rwkv_w2.shape[2]
    ir = rwkv_a2.shape[2]
    gr = rwkv_g2.shape[1]
    assert 2 * dr + 2 * ir + gr <= g and nb + 1 <= 8
    rows = l // GRID_W

    cpad = jnp.zeros((8, d), F32).at[:nb].set(c).at[nb].set(c_ctx)
    mods = adaln(cpad, ada_w, ada_b)

    eseg = _block_ones(g, HEAD_DIM)
    ekv = _block_ones(kvw, HEAD_DIM)
    cos_q, sin_q = _rope_tables(l, nh)
    ones_c, zeros_c = jnp.ones((lc, g), F32), jnp.zeros((lc, g), F32)

    splits = np.cumsum([0, g, g, g, dr, dr, ir, ir, gr, g, g, g, g, kvw, kvw, g])
    lowrank_w = splits[8] - splits[3]

    xl = x.reshape(nb * l, d)
    xc = ctx.reshape(nb * lc, d)
    zero_state = jnp.zeros((2, nb, HEAD_DIM, g), F32)

    for i in range(depth):
        need_ctx_out = i < depth - 1
        mod_l = mods[i, :nb].reshape(nb, N_MOD, 1, d)
        mod_c = mods[i, nb].reshape(N_MOD, 1, 1, d)
        ml = [mod_l[:, k] for k in range(N_MOD)]
        mc = [mod_c[k] for k in range(N_MOD)]

        wi = w_in[i]
        w_inp = jnp.concatenate([wi[:, :splits[8]], jnp.zeros((d, g - lowrank_w), F32),
                                 wi[:, splits[8]:splits[9]] * HEAD_DIM ** -0.5, wi[:, splits[9]:]], axis=1).astype(BF16)
        g1 = norm1_g[i].reshape(1, d)
        g2 = norm2_g[i].reshape(1, d)

        w2p = jnp.stack([_padded_rows(rwkv_w2[i, 0], 0, g), _padded_rows(rwkv_w2[i, 1], dr, g)]).astype(BF16)
        a2p = jnp.stack([_padded_rows(rwkv_a2[i, 0], 2 * dr, g),
                         _padded_rows(rwkv_a2[i, 1], 2 * dr + ir, g)]).astype(BF16)
        g2p = _padded_rows(rwkv_g2[i], 2 * dr + 2 * ir, g).astype(BF16)
        r_k = rwkv_r_k[i].reshape(1, g)
        gn_w = rwkv_gn_w[i].reshape(1, g)
        gn_b = rwkv_gn_b[i].reshape(1, g)
        rwkv_params = (rwkv_w0[i], rwkv_a0[i], w2p, a2p, rwkv_k_k[i].reshape(1, g), rwkv_k_a[i].reshape(1, g), eseg)
        q_gain = jnp.tile(gqa_q_gain[i], nh).reshape(1, g)
        k_gain = jnp.tile(gqa_k_gain[i], nkv).reshape(1, kvw)

        proj_l = inproj(xl, g1, ml[0], ml[1], w_inp, l, rwkv_params, (q_gain, k_gain, cos_q, sin_q, ekv), l, True)
        proj_c = inproj(xc, g1, mc[0], mc[1], w_inp, nb * lc, rwkv_params, (q_gain, k_gain, ones_c, zeros_c, ekv), lc, False)
        pa_l, pb_l, pd_l = proj_l[:3]
        pa_c, pb_c, pd_c = proj_c[:3]

        def scan_inputs(proj, seq):
            terms3 = [t.reshape(nb, seq, g) for t in proj[3:10]]
            return terms3, proj[10].reshape(nb, seq // SCAN_SUB, HEAD_DIM, g)

        terms3_c, vt_c = scan_inputs(proj_c, lc)
        yf_c, yb_c, state_c = wkv_scan(pa_c.reshape(nb, lc, 4 * g), *terms3_c, vt_c, eseg, zero_state)
        terms3_l, vt_l = scan_inputs(proj_l, l)
        yf_l, yb_l, _ = wkv_scan(pa_l.reshape(nb, l, 4 * g), *terms3_l, vt_l, eseg, state_c)

        bias = na_bias_table(na_rpb[i], rows)
        b_l = neighbourhood_attention(pb_l.reshape(nb, l, 3 * g), pb_c.reshape(nb, lc, 3 * g), bias).reshape(nb * l, g)

        qc_l, kc_l, vc_l = proj_l[11:14]
        qc_c, kc_c, vc_c = proj_c[11:14]
        kc_c, vc_c = kc_c.reshape(nb, lc, kvw), vc_c.reshape(nb, lc, kvw)
        k_all = jnp.concatenate([kc_c, kc_l.reshape(nb, l, kvw)], axis=1)
        v_all = jnp.concatenate([vc_c, vc_l.reshape(nb, l, kvw)], axis=1)
        c_l = attention(qc_l, k_all, v_all, l)

        w_bd = jax.scipy.linalg.block_diag(*[pool_w[i, k] for k in range(len(POOL_WINDOWS))]).astype(BF16)
        p_scale = pool_scale[i].reshape(1, g)
        d_l = pool_mixer(pd_l.reshape(nb, l, g), w_bd, p_scale).reshape(nb * l, g)

        w_out4 = w_out[i].reshape(4, g, d).astype(BF16)
        w1 = mlp_w1[i].astype(BF16)
        w2 = mlp_w2[i].astype(BF16)
        readout_params = (r_k, g2p, gn_w, gn_b, eseg)
        xl = outproj(yf_l.reshape(-1, HEAD_DIM, g), yb_l.reshape(-1, HEAD_DIM, g), pa_l, proj_l[6], proj_l[9],
                     readout_params, (b_l, c_l, d_l), w_out4, xl, ml[2], l)
        xl = mlp(xl, g2, ml[3], ml[4], ml[5], w1, w2, l)

        if need_ctx_out:
            b_c = attention(pb_c[:, :g], pb_c[:, g:2 * g].reshape(nb, lc, g), pb_c[:, 2 * g:].reshape(nb, lc, g), lc)
            c_c = attention(qc_c, kc_c, vc_c, lc)
            d_c = pool_mixer(pd_c.reshape(nb, lc, g), w_bd, p_scale).reshape(nb * lc, g)
            xc = outproj(yf_c.reshape(-1, HEAD_DIM, g), yb_c.reshape(-1, HEAD_DIM, g), pa_c, proj_c[6], proj_c[9],
                         readout_params, (b_c, c_c, d_c), w_out4, xc, mc[2], nb * lc)
            xc = mlp(xc, g2, mc[3], mc[4], mc[5], w1, w2, nb * lc)

    return final_norm(xl, final_g.reshape(1, d)).reshape(nb, l, d)
```

```python
import functools

import jax
import jax.numpy as jnp
import numpy as np
from jax import lax
from jax.experimental import pallas as pl
from jax.experimental.pallas import tpu as pltpu

F32 = jnp.float32
BF16 = jnp.bfloat16

HEAD_DIM = 64
GRID_W = 64
NA_ROWS = 8
NA_COLS = 16
ROPE_THETA = 10000.0
POOL_WINDOWS = (2, 4, 8, 16)
NORM_EPS = 1e-6
RWKV_GN_EPS = 1e-5 * HEAD_DIM
N_MOD = 6
LANES = 128
SCAN_BLOCK = 128
SCAN_SUB = 64
ATTN_CHUNK = 2048
VMEM_LIMIT = 52 * 1024 * 1024


def _params(sem):
    return pltpu.CompilerParams(dimension_semantics=sem, vmem_limit_bytes=VMEM_LIMIT)


def _tile(n, pref):
    t = min(n, pref)
    assert n % t == 0, (n, pref)
    return t


def _dot(a, b):
    return jnp.dot(a, b, preferred_element_type=F32)


def _dot_nt(a, b):
    return lax.dot_general(a, b, (((1,), (1,)), ((), ())), preferred_element_type=F32)


def _seg_sum(x, e):
    hi = x.astype(BF16)
    lo = (x - hi.astype(F32)).astype(BF16)
    return _dot(hi, e) + _dot(lo, e)


def _norm_mod(x, g, shift, scale):
    ms = jnp.mean(x * x, axis=-1, keepdims=True)
    h = x * lax.rsqrt(ms + NORM_EPS) * g
    return h * (1.0 + scale) + shift


def _adaln_kernel(c_ref, w_ref, b_ref, o_ref):
    c = c_ref[...]
    s = c * jax.nn.sigmoid(c)
    o_ref[0] = jnp.dot(s, w_ref[0], preferred_element_type=F32, precision=lax.Precision.HIGHEST) + b_ref[0]


def adaln(cpad, ada_w, ada_b):
    depth, d, n = ada_w.shape
    tn = _tile(n, 1536)
    return pl.pallas_call(
        _adaln_kernel,
        grid=(depth, n // tn),
        in_specs=[pl.BlockSpec((8, d), lambda i, j: (0, 0)),
                  pl.BlockSpec((1, d, tn), lambda i, j: (i, 0, j)),
                  pl.BlockSpec((1, 1, tn), lambda i, j: (i, 0, j))],
        out_specs=pl.BlockSpec((1, 8, tn), lambda i, j: (i, 0, j)),
        out_shape=jax.ShapeDtypeStruct((depth, 8, n), F32),
        compiler_params=_params(("arbitrary", "arbitrary")),
        name="adaln",
    )(cpad, ada_w, ada_b.reshape(depth, 1, n))


def _swap_halves(y):
    n = y.shape[-1]
    lane = lax.broadcasted_iota(jnp.int32, y.shape, 1)
    half = HEAD_DIM // 2
    return jnp.where(lane % HEAD_DIM < half, pltpu.roll(y, n - half, 1), pltpu.roll(y, half, 1))


def _to_tiles(x):
    g = x.shape[1]
    tiles = []
    for s in range(x.shape[0] // SCAN_SUB):
        blk = x[s * SCAN_SUB:(s + 1) * SCAN_SUB, :]
        tiles.append(jnp.concatenate([blk[:, h * HEAD_DIM:(h + 1) * HEAD_DIM].T for h in range(g // HEAD_DIM)], axis=1))
    return tiles


def _from_tiles(tiles):
    g = tiles[0].shape[1]
    return jnp.concatenate(
        [jnp.concatenate([t[:, h * SCAN_SUB:(h + 1) * SCAN_SUB].T for h in range(g // HEAD_DIM)], axis=1) for t in tiles],
        axis=0)


def _inproj_kernel(x_ref, g_ref, sh_ref, sc_ref, w_ref,
                   w0_ref, a0_ref, w2_ref, a2_ref, kk_w_ref, ka_ref, e_ref,
                   qg_ref, kg_ref, cos_ref, sin_ref, ek_ref,
                   oa_ref, ob_ref, od_ref,
                   kk_ref, decf_ref, bf_ref, kmf_ref, decb_ref, bb_ref, kmb_ref, vt_ref,
                   q_ref, kc_ref, vc_ref, *, rope):
    g = od_ref.shape[1]
    gk = kc_ref.shape[1]
    h = _norm_mod(x_ref[...], g_ref[...], sh_ref[0], sc_ref[0]).astype(BF16)
    e = e_ref[...]

    pa = _dot(h, w_ref[:, 0:4 * g])
    pc = _dot(h, w_ref[:, 7 * g:9 * g])

    oa_ref[...] = pa
    k = pa[:, g:2 * g]
    lr = pa[:, 3 * g:4 * g]
    for s, tile in enumerate(_to_tiles(pa[:, 2 * g:3 * g])):
        vt_ref[s] = tile
    kx = k * kk_w_ref[...]
    norm = jnp.sqrt(_seg_sum(kx * kx, e))
    kk = kx / jnp.maximum(norm, 1e-12)
    kk_ref[...] = kk
    lr_t = jnp.tanh(lr).astype(BF16)
    lr_b = lr.astype(BF16)
    for d, (dec_ref, b_ref, km_ref) in enumerate(((decf_ref, bf_ref, kmf_ref), (decb_ref, bb_ref, kmb_ref))):
        z = w0_ref[d:d + 1, :] + _dot(lr_t, w2_ref[d])
        softplus_neg = jnp.maximum(-z, 0.0) + jnp.log1p(jnp.exp(-jnp.abs(z)))
        w = -softplus_neg - 0.5
        dec_ref[...] = jnp.exp(-jnp.exp(w))
        a = jax.nn.sigmoid(a0_ref[d:d + 1, :] + _dot(lr_b, a2_ref[d]))
        b_ref[...] = kk * a
        km_ref[...] = k * (1.0 + (a - 1.0) * ka_ref[...])

    def normed(x, gain, seg):
        ms = _seg_sum(x * x, seg) * (1.0 / HEAD_DIM)
        return x * lax.rsqrt(ms + NORM_EPS) * gain

    q = normed(pc[:, :g], qg_ref[...], e)
    kc = normed(pc[:, g:g + gk], kg_ref[...], ek_ref[...])
    if rope:
        cos = cos_ref[...]
        sin = sin_ref[...]
        q = q * cos + _swap_halves(q) * sin
        kc = kc * cos[:, :gk] + _swap_halves(kc) * sin[:, :gk]
    q_ref[...] = (q * HEAD_DIM ** -0.5).astype(BF16)
    kc_ref[...] = kc.astype(BF16)
    vc_ref[...] = pc[:, g + gk:].astype(BF16)

    ob_ref[...] = _dot(h, w_ref[:, 4 * g:7 * g]).astype(ob_ref.dtype)

    od_ref[...] = _dot(h, w_ref[:, 9 * g:10 * g])


def inproj(x, gain, shift, scale, w, rows_per_mod, rwkv_params, gqa_params, seq_len, rope):
    r, d = x.shape
    g = w.shape[1] // 10
    gk = g // 2
    tm = _tile(min(rows_per_mod, seq_len), 512)
    tpb = rows_per_mod // tm
    nseq = seq_len // tm
    modspec = pl.BlockSpec((1, 1, d), lambda i: (i // tpb, 0, 0))
    full2 = lambda a: pl.BlockSpec(a.shape, lambda i: (0,) * a.ndim)
    rows = lambda n: pl.BlockSpec((tm, n), lambda i: (i, 0))
    q_gain, k_gain, cos, sin, ekv = gqa_params
    table = pl.BlockSpec((tm, g), lambda i: (i % nseq, 0))
    out_specs = ([rows(4 * g), rows(3 * g), rows(g)] + [rows(g)] * 7
                 + [pl.BlockSpec((tm // SCAN_SUB, HEAD_DIM, g), lambda i: (i, 0, 0))] + [rows(g), rows(gk), rows(gk)])
    out_shape = ([jax.ShapeDtypeStruct((r, 4 * g), F32), jax.ShapeDtypeStruct((r, 3 * g), BF16),
                  jax.ShapeDtypeStruct((r, g), F32)] + [jax.ShapeDtypeStruct((r, g), F32)] * 7
                 + [jax.ShapeDtypeStruct((r // SCAN_SUB, HEAD_DIM, g), F32), jax.ShapeDtypeStruct((r, g), BF16),
                    jax.ShapeDtypeStruct((r, gk), BF16), jax.ShapeDtypeStruct((r, gk), BF16)])
    return pl.pallas_call(
        functools.partial(_inproj_kernel, rope=rope),
        grid=(r // tm,),
        in_specs=[rows(d), pl.BlockSpec((1, d), lambda i: (0, 0)), modspec, modspec, full2(w)]
                 + [full2(p) for p in rwkv_params] + [full2(q_gain), full2(k_gain), table, table, full2(ekv)],
        out_specs=out_specs,
        out_shape=out_shape,
        compiler_params=_params(("arbitrary",)),
        name="inproj",
    )(x, gain, shift, scale, w, *rwkv_params, q_gain, k_gain, cos, sin, ekv)


def _outproj_kernel(yf_ref, yb_ref, p_ref, kmf_ref, kmb_ref, rk_ref, g2_ref, gnw_ref, gnb_ref, e_ref,
                    b_ref, c_ref, d_ref, w_ref, x_ref, gate_ref, o_ref):
    g = b_ref.shape[1]
    e = e_ref[...]
    r = p_ref[:, :g]
    v = p_ref[:, 2 * g:3 * g]
    lr = p_ref[:, 3 * g:4 * g]
    y = _from_tiles([yf_ref[s] + yb_ref[s] for s in range(yf_ref.shape[0])])
    mu = _seg_sum(y, e) * (1.0 / HEAD_DIM)
    yc = y - mu
    var = _seg_sum(yc * yc, e) * (1.0 / HEAD_DIM)
    yn = yc * lax.rsqrt(var + RWKV_GN_EPS) * gnw_ref[...] + gnb_ref[...]
    bonus = _seg_sum(r * (kmf_ref[...] + kmb_ref[...]) * rk_ref[...], e) * v
    a = (yn + bonus) * _dot(jax.nn.sigmoid(lr).astype(BF16), g2_ref[...])
    acc = _dot(b_ref[...].astype(BF16), w_ref[1])
    acc += _dot(c_ref[...].astype(BF16), w_ref[2])
    acc += _dot(d_ref[...].astype(BF16), w_ref[3])
    acc += _dot(a.astype(BF16), w_ref[0])
    o_ref[...] = x_ref[...] + gate_ref[0] * acc


def outproj(yf, yb, pa, km_f, km_b, readout_params, mix_bcd, w4, x, gate, rows_per_mod):
    r, d = x.shape
    g = w4.shape[1]
    tm = _tile(rows_per_mod, 512)
    tpb = rows_per_mod // tm
    rows = lambda n: pl.BlockSpec((tm, n), lambda i: (i, 0))
    tiles = pl.BlockSpec((tm // SCAN_SUB, HEAD_DIM, g), lambda i: (i, 0, 0))
    full2 = lambda a: pl.BlockSpec(a.shape, lambda i: (0,) * a.ndim)
    return pl.pallas_call(
        _outproj_kernel,
        grid=(r // tm,),
        in_specs=[tiles, tiles, rows(4 * g), rows(g), rows(g)] + [full2(p) for p in readout_params]
                 + [rows(g), rows(g), rows(g), full2(w4), rows(d), pl.BlockSpec((1, 1, d), lambda i: (i // tpb, 0, 0))],
        out_specs=rows(d),
        out_shape=jax.ShapeDtypeStruct((r, d), F32),
        compiler_params=_params(("arbitrary",)),
        name="outproj",
    )(yf, yb, pa, km_f, km_b, *readout_params, *mix_bcd, w4, x, gate)


def _mlp_kernel(x_ref, g_ref, sh_ref, sc_ref, gate_ref, w1_ref, w2_ref, o_ref, h_scr, acc_scr):
    j = pl.program_id(1)

    @pl.when(j == 0)
    def _():
        h_scr[...] = _norm_mod(x_ref[...], g_ref[...], sh_ref[0], sc_ref[0]).astype(BF16)
        acc_scr[...] = jnp.zeros_like(acc_scr)

    u = jnp.maximum(_dot(h_scr[...], w1_ref[...]), 0.0)
    acc_scr[...] += _dot((u * u).astype(BF16), w2_ref[...])

    @pl.when(j == pl.num_programs(1) - 1)
    def _():
        o_ref[...] = x_ref[...] + gate_ref[0] * acc_scr[...]


def mlp(x, g, shift, scale, gate, w1, w2, rows_per_mod):
    r, d = x.shape
    dff = w1.shape[1]
    tm = _tile(rows_per_mod, 1024)
    tf = _tile(dff, 1024)
    tpb = rows_per_mod // tm
    modspec = pl.BlockSpec((1, 1, d), lambda i, j: (i // tpb, 0, 0))
    return pl.pallas_call(
        _mlp_kernel,
        grid=(r // tm, dff // tf),
        in_specs=[pl.BlockSpec((tm, d), lambda i, j: (i, 0)),
                  pl.BlockSpec((1, d), lambda i, j: (0, 0)),
                  modspec, modspec, modspec,
                  pl.BlockSpec((d, tf), lambda i, j: (0, j)),
                  pl.BlockSpec((tf, d), lambda i, j: (j, 0))],
        out_specs=pl.BlockSpec((tm, d), lambda i, j: (i, 0)),
        out_shape=jax.ShapeDtypeStruct((r, d), F32),
        scratch_shapes=[pltpu.VMEM((tm, d), BF16), pltpu.VMEM((tm, d), F32)],
        compiler_params=_params(("arbitrary", "arbitrary")),
        name="mlp",
    )(x, g, shift, scale, gate, w1, w2)


def _final_norm_kernel(x_ref, g_ref, o_ref):
    x = x_ref[...]
    ms = jnp.mean(x * x, axis=-1, keepdims=True)
    o_ref[...] = x * lax.rsqrt(ms + NORM_EPS) * g_ref[...]


def final_norm(x, g):
    r, d = x.shape
    tm = _tile(r, 1024)
    return pl.pallas_call(
        _final_norm_kernel,
        grid=(r // tm,),
        in_specs=[pl.BlockSpec((tm, d), lambda i: (i, 0)), pl.BlockSpec((1, d), lambda i: (0, 0))],
        out_specs=pl.BlockSpec((tm, d), lambda i: (i, 0)),
        out_shape=jax.ShapeDtypeStruct((r, d), F32),
        compiler_params=_params(("arbitrary",)),
        name="final_norm",
    )(x, g)


def _pad_heads(q, n_heads, kv_width, rep):
    assert kv_width % LANES == 0
    t = q.shape[0]
    lane = lax.broadcasted_iota(jnp.int32, (t, LANES), 1)
    lo_half = lane < HEAD_DIM
    zero = jnp.zeros((t, LANES), q.dtype)
    rows = []
    for h in range(n_heads):
        src = h * HEAD_DIM
        dst = (h // rep) * HEAD_DIM
        piece = q[:, src // LANES * LANES:(src // LANES + 1) * LANES]
        if src % LANES != dst % LANES:
            piece = pltpu.roll(piece.astype(F32), HEAD_DIM, 1).astype(q.dtype)
        piece = jnp.where(lo_half if dst % LANES == 0 else ~lo_half, piece, zero)
        blocks = [piece if j == dst // LANES else zero for j in range(kv_width // LANES)]
        rows.append(jnp.concatenate(blocks, axis=1) if len(blocks) > 1 else piece)
    return jnp.concatenate(rows, axis=0)


def _gather_heads(res, n_heads, rep):
    t = res.shape[0] // n_heads
    lane = lax.broadcasted_iota(jnp.int32, (t, LANES), 1)
    lo_half = lane < HEAD_DIM
    pieces = []
    for h in range(n_heads):
        src = (h // rep) * HEAD_DIM
        dst = h * HEAD_DIM
        piece = res[h * t:(h + 1) * t, src // LANES * LANES:(src // LANES + 1) * LANES]
        if src % LANES != dst % LANES:
            piece = pltpu.roll(piece, HEAD_DIM, 1)
        pieces.append(piece)
    blocks = [jnp.where(lo_half, pieces[2 * j], pieces[2 * j + 1]) for j in range(n_heads // 2)]
    return jnp.concatenate(blocks, axis=1)


def _attn_kernel(q_ref, k_ref, v_ref, o_ref, *, rep):
    n_heads = q_ref.shape[1] // HEAD_DIM
    lk = k_ref.shape[1]
    q = _pad_heads(q_ref[...], n_heads, k_ref.shape[2], rep)
    edges = list(range(0, lk, ATTN_CHUNK)) + [lk]
    m = l = acc = None
    for k0, k1 in zip(edges[:-1], edges[1:]):
        s = _dot_nt(q, k_ref[0, k0:k1, :])
        m_new = jnp.max(s, axis=-1, keepdims=True)
        if m is not None:
            m_new = jnp.maximum(m, m_new)
        p = jnp.exp(s - m_new)
        l_new = jnp.sum(p, axis=-1, keepdims=True)
        acc_new = _dot(p.astype(BF16), v_ref[0, k0:k1, :])
        if m is not None:
            alpha = jnp.exp(m - m_new)
            l_new = alpha * l + l_new
            acc_new = alpha * acc + acc_new
        m, l, acc = m_new, l_new, acc_new
    o_ref[...] = _gather_heads(acc / l, n_heads, rep)


def attention(q, k, v, seq_len):
    r, gq = q.shape
    nb, lk, kw = k.shape
    rep = gq // kw
    tq = _tile(seq_len, 128)
    nq = seq_len // tq
    return pl.pallas_call(
        functools.partial(_attn_kernel, rep=rep),
        grid=(nb, nq),
        in_specs=[pl.BlockSpec((tq, gq), lambda i, j: (i * nq + j, 0)),
                  pl.BlockSpec((1, lk, kw), lambda i, j: (i, 0, 0)),
                  pl.BlockSpec((1, lk, kw), lambda i, j: (i, 0, 0))],
        out_specs=pl.BlockSpec((tq, gq), lambda i, j: (i * nq + j, 0)),
        out_shape=jax.ShapeDtypeStruct((r, gq), F32),
        compiler_params=_params(("arbitrary", "arbitrary")),
        name="attention",
    )(q, k, v)


def _na_kernel(p_ref, pc_ref, bias_ref, o_ref, *, rows, kh):
    g = o_ref.shape[2]
    n_heads = g // HEAD_DIM
    band = kh * GRID_W
    kc = pc_ref[0, :, g:2 * g]
    vc = pc_ref[0, :, 2 * g:3 * g]

    def one_row(r, carry):
        rs = jnp.clip(r - kh // 2, 0, rows - kh)
        q0 = pl.multiple_of(r * GRID_W, GRID_W)
        k0 = pl.multiple_of(rs * GRID_W, GRID_W)
        q = _pad_heads(p_ref[0, pl.ds(q0, GRID_W), 0:g], n_heads, g, 1)
        s1 = _dot_nt(q, p_ref[0, pl.ds(k0, band), g:2 * g]) + bias_ref[r - rs]
        s2 = _dot_nt(q, kc)
        m = jnp.maximum(jnp.max(s1, axis=-1, keepdims=True), jnp.max(s2, axis=-1, keepdims=True))
        p1 = jnp.exp(s1 - m)
        p2 = jnp.exp(s2 - m)
        l = jnp.sum(p1, axis=-1, keepdims=True) + jnp.sum(p2, axis=-1, keepdims=True)
        res = (_dot(p1.astype(BF16), p_ref[0, pl.ds(k0, band), 2 * g:3 * g]) + _dot(p2.astype(BF16), vc)) / l
        o_ref[0, pl.ds(q0, GRID_W), :] = _gather_heads(res, n_heads, 1)
        return carry

    lax.fori_loop(0, rows, one_row, 0, unroll=8)


def neighbourhood_attention(p, pc, bias):
    nb, l, g3 = p.shape
    g = g3 // 3
    lc = pc.shape[1]
    rows = l // GRID_W
    kh = min(NA_ROWS, rows)
    return pl.pallas_call(
        functools.partial(_na_kernel, rows=rows, kh=kh),
        grid=(nb,),
        in_specs=[pl.BlockSpec((1, l, g3), lambda i: (i, 0, 0)),
                  pl.BlockSpec((1, lc, g3), lambda i: (i, 0, 0)),
                  pl.BlockSpec(bias.shape, lambda i: (0, 0, 0))],
        out_specs=pl.BlockSpec((1, l, g), lambda i: (i, 0, 0)),
        out_shape=jax.ShapeDtypeStruct((nb, l, g), F32),
        compiler_params=_params(("arbitrary",)),
        name="neighbourhood_attention",
    )(p, pc, bias)


def _na_bias_kernel(rpb_ref, o_ref, *, kh):
    h = pl.program_id(0)
    q = lax.broadcasted_iota(jnp.int32, (GRID_W, GRID_W), 0)
    k = lax.broadcasted_iota(jnp.int32, (GRID_W, GRID_W), 1)
    start = jnp.clip(q - NA_COLS // 2, 0, GRID_W - NA_COLS)
    in_win = (k >= start) & (k < start + NA_COLS)
    off = k - q + NA_COLS - 1
    neg = jnp.full((GRID_W, GRID_W), -jnp.inf, F32)
    blocks = []
    for ro in range(2 * NA_ROWS - 1):
        t = neg
        for c in range(2 * NA_COLS - 1):
            t = jnp.where(off == c, rpb_ref[h, ro, c], t)
        blocks.append(jnp.where(in_win, t, neg))
    for di in range(kh):
        for i in range(kh):
            o_ref[di, :, i * GRID_W:(i + 1) * GRID_W] = blocks[i - di + NA_ROWS - 1]


def na_bias_table(rpb, rows):
    kh = min(NA_ROWS, rows)
    nh = rpb.shape[0]
    return pl.pallas_call(
        functools.partial(_na_bias_kernel, kh=kh),
        grid=(nh,),
        in_specs=[pl.BlockSpec(memory_space=pltpu.SMEM)],
        out_specs=pl.BlockSpec((kh, GRID_W, kh * GRID_W), lambda i: (0, i, 0)),
        out_shape=jax.ShapeDtypeStruct((kh, nh * GRID_W, kh * GRID_W), F32),
        compiler_params=_params(("arbitrary",)),
        name="na_bias",
    )(rpb)


def _shift_rows(x, d, t):
    n = x.shape[0]
    y = pltpu.roll(x, d % n, 0)
    src = t - d
    return jnp.where((src >= 0) & (src < n), y, 0.0)


def _pool_kernel(x_ref, w_ref, scale_ref, o_ref):
    x = x_ref[0]
    n, g = x.shape
    pg = g // len(POOL_WINDOWS)
    t = lax.broadcasted_iota(jnp.int32, x.shape, 0)
    group = lax.broadcasted_iota(jnp.int32, x.shape, 1) // pg
    fwd = x
    bwd = x
    cur = 1
    total = jnp.zeros_like(x)
    count = jnp.ones_like(x)
    for j, w in enumerate(POOL_WINDOWS):
        half = w // 2
        while cur < half:
            fwd = fwd + _shift_rows(fwd, -cur, t)
            bwd = bwd + _shift_rows(bwd, cur, t)
            cur *= 2
        win = _shift_rows(bwd, 1, t) + fwd
        lo = jnp.clip(t - half, 0, n)
        hi = jnp.clip(t - half + w, 0, n)
        total = jnp.where(group == j, win, total)
        count = jnp.where(group == j, (hi - lo).astype(F32), count)
    diff = total / count - x
    o_ref[0] = _dot(diff.astype(BF16), w_ref[...]) * scale_ref[...]


def pool_mixer(p, w_bd, scale):
    b, n, g = p.shape
    return pl.pallas_call(
        _pool_kernel,
        grid=(b,),
        in_specs=[pl.BlockSpec((1, n, g), lambda i: (i, 0, 0)),
                  pl.BlockSpec((g, g), lambda i: (0, 0)),
                  pl.BlockSpec((1, g), lambda i: (0, 0))],
        out_specs=pl.BlockSpec((1, n, g), lambda i: (i, 0, 0)),
        out_shape=jax.ShapeDtypeStruct((b, n, g), F32),
        compiler_params=_params(("arbitrary",)),
        name="pool_mixer",
    )(p, w_bd, scale)


def _wkv_scan_kernel(rf_ref, rb_ref, kkf_ref, kkb_ref, decf_ref, bf_ref, kmf_ref, decb_ref, bb_ref, kmb_ref,
                     vtf_ref, vtb_ref, e_ref, s0_ref, yf_ref, yb_ref, send_ref, s_scr):
    nb, t_blk, g = rf_ref.shape
    step_id = pl.program_id(0)

    @pl.when(step_id == 0)
    def _():
        s_scr[...] = s0_ref[...]

    yf_ref[...] = jnp.zeros_like(yf_ref)
    yb_ref[...] = jnp.zeros_like(yb_ref)
    lane = lax.broadcasted_iota(jnp.int32, (HEAD_DIM, g), 1) % SCAN_SUB
    lane128 = lax.broadcasted_iota(jnp.int32, (HEAD_DIM, LANES), 1)
    dirs = ((rf_ref, kkf_ref, decf_ref, bf_ref, kmf_ref, vtf_ref, yf_ref),
            (rb_ref, kkb_ref, decb_ref, bb_ref, kmb_ref, vtb_ref, yb_ref))
    ents = HEAD_DIM // 4
    for d in range(2):
        pltpu.matmul_push_rhs(e_ref[...], 0, d)
        pltpu.matmul_acc_lhs(0, jnp.zeros((16, g), BF16), d, load_staged_rhs=0)
        pltpu.matmul_pop(0, (16, g), F32, d)

    def write_y(d, b, y, tt_done, valid):
        y_ref = dirs[d][-1]
        tl = tt_done if d == 0 else t_blk - 1 - tt_done
        tl = jnp.clip(tl, 0, t_blk - 1)
        hit = (lane == tl % SCAN_SUB) & valid
        pltpu.store(y_ref.at[b, tl // SCAN_SUB], y, mask=hit)

    def rows_at(d, tt):
        tl = jnp.clip(tt if d == 0 else t_blk - 1 - tt, 0, t_blk - 1)
        return lambda ref, b: jnp.broadcast_to(ref[b, pl.ds(tl, 1), :], (HEAD_DIM, g))

    for d, (r_ref, kk_ref, dec_ref, b_ref, km_ref, vt_ref, y_ref) in enumerate(dirs):
        row0 = rows_at(d, 0)
        for b in range(nb):
            pltpu.matmul_acc_lhs(b * ents, s_scr[d, b].astype(BF16) * row0(kk_ref, b).astype(BF16), d)

    def one_token(tt, carry):
        for b in range(nb):
            for d, (r_ref, kk_ref, dec_ref, b_ref, km_ref, vt_ref, y_ref) in enumerate(dirs):
                tl = tt if d == 0 else t_blk - 1 - tt
                row = rows_at(d, tt)
                nxt = rows_at(d, tt + 1)
                idx = (lane128 // SCAN_SUB) * SCAN_SUB + tl % SCAN_SUB
                sa = pltpu.matmul_pop(b * ents, (HEAD_DIM, g), F32, d)
                write_y(d, b, pltpu.matmul_pop((nb + b) * ents, (HEAD_DIM, g), F32, d), tt - 1, tt > 0)
                vcol = jnp.concatenate([jnp.take_along_axis(vt_ref[b, tl // SCAN_SUB][:, j * LANES:(j + 1) * LANES], idx, axis=1)
                                        for j in range(g // LANES)], axis=1)
                s = s_scr[d, b] * row(dec_ref, b) - sa * row(b_ref, b) + vcol * row(km_ref, b)
                s_scr[d, b] = s
                s_bf = s.astype(BF16)
                pltpu.matmul_acc_lhs((nb + b) * ents, s_bf * row(r_ref, b).astype(BF16), d)
                pltpu.matmul_acc_lhs(b * ents, s_bf * nxt(kk_ref, b).astype(BF16), d)
        return carry

    lax.fori_loop(0, t_blk, one_token, 0, unroll=8)
    for d in range(2):
        for b in range(nb):
            write_y(d, b, pltpu.matmul_pop((nb + b) * ents, (HEAD_DIM, g), F32, d), t_blk - 1, True)
            pltpu.matmul_pop(b * ents, (HEAD_DIM, g), F32, d)

    @pl.when(step_id == pl.num_programs(0) - 1)
    def _():
        send_ref[...] = s_scr[...]


def wkv_scan(pa3, kk, dec_f, b_f, km_f, dec_b, b_b, km_b, vt, eseg, s0):
    nb, l, g = kk.shape
    t_blk = SCAN_BLOCK
    nsub = t_blk // SCAN_SUB
    nblk = l // t_blk
    assert l % t_blk == 0
    seq_f = pl.BlockSpec((nb, t_blk, g), lambda i: (0, i, 0))
    seq_b = pl.BlockSpec((nb, t_blk, g), lambda i: (0, nblk - 1 - i, 0))
    vt_f = pl.BlockSpec((nb, nsub, HEAD_DIM, g), lambda i: (0, i, 0, 0))
    vt_b = pl.BlockSpec((nb, nsub, HEAD_DIM, g), lambda i: (0, nblk - 1 - i, 0, 0))
    state = pl.BlockSpec(s0.shape, lambda i: (0, 0, 0, 0))
    return pl.pallas_call(
        _wkv_scan_kernel,
        grid=(nblk,),
        in_specs=[seq_f, seq_b, seq_f, seq_b, seq_f, seq_f, seq_f, seq_b, seq_b, seq_b,
                  vt_f, vt_b, pl.BlockSpec(eseg.shape, lambda i: (0, 0)), state],
        out_specs=[vt_f, vt_b, state],
        out_shape=[jax.ShapeDtypeStruct(vt.shape, F32), jax.ShapeDtypeStruct(vt.shape, F32),
                   jax.ShapeDtypeStruct(s0.shape, F32)],
        scratch_shapes=[pltpu.VMEM(s0.shape, F32)],
        compiler_params=_params(("arbitrary",)),
        name="wkv_scan",
    )(pa3, pa3, kk, kk, dec_f, b_f, km_f, dec_b, b_b, km_b, vt, vt, eseg, s0)


def _block_ones(n, seg):
    idx = np.arange(n) // seg
    return jnp.asarray(idx[:, None] == idx[None, :], dtype=BF16)


def _padded_rows(w, offset, n):
    return jnp.zeros((n, w.shape[1]), w.dtype).at[offset:offset + w.shape[0]].set(w)


def _rope_tables(n_tokens, n_rep):
    t = jnp.arange(n_tokens)
    row = (t // GRID_W).astype(F32)
    col = (t % GRID_W).astype(F32)
    n_freq = HEAD_DIM // 4
    inv_freq = ROPE_THETA ** (-jnp.arange(n_freq, dtype=F32) / n_freq)
    ang = jnp.concatenate([row[:, None] * inv_freq, col[:, None] * inv_freq], axis=-1)
    cos, sin = jnp.cos(ang), jnp.sin(ang)
    cos_h = jnp.concatenate([cos, cos], axis=-1)
    sin_h = jnp.concatenate([-sin, sin], axis=-1)
    return jnp.tile(cos_h, (1, n_rep)), jnp.tile(sin_h, (1, n_rep))


def kernel(x, c, ctx, c_ctx, ada_w, ada_b, norm1_g, norm2_g, w_in, w_out, rwkv_w0, rwkv_w2, rwkv_a0, rwkv_a2,
           rwkv_k_k, rwkv_k_a, rwkv_r_k, rwkv_g2, rwkv_gn_w, rwkv_gn_b, na_rpb, gqa_q_gain, gqa_k_gain, pool_w,
           pool_scale, mlp_w1, mlp_w2, final_g):
    nb, l, d = x.shape
    lc = ctx.shape[1]
    depth = ada_w.shape[0]
    g = d // 4
    nh = g // HEAD_DIM
    nkv = nh // 2
    kvw = nkv * HEAD_DIM
    dr = rwkv_w2.shape[2]
    ir = rwkv_a2.shape[2]
    gr = rwkv_g2.shape[1]
    assert 2 * dr + 2 * ir + gr <= g and nb + 1 <= 8
    rows = l // GRID_W

    cpad = jnp.zeros((8, d), F32).at[:nb].set(c).at[nb].set(c_ctx)
    mods = adaln(cpad, ada_w, ada_b)

    eseg = _block_ones(g, HEAD_DIM)
    ekv = _block_ones(kvw, HEAD_DIM)
    cos_q, sin_q = _rope_tables(l, nh)
    ones_c, zeros_c = jnp.ones((lc, g), F32), jnp.zeros((lc, g), F32)

    splits = np.cumsum([0, g, g, g, dr, dr, ir, ir, gr, g, g, g, g, kvw, kvw, g])
    lowrank_w = splits[8] - splits[3]

    xl = x.reshape(nb * l, d)
    xc = ctx.reshape(nb * lc, d)
    zero_state = jnp.zeros((2, nb, HEAD_DIM, g), F32)

    for i in range(depth):
        need_ctx_out = i < depth - 1
        mod_l = mods[i, :nb].reshape(nb, N_MOD, 1, d)
        mod_c = mods[i, nb].reshape(N_MOD, 1, 1, d)
        ml = [mod_l[:, k] for k in range(N_MOD)]
        mc = [mod_c[k] for k in range(N_MOD)]

        wi = w_in[i]
        w_inp = jnp.concatenate([wi[:, :splits[8]], jnp.zeros((d, g - lowrank_w), F32),
                                 wi[:, splits[8]:splits[9]] * HEAD_DIM ** -0.5, wi[:, splits[9]:]], axis=1).astype(BF16)
        g1 = norm1_g[i].reshape(1, d)
        g2 = norm2_g[i].reshape(1, d)

        w2p = jnp.stack([_padded_rows(rwkv_w2[i, 0], 0, g), _padded_rows(rwkv_w2[i, 1], dr, g)]).astype(BF16)
        a2p = jnp.stack([_padded_rows(rwkv_a2[i, 0], 2 * dr, g),
                         _padded_rows(rwkv_a2[i, 1], 2 * dr + ir, g)]).astype(BF16)
        g2p = _padded_rows(rwkv_g2[i], 2 * dr + 2 * ir, g).astype(BF16)
        r_k = rwkv_r_k[i].reshape(1, g)
        gn_w = rwkv_gn_w[i].reshape(1, g)
        gn_b = rwkv_gn_b[i].reshape(1, g)
        rwkv_params = (rwkv_w0[i], rwkv_a0[i], w2p, a2p, rwkv_k_k[i].reshape(1, g), rwkv_k_a[i].reshape(1, g), eseg)
        q_gain = jnp.tile(gqa_q_gain[i], nh).reshape(1, g)
        k_gain = jnp.tile(gqa_k_gain[i], nkv).reshape(1, kvw)

        proj_l = inproj(xl, g1, ml[0], ml[1], w_inp, l, rwkv_params, (q_gain, k_gain, cos_q, sin_q, ekv), l, True)
        proj_c = inproj(xc, g1, mc[0], mc[1], w_inp, nb * lc, rwkv_params, (q_gain, k_gain, ones_c, zeros_c, ekv), lc, False)
        pa_l, pb_l, pd_l = proj_l[:3]
        pa_c, pb_c, pd_c = proj_c[:3]

        def scan_inputs(proj, seq):
            terms3 = [t.reshape(nb, seq, g) for t in proj[3:10]]
            return terms3, proj[10].reshape(nb, seq // SCAN_SUB, HEAD_DIM, g)

        terms3_c, vt_c = scan_inputs(proj_c, lc)
        yf_c, yb_c, state_c = wkv_scan(pa_c.reshape(nb, lc, 4 * g), *terms3_c, vt_c, eseg, zero_state)
        terms3_l, vt_l = scan_inputs(proj_l, l)
        yf_l, yb_l, _ = wkv_scan(pa_l.reshape(nb, l, 4 * g), *terms3_l, vt_l, eseg, state_c)

        bias = na_bias_table(na_rpb[i], rows)
        b_l = neighbourhood_attention(pb_l.reshape(nb, l, 3 * g), pb_c.reshape(nb, lc, 3 * g), bias).reshape(nb * l, g)

        qc_l, kc_l, vc_l = proj_l[11:14]
        qc_c, kc_c, vc_c = proj_c[11:14]
        kc_c, vc_c = kc_c.reshape(nb, lc, kvw), vc_c.reshape(nb, lc, kvw)
        k_all = jnp.concatenate([kc_c, kc_l.reshape(nb, l, kvw)], axis=1)
        v_all = jnp.concatenate([vc_c, vc_l.reshape(nb, l, kvw)], axis=1)
        c_l = attention(qc_l, k_all, v_all, l)

        w_bd = jax.scipy.linalg.block_diag(*[pool_w[i, k] for k in range(len(POOL_WINDOWS))]).astype(BF16)
        p_scale = pool_scale[i].reshape(1, g)
        d_l = pool_mixer(pd_l.reshape(nb, l, g), w_bd, p_scale).reshape(nb * l, g)

        w_out4 = w_out[i].reshape(4, g, d).astype(BF16)
        w1 = mlp_w1[i].astype(BF16)
        w2 = mlp_w2[i].astype(BF16)
        readout_params = (r_k, g2p, gn_w, gn_b, eseg)
        xl = outproj(yf_l.reshape(-1, HEAD_DIM, g), yb_l.reshape(-1, HEAD_DIM, g), pa_l, proj_l[6], proj_l[9],
                     readout_params, (b_l, c_l, d_l), w_out4, xl, ml[2], l)
        xl = mlp(xl, g2, ml[3], ml[4], ml[5], w1, w2, l)

        if need_ctx_out:
            b_c = attention(pb_c[:, :g], pb_c[:, g:2 * g].reshape(nb, lc, g), pb_c[:, 2 * g:].reshape(nb, lc, g), lc)
            c_c = attention(qc_c, kc_c, vc_c, lc)
            d_c = pool_mixer(pd_c.reshape(nb, lc, g), w_bd, p_scale).reshape(nb * lc, g)
            xc = outproj(yf_c.reshape(-1, HEAD_DIM, g), yb_c.reshape(-1, HEAD_DIM, g), pa_c, proj_c[6], proj_c[9],
                         readout_params, (b_c, c_c, d_c), w_out4, xc, mc[2], nb * lc)
            xc = mlp(xc, g2, mc[3], mc[4], mc[5], w1, w2, nb * lc)

    return final_norm(xl, final_g.reshape(1, d)).reshape(nb, l, d)
```

```python
import functools

import jax
import jax.numpy as jnp
import numpy as np
from jax import lax
from jax.experimental import pallas as pl
from jax.experimental.pallas import tpu as pltpu

F32 = jnp.float32
BF16 = jnp.bfloat16

HEAD_DIM = 64
GRID_W = 64
NA_ROWS = 8
NA_COLS = 16
ROPE_THETA = 10000.0
POOL_WINDOWS = (2, 4, 8, 16)
NORM_EPS = 1e-6
RWKV_GN_EPS = 1e-5 * HEAD_DIM
N_MOD = 6
LANES = 128
SCAN_BLOCK = 128
SCAN_SUB = 64
ATTN_CHUNK = 2048
VMEM_LIMIT = 52 * 1024 * 1024


def _params(sem):
    return pltpu.CompilerParams(dimension_semantics=sem, vmem_limit_bytes=VMEM_LIMIT)


def _tile(n, pref):
    t = min(n, pref)
    assert n % t == 0, (n, pref)
    return t


def _dot(a, b):
    return jnp.dot(a, b, preferred_element_type=F32)


def _dot_nt(a, b):
    return lax.dot_general(a, b, (((1,), (1,)), ((), ())), preferred_element_type=F32)


def _seg_sum(x, e):
    hi = x.astype(BF16)
    lo = (x - hi.astype(F32)).astype(BF16)
    return _dot(hi, e) + _dot(lo, e)


def _norm_mod(x, g, shift, scale):
    ms = jnp.mean(x * x, axis=-1, keepdims=True)
    h = x * lax.rsqrt(ms + NORM_EPS) * g
    return h * (1.0 + scale) + shift


def _adaln_kernel(c_ref, w_ref, b_ref, o_ref):
    c = c_ref[...]
    s = c * jax.nn.sigmoid(c)
    o_ref[0] = jnp.dot(s, w_ref[0], preferred_element_type=F32, precision=lax.Precision.HIGHEST) + b_ref[0]


def adaln(cpad, ada_w, ada_b):
    depth, d, n = ada_w.shape
    tn = _tile(n, 1536)
    return pl.pallas_call(
        _adaln_kernel,
        grid=(depth, n // tn),
        in_specs=[pl.BlockSpec((8, d), lambda i, j: (0, 0)),
                  pl.BlockSpec((1, d, tn), lambda i, j: (i, 0, j)),
                  pl.BlockSpec((1, 1, tn), lambda i, j: (i, 0, j))],
        out_specs=pl.BlockSpec((1, 8, tn), lambda i, j: (i, 0, j)),
        out_shape=jax.ShapeDtypeStruct((depth, 8, n), F32),
        compiler_params=_params(("arbitrary", "arbitrary")),
        name="adaln",
    )(cpad, ada_w, ada_b.reshape(depth, 1, n))


def _swap_halves(y):
    n = y.shape[-1]
    lane = lax.broadcasted_iota(jnp.int32, y.shape, 1)
    half = HEAD_DIM // 2
    return jnp.where(lane % HEAD_DIM < half, pltpu.roll(y, n - half, 1), pltpu.roll(y, half, 1))


def _to_tiles(x):
    g = x.shape[1]
    tiles = []
    for s in range(x.shape[0] // SCAN_SUB):
        blk = x[s * SCAN_SUB:(s + 1) * SCAN_SUB, :]
        tiles.append(jnp.concatenate([blk[:, h * HEAD_DIM:(h + 1) * HEAD_DIM].T for h in range(g // HEAD_DIM)], axis=1))
    return tiles


def _from_tiles(tiles):
    g = tiles[0].shape[1]
    return jnp.concatenate(
        [jnp.concatenate([t[:, h * SCAN_SUB:(h + 1) * SCAN_SUB].T for h in range(g // HEAD_DIM)], axis=1) for t in tiles],
        axis=0)


def _split_store(o_ref, val):
    for j in range(o_ref.shape[0]):
        o_ref[j] = val[:, j * LANES:(j + 1) * LANES]


def _split_load(ref):
    return jnp.concatenate([ref[j] for j in range(ref.shape[0])], axis=1)


def _inproj_kernel(x_ref, g_ref, sh_ref, sc_ref, w_ref,
                   w0_ref, a0_ref, w2_ref, a2_ref, kk_w_ref, ka_ref, e_ref,
                   qg_ref, kg_ref, cos_ref, sin_ref, ek_ref,
                   oa_ref, ob_ref, od_ref,
                   r_ref, kk_ref, decf_ref, bf_ref, kmf_ref, decb_ref, bb_ref, kmb_ref, vt_ref,
                   q_ref, kc_ref, vc_ref, *, rope):
    g = od_ref.shape[1]
    gk = kc_ref.shape[1]
    h = _norm_mod(x_ref[...], g_ref[...], sh_ref[0], sc_ref[0]).astype(BF16)
    e = e_ref[...]

    pa = _dot(h, w_ref[:, 0:4 * g])
    pc = _dot(h, w_ref[:, 7 * g:9 * g])

    oa_ref[...] = pa
    _split_store(r_ref, pa[:, :g])
    k = pa[:, g:2 * g]
    lr = pa[:, 3 * g:4 * g]
    for s, tile in enumerate(_to_tiles(pa[:, 2 * g:3 * g])):
        vt_ref[s] = tile
    kx = k * kk_w_ref[...]
    norm = jnp.sqrt(_seg_sum(kx * kx, e))
    kk = kx / jnp.maximum(norm, 1e-12)
    _split_store(kk_ref, kk)
    lr_t = jnp.tanh(lr).astype(BF16)
    lr_b = lr.astype(BF16)
    for d, (dec_ref, b_ref, km_ref) in enumerate(((decf_ref, bf_ref, kmf_ref), (decb_ref, bb_ref, kmb_ref))):
        z = w0_ref[d:d + 1, :] + _dot(lr_t, w2_ref[d])
        softplus_neg = jnp.maximum(-z, 0.0) + jnp.log1p(jnp.exp(-jnp.abs(z)))
        w = -softplus_neg - 0.5
        _split_store(dec_ref, jnp.exp(-jnp.exp(w)))
        a = jax.nn.sigmoid(a0_ref[d:d + 1, :] + _dot(lr_b, a2_ref[d]))
        _split_store(b_ref, kk * a)
        _split_store(km_ref, k * (1.0 + (a - 1.0) * ka_ref[...]))

    def normed(x, gain, seg):
        ms = _seg_sum(x * x, seg) * (1.0 / HEAD_DIM)
        return x * lax.rsqrt(ms + NORM_EPS) * gain

    q = normed(pc[:, :g], qg_ref[...], e)
    kc = normed(pc[:, g:g + gk], kg_ref[...], ek_ref[...])
    if rope:
        cos = cos_ref[...]
        sin = sin_ref[...]
        q = q * cos + _swap_halves(q) * sin
        kc = kc * cos[:, :gk] + _swap_halves(kc) * sin[:, :gk]
    q_ref[...] = (q * HEAD_DIM ** -0.5).astype(BF16)
    kc_ref[...] = kc.astype(BF16)
    vc_ref[...] = pc[:, g + gk:].astype(BF16)

    ob_ref[...] = _dot(h, w_ref[:, 4 * g:7 * g]).astype(ob_ref.dtype)

    od_ref[...] = _dot(h, w_ref[:, 9 * g:10 * g])


def inproj(x, gain, shift, scale, w, rows_per_mod, rwkv_params, gqa_params, seq_len, rope):
    r, d = x.shape
    g = w.shape[1] // 10
    gk = g // 2
    tm = _tile(min(rows_per_mod, seq_len), 512)
    tpb = rows_per_mod // tm
    nseq = seq_len // tm
    modspec = pl.BlockSpec((1, 1, d), lambda i: (i // tpb, 0, 0))
    full2 = lambda a: pl.BlockSpec(a.shape, lambda i: (0,) * a.ndim)
    rows = lambda n: pl.BlockSpec((tm, n), lambda i: (i, 0))
    q_gain, k_gain, cos, sin, ekv = gqa_params
    table = pl.BlockSpec((tm, g), lambda i: (i % nseq, 0))
    split = pl.BlockSpec((g // LANES, tm, LANES), lambda i: (0, i, 0))
    out_specs = ([rows(4 * g), rows(3 * g), rows(g)] + [split] * 8
                 + [pl.BlockSpec((tm // SCAN_SUB, HEAD_DIM, g), lambda i: (i, 0, 0))] + [rows(g), rows(gk), rows(gk)])
    out_shape = ([jax.ShapeDtypeStruct((r, 4 * g), F32), jax.ShapeDtypeStruct((r, 3 * g), BF16),
                  jax.ShapeDtypeStruct((r, g), F32)] + [jax.ShapeDtypeStruct((g // LANES, r, LANES), F32)] * 8
                 + [jax.ShapeDtypeStruct((r // SCAN_SUB, HEAD_DIM, g), F32), jax.ShapeDtypeStruct((r, g), BF16),
                    jax.ShapeDtypeStruct((r, gk), BF16), jax.ShapeDtypeStruct((r, gk), BF16)])
    return pl.pallas_call(
        functools.partial(_inproj_kernel, rope=rope),
        grid=(r // tm,),
        in_specs=[rows(d), pl.BlockSpec((1, d), lambda i: (0, 0)), modspec, modspec, full2(w)]
                 + [full2(p) for p in rwkv_params] + [full2(q_gain), full2(k_gain), table, table, full2(ekv)],
        out_specs=out_specs,
        out_shape=out_shape,
        compiler_params=_params(("arbitrary",)),
        name="inproj",
    )(x, gain, shift, scale, w, *rwkv_params, q_gain, k_gain, cos, sin, ekv)


def _outproj_kernel(yf_ref, yb_ref, p_ref, kmf_ref, kmb_ref, rk_ref, g2_ref, gnw_ref, gnb_ref, e_ref,
                    b_ref, c_ref, d_ref, w_ref, x_ref, gate_ref, o_ref):
    g = b_ref.shape[1]
    e = e_ref[...]
    r = p_ref[:, :g]
    v = p_ref[:, 2 * g:3 * g]
    lr = p_ref[:, 3 * g:4 * g]
    y = _from_tiles([yf_ref[s] + yb_ref[s] for s in range(yf_ref.shape[0])])
    mu = _seg_sum(y, e) * (1.0 / HEAD_DIM)
    yc = y - mu
    var = _seg_sum(yc * yc, e) * (1.0 / HEAD_DIM)
    yn = yc * lax.rsqrt(var + RWKV_GN_EPS) * gnw_ref[...] + gnb_ref[...]
    bonus = _seg_sum(r * (_split_load(kmf_ref) + _split_load(kmb_ref)) * rk_ref[...], e) * v
    a = (yn + bonus) * _dot(jax.nn.sigmoid(lr).astype(BF16), g2_ref[...])
    acc = _dot(b_ref[...].astype(BF16), w_ref[1])
    acc += _dot(c_ref[...].astype(BF16), w_ref[2])
    acc += _dot(d_ref[...].astype(BF16), w_ref[3])
    acc += _dot(a.astype(BF16), w_ref[0])
    o_ref[...] = x_ref[...] + gate_ref[0] * acc


def outproj(yf, yb, pa, km_f, km_b, readout_params, mix_bcd, w4, x, gate, rows_per_mod):
    r, d = x.shape
    g = w4.shape[1]
    tm = _tile(rows_per_mod, 512)
    tpb = rows_per_mod // tm
    rows = lambda n: pl.BlockSpec((tm, n), lambda i: (i, 0))
    tiles = pl.BlockSpec((tm // SCAN_SUB, HEAD_DIM, g), lambda i: (i, 0, 0))
    split = pl.BlockSpec((g // LANES, tm, LANES), lambda i: (0, i, 0))
    full2 = lambda a: pl.BlockSpec(a.shape, lambda i: (0,) * a.ndim)
    return pl.pallas_call(
        _outproj_kernel,
        grid=(r // tm,),
        in_specs=[tiles, tiles, rows(4 * g), split, split] + [full2(p) for p in readout_params]
                 + [rows(g), rows(g), rows(g), full2(w4), rows(d), pl.BlockSpec((1, 1, d), lambda i: (i // tpb, 0, 0))],
        out_specs=rows(d),
        out_shape=jax.ShapeDtypeStruct((r, d), F32),
        compiler_params=_params(("arbitrary",)),
        name="outproj",
    )(yf, yb, pa, km_f, km_b, *readout_params, *mix_bcd, w4, x, gate)


def _mlp_kernel(x_ref, g_ref, sh_ref, sc_ref, gate_ref, w1_ref, w2_ref, o_ref, h_scr, acc_scr):
    j = pl.program_id(1)

    @pl.when(j == 0)
    def _():
        h_scr[...] = _norm_mod(x_ref[...], g_ref[...], sh_ref[0], sc_ref[0]).astype(BF16)
        acc_scr[...] = jnp.zeros_like(acc_scr)

    u = jnp.maximum(_dot(h_scr[...], w1_ref[...]), 0.0)
    acc_scr[...] += _dot((u * u).astype(BF16), w2_ref[...])

    @pl.when(j == pl.num_programs(1) - 1)
    def _():
        o_ref[...] = x_ref[...] + gate_ref[0] * acc_scr[...]


def mlp(x, g, shift, scale, gate, w1, w2, rows_per_mod):
    r, d = x.shape
    dff = w1.shape[1]
    tm = _tile(rows_per_mod, 1024)
    tf = _tile(dff, 1024)
    tpb = rows_per_mod // tm
    modspec = pl.BlockSpec((1, 1, d), lambda i, j: (i // tpb, 0, 0))
    return pl.pallas_call(
        _mlp_kernel,
        grid=(r // tm, dff // tf),
        in_specs=[pl.BlockSpec((tm, d), lambda i, j: (i, 0)),
                  pl.BlockSpec((1, d), lambda i, j: (0, 0)),
                  modspec, modspec, modspec,
                  pl.BlockSpec((d, tf), lambda i, j: (0, j)),
                  pl.BlockSpec((tf, d), lambda i, j: (j, 0))],
        out_specs=pl.BlockSpec((tm, d), lambda i, j: (i, 0)),
        out_shape=jax.ShapeDtypeStruct((r, d), F32),
        scratch_shapes=[pltpu.VMEM((tm, d), BF16), pltpu.VMEM((tm, d), F32)],
        compiler_params=_params(("arbitrary", "arbitrary")),
        name="mlp",
    )(x, g, shift, scale, gate, w1, w2)


def _final_norm_kernel(x_ref, g_ref, o_ref):
    x = x_ref[...]
    ms = jnp.mean(x * x, axis=-1, keepdims=True)
    o_ref[...] = x * lax.rsqrt(ms + NORM_EPS) * g_ref[...]


def final_norm(x, g):
    r, d = x.shape
    tm = _tile(r, 1024)
    return pl.pallas_call(
        _final_norm_kernel,
        grid=(r // tm,),
        in_specs=[pl.BlockSpec((tm, d), lambda i: (i, 0)), pl.BlockSpec((1, d), lambda i: (0, 0))],
        out_specs=pl.BlockSpec((tm, d), lambda i: (i, 0)),
        out_shape=jax.ShapeDtypeStruct((r, d), F32),
        compiler_params=_params(("arbitrary",)),
        name="final_norm",
    )(x, g)


def _pad_heads(q, n_heads, kv_width, rep):
    assert kv_width % LANES == 0
    t = q.shape[0]
    lane = lax.broadcasted_iota(jnp.int32, (t, LANES), 1)
    lo_half = lane < HEAD_DIM
    zero = jnp.zeros((t, LANES), q.dtype)
    rows = []
    for h in range(n_heads):
        src = h * HEAD_DIM
        dst = (h // rep) * HEAD_DIM
        piece = q[:, src // LANES * LANES:(src // LANES + 1) * LANES]
        if src % LANES != dst % LANES:
            piece = pltpu.roll(piece.astype(F32), HEAD_DIM, 1).astype(q.dtype)
        piece = jnp.where(lo_half if dst % LANES == 0 else ~lo_half, piece, zero)
        blocks = [piece if j == dst // LANES else zero for j in range(kv_width // LANES)]
        rows.append(jnp.concatenate(blocks, axis=1) if len(blocks) > 1 else piece)
    return jnp.concatenate(rows, axis=0)


def _gather_heads(res, n_heads, rep):
    t = res.shape[0] // n_heads
    lane = lax.broadcasted_iota(jnp.int32, (t, LANES), 1)
    lo_half = lane < HEAD_DIM
    pieces = []
    for h in range(n_heads):
        src = (h // rep) * HEAD_DIM
        dst = h * HEAD_DIM
        piece = res[h * t:(h + 1) * t, src // LANES * LANES:(src // LANES + 1) * LANES]
        if src % LANES != dst % LANES:
            piece = pltpu.roll(piece, HEAD_DIM, 1)
        pieces.append(piece)
    blocks = [jnp.where(lo_half, pieces[2 * j], pieces[2 * j + 1]) for j in range(n_heads // 2)]
    return jnp.concatenate(blocks, axis=1)


def _attn_kernel(q_ref, k_ref, v_ref, o_ref, *, rep):
    n_heads = q_ref.shape[1] // HEAD_DIM
    lk = k_ref.shape[1]
    q = _pad_heads(q_ref[...], n_heads, k_ref.shape[2], rep)
    edges = list(range(0, lk, ATTN_CHUNK)) + [lk]
    m = l = acc = None
    for k0, k1 in zip(edges[:-1], edges[1:]):
        s = _dot_nt(q, k_ref[0, k0:k1, :])
        m_new = jnp.max(s, axis=-1, keepdims=True)
        if m is not None:
            m_new = jnp.maximum(m, m_new)
        p = jnp.exp(s - m_new)
        l_new = jnp.sum(p, axis=-1, keepdims=True)
        acc_new = _dot(p.astype(BF16), v_ref[0, k0:k1, :])
        if m is not None:
            alpha = jnp.exp(m - m_new)
            l_new = alpha * l + l_new
            acc_new = alpha * acc + acc_new
        m, l, acc = m_new, l_new, acc_new
    o_ref[...] = _gather_heads(acc / l, n_heads, rep)


def attention(q, k, v, seq_len):
    r, gq = q.shape
    nb, lk, kw = k.shape
    rep = gq // kw
    tq = _tile(seq_len, 128)
    nq = seq_len // tq
    return pl.pallas_call(
        functools.partial(_attn_kernel, rep=rep),
        grid=(nb, nq),
        in_specs=[pl.BlockSpec((tq, gq), lambda i, j: (i * nq + j, 0)),
                  pl.BlockSpec((1, lk, kw), lambda i, j: (i, 0, 0)),
                  pl.BlockSpec((1, lk, kw), lambda i, j: (i, 0, 0))],
        out_specs=pl.BlockSpec((tq, gq), lambda i, j: (i * nq + j, 0)),
        out_shape=jax.ShapeDtypeStruct((r, gq), F32),
        compiler_params=_params(("arbitrary", "arbitrary")),
        name="attention",
    )(q, k, v)


def _na_kernel(p_ref, pc_ref, bias_ref, o_ref, *, rows, kh):
    g = o_ref.shape[2]
    n_heads = g // HEAD_DIM
    band = kh * GRID_W
    kc = pc_ref[0, :, g:2 * g]
    vc = pc_ref[0, :, 2 * g:3 * g]

    def one_row(r, carry):
        rs = jnp.clip(r - kh // 2, 0, rows - kh)
        q0 = pl.multiple_of(r * GRID_W, GRID_W)
        k0 = pl.multiple_of(rs * GRID_W, GRID_W)
        q = _pad_heads(p_ref[0, pl.ds(q0, GRID_W), 0:g], n_heads, g, 1)
        s1 = _dot_nt(q, p_ref[0, pl.ds(k0, band), g:2 * g]) + bias_ref[r - rs]
        s2 = _dot_nt(q, kc)
        m = jnp.maximum(jnp.max(s1, axis=-1, keepdims=True), jnp.max(s2, axis=-1, keepdims=True))
        p1 = jnp.exp(s1 - m)
        p2 = jnp.exp(s2 - m)
        l = jnp.sum(p1, axis=-1, keepdims=True) + jnp.sum(p2, axis=-1, keepdims=True)
        res = (_dot(p1.astype(BF16), p_ref[0, pl.ds(k0, band), 2 * g:3 * g]) + _dot(p2.astype(BF16), vc)) / l
        o_ref[0, pl.ds(q0, GRID_W), :] = _gather_heads(res, n_heads, 1)
        return carry

    lax.fori_loop(0, rows, one_row, 0, unroll=8)


def neighbourhood_attention(p, pc, bias):
    nb, l, g3 = p.shape
    g = g3 // 3
    lc = pc.shape[1]
    rows = l // GRID_W
    kh = min(NA_ROWS, rows)
    return pl.pallas_call(
        functools.partial(_na_kernel, rows=rows, kh=kh),
        grid=(nb,),
        in_specs=[pl.BlockSpec((1, l, g3), lambda i: (i, 0, 0)),
                  pl.BlockSpec((1, lc, g3), lambda i: (i, 0, 0)),
                  pl.BlockSpec(bias.shape, lambda i: (0, 0, 0))],
        out_specs=pl.BlockSpec((1, l, g), lambda i: (i, 0, 0)),
        out_shape=jax.ShapeDtypeStruct((nb, l, g), F32),
        compiler_params=_params(("arbitrary",)),
        name="neighbourhood_attention",
    )(p, pc, bias)


def _na_bias_kernel(rpb_ref, o_ref, *, kh):
    h = pl.program_id(0)
    q = lax.broadcasted_iota(jnp.int32, (GRID_W, GRID_W), 0)
    k = lax.broadcasted_iota(jnp.int32, (GRID_W, GRID_W), 1)
    start = jnp.clip(q - NA_COLS // 2, 0, GRID_W - NA_COLS)
    in_win = (k >= start) & (k < start + NA_COLS)
    off = k - q + NA_COLS - 1
    neg = jnp.full((GRID_W, GRID_W), -jnp.inf, F32)
    blocks = []
    for ro in range(2 * NA_ROWS - 1):
        t = neg
        for c in range(2 * NA_COLS - 1):
            t = jnp.where(off == c, rpb_ref[h, ro, c], t)
        blocks.append(jnp.where(in_win, t, neg))
    for di in range(kh):
        for i in range(kh):
            o_ref[di, :, i * GRID_W:(i + 1) * GRID_W] = blocks[i - di + NA_ROWS - 1]


def na_bias_table(rpb, rows):
    kh = min(NA_ROWS, rows)
    nh = rpb.shape[0]
    return pl.pallas_call(
        functools.partial(_na_bias_kernel, kh=kh),
        grid=(nh,),
        in_specs=[pl.BlockSpec(memory_space=pltpu.SMEM)],
        out_specs=pl.BlockSpec((kh, GRID_W, kh * GRID_W), lambda i: (0, i, 0)),
        out_shape=jax.ShapeDtypeStruct((kh, nh * GRID_W, kh * GRID_W), F32),
        compiler_params=_params(("arbitrary",)),
        name="na_bias",
    )(rpb)


def _pool_kernel(x_ref, inv_ref, w_ref, scale_ref, o_ref):
    x = x_ref[0]
    n, g = x.shape
    pg = g // len(POOL_WINDOWS)
    pad = max(POOL_WINDOWS) // 2
    zeros = jnp.zeros((pad, g), F32)
    xe = jnp.concatenate([zeros, x, zeros], axis=0)
    ne = n + 2 * pad
    group = lax.broadcasted_iota(jnp.int32, xe.shape, 1) // pg
    fwd = xe
    bwd = xe
    cur = 1
    total = jnp.zeros_like(xe)
    for j, w in enumerate(POOL_WINDOWS):
        half = w // 2
        while cur < half:
            fwd = fwd + pltpu.roll(fwd, ne - cur, 0)
            bwd = bwd + pltpu.roll(bwd, cur, 0)
            cur *= 2
        total = jnp.where(group == j, pltpu.roll(bwd, 1, 0) + fwd, total)
    diff = total[pad:pad + n] * inv_ref[...] - x
    o_ref[0] = _dot(diff.astype(BF16), w_ref[...]) * scale_ref[...]


def _pool_inverse_counts(n, g):
    t = np.arange(n)
    cols = []
    for w in POOL_WINDOWS:
        lo = np.clip(t - w // 2, 0, n)
        hi = np.clip(t - w // 2 + w, 0, n)
        cols.append(np.repeat((1.0 / (hi - lo))[:, None], g // len(POOL_WINDOWS), axis=1))
    return jnp.asarray(np.concatenate(cols, axis=1), dtype=F32)


def pool_mixer(p, w_bd, scale):
    b, n, g = p.shape
    return pl.pallas_call(
        _pool_kernel,
        grid=(b,),
        in_specs=[pl.BlockSpec((1, n, g), lambda i: (i, 0, 0)),
                  pl.BlockSpec((n, g), lambda i: (0, 0)),
                  pl.BlockSpec((g, g), lambda i: (0, 0)),
                  pl.BlockSpec((1, g), lambda i: (0, 0))],
        out_specs=pl.BlockSpec((1, n, g), lambda i: (i, 0, 0)),
        out_shape=jax.ShapeDtypeStruct((b, n, g), F32),
        compiler_params=_params(("arbitrary",)),
        name="pool_mixer",
    )(p, _pool_inverse_counts(n, g), w_bd, scale)


def _wkv_scan_kernel(rf_ref, rb_ref, kkf_ref, kkb_ref, decf_ref, bf_ref, kmf_ref, decb_ref, bb_ref, kmb_ref,
                     vtf_ref, vtb_ref, e_ref, s0_ref, yf_ref, yb_ref, send_ref, s_scr):
    npair, nb, t_blk, _ = rf_ref.shape
    g = npair * LANES
    step_id = pl.program_id(0)

    @pl.when(step_id == 0)
    def _():
        s_scr[...] = s0_ref[...]

    yf_ref[...] = jnp.zeros_like(yf_ref)
    yb_ref[...] = jnp.zeros_like(yb_ref)
    lane = lax.broadcasted_iota(jnp.int32, (HEAD_DIM, g), 1) % SCAN_SUB
    lane128 = lax.broadcasted_iota(jnp.int32, (HEAD_DIM, LANES), 1)
    dirs = ((rf_ref, kkf_ref, decf_ref, bf_ref, kmf_ref, vtf_ref, yf_ref),
            (rb_ref, kkb_ref, decb_ref, bb_ref, kmb_ref, vtb_ref, yb_ref))
    ents = HEAD_DIM // 4
    for d in range(2):
        pltpu.matmul_push_rhs(e_ref[...], 0, d)
        pltpu.matmul_acc_lhs(0, jnp.zeros((16, g), BF16), d, load_staged_rhs=0)
        pltpu.matmul_pop(0, (16, g), F32, d)

    def write_y(d, b, y, tt_done, valid):
        y_ref = dirs[d][-1]
        tl = tt_done if d == 0 else t_blk - 1 - tt_done
        tl = jnp.clip(tl, 0, t_blk - 1)
        hit = (lane == tl % SCAN_SUB) & valid
        pltpu.store(y_ref.at[b, tl // SCAN_SUB], y, mask=hit)

    def rows_at(d, tt):
        tl = jnp.clip(tt if d == 0 else t_blk - 1 - tt, 0, t_blk - 1)

        def row(ref, b):
            slabs = [ref[j, b, pl.ds(tl, 8, stride=0), :] for j in range(npair)]
            return jnp.concatenate([jnp.concatenate(slabs, axis=1)] * (HEAD_DIM // 8), axis=0)

        return row

    for d, (r_ref, kk_ref, dec_ref, b_ref, km_ref, vt_ref, y_ref) in enumerate(dirs):
        row0 = rows_at(d, 0)
        for b in range(nb):
            pltpu.matmul_acc_lhs(b * ents, s_scr[d, b].astype(BF16) * row0(kk_ref, b).astype(BF16), d)

    def one_token(tt, carry):
        for b in range(nb):
            for d, (r_ref, kk_ref, dec_ref, b_ref, km_ref, vt_ref, y_ref) in enumerate(dirs):
                tl = tt if d == 0 else t_blk - 1 - tt
                row = rows_at(d, tt)
                nxt = rows_at(d, tt + 1)
                idx = (lane128 // SCAN_SUB) * SCAN_SUB + tl % SCAN_SUB
                sa = pltpu.matmul_pop(b * ents, (HEAD_DIM, g), F32, d)
                write_y(d, b, pltpu.matmul_pop((nb + b) * ents, (HEAD_DIM, g), F32, d), tt - 1, tt > 0)
                vcol = jnp.concatenate([jnp.take_along_axis(vt_ref[b, tl // SCAN_SUB][:, j * LANES:(j + 1) * LANES], idx, axis=1)
                                        for j in range(g // LANES)], axis=1)
                s = s_scr[d, b] * row(dec_ref, b) - sa * row(b_ref, b) + vcol * row(km_ref, b)
                s_scr[d, b] = s
                s_bf = s.astype(BF16)
                pltpu.matmul_acc_lhs((nb + b) * ents, s_bf * row(r_ref, b).astype(BF16), d)
                pltpu.matmul_acc_lhs(b * ents, s_bf * nxt(kk_ref, b).astype(BF16), d)
        return carry

    lax.fori_loop(0, t_blk, one_token, 0, unroll=8)
    for d in range(2):
        for b in range(nb):
            write_y(d, b, pltpu.matmul_pop((nb + b) * ents, (HEAD_DIM, g), F32, d), t_blk - 1, True)
            pltpu.matmul_pop(b * ents, (HEAD_DIM, g), F32, d)

    @pl.when(step_id == pl.num_programs(0) - 1)
    def _():
        send_ref[...] = s_scr[...]


def wkv_scan(r, kk, dec_f, b_f, km_f, dec_b, b_b, km_b, vt, eseg, s0):
    npair, nb, l, _ = kk.shape
    g = npair * LANES
    t_blk = SCAN_BLOCK
    nsub = t_blk // SCAN_SUB
    nblk = l // t_blk
    assert l % t_blk == 0
    seq_f = pl.BlockSpec((npair, nb, t_blk, LANES), lambda i: (0, 0, i, 0))
    seq_b = pl.BlockSpec((npair, nb, t_blk, LANES), lambda i: (0, 0, nblk - 1 - i, 0))
    vt_f = pl.BlockSpec((nb, nsub, HEAD_DIM, g), lambda i: (0, i, 0, 0))
    vt_b = pl.BlockSpec((nb, nsub, HEAD_DIM, g), lambda i: (0, nblk - 1 - i, 0, 0))
    state = pl.BlockSpec(s0.shape, lambda i: (0, 0, 0, 0))
    return pl.pallas_call(
        _wkv_scan_kernel,
        grid=(nblk,),
        in_specs=[seq_f, seq_b, seq_f, seq_b, seq_f, seq_f, seq_f, seq_b, seq_b, seq_b,
                  vt_f, vt_b, pl.BlockSpec(eseg.shape, lambda i: (0, 0)), state],
        out_specs=[vt_f, vt_b, state],
        out_shape=[jax.ShapeDtypeStruct(vt.shape, F32), jax.ShapeDtypeStruct(vt.shape, F32),
                   jax.ShapeDtypeStruct(s0.shape, F32)],
        scratch_shapes=[pltpu.VMEM(s0.shape, F32)],
        compiler_params=_params(("arbitrary",)),
        name="wkv_scan",
    )(r, r, kk, kk, dec_f, b_f, km_f, dec_b, b_b, km_b, vt, vt, eseg, s0)


def _block_ones(n, seg):
    idx = np.arange(n) // seg
    return jnp.asarray(idx[:, None] == idx[None, :], dtype=BF16)


def _padded_rows(w, offset, n):
    return jnp.zeros((n, w.shape[1]), w.dtype).at[offset:offset + w.shape[0]].set(w)


def _rope_tables(n_tokens, n_rep):
    t = jnp.arange(n_tokens)
    row = (t // GRID_W).astype(F32)
    col = (t % GRID_W).astype(F32)
    n_freq = HEAD_DIM // 4
    inv_freq = ROPE_THETA ** (-jnp.arange(n_freq, dtype=F32) / n_freq)
    ang = jnp.concatenate([row[:, None] * inv_freq, col[:, None] * inv_freq], axis=-1)
    cos, sin = jnp.cos(ang), jnp.sin(ang)
    cos_h = jnp.concatenate([cos, cos], axis=-1)
    sin_h = jnp.concatenate([-sin, sin], axis=-1)
    return jnp.tile(cos_h, (1, n_rep)), jnp.tile(sin_h, (1, n_rep))


def kernel(x, c, ctx, c_ctx, ada_w, ada_b, norm1_g, norm2_g, w_in, w_out, rwkv_w0, rwkv_w2, rwkv_a0, rwkv_a2,
           rwkv_k_k, rwkv_k_a, rwkv_r_k, rwkv_g2, rwkv_gn_w, rwkv_gn_b, na_rpb, gqa_q_gain, gqa_k_gain, pool_w,
           pool_scale, mlp_w1, mlp_w2, final_g):
    nb, l, d = x.shape
    lc = ctx.shape[1]
    depth = ada_w.shape[0]
    g = d // 4
    nh = g // HEAD_DIM
    nkv = nh // 2
    kvw = nkv * HEAD_DIM
    dr = rwkv_w2.shape[2]
    ir = rwkv_a2.shape[2]
    gr = rwkv_g2.shape[1]
    assert 2 * dr + 2 * ir + gr <= g and nb + 1 <= 8
    rows = l // GRID_W

    cpad = jnp.zeros((8, d), F32).at[:nb].set(c).at[nb].set(c_ctx)
    mods = adaln(cpad, ada_w, ada_b)

    eseg = _block_ones(g, HEAD_DIM)
    ekv = _block_ones(kvw, HEAD_DIM)
    cos_q, sin_q = _rope_tables(l, nh)
    ones_c, zeros_c = jnp.ones((lc, g), F32), jnp.zeros((lc, g), F32)

    splits = np.cumsum([0, g, g, g, dr, dr, ir, ir, gr, g, g, g, g, kvw, kvw, g])
    lowrank_w = splits[8] - splits[3]

    xl = x.reshape(nb * l, d)
    xc = ctx.reshape(nb * lc, d)
    zero_state = jnp.zeros((2, nb, HEAD_DIM, g), F32)

    for i in range(depth):
        need_ctx_out = i < depth - 1
        mod_l = mods[i, :nb].reshape(nb, N_MOD, 1, d)
        mod_c = mods[i, nb].reshape(N_MOD, 1, 1, d)
        ml = [mod_l[:, k] for k in range(N_MOD)]
        mc = [mod_c[k] for k in range(N_MOD)]

        wi = w_in[i]
        w_inp = jnp.concatenate([wi[:, :splits[8]], jnp.zeros((d, g - lowrank_w), F32),
                                 wi[:, splits[8]:splits[9]] * HEAD_DIM ** -0.5, wi[:, splits[9]:]], axis=1).astype(BF16)
        g1 = norm1_g[i].reshape(1, d)
        g2 = norm2_g[i].reshape(1, d)

        w2p = jnp.stack([_padded_rows(rwkv_w2[i, 0], 0, g), _padded_rows(rwkv_w2[i, 1], dr, g)]).astype(BF16)
        a2p = jnp.stack([_padded_rows(rwkv_a2[i, 0], 2 * dr, g),
                         _padded_rows(rwkv_a2[i, 1], 2 * dr + ir, g)]).astype(BF16)
        g2p = _padded_rows(rwkv_g2[i], 2 * dr + 2 * ir, g).astype(BF16)
        r_k = rwkv_r_k[i].reshape(1, g)
        gn_w = rwkv_gn_w[i].reshape(1, g)
        gn_b = rwkv_gn_b[i].reshape(1, g)
        rwkv_params = (rwkv_w0[i], rwkv_a0[i], w2p, a2p, rwkv_k_k[i].reshape(1, g), rwkv_k_a[i].reshape(1, g), eseg)
        q_gain = jnp.tile(gqa_q_gain[i], nh).reshape(1, g)
        k_gain = jnp.tile(gqa_k_gain[i], nkv).reshape(1, kvw)

        proj_l = inproj(xl, g1, ml[0], ml[1], w_inp, l, rwkv_params, (q_gain, k_gain, cos_q, sin_q, ekv), l, True)
        proj_c = inproj(xc, g1, mc[0], mc[1], w_inp, nb * lc, rwkv_params, (q_gain, k_gain, ones_c, zeros_c, ekv), lc, False)
        pa_l, pb_l, pd_l = proj_l[:3]
        pa_c, pb_c, pd_c = proj_c[:3]

        def scan_inputs(proj, seq):
            rows4 = [t.reshape(g // LANES, nb, seq, LANES) for t in proj[3:11]]
            return rows4, proj[11].reshape(nb, seq // SCAN_SUB, HEAD_DIM, g)

        rows4_c, vt_c = scan_inputs(proj_c, lc)
        yf_c, yb_c, state_c = wkv_scan(*rows4_c, vt_c, eseg, zero_state)
        rows4_l, vt_l = scan_inputs(proj_l, l)
        yf_l, yb_l, _ = wkv_scan(*rows4_l, vt_l, eseg, state_c)

        bias = na_bias_table(na_rpb[i], rows)
        b_l = neighbourhood_attention(pb_l.reshape(nb, l, 3 * g), pb_c.reshape(nb, lc, 3 * g), bias).reshape(nb * l, g)

        qc_l, kc_l, vc_l = proj_l[12:15]
        qc_c, kc_c, vc_c = proj_c[12:15]
        kc_c, vc_c = kc_c.reshape(nb, lc, kvw), vc_c.reshape(nb, lc, kvw)
        k_all = jnp.concatenate([kc_c, kc_l.reshape(nb, l, kvw)], axis=1)
        v_all = jnp.concatenate([vc_c, vc_l.reshape(nb, l, kvw)], axis=1)
        c_l = attention(qc_l, k_all, v_all, l)

        w_bd = jax.scipy.linalg.block_diag(*[pool_w[i, k] for k in range(len(POOL_WINDOWS))]).astype(BF16)
        p_scale = pool_scale[i].reshape(1, g)
        d_l = pool_mixer(pd_l.reshape(nb, l, g), w_bd, p_scale).reshape(nb * l, g)

        w_out4 = w_out[i].reshape(4, g, d).astype(BF16)
        w1 = mlp_w1[i].astype(BF16)
        w2 = mlp_w2[i].astype(BF16)
        readout_params = (r_k, g2p, gn_w, gn_b, eseg)
        xl = outproj(yf_l.reshape(-1, HEAD_DIM, g), yb_l.reshape(-1, HEAD_DIM, g), pa_l, proj_l[7], proj_l[10],
                     readout_params, (b_l, c_l, d_l), w_out4, xl, ml[2], l)
        xl = mlp(xl, g2, ml[3], ml[4], ml[5], w1, w2, l)

        if need_ctx_out:
            b_c = attention(pb_c[:, :g], pb_c[:, g:2 * g].reshape(nb, lc, g), pb_c[:, 2 * g:].reshape(nb, lc, g), lc)
            c_c = attention(qc_c, kc_c, vc_c, lc)
            d_c = pool_mixer(pd_c.reshape(nb, lc, g), w_bd, p_scale).reshape(nb * lc, g)
            xc = outproj(yf_c.reshape(-1, HEAD_DIM, g), yb_c.reshape(-1, HEAD_DIM, g), pa_c, proj_c[7], proj_c[10],
                         readout_params, (b_c, c_c, d_c), w_out4, xc, mc[2], nb * lc)
            xc = mlp(xc, g2, mc[3], mc[4], mc[5], w1, w2, nb * lc)

    return final_norm(xl, final_g.reshape(1, d)).reshape(nb, l, d)
```

```python
import functools

import jax
import jax.numpy as jnp
import numpy as np
from jax import lax
from jax.experimental import pallas as pl
from jax.experimental.pallas import tpu as pltpu

F32 = jnp.float32
BF16 = jnp.bfloat16

HEAD_DIM = 64
GRID_W = 64
NA_ROWS = 8
NA_COLS = 16
ROPE_THETA = 10000.0
POOL_WINDOWS = (2, 4, 8, 16)
NORM_EPS = 1e-6
RWKV_GN_EPS = 1e-5 * HEAD_DIM
N_MOD = 6
LANES = 128
SCAN_BLOCK = 128
SCAN_SUB = 64
ATTN_CHUNK = 2048
VMEM_LIMIT = 52 * 1024 * 1024


def _params(sem):
    return pltpu.CompilerParams(dimension_semantics=sem, vmem_limit_bytes=VMEM_LIMIT)


def _tile(n, pref):
    t = min(n, pref)
    assert n % t == 0, (n, pref)
    return t


def _dot(a, b):
    return jnp.dot(a, b, preferred_element_type=F32)


def _dot_nt(a, b):
    return lax.dot_general(a, b, (((1,), (1,)), ((), ())), preferred_element_type=F32)


def _seg_sum(x, e):
    hi = x.astype(BF16)
    lo = (x - hi.astype(F32)).astype(BF16)
    return _dot(hi, e) + _dot(lo, e)


def _norm_mod(x, g, shift, scale):
    ms = jnp.mean(x * x, axis=-1, keepdims=True)
    h = x * lax.rsqrt(ms + NORM_EPS) * g
    return h * (1.0 + scale) + shift


def _adaln_kernel(c_ref, w_ref, b_ref, o_ref):
    c = c_ref[...]
    s = c * jax.nn.sigmoid(c)
    o_ref[0] = jnp.dot(s, w_ref[0], preferred_element_type=F32, precision=lax.Precision.HIGHEST) + b_ref[0]


def adaln(cpad, ada_w, ada_b):
    depth, d, n = ada_w.shape
    tn = _tile(n, 1536)
    return pl.pallas_call(
        _adaln_kernel,
        grid=(depth, n // tn),
        in_specs=[pl.BlockSpec((8, d), lambda i, j: (0, 0)),
                  pl.BlockSpec((1, d, tn), lambda i, j: (i, 0, j)),
                  pl.BlockSpec((1, 1, tn), lambda i, j: (i, 0, j))],
        out_specs=pl.BlockSpec((1, 8, tn), lambda i, j: (i, 0, j)),
        out_shape=jax.ShapeDtypeStruct((depth, 8, n), F32),
        compiler_params=_params(("arbitrary", "arbitrary")),
        name="adaln",
    )(cpad, ada_w, ada_b.reshape(depth, 1, n))


def _swap_halves(y):
    n = y.shape[-1]
    lane = lax.broadcasted_iota(jnp.int32, y.shape, 1)
    half = HEAD_DIM // 2
    return jnp.where(lane % HEAD_DIM < half, pltpu.roll(y, n - half, 1), pltpu.roll(y, half, 1))


def _to_tiles(x):
    g = x.shape[1]
    tiles = []
    for s in range(x.shape[0] // SCAN_SUB):
        blk = x[s * SCAN_SUB:(s + 1) * SCAN_SUB, :]
        tiles.append(jnp.concatenate([blk[:, h * HEAD_DIM:(h + 1) * HEAD_DIM].T for h in range(g // HEAD_DIM)], axis=1))
    return tiles


def _from_tiles(tiles):
    g = tiles[0].shape[1]
    return jnp.concatenate(
        [jnp.concatenate([t[:, h * SCAN_SUB:(h + 1) * SCAN_SUB].T for h in range(g // HEAD_DIM)], axis=1) for t in tiles],
        axis=0)


def _split_store(o_ref, val):
    for j in range(o_ref.shape[0]):
        o_ref[j] = val[:, j * LANES:(j + 1) * LANES]


def _split_load(ref):
    return jnp.concatenate([ref[j] for j in range(ref.shape[0])], axis=1)


def _inproj_kernel(x_ref, g_ref, sh_ref, sc_ref, w_ref,
                   w0_ref, a0_ref, w2_ref, a2_ref, kk_w_ref, ka_ref, e_ref,
                   qg_ref, kg_ref, cos_ref, sin_ref, ek_ref,
                   oa_ref, ob_ref, od_ref,
                   r_ref, kk_ref, decf_ref, bf_ref, kmf_ref, decb_ref, bb_ref, kmb_ref, vt_ref,
                   q_ref, kc_ref, vc_ref, *, rope):
    g = od_ref.shape[1]
    gk = kc_ref.shape[1]
    h = _norm_mod(x_ref[...], g_ref[...], sh_ref[0], sc_ref[0]).astype(BF16)
    e = e_ref[...]

    pa = _dot(h, w_ref[:, 0:4 * g])
    pc = _dot(h, w_ref[:, 7 * g:9 * g])

    oa_ref[...] = jnp.concatenate([pa[:, :g], pa[:, 2 * g:]], axis=1)
    _split_store(r_ref, pa[:, :g])
    k = pa[:, g:2 * g]
    lr = pa[:, 3 * g:4 * g]
    for s, tile in enumerate(_to_tiles(pa[:, 2 * g:3 * g])):
        vt_ref[s] = tile
    kx = k * kk_w_ref[...]
    norm = jnp.sqrt(_seg_sum(kx * kx, e))
    kk = kx / jnp.maximum(norm, 1e-12)
    _split_store(kk_ref, kk)
    lr_t = jnp.tanh(lr).astype(BF16)
    lr_b = lr.astype(BF16)
    for d, (dec_ref, b_ref, km_ref) in enumerate(((decf_ref, bf_ref, kmf_ref), (decb_ref, bb_ref, kmb_ref))):
        z = w0_ref[d:d + 1, :] + _dot(lr_t, w2_ref[d])
        softplus_neg = jnp.maximum(-z, 0.0) + jnp.log1p(jnp.exp(-jnp.abs(z)))
        w = -softplus_neg - 0.5
        _split_store(dec_ref, jnp.exp(-jnp.exp(w)))
        a = jax.nn.sigmoid(a0_ref[d:d + 1, :] + _dot(lr_b, a2_ref[d]))
        _split_store(b_ref, kk * a)
        _split_store(km_ref, k * (1.0 + (a - 1.0) * ka_ref[...]))

    def normed(x, gain, seg):
        ms = _seg_sum(x * x, seg) * (1.0 / HEAD_DIM)
        return x * lax.rsqrt(ms + NORM_EPS) * gain

    q = normed(pc[:, :g], qg_ref[...], e)
    kc = normed(pc[:, g:g + gk], kg_ref[...], ek_ref[...])
    if rope:
        cos = cos_ref[...]
        sin = sin_ref[...]
        q = q * cos + _swap_halves(q) * sin
        kc = kc * cos[:, :gk] + _swap_halves(kc) * sin[:, :gk]
    q_ref[...] = (q * HEAD_DIM ** -0.5).astype(BF16)
    kc_ref[...] = kc.astype(BF16)
    vc_ref[...] = pc[:, g + gk:].astype(BF16)

    ob_ref[...] = _dot(h, w_ref[:, 4 * g:7 * g]).astype(ob_ref.dtype)

    od_ref[...] = _dot(h, w_ref[:, 9 * g:10 * g])


def inproj(x, gain, shift, scale, w, rows_per_mod, rwkv_params, gqa_params, seq_len, rope):
    r, d = x.shape
    g = w.shape[1] // 10
    gk = g // 2
    tm = _tile(min(rows_per_mod, seq_len), 512)
    tpb = rows_per_mod // tm
    nseq = seq_len // tm
    modspec = pl.BlockSpec((1, 1, d), lambda i: (i // tpb, 0, 0))
    full2 = lambda a: pl.BlockSpec(a.shape, lambda i: (0,) * a.ndim)
    rows = lambda n: pl.BlockSpec((tm, n), lambda i: (i, 0))
    q_gain, k_gain, cos, sin, ekv = gqa_params
    table = pl.BlockSpec((tm, g), lambda i: (i % nseq, 0))
    split = pl.BlockSpec((g // LANES, tm, LANES), lambda i: (0, i, 0))
    out_specs = ([rows(3 * g), rows(3 * g), rows(g)] + [split] * 8
                 + [pl.BlockSpec((tm // SCAN_SUB, HEAD_DIM, g), lambda i: (i, 0, 0))] + [rows(g), rows(gk), rows(gk)])
    out_shape = ([jax.ShapeDtypeStruct((r, 3 * g), F32), jax.ShapeDtypeStruct((r, 3 * g), BF16),
                  jax.ShapeDtypeStruct((r, g), F32)] + [jax.ShapeDtypeStruct((g // LANES, r, LANES), F32)] * 8
                 + [jax.ShapeDtypeStruct((r // SCAN_SUB, HEAD_DIM, g), F32), jax.ShapeDtypeStruct((r, g), BF16),
                    jax.ShapeDtypeStruct((r, gk), BF16), jax.ShapeDtypeStruct((r, gk), BF16)])
    return pl.pallas_call(
        functools.partial(_inproj_kernel, rope=rope),
        grid=(r // tm,),
        in_specs=[rows(d), pl.BlockSpec((1, d), lambda i: (0, 0)), modspec, modspec, full2(w)]
                 + [full2(p) for p in rwkv_params] + [full2(q_gain), full2(k_gain), table, table, full2(ekv)],
        out_specs=out_specs,
        out_shape=out_shape,
        compiler_params=_params(("arbitrary",)),
        name="inproj",
    )(x, gain, shift, scale, w, *rwkv_params, q_gain, k_gain, cos, sin, ekv)


def _outproj_kernel(yf_ref, yb_ref, p_ref, kmf_ref, kmb_ref, rk_ref, g2_ref, gnw_ref, gnb_ref, e_ref,
                    b_ref, c_ref, d_ref, w_ref, x_ref, gate_ref, o_ref):
    g = b_ref.shape[1]
    e = e_ref[...]
    r = p_ref[:, :g]
    v = p_ref[:, g:2 * g]
    lr = p_ref[:, 2 * g:3 * g]
    y = _from_tiles([yf_ref[s] + yb_ref[s] for s in range(yf_ref.shape[0])])
    mu = _seg_sum(y, e) * (1.0 / HEAD_DIM)
    yc = y - mu
    var = _seg_sum(yc * yc, e) * (1.0 / HEAD_DIM)
    yn = yc * lax.rsqrt(var + RWKV_GN_EPS) * gnw_ref[...] + gnb_ref[...]
    bonus = _seg_sum(r * (_split_load(kmf_ref) + _split_load(kmb_ref)) * rk_ref[...], e) * v
    a = (yn + bonus) * _dot(jax.nn.sigmoid(lr).astype(BF16), g2_ref[...])
    acc = _dot(b_ref[...], w_ref[1])
    acc += _dot(c_ref[...], w_ref[2])
    acc += _dot(d_ref[...], w_ref[3])
    acc += _dot(a.astype(BF16), w_ref[0])
    o_ref[...] = x_ref[...] + gate_ref[0] * acc


def outproj(yf, yb, pa, km_f, km_b, readout_params, mix_bcd, w4, x, gate, rows_per_mod):
    r, d = x.shape
    g = w4.shape[1]
    tm = _tile(rows_per_mod, 512)
    tpb = rows_per_mod // tm
    rows = lambda n: pl.BlockSpec((tm, n), lambda i: (i, 0))
    tiles = pl.BlockSpec((tm // SCAN_SUB, HEAD_DIM, g), lambda i: (i, 0, 0))
    split = pl.BlockSpec((g // LANES, tm, LANES), lambda i: (0, i, 0))
    full2 = lambda a: pl.BlockSpec(a.shape, lambda i: (0,) * a.ndim)
    return pl.pallas_call(
        _outproj_kernel,
        grid=(r // tm,),
        in_specs=[tiles, tiles, rows(3 * g), split, split] + [full2(p) for p in readout_params]
                 + [rows(g), rows(g), rows(g), full2(w4), rows(d), pl.BlockSpec((1, 1, d), lambda i: (i // tpb, 0, 0))],
        out_specs=rows(d),
        out_shape=jax.ShapeDtypeStruct((r, d), F32),
        compiler_params=_params(("arbitrary",)),
        name="outproj",
    )(yf, yb, pa, km_f, km_b, *readout_params, *mix_bcd, w4, x, gate)


def _mlp_kernel(x_ref, g_ref, sh_ref, sc_ref, gate_ref, w1_ref, w2_ref, o_ref, h_scr, acc_scr):
    j = pl.program_id(1)

    @pl.when(j == 0)
    def _():
        h_scr[...] = _norm_mod(x_ref[...], g_ref[...], sh_ref[0], sc_ref[0]).astype(BF16)
        acc_scr[...] = jnp.zeros_like(acc_scr)

    u = jnp.maximum(_dot(h_scr[...], w1_ref[...]), 0.0)
    acc_scr[...] += _dot((u * u).astype(BF16), w2_ref[...])

    @pl.when(j == pl.num_programs(1) - 1)
    def _():
        o_ref[...] = x_ref[...] + gate_ref[0] * acc_scr[...]


def mlp(x, g, shift, scale, gate, w1, w2, rows_per_mod):
    r, d = x.shape
    dff = w1.shape[1]
    tm = _tile(rows_per_mod, 1024)
    tf = _tile(dff, 1024)
    tpb = rows_per_mod // tm
    modspec = pl.BlockSpec((1, 1, d), lambda i, j: (i // tpb, 0, 0))
    return pl.pallas_call(
        _mlp_kernel,
        grid=(r // tm, dff // tf),
        in_specs=[pl.BlockSpec((tm, d), lambda i, j: (i, 0)),
                  pl.BlockSpec((1, d), lambda i, j: (0, 0)),
                  modspec, modspec, modspec,
                  pl.BlockSpec((d, tf), lambda i, j: (0, j)),
                  pl.BlockSpec((tf, d), lambda i, j: (j, 0))],
        out_specs=pl.BlockSpec((tm, d), lambda i, j: (i, 0)),
        out_shape=jax.ShapeDtypeStruct((r, d), F32),
        scratch_shapes=[pltpu.VMEM((tm, d), BF16), pltpu.VMEM((tm, d), F32)],
        compiler_params=_params(("arbitrary", "arbitrary")),
        name="mlp",
    )(x, g, shift, scale, gate, w1, w2)


def _final_norm_kernel(x_ref, g_ref, o_ref):
    x = x_ref[...]
    ms = jnp.mean(x * x, axis=-1, keepdims=True)
    o_ref[...] = x * lax.rsqrt(ms + NORM_EPS) * g_ref[...]


def final_norm(x, g):
    r, d = x.shape
    tm = _tile(r, 1024)
    return pl.pallas_call(
        _final_norm_kernel,
        grid=(r // tm,),
        in_specs=[pl.BlockSpec((tm, d), lambda i: (i, 0)), pl.BlockSpec((1, d), lambda i: (0, 0))],
        out_specs=pl.BlockSpec((tm, d), lambda i: (i, 0)),
        out_shape=jax.ShapeDtypeStruct((r, d), F32),
        compiler_params=_params(("arbitrary",)),
        name="final_norm",
    )(x, g)


def _pad_heads(q, n_heads, kv_width, rep):
    assert kv_width % LANES == 0
    t = q.shape[0]
    lane = lax.broadcasted_iota(jnp.int32, (t, LANES), 1)
    lo_half = lane < HEAD_DIM
    zero = jnp.zeros((t, LANES), q.dtype)
    rows = []
    for h in range(n_heads):
        src = h * HEAD_DIM
        dst = (h // rep) * HEAD_DIM
        piece = q[:, src // LANES * LANES:(src // LANES + 1) * LANES]
        if src % LANES != dst % LANES:
            piece = pltpu.roll(piece.astype(F32), HEAD_DIM, 1).astype(q.dtype)
        piece = jnp.where(lo_half if dst % LANES == 0 else ~lo_half, piece, zero)
        blocks = [piece if j == dst // LANES else zero for j in range(kv_width // LANES)]
        rows.append(jnp.concatenate(blocks, axis=1) if len(blocks) > 1 else piece)
    return jnp.concatenate(rows, axis=0)


def _gather_heads(res, n_heads, rep):
    t = res.shape[0] // n_heads
    lane = lax.broadcasted_iota(jnp.int32, (t, LANES), 1)
    lo_half = lane < HEAD_DIM
    pieces = []
    for h in range(n_heads):
        src = (h // rep) * HEAD_DIM
        dst = h * HEAD_DIM
        piece = res[h * t:(h + 1) * t, src // LANES * LANES:(src // LANES + 1) * LANES]
        if src % LANES != dst % LANES:
            piece = pltpu.roll(piece, HEAD_DIM, 1)
        pieces.append(piece)
    blocks = [jnp.where(lo_half, pieces[2 * j], pieces[2 * j + 1]) for j in range(n_heads // 2)]
    return jnp.concatenate(blocks, axis=1)


def _attn_kernel(q_ref, k_ref, v_ref, o_ref, *, rep):
    n_heads = q_ref.shape[1] // HEAD_DIM
    lk = k_ref.shape[1]
    q = _pad_heads(q_ref[...], n_heads, k_ref.shape[2], rep)
    edges = list(range(0, lk, ATTN_CHUNK)) + [lk]
    m = l = acc = None
    for k0, k1 in zip(edges[:-1], edges[1:]):
        s = _dot_nt(q, k_ref[0, k0:k1, :])
        m_new = jnp.max(s, axis=-1, keepdims=True)
        if m is not None:
            m_new = jnp.maximum(m, m_new)
        p = jnp.exp(s - m_new)
        l_new = jnp.sum(p, axis=-1, keepdims=True)
        acc_new = _dot(p.astype(BF16), v_ref[0, k0:k1, :])
        if m is not None:
            alpha = jnp.exp(m - m_new)
            l_new = alpha * l + l_new
            acc_new = alpha * acc + acc_new
        m, l, acc = m_new, l_new, acc_new
    o_ref[...] = _gather_heads(acc / l, n_heads, rep).astype(o_ref.dtype)


def attention(q, k, v, seq_len):
    r, gq = q.shape
    nb, lk, kw = k.shape
    rep = gq // kw
    tq = _tile(seq_len, 128)
    nq = seq_len // tq
    return pl.pallas_call(
        functools.partial(_attn_kernel, rep=rep),
        grid=(nb, nq),
        in_specs=[pl.BlockSpec((tq, gq), lambda i, j: (i * nq + j, 0)),
                  pl.BlockSpec((1, lk, kw), lambda i, j: (i, 0, 0)),
                  pl.BlockSpec((1, lk, kw), lambda i, j: (i, 0, 0))],
        out_specs=pl.BlockSpec((tq, gq), lambda i, j: (i * nq + j, 0)),
        out_shape=jax.ShapeDtypeStruct((r, gq), BF16),
        compiler_params=_params(("arbitrary", "arbitrary")),
        name="attention",
    )(q, k, v)


def _na_kernel(p_ref, pc_ref, bias_ref, o_ref, *, rows, kh):
    g = o_ref.shape[2]
    n_heads = g // HEAD_DIM
    band = kh * GRID_W
    kc = pc_ref[0, :, g:2 * g]
    vc = pc_ref[0, :, 2 * g:3 * g]

    def one_row(r, carry):
        rs = jnp.clip(r - kh // 2, 0, rows - kh)
        q0 = pl.multiple_of(r * GRID_W, GRID_W)
        k0 = pl.multiple_of(rs * GRID_W, GRID_W)
        q = _pad_heads(p_ref[0, pl.ds(q0, GRID_W), 0:g], n_heads, g, 1)
        s1 = _dot_nt(q, p_ref[0, pl.ds(k0, band), g:2 * g]) + bias_ref[r - rs]
        s2 = _dot_nt(q, kc)
        m = jnp.maximum(jnp.max(s1, axis=-1, keepdims=True), jnp.max(s2, axis=-1, keepdims=True))
        p1 = jnp.exp(s1 - m)
        p2 = jnp.exp(s2 - m)
        l = jnp.sum(p1, axis=-1, keepdims=True) + jnp.sum(p2, axis=-1, keepdims=True)
        res = (_dot(p1.astype(BF16), p_ref[0, pl.ds(k0, band), 2 * g:3 * g]) + _dot(p2.astype(BF16), vc)) / l
        o_ref[0, pl.ds(q0, GRID_W), :] = _gather_heads(res, n_heads, 1).astype(o_ref.dtype)
        return carry

    lax.fori_loop(0, rows, one_row, 0, unroll=8)


def neighbourhood_attention(p, pc, bias):
    nb, l, g3 = p.shape
    g = g3 // 3
    lc = pc.shape[1]
    rows = l // GRID_W
    kh = min(NA_ROWS, rows)
    return pl.pallas_call(
        functools.partial(_na_kernel, rows=rows, kh=kh),
        grid=(nb,),
        in_specs=[pl.BlockSpec((1, l, g3), lambda i: (i, 0, 0)),
                  pl.BlockSpec((1, lc, g3), lambda i: (i, 0, 0)),
                  pl.BlockSpec(bias.shape, lambda i: (0, 0, 0))],
        out_specs=pl.BlockSpec((1, l, g), lambda i: (i, 0, 0)),
        out_shape=jax.ShapeDtypeStruct((nb, l, g), BF16),
        compiler_params=_params(("arbitrary",)),
        name="neighbourhood_attention",
    )(p, pc, bias)


def _na_bias_kernel(rpb_ref, o_ref, *, kh):
    h = pl.program_id(0)
    q = lax.broadcasted_iota(jnp.int32, (GRID_W, GRID_W), 0)
    k = lax.broadcasted_iota(jnp.int32, (GRID_W, GRID_W), 1)
    start = jnp.clip(q - NA_COLS // 2, 0, GRID_W - NA_COLS)
    in_win = (k >= start) & (k < start + NA_COLS)
    off = k - q + NA_COLS - 1
    neg = jnp.full((GRID_W, GRID_W), -jnp.inf, F32)
    blocks = []
    for ro in range(2 * NA_ROWS - 1):
        t = neg
        for c in range(2 * NA_COLS - 1):
            t = jnp.where(off == c, rpb_ref[h, ro, c], t)
        blocks.append(jnp.where(in_win, t, neg))
    for di in range(kh):
        for i in range(kh):
            o_ref[di, :, i * GRID_W:(i + 1) * GRID_W] = blocks[i - di + NA_ROWS - 1]


def na_bias_table(rpb, rows):
    kh = min(NA_ROWS, rows)
    nh = rpb.shape[0]
    return pl.pallas_call(
        functools.partial(_na_bias_kernel, kh=kh),
        grid=(nh,),
        in_specs=[pl.BlockSpec(memory_space=pltpu.SMEM)],
        out_specs=pl.BlockSpec((kh, GRID_W, kh * GRID_W), lambda i: (0, i, 0)),
        out_shape=jax.ShapeDtypeStruct((kh, nh * GRID_W, kh * GRID_W), F32),
        compiler_params=_params(("arbitrary",)),
        name="na_bias",
    )(rpb)


def _pool_kernel(x_ref, inv_ref, w_ref, scale_ref, o_ref):
    x = x_ref[0]
    n, g = x.shape
    pg = g // len(POOL_WINDOWS)
    pad = max(POOL_WINDOWS) // 2
    zeros = jnp.zeros((pad, g), F32)
    xe = jnp.concatenate([zeros, x, zeros], axis=0)
    ne = n + 2 * pad
    group = lax.broadcasted_iota(jnp.int32, xe.shape, 1) // pg
    fwd = xe
    bwd = xe
    cur = 1
    total = jnp.zeros_like(xe)
    for j, w in enumerate(POOL_WINDOWS):
        half = w // 2
        while cur < half:
            fwd = fwd + pltpu.roll(fwd, ne - cur, 0)
            bwd = bwd + pltpu.roll(bwd, cur, 0)
            cur *= 2
        total = jnp.where(group == j, pltpu.roll(bwd, 1, 0) + fwd, total)
    diff = total[pad:pad + n] * inv_ref[...] - x
    o_ref[0] = (_dot(diff.astype(BF16), w_ref[...]) * scale_ref[...]).astype(o_ref.dtype)


def _pool_inverse_counts(n, g):
    t = np.arange(n)
    cols = []
    for w in POOL_WINDOWS:
        lo = np.clip(t - w // 2, 0, n)
        hi = np.clip(t - w // 2 + w, 0, n)
        cols.append(np.repeat((1.0 / (hi - lo))[:, None], g // len(POOL_WINDOWS), axis=1))
    return jnp.asarray(np.concatenate(cols, axis=1), dtype=F32)


def pool_mixer(p, w_bd, scale):
    b, n, g = p.shape
    return pl.pallas_call(
        _pool_kernel,
        grid=(b,),
        in_specs=[pl.BlockSpec((1, n, g), lambda i: (i, 0, 0)),
                  pl.BlockSpec((n, g), lambda i: (0, 0)),
                  pl.BlockSpec((g, g), lambda i: (0, 0)),
                  pl.BlockSpec((1, g), lambda i: (0, 0))],
        out_specs=pl.BlockSpec((1, n, g), lambda i: (i, 0, 0)),
        out_shape=jax.ShapeDtypeStruct((b, n, g), BF16),
        compiler_params=_params(("arbitrary",)),
        name="pool_mixer",
    )(p, _pool_inverse_counts(n, g), w_bd, scale)


def _wkv_scan_kernel(rf_ref, rb_ref, kkf_ref, kkb_ref, decf_ref, bf_ref, kmf_ref, decb_ref, bb_ref, kmb_ref,
                     vtf_ref, vtb_ref, e_ref, s0_ref, yf_ref, yb_ref, send_ref, s_scr):
    npair, nb, t_blk, _ = rf_ref.shape
    g = npair * LANES
    step_id = pl.program_id(0)

    @pl.when(step_id == 0)
    def _():
        s_scr[...] = s0_ref[...]

    yf_ref[...] = jnp.zeros_like(yf_ref)
    yb_ref[...] = jnp.zeros_like(yb_ref)
    lane = lax.broadcasted_iota(jnp.int32, (HEAD_DIM, g), 1) % SCAN_SUB
    lane128 = lax.broadcasted_iota(jnp.int32, (HEAD_DIM, LANES), 1)
    dirs = ((rf_ref, kkf_ref, decf_ref, bf_ref, kmf_ref, vtf_ref, yf_ref),
            (rb_ref, kkb_ref, decb_ref, bb_ref, kmb_ref, vtb_ref, yb_ref))
    ents = HEAD_DIM // 4
    for d in range(2):
        pltpu.matmul_push_rhs(e_ref[...], 0, d)
        pltpu.matmul_acc_lhs(0, jnp.zeros((16, g), BF16), d, load_staged_rhs=0)
        pltpu.matmul_pop(0, (16, g), F32, d)

    def write_y(d, b, y, tt_done, valid):
        y_ref = dirs[d][-1]
        tl = tt_done if d == 0 else t_blk - 1 - tt_done
        tl = jnp.clip(tl, 0, t_blk - 1)
        hit = (lane == tl % SCAN_SUB) & valid
        pltpu.store(y_ref.at[b, tl // SCAN_SUB], y, mask=hit)

    def rows_at(d, tt):
        tl = jnp.clip(tt if d == 0 else t_blk - 1 - tt, 0, t_blk - 1)

        def row(ref, b):
            slabs = [ref[j, b, pl.ds(tl, 8, stride=0), :] for j in range(npair)]
            return jnp.concatenate([jnp.concatenate(slabs, axis=1)] * (HEAD_DIM // 8), axis=0)

        return row

    for d, (r_ref, kk_ref, dec_ref, b_ref, km_ref, vt_ref, y_ref) in enumerate(dirs):
        row0 = rows_at(d, 0)
        for b in range(nb):
            pltpu.matmul_acc_lhs(b * ents, s_scr[d, b].astype(BF16) * row0(kk_ref, b).astype(BF16), d)

    def one_token(tt, carry):
        for b in range(nb):
            for d, (r_ref, kk_ref, dec_ref, b_ref, km_ref, vt_ref, y_ref) in enumerate(dirs):
                tl = tt if d == 0 else t_blk - 1 - tt
                row = rows_at(d, tt)
                nxt = rows_at(d, tt + 1)
                idx = (lane128 // SCAN_SUB) * SCAN_SUB + tl % SCAN_SUB
                sa = pltpu.matmul_pop(b * ents, (HEAD_DIM, g), F32, d)
                write_y(d, b, pltpu.matmul_pop((nb + b) * ents, (HEAD_DIM, g), F32, d), tt - 1, tt > 0)
                vcol = jnp.concatenate([jnp.take_along_axis(vt_ref[b, tl // SCAN_SUB][:, j * LANES:(j + 1) * LANES], idx, axis=1)
                                        for j in range(g // LANES)], axis=1)
                s = s_scr[d, b] * row(dec_ref, b) - sa * row(b_ref, b) + vcol * row(km_ref, b)
                s_scr[d, b] = s
                s_bf = s.astype(BF16)
                pltpu.matmul_acc_lhs((nb + b) * ents, s_bf * row(r_ref, b).astype(BF16), d)
                pltpu.matmul_acc_lhs(b * ents, s_bf * nxt(kk_ref, b).astype(BF16), d)
        return carry

    lax.fori_loop(0, t_blk, one_token, 0, unroll=8)
    for d in range(2):
        for b in range(nb):
            write_y(d, b, pltpu.matmul_pop((nb + b) * ents, (HEAD_DIM, g), F32, d), t_blk - 1, True)
            pltpu.matmul_pop(b * ents, (HEAD_DIM, g), F32, d)

    @pl.when(step_id == pl.num_programs(0) - 1)
    def _():
        send_ref[...] = s_scr[...]


def wkv_scan(r, kk, dec_f, b_f, km_f, dec_b, b_b, km_b, vt, eseg, s0):
    npair, nb, l, _ = kk.shape
    g = npair * LANES
    t_blk = SCAN_BLOCK
    nsub = t_blk // SCAN_SUB
    nblk = l // t_blk
    assert l % t_blk == 0
    seq_f = pl.BlockSpec((npair, nb, t_blk, LANES), lambda i: (0, 0, i, 0))
    seq_b = pl.BlockSpec((npair, nb, t_blk, LANES), lambda i: (0, 0, nblk - 1 - i, 0))
    vt_f = pl.BlockSpec((nb, nsub, HEAD_DIM, g), lambda i: (0, i, 0, 0))
    vt_b = pl.BlockSpec((nb, nsub, HEAD_DIM, g), lambda i: (0, nblk - 1 - i, 0, 0))
    state = pl.BlockSpec(s0.shape, lambda i: (0, 0, 0, 0))
    return pl.pallas_call(
        _wkv_scan_kernel,
        grid=(nblk,),
        in_specs=[seq_f, seq_b, seq_f, seq_b, seq_f, seq_f, seq_f, seq_b, seq_b, seq_b,
                  vt_f, vt_b, pl.BlockSpec(eseg.shape, lambda i: (0, 0)), state],
        out_specs=[vt_f, vt_b, state],
        out_shape=[jax.ShapeDtypeStruct(vt.shape, F32), jax.ShapeDtypeStruct(vt.shape, F32),
                   jax.ShapeDtypeStruct(s0.shape, F32)],
        scratch_shapes=[pltpu.VMEM(s0.shape, F32)],
        compiler_params=_params(("arbitrary",)),
        name="wkv_scan",
    )(r, r, kk, kk, dec_f, b_f, km_f, dec_b, b_b, km_b, vt, vt, eseg, s0)


def _block_ones(n, seg):
    idx = np.arange(n) // seg
    return jnp.asarray(idx[:, None] == idx[None, :], dtype=BF16)


def _padded_rows(w, offset, n):
    return jnp.zeros((n, w.shape[1]), w.dtype).at[offset:offset + w.shape[0]].set(w)


def _rope_tables(n_tokens, n_rep):
    t = jnp.arange(n_tokens)
    row = (t // GRID_W).astype(F32)
    col = (t % GRID_W).astype(F32)
    n_freq = HEAD_DIM // 4
    inv_freq = ROPE_THETA ** (-jnp.arange(n_freq, dtype=F32) / n_freq)
    ang = jnp.concatenate([row[:, None] * inv_freq, col[:, None] * inv_freq], axis=-1)
    cos, sin = jnp.cos(ang), jnp.sin(ang)
    cos_h = jnp.concatenate([cos, cos], axis=-1)
    sin_h = jnp.concatenate([-sin, sin], axis=-1)
    return jnp.tile(cos_h, (1, n_rep)), jnp.tile(sin_h, (1, n_rep))


def kernel(x, c, ctx, c_ctx, ada_w, ada_b, norm1_g, norm2_g, w_in, w_out, rwkv_w0, rwkv_w2, rwkv_a0, rwkv_a2,
           rwkv_k_k, rwkv_k_a, rwkv_r_k, rwkv_g2, rwkv_gn_w, rwkv_gn_b, na_rpb, gqa_q_gain, gqa_k_gain, pool_w,
           pool_scale, mlp_w1, mlp_w2, final_g):
    nb, l, d = x.shape
    lc = ctx.shape[1]
    depth = ada_w.shape[0]
    g = d // 4
    nh = g // HEAD_DIM
    nkv = nh // 2
    kvw = nkv * HEAD_DIM
    dr = rwkv_w2.shape[2]
    ir = rwkv_a2.shape[2]
    gr = rwkv_g2.shape[1]
    assert 2 * dr + 2 * ir + gr <= g and nb + 1 <= 8
    rows = l // GRID_W

    cpad = jnp.zeros((8, d), F32).at[:nb].set(c).at[nb].set(c_ctx)
    mods = adaln(cpad, ada_w, ada_b)

    eseg = _block_ones(g, HEAD_DIM)
    ekv = _block_ones(kvw, HEAD_DIM)
    cos_q, sin_q = _rope_tables(l, nh)
    ones_c, zeros_c = jnp.ones((lc, g), F32), jnp.zeros((lc, g), F32)

    splits = np.cumsum([0, g, g, g, dr, dr, ir, ir, gr, g, g, g, g, kvw, kvw, g])
    lowrank_w = splits[8] - splits[3]

    xl = x.reshape(nb * l, d)
    xc = ctx.reshape(nb * lc, d)
    zero_state = jnp.zeros((2, nb, HEAD_DIM, g), F32)

    for i in range(depth):
        need_ctx_out = i < depth - 1
        mod_l = mods[i, :nb].reshape(nb, N_MOD, 1, d)
        mod_c = mods[i, nb].reshape(N_MOD, 1, 1, d)
        ml = [mod_l[:, k] for k in range(N_MOD)]
        mc = [mod_c[k] for k in range(N_MOD)]

        wi = w_in[i]
        w_inp = jnp.concatenate([wi[:, :splits[8]], jnp.zeros((d, g - lowrank_w), F32),
                                 wi[:, splits[8]:splits[9]] * HEAD_DIM ** -0.5, wi[:, splits[9]:]], axis=1).astype(BF16)
        g1 = norm1_g[i].reshape(1, d)
        g2 = norm2_g[i].reshape(1, d)

        w2p = jnp.stack([_padded_rows(rwkv_w2[i, 0], 0, g), _padded_rows(rwkv_w2[i, 1], dr, g)]).astype(BF16)
        a2p = jnp.stack([_padded_rows(rwkv_a2[i, 0], 2 * dr, g),
                         _padded_rows(rwkv_a2[i, 1], 2 * dr + ir, g)]).astype(BF16)
        g2p = _padded_rows(rwkv_g2[i], 2 * dr + 2 * ir, g).astype(BF16)
        r_k = rwkv_r_k[i].reshape(1, g)
        gn_w = rwkv_gn_w[i].reshape(1, g)
        gn_b = rwkv_gn_b[i].reshape(1, g)
        rwkv_params = (rwkv_w0[i], rwkv_a0[i], w2p, a2p, rwkv_k_k[i].reshape(1, g), rwkv_k_a[i].reshape(1, g), eseg)
        q_gain = jnp.tile(gqa_q_gain[i], nh).reshape(1, g)
        k_gain = jnp.tile(gqa_k_gain[i], nkv).reshape(1, kvw)

        proj_l = inproj(xl, g1, ml[0], ml[1], w_inp, l, rwkv_params, (q_gain, k_gain, cos_q, sin_q, ekv), l, True)
        proj_c = inproj(xc, g1, mc[0], mc[1], w_inp, nb * lc, rwkv_params, (q_gain, k_gain, ones_c, zeros_c, ekv), lc, False)
        pa_l, pb_l, pd_l = proj_l[:3]
        pa_c, pb_c, pd_c = proj_c[:3]

        def scan_inputs(proj, seq):
            rows4 = [t.reshape(g // LANES, nb, seq, LANES) for t in proj[3:11]]
            return rows4, proj[11].reshape(nb, seq // SCAN_SUB, HEAD_DIM, g)

        rows4_c, vt_c = scan_inputs(proj_c, lc)
        yf_c, yb_c, state_c = wkv_scan(*rows4_c, vt_c, eseg, zero_state)
        rows4_l, vt_l = scan_inputs(proj_l, l)
        yf_l, yb_l, _ = wkv_scan(*rows4_l, vt_l, eseg, state_c)

        bias = na_bias_table(na_rpb[i], rows)
        b_l = neighbourhood_attention(pb_l.reshape(nb, l, 3 * g), pb_c.reshape(nb, lc, 3 * g), bias).reshape(nb * l, g)

        qc_l, kc_l, vc_l = proj_l[12:15]
        qc_c, kc_c, vc_c = proj_c[12:15]
        kc_c, vc_c = kc_c.reshape(nb, lc, kvw), vc_c.reshape(nb, lc, kvw)
        k_all = jnp.concatenate([kc_c, kc_l.reshape(nb, l, kvw)], axis=1)
        v_all = jnp.concatenate([vc_c, vc_l.reshape(nb, l, kvw)], axis=1)
        c_l = attention(qc_l, k_all, v_all, l)

        w_bd = jax.scipy.linalg.block_diag(*[pool_w[i, k] for k in range(len(POOL_WINDOWS))]).astype(BF16)
        p_scale = pool_scale[i].reshape(1, g)
        d_l = pool_mixer(pd_l.reshape(nb, l, g), w_bd, p_scale).reshape(nb * l, g)

        w_out4 = w_out[i].reshape(4, g, d).astype(BF16)
        w1 = mlp_w1[i].astype(BF16)
        w2 = mlp_w2[i].astype(BF16)
        readout_params = (r_k, g2p, gn_w, gn_b, eseg)
        xl = outproj(yf_l.reshape(-1, HEAD_DIM, g), yb_l.reshape(-1, HEAD_DIM, g), pa_l, proj_l[7], proj_l[10],
                     readout_params, (b_l, c_l, d_l), w_out4, xl, ml[2], l)
        xl = mlp(xl, g2, ml[3], ml[4], ml[5], w1, w2, l)

        if need_ctx_out:
            b_c = attention(pb_c[:, :g], pb_c[:, g:2 * g].reshape(nb, lc, g), pb_c[:, 2 * g:].reshape(nb, lc, g), lc)
            c_c = attention(qc_c, kc_c, vc_c, lc)
            d_c = pool_mixer(pd_c.reshape(nb, lc, g), w_bd, p_scale).reshape(nb * lc, g)
            xc = outproj(yf_c.reshape(-1, HEAD_DIM, g), yb_c.reshape(-1, HEAD_DIM, g), pa_c, proj_c[7], proj_c[10],
                         readout_params, (b_c, c_c, d_c), w_out4, xc, mc[2], nb * lc)
            xc = mlp(xc, g2, mc[3], mc[4], mc[5], w1, w2, nb * lc)

    return final_norm(xl, final_g.reshape(1, d)).reshape(nb, l, d)
```

```python
import functools

import jax
import jax.numpy as jnp
import numpy as np
from jax import lax
from jax.experimental import pallas as pl
from jax.experimental.pallas import tpu as pltpu

F32 = jnp.float32
BF16 = jnp.bfloat16

HEAD_DIM = 64
GRID_W = 64
NA_ROWS = 8
NA_COLS = 16
ROPE_THETA = 10000.0
POOL_WINDOWS = (2, 4, 8, 16)
NORM_EPS = 1e-6
RWKV_GN_EPS = 1e-5 * HEAD_DIM
N_MOD = 6
LANES = 128
SCAN_BLOCK = 128
SCAN_SUB = 64
ATTN_CHUNK = 2048
MLP_CHUNK = 1024
VMEM_LIMIT = 52 * 1024 * 1024


def _params(sem):
    return pltpu.CompilerParams(dimension_semantics=sem, vmem_limit_bytes=VMEM_LIMIT)


def _tile(n, pref):
    t = min(n, pref)
    assert n % t == 0, (n, pref)
    return t


def _dot(a, b):
    return jnp.dot(a, b, preferred_element_type=F32)


def _dot_nt(a, b):
    return lax.dot_general(a, b, (((1,), (1,)), ((), ())), preferred_element_type=F32)


def _seg_sum(x, e):
    hi = x.astype(BF16)
    lo = (x - hi.astype(F32)).astype(BF16)
    return _dot(hi, e) + _dot(lo, e)


def _norm_mod(x, g, shift, scale):
    ms = jnp.mean(x * x, axis=-1, keepdims=True)
    h = x * lax.rsqrt(ms + NORM_EPS) * g
    return h * (1.0 + scale) + shift


def _adaln_kernel(c_ref, w_ref, b_ref, o_ref):
    c = c_ref[...]
    s = c * jax.nn.sigmoid(c)
    o_ref[0] = jnp.dot(s, w_ref[0], preferred_element_type=F32, precision=lax.Precision.HIGHEST) + b_ref[0]


def adaln(cpad, ada_w, ada_b):
    depth, d, n = ada_w.shape
    tn = _tile(n, 1536)
    return pl.pallas_call(
        _adaln_kernel,
        grid=(depth, n // tn),
        in_specs=[pl.BlockSpec((8, d), lambda i, j: (0, 0)),
                  pl.BlockSpec((1, d, tn), lambda i, j: (i, 0, j)),
                  pl.BlockSpec((1, 1, tn), lambda i, j: (i, 0, j))],
        out_specs=pl.BlockSpec((1, 8, tn), lambda i, j: (i, 0, j)),
        out_shape=jax.ShapeDtypeStruct((depth, 8, n), F32),
        compiler_params=_params(("arbitrary", "arbitrary")),
        name="adaln",
    )(cpad, ada_w, ada_b.reshape(depth, 1, n))


def _swap_halves(y):
    n = y.shape[-1]
    lane = lax.broadcasted_iota(jnp.int32, y.shape, 1)
    half = HEAD_DIM // 2
    return jnp.where(lane % HEAD_DIM < half, pltpu.roll(y, n - half, 1), pltpu.roll(y, half, 1))


def _to_tiles(x):
    g = x.shape[1]
    tiles = []
    for s in range(x.shape[0] // SCAN_SUB):
        blk = x[s * SCAN_SUB:(s + 1) * SCAN_SUB, :]
        tiles.append(jnp.concatenate([blk[:, h * HEAD_DIM:(h + 1) * HEAD_DIM].T for h in range(g // HEAD_DIM)], axis=1))
    return tiles


def _from_tiles(tiles):
    g = tiles[0].shape[1]
    return jnp.concatenate(
        [jnp.concatenate([t[:, h * SCAN_SUB:(h + 1) * SCAN_SUB].T for h in range(g // HEAD_DIM)], axis=1) for t in tiles],
        axis=0)


def _split_store(o_ref, val):
    for j in range(o_ref.shape[0]):
        o_ref[j] = val[:, j * LANES:(j + 1) * LANES]


def _split_load(ref):
    return jnp.concatenate([ref[j] for j in range(ref.shape[0])], axis=1)


def _inproj_kernel(x_ref, g_ref, sh_ref, sc_ref, w_ref,
                   w0_ref, a0_ref, w2_ref, a2_ref, kk_w_ref, ka_ref, e_ref,
                   qg_ref, kg_ref, cos_ref, sin_ref, ek_ref,
                   oa_ref, ob_ref, od_ref,
                   r_ref, kk_ref, decf_ref, bf_ref, kmf_ref, decb_ref, bb_ref, kmb_ref, vt_ref,
                   q_ref, kc_ref, vc_ref, *, rope):
    g = od_ref.shape[1]
    gk = kc_ref.shape[1]
    h = _norm_mod(x_ref[...], g_ref[...], sh_ref[0], sc_ref[0]).astype(BF16)
    e = e_ref[...]

    pa = _dot(h, w_ref[:, 0:4 * g])
    pc = _dot(h, w_ref[:, 7 * g:9 * g])

    oa_ref[...] = jnp.concatenate([pa[:, :g], pa[:, 2 * g:]], axis=1)
    _split_store(r_ref, pa[:, :g])
    k = pa[:, g:2 * g]
    lr = pa[:, 3 * g:4 * g]
    for s, tile in enumerate(_to_tiles(pa[:, 2 * g:3 * g])):
        vt_ref[s] = tile
    kx = k * kk_w_ref[...]
    norm = jnp.sqrt(_seg_sum(kx * kx, e))
    kk = kx / jnp.maximum(norm, 1e-12)
    _split_store(kk_ref, kk)
    lr_t = jnp.tanh(lr).astype(BF16)
    lr_b = lr.astype(BF16)
    for d, (dec_ref, b_ref, km_ref) in enumerate(((decf_ref, bf_ref, kmf_ref), (decb_ref, bb_ref, kmb_ref))):
        z = w0_ref[d:d + 1, :] + _dot(lr_t, w2_ref[d])
        softplus_neg = jnp.maximum(-z, 0.0) + jnp.log1p(jnp.exp(-jnp.abs(z)))
        w = -softplus_neg - 0.5
        _split_store(dec_ref, jnp.exp(-jnp.exp(w)))
        a = jax.nn.sigmoid(a0_ref[d:d + 1, :] + _dot(lr_b, a2_ref[d]))
        _split_store(b_ref, kk * a)
        _split_store(km_ref, k * (1.0 + (a - 1.0) * ka_ref[...]))

    def normed(x, gain, seg):
        ms = _seg_sum(x * x, seg) * (1.0 / HEAD_DIM)
        return x * lax.rsqrt(ms + NORM_EPS) * gain

    q = normed(pc[:, :g], qg_ref[...], e)
    kc = normed(pc[:, g:g + gk], kg_ref[...], ek_ref[...])
    if rope:
        cos = cos_ref[...]
        sin = sin_ref[...]
        q = q * cos + _swap_halves(q) * sin
        kc = kc * cos[:, :gk] + _swap_halves(kc) * sin[:, :gk]
    q_ref[...] = (q * HEAD_DIM ** -0.5).astype(BF16)
    kc_ref[...] = kc.astype(BF16)
    vc_ref[...] = pc[:, g + gk:].astype(BF16)

    ob_ref[...] = _dot(h, w_ref[:, 4 * g:7 * g]).astype(ob_ref.dtype)

    od_ref[...] = _dot(h, w_ref[:, 9 * g:10 * g])


def inproj(x, gain, shift, scale, w, rows_per_mod, rwkv_params, gqa_params, seq_len, rope):
    r, d = x.shape
    g = w.shape[1] // 10
    gk = g // 2
    tm = _tile(min(rows_per_mod, seq_len), 512)
    tpb = rows_per_mod // tm
    nseq = seq_len // tm
    modspec = pl.BlockSpec((1, 1, d), lambda i: (i // tpb, 0, 0))
    full2 = lambda a: pl.BlockSpec(a.shape, lambda i: (0,) * a.ndim)
    rows = lambda n: pl.BlockSpec((tm, n), lambda i: (i, 0))
    q_gain, k_gain, cos, sin, ekv = gqa_params
    table = pl.BlockSpec((tm, g), lambda i: (i % nseq, 0))
    split = pl.BlockSpec((g // LANES, tm, LANES), lambda i: (0, i, 0))
    out_specs = ([rows(3 * g), rows(3 * g), rows(g)] + [split] * 8
                 + [pl.BlockSpec((tm // SCAN_SUB, HEAD_DIM, g), lambda i: (i, 0, 0))] + [rows(g), rows(gk), rows(gk)])
    out_shape = ([jax.ShapeDtypeStruct((r, 3 * g), F32), jax.ShapeDtypeStruct((r, 3 * g), BF16),
                  jax.ShapeDtypeStruct((r, g), F32)] + [jax.ShapeDtypeStruct((g // LANES, r, LANES), F32)] * 8
                 + [jax.ShapeDtypeStruct((r // SCAN_SUB, HEAD_DIM, g), F32), jax.ShapeDtypeStruct((r, g), BF16),
                    jax.ShapeDtypeStruct((r, gk), BF16), jax.ShapeDtypeStruct((r, gk), BF16)])
    return pl.pallas_call(
        functools.partial(_inproj_kernel, rope=rope),
        grid=(r // tm,),
        in_specs=[rows(d), pl.BlockSpec((1, d), lambda i: (0, 0)), modspec, modspec, full2(w)]
                 + [full2(p) for p in rwkv_params] + [full2(q_gain), full2(k_gain), table, table, full2(ekv)],
        out_specs=out_specs,
        out_shape=out_shape,
        compiler_params=_params(("arbitrary",)),
        name="inproj",
    )(x, gain, shift, scale, w, *rwkv_params, q_gain, k_gain, cos, sin, ekv)


def _outproj_kernel(yf_ref, yb_ref, p_ref, kmf_ref, kmb_ref, rk_ref, g2_ref, gnw_ref, gnb_ref, e_ref,
                    b_ref, c_ref, d_ref, w_ref, x_ref, gate_ref, o_ref):
    g = b_ref.shape[1]
    e = e_ref[...]
    r = p_ref[:, :g]
    v = p_ref[:, g:2 * g]
    lr = p_ref[:, 2 * g:3 * g]
    y = _from_tiles([yf_ref[s] + yb_ref[s] for s in range(yf_ref.shape[0])])
    mu = _seg_sum(y, e) * (1.0 / HEAD_DIM)
    yc = y - mu
    var = _seg_sum(yc * yc, e) * (1.0 / HEAD_DIM)
    yn = yc * lax.rsqrt(var + RWKV_GN_EPS) * gnw_ref[...] + gnb_ref[...]
    bonus = _seg_sum(r * (_split_load(kmf_ref) + _split_load(kmb_ref)) * rk_ref[...], e) * v
    a = (yn + bonus) * _dot(jax.nn.sigmoid(lr).astype(BF16), g2_ref[...])
    acc = _dot(b_ref[...], w_ref[1])
    acc += _dot(c_ref[...], w_ref[2])
    acc += _dot(d_ref[...], w_ref[3])
    acc += _dot(a.astype(BF16), w_ref[0])
    o_ref[...] = x_ref[...] + gate_ref[0] * acc


def outproj(yf, yb, pa, km_f, km_b, readout_params, mix_bcd, w4, x, gate, rows_per_mod):
    r, d = x.shape
    g = w4.shape[1]
    tm = _tile(rows_per_mod, 512)
    tpb = rows_per_mod // tm
    rows = lambda n: pl.BlockSpec((tm, n), lambda i: (i, 0))
    tiles = pl.BlockSpec((tm // SCAN_SUB, HEAD_DIM, g), lambda i: (i, 0, 0))
    split = pl.BlockSpec((g // LANES, tm, LANES), lambda i: (0, i, 0))
    full2 = lambda a: pl.BlockSpec(a.shape, lambda i: (0,) * a.ndim)
    return pl.pallas_call(
        _outproj_kernel,
        grid=(r // tm,),
        in_specs=[tiles, tiles, rows(3 * g), split, split] + [full2(p) for p in readout_params]
                 + [rows(g), rows(g), rows(g), full2(w4), rows(d), pl.BlockSpec((1, 1, d), lambda i: (i // tpb, 0, 0))],
        out_specs=rows(d),
        out_shape=jax.ShapeDtypeStruct((r, d), F32),
        compiler_params=_params(("arbitrary",)),
        name="outproj",
    )(yf, yb, pa, km_f, km_b, *readout_params, *mix_bcd, w4, x, gate)


def _mlp_kernel(x_ref, g_ref, sh_ref, sc_ref, gate_ref, w1_ref, w2_ref, o_ref):
    x = x_ref[...]
    h = _norm_mod(x, g_ref[...], sh_ref[0], sc_ref[0]).astype(BF16)
    dff = w1_ref.shape[1]
    acc = None
    for c0 in range(0, dff, MLP_CHUNK):
        u = jnp.maximum(_dot(h, w1_ref[:, c0:c0 + MLP_CHUNK]), 0.0)
        part = _dot((u * u).astype(BF16), w2_ref[c0:c0 + MLP_CHUNK, :])
        acc = part if acc is None else acc + part
    o_ref[...] = x + gate_ref[0] * acc


def mlp(x, g, shift, scale, gate, w1, w2, rows_per_mod):
    r, d = x.shape
    assert w1.shape[1] % MLP_CHUNK == 0
    tm = _tile(rows_per_mod, 512)
    tpb = rows_per_mod // tm
    modspec = pl.BlockSpec((1, 1, d), lambda i: (i // tpb, 0, 0))
    resident = lambda a: pl.BlockSpec(a.shape, lambda i: (0, 0), pipeline_mode=pl.Buffered(1))
    return pl.pallas_call(
        _mlp_kernel,
        grid=(r // tm,),
        in_specs=[pl.BlockSpec((tm, d), lambda i: (i, 0)),
                  pl.BlockSpec((1, d), lambda i: (0, 0)),
                  modspec, modspec, modspec, resident(w1), resident(w2)],
        out_specs=pl.BlockSpec((tm, d), lambda i: (i, 0)),
        out_shape=jax.ShapeDtypeStruct((r, d), F32),
        compiler_params=_params(("arbitrary",)),
        name="mlp",
    )(x, g, shift, scale, gate, w1, w2)


def _final_norm_kernel(x_ref, g_ref, o_ref):
    x = x_ref[...]
    ms = jnp.mean(x * x, axis=-1, keepdims=True)
    o_ref[...] = x * lax.rsqrt(ms + NORM_EPS) * g_ref[...]


def final_norm(x, g):
    r, d = x.shape
    tm = _tile(r, 1024)
    return pl.pallas_call(
        _final_norm_kernel,
        grid=(r // tm,),
        in_specs=[pl.BlockSpec((tm, d), lambda i: (i, 0)), pl.BlockSpec((1, d), lambda i: (0, 0))],
        out_specs=pl.BlockSpec((tm, d), lambda i: (i, 0)),
        out_shape=jax.ShapeDtypeStruct((r, d), F32),
        compiler_params=_params(("arbitrary",)),
        name="final_norm",
    )(x, g)


def _pad_heads(q, n_heads, kv_width, rep):
    assert kv_width % LANES == 0
    t = q.shape[0]
    lane = lax.broadcasted_iota(jnp.int32, (t, LANES), 1)
    lo_half = lane < HEAD_DIM
    zero = jnp.zeros((t, LANES), q.dtype)
    rows = []
    for h in range(n_heads):
        src = h * HEAD_DIM
        dst = (h // rep) * HEAD_DIM
        piece = q[:, src // LANES * LANES:(src // LANES + 1) * LANES]
        if src % LANES != dst % LANES:
            piece = pltpu.roll(piece.astype(F32), HEAD_DIM, 1).astype(q.dtype)
        piece = jnp.where(lo_half if dst % LANES == 0 else ~lo_half, piece, zero)
        blocks = [piece if j == dst // LANES else zero for j in range(kv_width // LANES)]
        rows.append(jnp.concatenate(blocks, axis=1) if len(blocks) > 1 else piece)
    return jnp.concatenate(rows, axis=0)


def _gather_heads(res, n_heads, rep):
    t = res.shape[0] // n_heads
    lane = lax.broadcasted_iota(jnp.int32, (t, LANES), 1)
    lo_half = lane < HEAD_DIM
    pieces = []
    for h in range(n_heads):
        src = (h // rep) * HEAD_DIM
        dst = h * HEAD_DIM
        piece = res[h * t:(h + 1) * t, src // LANES * LANES:(src // LANES + 1) * LANES]
        if src % LANES != dst % LANES:
            piece = pltpu.roll(piece, HEAD_DIM, 1)
        pieces.append(piece)
    blocks = [jnp.where(lo_half, pieces[2 * j], pieces[2 * j + 1]) for j in range(n_heads // 2)]
    return jnp.concatenate(blocks, axis=1)


def _attn_kernel(q_ref, k_ref, v_ref, o_ref, *, rep):
    n_heads = q_ref.shape[1] // HEAD_DIM
    lk = k_ref.shape[1]
    q = _pad_heads(q_ref[...], n_heads, k_ref.shape[2], rep)
    edges = list(range(0, lk, ATTN_CHUNK)) + [lk]
    m = l = acc = None
    for k0, k1 in zip(edges[:-1], edges[1:]):
        s = _dot_nt(q, k_ref[0, k0:k1, :])
        m_new = jnp.max(s, axis=-1, keepdims=True)
        if m is not None:
            m_new = jnp.maximum(m, m_new)
        p = jnp.exp(s - m_new)
        l_new = jnp.sum(p, axis=-1, keepdims=True)
        acc_new = _dot(p.astype(BF16), v_ref[0, k0:k1, :])
        if m is not None:
            alpha = jnp.exp(m - m_new)
            l_new = alpha * l + l_new
            acc_new = alpha * acc + acc_new
        m, l, acc = m_new, l_new, acc_new
    o_ref[...] = _gather_heads(acc / l, n_heads, rep).astype(o_ref.dtype)


def attention(q, k, v, seq_len):
    r, gq = q.shape
    nb, lk, kw = k.shape
    rep = gq // kw
    tq = _tile(seq_len, 128)
    nq = seq_len // tq
    return pl.pallas_call(
        functools.partial(_attn_kernel, rep=rep),
        grid=(nb, nq),
        in_specs=[pl.BlockSpec((tq, gq), lambda i, j: (i * nq + j, 0)),
                  pl.BlockSpec((1, lk, kw), lambda i, j: (i, 0, 0)),
                  pl.BlockSpec((1, lk, kw), lambda i, j: (i, 0, 0))],
        out_specs=pl.BlockSpec((tq, gq), lambda i, j: (i * nq + j, 0)),
        out_shape=jax.ShapeDtypeStruct((r, gq), BF16),
        compiler_params=_params(("arbitrary", "arbitrary")),
        name="attention",
    )(q, k, v)


def _na_kernel(p_ref, pc_ref, bias_ref, o_ref, *, rows, kh):
    g = o_ref.shape[2]
    n_heads = g // HEAD_DIM
    band = kh * GRID_W
    kc = pc_ref[0, :, g:2 * g]
    vc = pc_ref[0, :, 2 * g:3 * g]

    def one_row(r, carry):
        rs = jnp.clip(r - kh // 2, 0, rows - kh)
        q0 = pl.multiple_of(r * GRID_W, GRID_W)
        k0 = pl.multiple_of(rs * GRID_W, GRID_W)
        q = _pad_heads(p_ref[0, pl.ds(q0, GRID_W), 0:g], n_heads, g, 1)
        s1 = _dot_nt(q, p_ref[0, pl.ds(k0, band), g:2 * g]) + bias_ref[r - rs]
        s2 = _dot_nt(q, kc)
        m = jnp.maximum(jnp.max(s1, axis=-1, keepdims=True), jnp.max(s2, axis=-1, keepdims=True))
        p1 = jnp.exp(s1 - m)
        p2 = jnp.exp(s2 - m)
        l = jnp.sum(p1, axis=-1, keepdims=True) + jnp.sum(p2, axis=-1, keepdims=True)
        res = (_dot(p1.astype(BF16), p_ref[0, pl.ds(k0, band), 2 * g:3 * g]) + _dot(p2.astype(BF16), vc)) / l
        o_ref[0, pl.ds(q0, GRID_W), :] = _gather_heads(res, n_heads, 1).astype(o_ref.dtype)
        return carry

    lax.fori_loop(0, rows, one_row, 0, unroll=8)


def neighbourhood_attention(p, pc, bias):
    nb, l, g3 = p.shape
    g = g3 // 3
    lc = pc.shape[1]
    rows = l // GRID_W
    kh = min(NA_ROWS, rows)
    return pl.pallas_call(
        functools.partial(_na_kernel, rows=rows, kh=kh),
        grid=(nb,),
        in_specs=[pl.BlockSpec((1, l, g3), lambda i: (i, 0, 0)),
                  pl.BlockSpec((1, lc, g3), lambda i: (i, 0, 0)),
                  pl.BlockSpec(bias.shape, lambda i: (0, 0, 0))],
        out_specs=pl.BlockSpec((1, l, g), lambda i: (i, 0, 0)),
        out_shape=jax.ShapeDtypeStruct((nb, l, g), BF16),
        compiler_params=_params(("arbitrary",)),
        name="neighbourhood_attention",
    )(p, pc, bias)


def _na_bias_kernel(rpb_ref, o_ref, *, kh):
    h = pl.program_id(0)
    q = lax.broadcasted_iota(jnp.int32, (GRID_W, GRID_W), 0)
    k = lax.broadcasted_iota(jnp.int32, (GRID_W, GRID_W), 1)
    start = jnp.clip(q - NA_COLS // 2, 0, GRID_W - NA_COLS)
    in_win = (k >= start) & (k < start + NA_COLS)
    off = k - q + NA_COLS - 1
    neg = jnp.full((GRID_W, GRID_W), -jnp.inf, F32)
    blocks = []
    for ro in range(2 * NA_ROWS - 1):
        t = neg
        for c in range(2 * NA_COLS - 1):
            t = jnp.where(off == c, rpb_ref[h, ro, c], t)
        blocks.append(jnp.where(in_win, t, neg))
    for di in range(kh):
        for i in range(kh):
            o_ref[di, :, i * GRID_W:(i + 1) * GRID_W] = blocks[i - di + NA_ROWS - 1]


def na_bias_table(rpb, rows):
    kh = min(NA_ROWS, rows)
    nh = rpb.shape[0]
    return pl.pallas_call(
        functools.partial(_na_bias_kernel, kh=kh),
        grid=(nh,),
        in_specs=[pl.BlockSpec(memory_space=pltpu.SMEM)],
        out_specs=pl.BlockSpec((kh, GRID_W, kh * GRID_W), lambda i: (0, i, 0)),
        out_shape=jax.ShapeDtypeStruct((kh, nh * GRID_W, kh * GRID_W), F32),
        compiler_params=_params(("arbitrary",)),
        name="na_bias",
    )(rpb)


def _pool_kernel(x_ref, inv_ref, w_ref, scale_ref, o_ref):
    x = x_ref[0]
    n, g = x.shape
    pg = g // len(POOL_WINDOWS)
    pad = max(POOL_WINDOWS) // 2
    zeros = jnp.zeros((pad, g), F32)
    xe = jnp.concatenate([zeros, x, zeros], axis=0)
    ne = n + 2 * pad
    group = lax.broadcasted_iota(jnp.int32, xe.shape, 1) // pg
    fwd = xe
    bwd = xe
    cur = 1
    total = jnp.zeros_like(xe)
    for j, w in enumerate(POOL_WINDOWS):
        half = w // 2
        while cur < half:
            fwd = fwd + pltpu.roll(fwd, ne - cur, 0)
            bwd = bwd + pltpu.roll(bwd, cur, 0)
            cur *= 2
        total = jnp.where(group == j, pltpu.roll(bwd, 1, 0) + fwd, total)
    diff = total[pad:pad + n] * inv_ref[...] - x
    o_ref[0] = (_dot(diff.astype(BF16), w_ref[...]) * scale_ref[...]).astype(o_ref.dtype)


def _pool_inverse_counts(n, g):
    t = np.arange(n)
    cols = []
    for w in POOL_WINDOWS:
        lo = np.clip(t - w // 2, 0, n)
        hi = np.clip(t - w // 2 + w, 0, n)
        cols.append(np.repeat((1.0 / (hi - lo))[:, None], g // len(POOL_WINDOWS), axis=1))
    return jnp.asarray(np.concatenate(cols, axis=1), dtype=F32)


def pool_mixer(p, w_bd, scale):
    b, n, g = p.shape
    return pl.pallas_call(
        _pool_kernel,
        grid=(b,),
        in_specs=[pl.BlockSpec((1, n, g), lambda i: (i, 0, 0)),
                  pl.BlockSpec((n, g), lambda i: (0, 0)),
                  pl.BlockSpec((g, g), lambda i: (0, 0)),
                  pl.BlockSpec((1, g), lambda i: (0, 0))],
        out_specs=pl.BlockSpec((1, n, g), lambda i: (i, 0, 0)),
        out_shape=jax.ShapeDtypeStruct((b, n, g), BF16),
        compiler_params=_params(("arbitrary",)),
        name="pool_mixer",
    )(p, _pool_inverse_counts(n, g), w_bd, scale)


def _wkv_scan_kernel(rf_ref, rb_ref, kkf_ref, kkb_ref, decf_ref, bf_ref, kmf_ref, decb_ref, bb_ref, kmb_ref,
                     vtf_ref, vtb_ref, e_ref, s0_ref, yf_ref, yb_ref, send_ref, s_scr):
    npair, nb, t_blk, _ = rf_ref.shape
    g = npair * LANES
    step_id = pl.program_id(0)

    @pl.when(step_id == 0)
    def _():
        s_scr[...] = s0_ref[...]

    yf_ref[...] = jnp.zeros_like(yf_ref)
    yb_ref[...] = jnp.zeros_like(yb_ref)
    lane = lax.broadcasted_iota(jnp.int32, (HEAD_DIM, g), 1) % SCAN_SUB
    lane128 = lax.broadcasted_iota(jnp.int32, (HEAD_DIM, LANES), 1)
    dirs = ((rf_ref, kkf_ref, decf_ref, bf_ref, kmf_ref, vtf_ref, yf_ref),
            (rb_ref, kkb_ref, decb_ref, bb_ref, kmb_ref, vtb_ref, yb_ref))
    ents = HEAD_DIM // 4
    for d in range(2):
        pltpu.matmul_push_rhs(e_ref[...], 0, d)
        pltpu.matmul_acc_lhs(0, jnp.zeros((16, g), BF16), d, load_staged_rhs=0)
        pltpu.matmul_pop(0, (16, g), F32, d)

    def write_y(d, b, y, tt_done, valid):
        y_ref = dirs[d][-1]
        tl = tt_done if d == 0 else t_blk - 1 - tt_done
        tl = jnp.clip(tl, 0, t_blk - 1)
        hit = (lane == tl % SCAN_SUB) & valid
        pltpu.store(y_ref.at[b, tl // SCAN_SUB], y, mask=hit)

    def rows_at(d, tt):
        tl = jnp.clip(tt if d == 0 else t_blk - 1 - tt, 0, t_blk - 1)

        def row(ref, b):
            slabs = [ref[j, b, pl.ds(tl, 8, stride=0), :] for j in range(npair)]
            return jnp.concatenate([jnp.concatenate(slabs, axis=1)] * (HEAD_DIM // 8), axis=0)

        return row

    for d, (r_ref, kk_ref, dec_ref, b_ref, km_ref, vt_ref, y_ref) in enumerate(dirs):
        row0 = rows_at(d, 0)
        for b in range(nb):
            pltpu.matmul_acc_lhs(b * ents, s_scr[d, b].astype(BF16) * row0(kk_ref, b).astype(BF16), d)

    def one_token(tt, carry):
        for b in range(nb):
            for d, (r_ref, kk_ref, dec_ref, b_ref, km_ref, vt_ref, y_ref) in enumerate(dirs):
                tl = tt if d == 0 else t_blk - 1 - tt
                row = rows_at(d, tt)
                nxt = rows_at(d, tt + 1)
                idx = (lane128 // SCAN_SUB) * SCAN_SUB + tl % SCAN_SUB
                sa = pltpu.matmul_pop(b * ents, (HEAD_DIM, g), F32, d)
                write_y(d, b, pltpu.matmul_pop((nb + b) * ents, (HEAD_DIM, g), F32, d), tt - 1, tt > 0)
                vcol = jnp.concatenate([jnp.take_along_axis(vt_ref[b, tl // SCAN_SUB][:, j * LANES:(j + 1) * LANES], idx, axis=1)
                                        for j in range(g // LANES)], axis=1)
                s = s_scr[d, b] * row(dec_ref, b) - sa * row(b_ref, b) + vcol * row(km_ref, b)
                s_scr[d, b] = s
                s_bf = s.astype(BF16)
                pltpu.matmul_acc_lhs((nb + b) * ents, s_bf * row(r_ref, b).astype(BF16), d)
                pltpu.matmul_acc_lhs(b * ents, s_bf * nxt(kk_ref, b).astype(BF16), d)
        return carry

    lax.fori_loop(0, t_blk, one_token, 0, unroll=8)
    for d in range(2):
        for b in range(nb):
            write_y(d, b, pltpu.matmul_pop((nb + b) * ents, (HEAD_DIM, g), F32, d), t_blk - 1, True)
            pltpu.matmul_pop(b * ents, (HEAD_DIM, g), F32, d)

    @pl.when(step_id == pl.num_programs(0) - 1)
    def _():
        send_ref[...] = s_scr[...]


def wkv_scan(r, kk, dec_f, b_f, km_f, dec_b, b_b, km_b, vt, eseg, s0):
    npair, nb, l, _ = kk.shape
    g = npair * LANES
    t_blk = SCAN_BLOCK
    nsub = t_blk // SCAN_SUB
    nblk = l // t_blk
    assert l % t_blk == 0
    seq_f = pl.BlockSpec((npair, nb, t_blk, LANES), lambda i: (0, 0, i, 0))
    seq_b = pl.BlockSpec((npair, nb, t_blk, LANES), lambda i: (0, 0, nblk - 1 - i, 0))
    vt_f = pl.BlockSpec((nb, nsub, HEAD_DIM, g), lambda i: (0, i, 0, 0))
    vt_b = pl.BlockSpec((nb, nsub, HEAD_DIM, g), lambda i: (0, nblk - 1 - i, 0, 0))
    state = pl.BlockSpec(s0.shape, lambda i: (0, 0, 0, 0))
    return pl.pallas_call(
        _wkv_scan_kernel,
        grid=(nblk,),
        in_specs=[seq_f, seq_b, seq_f, seq_b, seq_f, seq_f, seq_f, seq_b, seq_b, seq_b,
                  vt_f, vt_b, pl.BlockSpec(eseg.shape, lambda i: (0, 0)), state],
        out_specs=[vt_f, vt_b, state],
        out_shape=[jax.ShapeDtypeStruct(vt.shape, F32), jax.ShapeDtypeStruct(vt.shape, F32),
                   jax.ShapeDtypeStruct(s0.shape, F32)],
        scratch_shapes=[pltpu.VMEM(s0.shape, F32)],
        compiler_params=_params(("arbitrary",)),
        name="wkv_scan",
    )(r, r, kk, kk, dec_f, b_f, km_f, dec_b, b_b, km_b, vt, vt, eseg, s0)


def _block_ones(n, seg):
    idx = np.arange(n) // seg
    return jnp.asarray(idx[:, None] == idx[None, :], dtype=BF16)


def _padded_rows(w, offset, n):
    return jnp.zeros((n, w.shape[1]), w.dtype).at[offset:offset + w.shape[0]].set(w)


def _rope_tables(n_tokens, n_rep):
    t = jnp.arange(n_tokens)
    row = (t // GRID_W).astype(F32)
    col = (t % GRID_W).astype(F32)
    n_freq = HEAD_DIM // 4
    inv_freq = ROPE_THETA ** (-jnp.arange(n_freq, dtype=F32) / n_freq)
    ang = jnp.concatenate([row[:, None] * inv_freq, col[:, None] * inv_freq], axis=-1)
    cos, sin = jnp.cos(ang), jnp.sin(ang)
    cos_h = jnp.concatenate([cos, cos], axis=-1)
    sin_h = jnp.concatenate([-sin, sin], axis=-1)
    return jnp.tile(cos_h, (1, n_rep)), jnp.tile(sin_h, (1, n_rep))


def kernel(x, c, ctx, c_ctx, ada_w, ada_b, norm1_g, norm2_g, w_in, w_out, rwkv_w0, rwkv_w2, rwkv_a0, rwkv_a2,
           rwkv_k_k, rwkv_k_a, rwkv_r_k, rwkv_g2, rwkv_gn_w, rwkv_gn_b, na_rpb, gqa_q_gain, gqa_k_gain, pool_w,
           pool_scale, mlp_w1, mlp_w2, final_g):
    nb, l, d = x.shape
    lc = ctx.shape[1]
    depth = ada_w.shape[0]
    g = d // 4
    nh = g // HEAD_DIM
    nkv = nh // 2
    kvw = nkv * HEAD_DIM
    dr = rwkv_w2.shape[2]
    ir = rwkv_a2.shape[2]
    gr = rwkv_g2.shape[1]
    assert 2 * dr + 2 * ir + gr <= g and nb + 1 <= 8
    rows = l // GRID_W

    cpad = jnp.zeros((8, d), F32).at[:nb].set(c).at[nb].set(c_ctx)
    mods = adaln(cpad, ada_w, ada_b)

    eseg = _block_ones(g, HEAD_DIM)
    ekv = _block_ones(kvw, HEAD_DIM)
    cos_q, sin_q = _rope_tables(l, nh)
    ones_c, zeros_c = jnp.ones((lc, g), F32), jnp.zeros((lc, g), F32)

    splits = np.cumsum([0, g, g, g, dr, dr, ir, ir, gr, g, g, g, g, kvw, kvw, g])
    lowrank_w = splits[8] - splits[3]

    xl = x.reshape(nb * l, d)
    xc = ctx.reshape(nb * lc, d)
    zero_state = jnp.zeros((2, nb, HEAD_DIM, g), F32)

    for i in range(depth):
        need_ctx_out = i < depth - 1
        mod_l = mods[i, :nb].reshape(nb, N_MOD, 1, d)
        mod_c = mods[i, nb].reshape(N_MOD, 1, 1, d)
        ml = [mod_l[:, k] for k in range(N_MOD)]
        mc = [mod_c[k] for k in range(N_MOD)]

        wi = w_in[i]
        w_inp = jnp.concatenate([wi[:, :splits[8]], jnp.zeros((d, g - lowrank_w), F32),
                                 wi[:, splits[8]:splits[9]] * HEAD_DIM ** -0.5, wi[:, splits[9]:]], axis=1).astype(BF16)
        g1 = norm1_g[i].reshape(1, d)
        g2 = norm2_g[i].reshape(1, d)

        w2p = jnp.stack([_padded_rows(rwkv_w2[i, 0], 0, g), _padded_rows(rwkv_w2[i, 1], dr, g)]).astype(BF16)
        a2p = jnp.stack([_padded_rows(rwkv_a2[i, 0], 2 * dr, g),
                         _padded_rows(rwkv_a2[i, 1], 2 * dr + ir, g)]).astype(BF16)
        g2p = _padded_rows(rwkv_g2[i], 2 * dr + 2 * ir, g).astype(BF16)
        r_k = rwkv_r_k[i].reshape(1, g)
        gn_w = rwkv_gn_w[i].reshape(1, g)
        gn_b = rwkv_gn_b[i].reshape(1, g)
        rwkv_params = (rwkv_w0[i], rwkv_a0[i], w2p, a2p, rwkv_k_k[i].reshape(1, g), rwkv_k_a[i].reshape(1, g), eseg)
        q_gain = jnp.tile(gqa_q_gain[i], nh).reshape(1, g)
        k_gain = jnp.tile(gqa_k_gain[i], nkv).reshape(1, kvw)

        proj_l = inproj(xl, g1, ml[0], ml[1], w_inp, l, rwkv_params, (q_gain, k_gain, cos_q, sin_q, ekv), l, True)
        proj_c = inproj(xc, g1, mc[0], mc[1], w_inp, nb * lc, rwkv_params, (q_gain, k_gain, ones_c, zeros_c, ekv), lc, False)
        pa_l, pb_l, pd_l = proj_l[:3]
        pa_c, pb_c, pd_c = proj_c[:3]

        def scan_inputs(proj, seq):
            rows4 = [t.reshape(g // LANES, nb, seq, LANES) for t in proj[3:11]]
            return rows4, proj[11].reshape(nb, seq // SCAN_SUB, HEAD_DIM, g)

        rows4_c, vt_c = scan_inputs(proj_c, lc)
        yf_c, yb_c, state_c = wkv_scan(*rows4_c, vt_c, eseg, zero_state)
        rows4_l, vt_l = scan_inputs(proj_l, l)
        yf_l, yb_l, _ = wkv_scan(*rows4_l, vt_l, eseg, state_c)

        bias = na_bias_table(na_rpb[i], rows)
        b_l = neighbourhood_attention(pb_l.reshape(nb, l, 3 * g), pb_c.reshape(nb, lc, 3 * g), bias).reshape(nb * l, g)

        qc_l, kc_l, vc_l = proj_l[12:15]
        qc_c, kc_c, vc_c = proj_c[12:15]
        kc_c, vc_c = kc_c.reshape(nb, lc, kvw), vc_c.reshape(nb, lc, kvw)
        k_all = jnp.concatenate([kc_c, kc_l.reshape(nb, l, kvw)], axis=1)
        v_all = jnp.concatenate([vc_c, vc_l.reshape(nb, l, kvw)], axis=1)
        c_l = attention(qc_l, k_all, v_all, l)

        w_bd = jax.scipy.linalg.block_diag(*[pool_w[i, k] for k in range(len(POOL_WINDOWS))]).astype(BF16)
        p_scale = pool_scale[i].reshape(1, g)
        d_l = pool_mixer(pd_l.reshape(nb, l, g), w_bd, p_scale).reshape(nb * l, g)

        w_out4 = w_out[i].reshape(4, g, d).astype(BF16)
        w1 = mlp_w1[i].astype(BF16)
        w2 = mlp_w2[i].astype(BF16)
        readout_params = (r_k, g2p, gn_w, gn_b, eseg)
        xl = outproj(yf_l.reshape(-1, HEAD_DIM, g), yb_l.reshape(-1, HEAD_DIM, g), pa_l, proj_l[7], proj_l[10],
                     readout_params, (b_l, c_l, d_l), w_out4, xl, ml[2], l)
        xl = mlp(xl, g2, ml[3], ml[4], ml[5], w1, w2, l)

        if need_ctx_out:
            b_c = attention(pb_c[:, :g], pb_c[:, g:2 * g].reshape(nb, lc, g), pb_c[:, 2 * g:].reshape(nb, lc, g), lc)
            c_c = attention(qc_c, kc_c, vc_c, lc)
            d_c = pool_mixer(pd_c.reshape(nb, lc, g), w_bd, p_scale).reshape(nb * lc, g)
            xc = outproj(yf_c.reshape(-1, HEAD_DIM, g), yb_c.reshape(-1, HEAD_DIM, g), pa_c, proj_c[7], proj_c[10],
                         readout_params, (b_c, c_c, d_c), w_out4, xc, mc[2], nb * lc)
            xc = mlp(xc, g2, mc[3], mc[4], mc[5], w1, w2, nb * lc)

    return final_norm(xl, final_g.reshape(1, d)).reshape(nb, l, d)
```

```python
import functools

import jax
import jax.numpy as jnp
import numpy as np
from jax import lax
from jax.experimental import pallas as pl
from jax.experimental.pallas import tpu as pltpu

F32 = jnp.float32
BF16 = jnp.bfloat16

HEAD_DIM = 64
GRID_W = 64
NA_ROWS = 8
NA_COLS = 16
ROPE_THETA = 10000.0
POOL_WINDOWS = (2, 4, 8, 16)
NORM_EPS = 1e-6
RWKV_GN_EPS = 1e-5 * HEAD_DIM
N_MOD = 6
LANES = 128
SCAN_BLOCK = 128
SCAN_SUB = 64
ATTN_CHUNK = 2048
MLP_CHUNK = 1024
VMEM_LIMIT = 52 * 1024 * 1024


def _params(sem):
    return pltpu.CompilerParams(dimension_semantics=sem, vmem_limit_bytes=VMEM_LIMIT)


def _tile(n, pref):
    t = min(n, pref)
    assert n % t == 0, (n, pref)
    return t


def _dot(a, b):
    return jnp.dot(a, b, preferred_element_type=F32)


def _dot_nt(a, b):
    return lax.dot_general(a, b, (((1,), (1,)), ((), ())), preferred_element_type=F32)


def _seg_sum(x, e):
    hi = x.astype(BF16)
    lo = (x - hi.astype(F32)).astype(BF16)
    return _dot(hi, e) + _dot(lo, e)


def _norm_mod(x, g, shift, scale):
    ms = jnp.mean(x * x, axis=-1, keepdims=True)
    h = x * lax.rsqrt(ms + NORM_EPS) * g
    return h * (1.0 + scale) + shift


def _adaln_kernel(c_ref, w_ref, b_ref, o_ref):
    c = c_ref[...]
    s = c * jax.nn.sigmoid(c)
    o_ref[0] = jnp.dot(s, w_ref[0], preferred_element_type=F32, precision=lax.Precision.HIGHEST) + b_ref[0]


def adaln(cpad, ada_w, ada_b):
    depth, d, n = ada_w.shape
    tn = _tile(n, 1536)
    return pl.pallas_call(
        _adaln_kernel,
        grid=(depth, n // tn),
        in_specs=[pl.BlockSpec((8, d), lambda i, j: (0, 0)),
                  pl.BlockSpec((1, d, tn), lambda i, j: (i, 0, j)),
                  pl.BlockSpec((1, 1, tn), lambda i, j: (i, 0, j))],
        out_specs=pl.BlockSpec((1, 8, tn), lambda i, j: (i, 0, j)),
        out_shape=jax.ShapeDtypeStruct((depth, 8, n), F32),
        compiler_params=_params(("arbitrary", "arbitrary")),
        name="adaln",
    )(cpad, ada_w, ada_b.reshape(depth, 1, n))


def _swap_halves(y):
    n = y.shape[-1]
    lane = lax.broadcasted_iota(jnp.int32, y.shape, 1)
    half = HEAD_DIM // 2
    return jnp.where(lane % HEAD_DIM < half, pltpu.roll(y, n - half, 1), pltpu.roll(y, half, 1))


def _to_tiles(x):
    g = x.shape[1]
    tiles = []
    for s in range(x.shape[0] // SCAN_SUB):
        blk = x[s * SCAN_SUB:(s + 1) * SCAN_SUB, :]
        tiles.append(jnp.concatenate([blk[:, h * HEAD_DIM:(h + 1) * HEAD_DIM].T for h in range(g // HEAD_DIM)], axis=1))
    return tiles


def _from_tiles(tiles):
    g = tiles[0].shape[1]
    return jnp.concatenate(
        [jnp.concatenate([t[:, h * SCAN_SUB:(h + 1) * SCAN_SUB].T for h in range(g // HEAD_DIM)], axis=1) for t in tiles],
        axis=0)


def _split_store(o_ref, val):
    for j in range(o_ref.shape[0]):
        o_ref[j] = val[:, j * LANES:(j + 1) * LANES]


def _split_load(ref):
    return jnp.concatenate([ref[j] for j in range(ref.shape[0])], axis=1)


def _inproj_kernel(x_ref, g_ref, sh_ref, sc_ref, w_ref,
                   w0_ref, a0_ref, w2_ref, a2_ref, kk_w_ref, ka_ref, e_ref,
                   qg_ref, kg_ref, cos_ref, sin_ref, ek_ref,
                   oa_ref, ob_ref, od_ref,
                   r_ref, kk_ref, decf_ref, bf_ref, kmf_ref, decb_ref, bb_ref, kmb_ref, vt_ref,
                   q_ref, kc_ref, vc_ref, *, rope):
    g = od_ref.shape[1]
    gk = kc_ref.shape[1]
    h = _norm_mod(x_ref[...], g_ref[...], sh_ref[0], sc_ref[0]).astype(BF16)
    e = e_ref[...]

    pa = _dot(h, w_ref[:, 0:4 * g])
    pc = _dot(h, w_ref[:, 7 * g:9 * g])

    oa_ref[...] = jnp.concatenate([pa[:, :g], pa[:, 2 * g:]], axis=1)
    _split_store(r_ref, pa[:, :g])
    k = pa[:, g:2 * g]
    lr = pa[:, 3 * g:4 * g]
    for s, tile in enumerate(_to_tiles(pa[:, 2 * g:3 * g])):
        vt_ref[s] = tile
    kx = k * kk_w_ref[...]
    norm = jnp.sqrt(_seg_sum(kx * kx, e))
    kk = kx / jnp.maximum(norm, 1e-12)
    _split_store(kk_ref, kk)
    lr_t = jnp.tanh(lr).astype(BF16)
    lr_b = lr.astype(BF16)
    for d, (dec_ref, b_ref, km_ref) in enumerate(((decf_ref, bf_ref, kmf_ref), (decb_ref, bb_ref, kmb_ref))):
        z = w0_ref[d:d + 1, :] + _dot(lr_t, w2_ref[d])
        softplus_neg = jnp.maximum(-z, 0.0) + jnp.log1p(jnp.exp(-jnp.abs(z)))
        w = -softplus_neg - 0.5
        _split_store(dec_ref, jnp.exp(-jnp.exp(w)))
        a = jax.nn.sigmoid(a0_ref[d:d + 1, :] + _dot(lr_b, a2_ref[d]))
        _split_store(b_ref, kk * a)
        _split_store(km_ref, k * (1.0 + (a - 1.0) * ka_ref[...]))

    def normed(x, gain, seg):
        ms = _seg_sum(x * x, seg) * (1.0 / HEAD_DIM)
        return x * lax.rsqrt(ms + NORM_EPS) * gain

    q = normed(pc[:, :g], qg_ref[...], e)
    kc = normed(pc[:, g:g + gk], kg_ref[...], ek_ref[...])
    if rope:
        cos = cos_ref[...]
        sin = sin_ref[...]
        q = q * cos + _swap_halves(q) * sin
        kc = kc * cos[:, :gk] + _swap_halves(kc) * sin[:, :gk]
    q_ref[...] = (q * HEAD_DIM ** -0.5).astype(BF16)
    kc_ref[...] = kc.astype(BF16)
    vc_ref[...] = pc[:, g + gk:].astype(BF16)

    ob_ref[...] = _dot(h, w_ref[:, 4 * g:7 * g]).astype(ob_ref.dtype)

    od_ref[...] = _dot(h, w_ref[:, 9 * g:10 * g])


def inproj(x, gain, shift, scale, w, rows_per_mod, rwkv_params, gqa_params, seq_len, rope):
    r, d = x.shape
    g = w.shape[1] // 10
    gk = g // 2
    tm = _tile(min(rows_per_mod, seq_len), 512)
    tpb = rows_per_mod // tm
    nseq = seq_len // tm
    modspec = pl.BlockSpec((1, 1, d), lambda i: (i // tpb, 0, 0))
    full2 = lambda a: pl.BlockSpec(a.shape, lambda i: (0,) * a.ndim)
    rows = lambda n: pl.BlockSpec((tm, n), lambda i: (i, 0))
    q_gain, k_gain, cos, sin, ekv = gqa_params
    table = pl.BlockSpec((tm, g), lambda i: (i % nseq, 0))
    split = pl.BlockSpec((g // LANES, tm, LANES), lambda i: (0, i, 0))
    out_specs = ([rows(3 * g), rows(3 * g), rows(g)] + [split] * 8
                 + [pl.BlockSpec((tm // SCAN_SUB, HEAD_DIM, g), lambda i: (i, 0, 0))] + [rows(g), rows(gk), rows(gk)])
    out_shape = ([jax.ShapeDtypeStruct((r, 3 * g), F32), jax.ShapeDtypeStruct((r, 3 * g), BF16),
                  jax.ShapeDtypeStruct((r, g), F32)] + [jax.ShapeDtypeStruct((g // LANES, r, LANES), F32)] * 8
                 + [jax.ShapeDtypeStruct((r // SCAN_SUB, HEAD_DIM, g), F32), jax.ShapeDtypeStruct((r, g), BF16),
                    jax.ShapeDtypeStruct((r, gk), BF16), jax.ShapeDtypeStruct((r, gk), BF16)])
    return pl.pallas_call(
        functools.partial(_inproj_kernel, rope=rope),
        grid=(r // tm,),
        in_specs=[rows(d), pl.BlockSpec((1, d), lambda i: (0, 0)), modspec, modspec, full2(w)]
                 + [full2(p) for p in rwkv_params] + [full2(q_gain), full2(k_gain), table, table, full2(ekv)],
        out_specs=out_specs,
        out_shape=out_shape,
        compiler_params=_params(("arbitrary",)),
        name="inproj",
    )(x, gain, shift, scale, w, *rwkv_params, q_gain, k_gain, cos, sin, ekv)


def _mix_mlp_kernel(yf_ref, yb_ref, p_ref, kmf_ref, kmb_ref, rk_ref, g2_ref, gnw_ref, gnb_ref, e_ref,
                    b_ref, c_ref, d_ref, wo_ref, x_ref, gate1_ref,
                    gn2_ref, sh_ref, sc_ref, gate2_ref, w1_ref, w2_ref, o_ref):
    g = b_ref.shape[1]
    e = e_ref[...]
    r = p_ref[:, :g]
    v = p_ref[:, g:2 * g]
    lr = p_ref[:, 2 * g:3 * g]
    y = _from_tiles([yf_ref[s] + yb_ref[s] for s in range(yf_ref.shape[0])])
    mu = _seg_sum(y, e) * (1.0 / HEAD_DIM)
    yc = y - mu
    var = _seg_sum(yc * yc, e) * (1.0 / HEAD_DIM)
    yn = yc * lax.rsqrt(var + RWKV_GN_EPS) * gnw_ref[...] + gnb_ref[...]
    bonus = _seg_sum(r * (_split_load(kmf_ref) + _split_load(kmb_ref)) * rk_ref[...], e) * v
    a = (yn + bonus) * _dot(jax.nn.sigmoid(lr).astype(BF16), g2_ref[...])
    acc = _dot(b_ref[...], wo_ref[1])
    acc += _dot(c_ref[...], wo_ref[2])
    acc += _dot(d_ref[...], wo_ref[3])
    acc += _dot(a.astype(BF16), wo_ref[0])
    x = x_ref[...] + gate1_ref[0] * acc
    h = _norm_mod(x, gn2_ref[...], sh_ref[0], sc_ref[0]).astype(BF16)
    acc = None
    for c0 in range(0, w1_ref.shape[1], MLP_CHUNK):
        u = jnp.maximum(_dot(h, w1_ref[:, c0:c0 + MLP_CHUNK]), 0.0)
        part = _dot((u * u).astype(BF16), w2_ref[c0:c0 + MLP_CHUNK, :])
        acc = part if acc is None else acc + part
    o_ref[...] = x + gate2_ref[0] * acc


def mix_mlp(yf, yb, pa, km_f, km_b, readout_params, mix_bcd, w4, x, gate1, gn2, shift, scale, gate2, w1, w2, rows_per_mod):
    r, d = x.shape
    g = w4.shape[1]
    assert w1.shape[1] % MLP_CHUNK == 0
    tm = _tile(rows_per_mod, 512)
    tpb = rows_per_mod // tm
    rows = lambda n: pl.BlockSpec((tm, n), lambda i: (i, 0))
    tiles = pl.BlockSpec((tm // SCAN_SUB, HEAD_DIM, g), lambda i: (i, 0, 0))
    split = pl.BlockSpec((g // LANES, tm, LANES), lambda i: (0, i, 0))
    full2 = lambda a: pl.BlockSpec(a.shape, lambda i: (0,) * a.ndim)
    resident = lambda a: pl.BlockSpec(a.shape, lambda i: (0,) * a.ndim, pipeline_mode=pl.Buffered(1))
    modspec = pl.BlockSpec((1, 1, d), lambda i: (i // tpb, 0, 0))
    return pl.pallas_call(
        _mix_mlp_kernel,
        grid=(r // tm,),
        in_specs=[tiles, tiles, rows(3 * g), split, split] + [full2(p) for p in readout_params]
                 + [rows(g), rows(g), rows(g), resident(w4), rows(d), modspec,
                    pl.BlockSpec((1, d), lambda i: (0, 0)), modspec, modspec, modspec, resident(w1), resident(w2)],
        out_specs=rows(d),
        out_shape=jax.ShapeDtypeStruct((r, d), F32),
        compiler_params=_params(("arbitrary",)),
        name="mix_mlp",
    )(yf, yb, pa, km_f, km_b, *readout_params, *mix_bcd, w4, x, gate1, gn2, shift, scale, gate2, w1, w2)


def _final_norm_kernel(x_ref, g_ref, o_ref):
    x = x_ref[...]
    ms = jnp.mean(x * x, axis=-1, keepdims=True)
    o_ref[...] = x * lax.rsqrt(ms + NORM_EPS) * g_ref[...]


def final_norm(x, g):
    r, d = x.shape
    tm = _tile(r, 1024)
    return pl.pallas_call(
        _final_norm_kernel,
        grid=(r // tm,),
        in_specs=[pl.BlockSpec((tm, d), lambda i: (i, 0)), pl.BlockSpec((1, d), lambda i: (0, 0))],
        out_specs=pl.BlockSpec((tm, d), lambda i: (i, 0)),
        out_shape=jax.ShapeDtypeStruct((r, d), F32),
        compiler_params=_params(("arbitrary",)),
        name="final_norm",
    )(x, g)


def _pad_heads(q, n_heads, kv_width, rep):
    assert kv_width % LANES == 0
    t = q.shape[0]
    lane = lax.broadcasted_iota(jnp.int32, (t, LANES), 1)
    lo_half = lane < HEAD_DIM
    zero = jnp.zeros((t, LANES), q.dtype)
    rows = []
    for h in range(n_heads):
        src = h * HEAD_DIM
        dst = (h // rep) * HEAD_DIM
        piece = q[:, src // LANES * LANES:(src // LANES + 1) * LANES]
        if src % LANES != dst % LANES:
            piece = pltpu.roll(piece.astype(F32), HEAD_DIM, 1).astype(q.dtype)
        piece = jnp.where(lo_half if dst % LANES == 0 else ~lo_half, piece, zero)
        blocks = [piece if j == dst // LANES else zero for j in range(kv_width // LANES)]
        rows.append(jnp.concatenate(blocks, axis=1) if len(blocks) > 1 else piece)
    return jnp.concatenate(rows, axis=0)


def _gather_heads(res, n_heads, rep):
    t = res.shape[0] // n_heads
    lane = lax.broadcasted_iota(jnp.int32, (t, LANES), 1)
    lo_half = lane < HEAD_DIM
    pieces = []
    for h in range(n_heads):
        src = (h // rep) * HEAD_DIM
        dst = h * HEAD_DIM
        piece = res[h * t:(h + 1) * t, src // LANES * LANES:(src // LANES + 1) * LANES]
        if src % LANES != dst % LANES:
            piece = pltpu.roll(piece, HEAD_DIM, 1)
        pieces.append(piece)
    blocks = [jnp.where(lo_half, pieces[2 * j], pieces[2 * j + 1]) for j in range(n_heads // 2)]
    return jnp.concatenate(blocks, axis=1)


def _attn_kernel(q_ref, k_ref, v_ref, o_ref, *, rep):
    n_heads = q_ref.shape[1] // HEAD_DIM
    lk = k_ref.shape[1]
    q = _pad_heads(q_ref[...], n_heads, k_ref.shape[2], rep)
    edges = list(range(0, lk, ATTN_CHUNK)) + [lk]
    m = l = acc = None
    for k0, k1 in zip(edges[:-1], edges[1:]):
        s = _dot_nt(q, k_ref[0, k0:k1, :])
        m_new = jnp.max(s, axis=-1, keepdims=True)
        if m is not None:
            m_new = jnp.maximum(m, m_new)
        p = jnp.exp(s - m_new)
        l_new = jnp.sum(p, axis=-1, keepdims=True)
        acc_new = _dot(p.astype(BF16), v_ref[0, k0:k1, :])
        if m is not None:
            alpha = jnp.exp(m - m_new)
            l_new = alpha * l + l_new
            acc_new = alpha * acc + acc_new
        m, l, acc = m_new, l_new, acc_new
    o_ref[...] = _gather_heads(acc / l, n_heads, rep).astype(o_ref.dtype)


def attention(q, k, v, seq_len):
    r, gq = q.shape
    nb, lk, kw = k.shape
    rep = gq // kw
    tq = _tile(seq_len, 128)
    nq = seq_len // tq
    return pl.pallas_call(
        functools.partial(_attn_kernel, rep=rep),
        grid=(nb, nq),
        in_specs=[pl.BlockSpec((tq, gq), lambda i, j: (i * nq + j, 0)),
                  pl.BlockSpec((1, lk, kw), lambda i, j: (i, 0, 0)),
                  pl.BlockSpec((1, lk, kw), lambda i, j: (i, 0, 0))],
        out_specs=pl.BlockSpec((tq, gq), lambda i, j: (i * nq + j, 0)),
        out_shape=jax.ShapeDtypeStruct((r, gq), BF16),
        compiler_params=_params(("arbitrary", "arbitrary")),
        name="attention",
    )(q, k, v)


def _na_kernel(p_ref, pc_ref, bias_ref, o_ref, *, rows, kh):
    g = o_ref.shape[2]
    n_heads = g // HEAD_DIM
    band = kh * GRID_W
    kc = pc_ref[0, :, g:2 * g]
    vc = pc_ref[0, :, 2 * g:3 * g]

    def one_row(r, carry):
        rs = jnp.clip(r - kh // 2, 0, rows - kh)
        q0 = pl.multiple_of(r * GRID_W, GRID_W)
        k0 = pl.multiple_of(rs * GRID_W, GRID_W)
        q = _pad_heads(p_ref[0, pl.ds(q0, GRID_W), 0:g], n_heads, g, 1)
        s1 = _dot_nt(q, p_ref[0, pl.ds(k0, band), g:2 * g]) + bias_ref[r - rs]
        s2 = _dot_nt(q, kc)
        m = jnp.maximum(jnp.max(s1, axis=-1, keepdims=True), jnp.max(s2, axis=-1, keepdims=True))
        p1 = jnp.exp(s1 - m)
        p2 = jnp.exp(s2 - m)
        l = jnp.sum(p1, axis=-1, keepdims=True) + jnp.sum(p2, axis=-1, keepdims=True)
        res = (_dot(p1.astype(BF16), p_ref[0, pl.ds(k0, band), 2 * g:3 * g]) + _dot(p2.astype(BF16), vc)) / l
        o_ref[0, pl.ds(q0, GRID_W), :] = _gather_heads(res, n_heads, 1).astype(o_ref.dtype)
        return carry

    lax.fori_loop(0, rows, one_row, 0, unroll=8)


def neighbourhood_attention(p, pc, bias):
    nb, l, g3 = p.shape
    g = g3 // 3
    lc = pc.shape[1]
    rows = l // GRID_W
    kh = min(NA_ROWS, rows)
    return pl.pallas_call(
        functools.partial(_na_kernel, rows=rows, kh=kh),
        grid=(nb,),
        in_specs=[pl.BlockSpec((1, l, g3), lambda i: (i, 0, 0)),
                  pl.BlockSpec((1, lc, g3), lambda i: (i, 0, 0)),
                  pl.BlockSpec(bias.shape, lambda i: (0, 0, 0))],
        out_specs=pl.BlockSpec((1, l, g), lambda i: (i, 0, 0)),
        out_shape=jax.ShapeDtypeStruct((nb, l, g), BF16),
        compiler_params=_params(("arbitrary",)),
        name="neighbourhood_attention",
    )(p, pc, bias)


def _na_bias_kernel(rpb_ref, o_ref, *, kh):
    h = pl.program_id(0)
    q = lax.broadcasted_iota(jnp.int32, (GRID_W, GRID_W), 0)
    k = lax.broadcasted_iota(jnp.int32, (GRID_W, GRID_W), 1)
    start = jnp.clip(q - NA_COLS // 2, 0, GRID_W - NA_COLS)
    in_win = (k >= start) & (k < start + NA_COLS)
    off = k - q + NA_COLS - 1
    neg = jnp.full((GRID_W, GRID_W), -jnp.inf, F32)
    blocks = []
    for ro in range(2 * NA_ROWS - 1):
        t = neg
        for c in range(2 * NA_COLS - 1):
            t = jnp.where(off == c, rpb_ref[h, ro, c], t)
        blocks.append(jnp.where(in_win, t, neg))
    for di in range(kh):
        for i in range(kh):
            o_ref[di, :, i * GRID_W:(i + 1) * GRID_W] = blocks[i - di + NA_ROWS - 1]


def na_bias_table(rpb, rows):
    kh = min(NA_ROWS, rows)
    nh = rpb.shape[0]
    return pl.pallas_call(
        functools.partial(_na_bias_kernel, kh=kh),
        grid=(nh,),
        in_specs=[pl.BlockSpec(memory_space=pltpu.SMEM)],
        out_specs=pl.BlockSpec((kh, GRID_W, kh * GRID_W), lambda i: (0, i, 0)),
        out_shape=jax.ShapeDtypeStruct((kh, nh * GRID_W, kh * GRID_W), F32),
        compiler_params=_params(("arbitrary",)),
        name="na_bias",
    )(rpb)


def _pool_kernel(x_ref, inv_ref, w_ref, scale_ref, o_ref):
    x = x_ref[0]
    n, g = x.shape
    pg = g // len(POOL_WINDOWS)
    pad = max(POOL_WINDOWS) // 2
    zeros = jnp.zeros((pad, g), F32)
    xe = jnp.concatenate([zeros, x, zeros], axis=0)
    ne = n + 2 * pad
    group = lax.broadcasted_iota(jnp.int32, xe.shape, 1) // pg
    fwd = xe
    bwd = xe
    cur = 1
    total = jnp.zeros_like(xe)
    for j, w in enumerate(POOL_WINDOWS):
        half = w // 2
        while cur < half:
            fwd = fwd + pltpu.roll(fwd, ne - cur, 0)
            bwd = bwd + pltpu.roll(bwd, cur, 0)
            cur *= 2
        total = jnp.where(group == j, pltpu.roll(bwd, 1, 0) + fwd, total)
    diff = total[pad:pad + n] * inv_ref[...] - x
    o_ref[0] = (_dot(diff.astype(BF16), w_ref[...]) * scale_ref[...]).astype(o_ref.dtype)


def _pool_inverse_counts(n, g):
    t = np.arange(n)
    cols = []
    for w in POOL_WINDOWS:
        lo = np.clip(t - w // 2, 0, n)
        hi = np.clip(t - w // 2 + w, 0, n)
        cols.append(np.repeat((1.0 / (hi - lo))[:, None], g // len(POOL_WINDOWS), axis=1))
    return jnp.asarray(np.concatenate(cols, axis=1), dtype=F32)


def pool_mixer(p, w_bd, scale):
    b, n, g = p.shape
    return pl.pallas_call(
        _pool_kernel,
        grid=(b,),
        in_specs=[pl.BlockSpec((1, n, g), lambda i: (i, 0, 0)),
                  pl.BlockSpec((n, g), lambda i: (0, 0)),
                  pl.BlockSpec((g, g), lambda i: (0, 0)),
                  pl.BlockSpec((1, g), lambda i: (0, 0))],
        out_specs=pl.BlockSpec((1, n, g), lambda i: (i, 0, 0)),
        out_shape=jax.ShapeDtypeStruct((b, n, g), BF16),
        compiler_params=_params(("arbitrary",)),
        name="pool_mixer",
    )(p, _pool_inverse_counts(n, g), w_bd, scale)


def _wkv_scan_kernel(rf_ref, rb_ref, kkf_ref, kkb_ref, decf_ref, bf_ref, kmf_ref, decb_ref, bb_ref, kmb_ref,
                     vtf_ref, vtb_ref, e_ref, s0_ref, yf_ref, yb_ref, send_ref, s_scr):
    npair, nb, t_blk, _ = rf_ref.shape
    g = npair * LANES
    step_id = pl.program_id(0)

    @pl.when(step_id == 0)
    def _():
        s_scr[...] = s0_ref[...]

    yf_ref[...] = jnp.zeros_like(yf_ref)
    yb_ref[...] = jnp.zeros_like(yb_ref)
    lane = lax.broadcasted_iota(jnp.int32, (HEAD_DIM, g), 1) % SCAN_SUB
    lane128 = lax.broadcasted_iota(jnp.int32, (HEAD_DIM, LANES), 1)
    dirs = ((rf_ref, kkf_ref, decf_ref, bf_ref, kmf_ref, vtf_ref, yf_ref),
            (rb_ref, kkb_ref, decb_ref, bb_ref, kmb_ref, vtb_ref, yb_ref))
    ents = HEAD_DIM // 4
    for d in range(2):
        pltpu.matmul_push_rhs(e_ref[...], 0, d)
        pltpu.matmul_acc_lhs(0, jnp.zeros((16, g), BF16), d, load_staged_rhs=0)
        pltpu.matmul_pop(0, (16, g), F32, d)

    def write_y(d, b, y, tt_done, valid):
        y_ref = dirs[d][-1]
        tl = tt_done if d == 0 else t_blk - 1 - tt_done
        tl = jnp.clip(tl, 0, t_blk - 1)
        hit = (lane == tl % SCAN_SUB) & valid
        pltpu.store(y_ref.at[b, tl // SCAN_SUB], y, mask=hit)

    def rows_at(d, tt):
        tl = jnp.clip(tt if d == 0 else t_blk - 1 - tt, 0, t_blk - 1)

        def row(ref, b):
            slabs = [ref[j, b, pl.ds(tl, 8, stride=0), :] for j in range(npair)]
            return jnp.concatenate([jnp.concatenate(slabs, axis=1)] * (HEAD_DIM // 8), axis=0)

        return row

    for d, (r_ref, kk_ref, dec_ref, b_ref, km_ref, vt_ref, y_ref) in enumerate(dirs):
        row0 = rows_at(d, 0)
        for b in range(nb):
            pltpu.matmul_acc_lhs(b * ents, s_scr[d, b].astype(BF16) * row0(kk_ref, b).astype(BF16), d)

    def one_token(tt, carry):
        for b in range(nb):
            for d, (r_ref, kk_ref, dec_ref, b_ref, km_ref, vt_ref, y_ref) in enumerate(dirs):
                tl = tt if d == 0 else t_blk - 1 - tt
                row = rows_at(d, tt)
                nxt = rows_at(d, tt + 1)
                idx = (lane128 // SCAN_SUB) * SCAN_SUB + tl % SCAN_SUB
                sa = pltpu.matmul_pop(b * ents, (HEAD_DIM, g), F32, d)
                write_y(d, b, pltpu.matmul_pop((nb + b) * ents, (HEAD_DIM, g), F32, d), tt - 1, tt > 0)
                vcol = jnp.concatenate([jnp.take_along_axis(vt_ref[b, tl // SCAN_SUB][:, j * LANES:(j + 1) * LANES], idx, axis=1)
                                        for j in range(g // LANES)], axis=1)
                s = s_scr[d, b] * row(dec_ref, b) - sa * row(b_ref, b) + vcol * row(km_ref, b)
                s_scr[d, b] = s
                s_bf = s.astype(BF16)
                pltpu.matmul_acc_lhs((nb + b) * ents, s_bf * row(r_ref, b).astype(BF16), d)
                pltpu.matmul_acc_lhs(b * ents, s_bf * nxt(kk_ref, b).astype(BF16), d)
        return carry

    lax.fori_loop(0, t_blk, one_token, 0, unroll=8)
    for d in range(2):
        for b in range(nb):
            write_y(d, b, pltpu.matmul_pop((nb + b) * ents, (HEAD_DIM, g), F32, d), t_blk - 1, True)
            pltpu.matmul_pop(b * ents, (HEAD_DIM, g), F32, d)

    @pl.when(step_id == pl.num_programs(0) - 1)
    def _():
        send_ref[...] = s_scr[...]


def wkv_scan(r, kk, dec_f, b_f, km_f, dec_b, b_b, km_b, vt, eseg, s0):
    npair, nb, l, _ = kk.shape
    g = npair * LANES
    t_blk = SCAN_BLOCK
    nsub = t_blk // SCAN_SUB
    nblk = l // t_blk
    assert l % t_blk == 0
    seq_f = pl.BlockSpec((npair, nb, t_blk, LANES), lambda i: (0, 0, i, 0))
    seq_b = pl.BlockSpec((npair, nb, t_blk, LANES), lambda i: (0, 0, nblk - 1 - i, 0))
    vt_f = pl.BlockSpec((nb, nsub, HEAD_DIM, g), lambda i: (0, i, 0, 0))
    vt_b = pl.BlockSpec((nb, nsub, HEAD_DIM, g), lambda i: (0, nblk - 1 - i, 0, 0))
    state = pl.BlockSpec(s0.shape, lambda i: (0, 0, 0, 0))
    return pl.pallas_call(
        _wkv_scan_kernel,
        grid=(nblk,),
        in_specs=[seq_f, seq_b, seq_f, seq_b, seq_f, seq_f, seq_f, seq_b, seq_b, seq_b,
                  vt_f, vt_b, pl.BlockSpec(eseg.shape, lambda i: (0, 0)), state],
        out_specs=[vt_f, vt_b, state],
        out_shape=[jax.ShapeDtypeStruct(vt.shape, F32), jax.ShapeDtypeStruct(vt.shape, F32),
                   jax.ShapeDtypeStruct(s0.shape, F32)],
        scratch_shapes=[pltpu.VMEM(s0.shape, F32)],
        compiler_params=_params(("arbitrary",)),
        name="wkv_scan",
    )(r, r, kk, kk, dec_f, b_f, km_f, dec_b, b_b, km_b, vt, vt, eseg, s0)


def _block_ones(n, seg):
    idx = np.arange(n) // seg
    return jnp.asarray(idx[:, None] == idx[None, :], dtype=BF16)


def _padded_rows(w, offset, n):
    return jnp.zeros((n, w.shape[1]), w.dtype).at[offset:offset + w.shape[0]].set(w)


def _rope_tables(n_tokens, n_rep):
    t = jnp.arange(n_tokens)
    row = (t // GRID_W).astype(F32)
    col = (t % GRID_W).astype(F32)
    n_freq = HEAD_DIM // 4
    inv_freq = ROPE_THETA ** (-jnp.arange(n_freq, dtype=F32) / n_freq)
    ang = jnp.concatenate([row[:, None] * inv_freq, col[:, None] * inv_freq], axis=-1)
    cos, sin = jnp.cos(ang), jnp.sin(ang)
    cos_h = jnp.concatenate([cos, cos], axis=-1)
    sin_h = jnp.concatenate([-sin, sin], axis=-1)
    return jnp.tile(cos_h, (1, n_rep)), jnp.tile(sin_h, (1, n_rep))


def kernel(x, c, ctx, c_ctx, ada_w, ada_b, norm1_g, norm2_g, w_in, w_out, rwkv_w0, rwkv_w2, rwkv_a0, rwkv_a2,
           rwkv_k_k, rwkv_k_a, rwkv_r_k, rwkv_g2, rwkv_gn_w, rwkv_gn_b, na_rpb, gqa_q_gain, gqa_k_gain, pool_w,
           pool_scale, mlp_w1, mlp_w2, final_g):
    nb, l, d = x.shape
    lc = ctx.shape[1]
    depth = ada_w.shape[0]
    g = d // 4
    nh = g // HEAD_DIM
    nkv = nh // 2
    kvw = nkv * HEAD_DIM
    dr = rwkv_w2.shape[2]
    ir = rwkv_a2.shape[2]
    gr = rwkv_g2.shape[1]
    assert 2 * dr + 2 * ir + gr <= g and nb + 1 <= 8
    rows = l // GRID_W

    cpad = jnp.zeros((8, d), F32).at[:nb].set(c).at[nb].set(c_ctx)
    mods = adaln(cpad, ada_w, ada_b)

    eseg = _block_ones(g, HEAD_DIM)
    ekv = _block_ones(kvw, HEAD_DIM)
    cos_q, sin_q = _rope_tables(l, nh)
    ones_c, zeros_c = jnp.ones((lc, g), F32), jnp.zeros((lc, g), F32)

    splits = np.cumsum([0, g, g, g, dr, dr, ir, ir, gr, g, g, g, g, kvw, kvw, g])
    lowrank_w = splits[8] - splits[3]

    xl = x.reshape(nb * l, d)
    xc = ctx.reshape(nb * lc, d)
    zero_state = jnp.zeros((2, nb, HEAD_DIM, g), F32)

    for i in range(depth):
        need_ctx_out = i < depth - 1
        mod_l = mods[i, :nb].reshape(nb, N_MOD, 1, d)
        mod_c = mods[i, nb].reshape(N_MOD, 1, 1, d)
        ml = [mod_l[:, k] for k in range(N_MOD)]
        mc = [mod_c[k] for k in range(N_MOD)]

        wi = w_in[i]
        w_inp = jnp.concatenate([wi[:, :splits[8]], jnp.zeros((d, g - lowrank_w), F32),
                                 wi[:, splits[8]:splits[9]] * HEAD_DIM ** -0.5, wi[:, splits[9]:]], axis=1).astype(BF16)
        g1 = norm1_g[i].reshape(1, d)
        g2 = norm2_g[i].reshape(1, d)

        w2p = jnp.stack([_padded_rows(rwkv_w2[i, 0], 0, g), _padded_rows(rwkv_w2[i, 1], dr, g)]).astype(BF16)
        a2p = jnp.stack([_padded_rows(rwkv_a2[i, 0], 2 * dr, g),
                         _padded_rows(rwkv_a2[i, 1], 2 * dr + ir, g)]).astype(BF16)
        g2p = _padded_rows(rwkv_g2[i], 2 * dr + 2 * ir, g).astype(BF16)
        r_k = rwkv_r_k[i].reshape(1, g)
        gn_w = rwkv_gn_w[i].reshape(1, g)
        gn_b = rwkv_gn_b[i].reshape(1, g)
        rwkv_params = (rwkv_w0[i], rwkv_a0[i], w2p, a2p, rwkv_k_k[i].reshape(1, g), rwkv_k_a[i].reshape(1, g), eseg)
        q_gain = jnp.tile(gqa_q_gain[i], nh).reshape(1, g)
        k_gain = jnp.tile(gqa_k_gain[i], nkv).reshape(1, kvw)

        proj_l = inproj(xl, g1, ml[0], ml[1], w_inp, l, rwkv_params, (q_gain, k_gain, cos_q, sin_q, ekv), l, True)
        proj_c = inproj(xc, g1, mc[0], mc[1], w_inp, nb * lc, rwkv_params, (q_gain, k_gain, ones_c, zeros_c, ekv), lc, False)
        pa_l, pb_l, pd_l = proj_l[:3]
        pa_c, pb_c, pd_c = proj_c[:3]

        def scan_inputs(proj, seq):
            rows4 = [t.reshape(g // LANES, nb, seq, LANES) for t in proj[3:11]]
            return rows4, proj[11].reshape(nb, seq // SCAN_SUB, HEAD_DIM, g)

        rows4_c, vt_c = scan_inputs(proj_c, lc)
        yf_c, yb_c, state_c = wkv_scan(*rows4_c, vt_c, eseg, zero_state)
        rows4_l, vt_l = scan_inputs(proj_l, l)
        yf_l, yb_l, _ = wkv_scan(*rows4_l, vt_l, eseg, state_c)

        bias = na_bias_table(na_rpb[i], rows)
        b_l = neighbourhood_attention(pb_l.reshape(nb, l, 3 * g), pb_c.reshape(nb, lc, 3 * g), bias).reshape(nb * l, g)

        qc_l, kc_l, vc_l = proj_l[12:15]
        qc_c, kc_c, vc_c = proj_c[12:15]
        kc_c, vc_c = kc_c.reshape(nb, lc, kvw), vc_c.reshape(nb, lc, kvw)
        k_all = jnp.concatenate([kc_c, kc_l.reshape(nb, l, kvw)], axis=1)
        v_all = jnp.concatenate([vc_c, vc_l.reshape(nb, l, kvw)], axis=1)
        c_l = attention(qc_l, k_all, v_all, l)

        w_bd = jax.scipy.linalg.block_diag(*[pool_w[i, k] for k in range(len(POOL_WINDOWS))]).astype(BF16)
        p_scale = pool_scale[i].reshape(1, g)
        d_l = pool_mixer(pd_l.reshape(nb, l, g), w_bd, p_scale).reshape(nb * l, g)

        w_out4 = w_out[i].reshape(4, g, d).astype(BF16)
        w1 = mlp_w1[i].astype(BF16)
        w2 = mlp_w2[i].astype(BF16)
        readout_params = (r_k, g2p, gn_w, gn_b, eseg)
        xl = mix_mlp(yf_l.reshape(-1, HEAD_DIM, g), yb_l.reshape(-1, HEAD_DIM, g), pa_l, proj_l[7], proj_l[10],
                     readout_params, (b_l, c_l, d_l), w_out4, xl, ml[2], g2, ml[3], ml[4], ml[5], w1, w2, l)

        if need_ctx_out:
            b_c = attention(pb_c[:, :g], pb_c[:, g:2 * g].reshape(nb, lc, g), pb_c[:, 2 * g:].reshape(nb, lc, g), lc)
            c_c = attention(qc_c, kc_c, vc_c, lc)
            d_c = pool_mixer(pd_c.reshape(nb, lc, g), w_bd, p_scale).reshape(nb * lc, g)
            xc = mix_mlp(yf_c.reshape(-1, HEAD_DIM, g), yb_c.reshape(-1, HEAD_DIM, g), pa_c, proj_c[7], proj_c[10],
                         readout_params, (b_c, c_c, d_c), w_out4, xc, mc[2], g2, mc[3], mc[4], mc[5], w1, w2, nb * lc)

    return final_norm(xl, final_g.reshape(1, d)).reshape(nb, l, d)
```

```python
import functools

import jax
import jax.numpy as jnp
import numpy as np
from jax import lax
from jax.experimental import pallas as pl
from jax.experimental.pallas import tpu as pltpu

F32 = jnp.float32
BF16 = jnp.bfloat16

HEAD_DIM = 64
GRID_W = 64
NA_ROWS = 8
NA_COLS = 16
ROPE_THETA = 10000.0
POOL_WINDOWS = (2, 4, 8, 16)
NORM_EPS = 1e-6
RWKV_GN_EPS = 1e-5 * HEAD_DIM
N_MOD = 6
LANES = 128
SCAN_BLOCK = 128
SCAN_SUB = 64
ATTN_CHUNK = 2048
MLP_CHUNK = 1024
VMEM_LIMIT = 52 * 1024 * 1024


def _params(sem):
    return pltpu.CompilerParams(dimension_semantics=sem, vmem_limit_bytes=VMEM_LIMIT)


def _tile(n, pref):
    t = min(n, pref)
    assert n % t == 0, (n, pref)
    return t


def _dot(a, b):
    return jnp.dot(a, b, preferred_element_type=F32)


def _dot_nt(a, b):
    return lax.dot_general(a, b, (((1,), (1,)), ((), ())), preferred_element_type=F32)


def _seg_sum(x, e):
    hi = x.astype(BF16)
    lo = (x - hi.astype(F32)).astype(BF16)
    return _dot(hi, e) + _dot(lo, e)


def _norm_mod(x, g, shift, scale):
    ms = jnp.mean(x * x, axis=-1, keepdims=True)
    h = x * lax.rsqrt(ms + NORM_EPS) * g
    return h * (1.0 + scale) + shift


def _adaln_kernel(c_ref, w_ref, b_ref, o_ref):
    c = c_ref[...]
    s = c * jax.nn.sigmoid(c)
    o_ref[0] = jnp.dot(s, w_ref[0], preferred_element_type=F32, precision=lax.Precision.HIGHEST) + b_ref[0]


def adaln(cpad, ada_w, ada_b):
    depth, d, n = ada_w.shape
    tn = _tile(n, 1536)
    return pl.pallas_call(
        _adaln_kernel,
        grid=(depth, n // tn),
        in_specs=[pl.BlockSpec((8, d), lambda i, j: (0, 0)),
                  pl.BlockSpec((1, d, tn), lambda i, j: (i, 0, j)),
                  pl.BlockSpec((1, 1, tn), lambda i, j: (i, 0, j))],
        out_specs=pl.BlockSpec((1, 8, tn), lambda i, j: (i, 0, j)),
        out_shape=jax.ShapeDtypeStruct((depth, 8, n), F32),
        compiler_params=_params(("arbitrary", "arbitrary")),
        name="adaln",
    )(cpad, ada_w, ada_b.reshape(depth, 1, n))


def _swap_halves(y):
    n = y.shape[-1]
    lane = lax.broadcasted_iota(jnp.int32, y.shape, 1)
    half = HEAD_DIM // 2
    return jnp.where(lane % HEAD_DIM < half, pltpu.roll(y, n - half, 1), pltpu.roll(y, half, 1))


def _to_tiles(x):
    g = x.shape[1]
    tiles = []
    for s in range(x.shape[0] // SCAN_SUB):
        blk = x[s * SCAN_SUB:(s + 1) * SCAN_SUB, :]
        tiles.append(jnp.concatenate([blk[:, h * HEAD_DIM:(h + 1) * HEAD_DIM].T for h in range(g // HEAD_DIM)], axis=1))
    return tiles


def _from_tiles(tiles):
    g = tiles[0].shape[1]
    return jnp.concatenate(
        [jnp.concatenate([t[:, h * SCAN_SUB:(h + 1) * SCAN_SUB].T for h in range(g // HEAD_DIM)], axis=1) for t in tiles],
        axis=0)


def _split_store(o_ref, val):
    for j in range(o_ref.shape[0]):
        o_ref[j] = val[:, j * LANES:(j + 1) * LANES]


def _split_load(ref):
    return jnp.concatenate([ref[j] for j in range(ref.shape[0])], axis=1)


def _inproj_kernel(x_ref, g_ref, sh_ref, sc_ref, w_ref,
                   w0_ref, a0_ref, w2_ref, a2_ref, kk_w_ref, ka_ref, e_ref,
                   qg_ref, kg_ref, cos_ref, sin_ref, ek_ref,
                   oa_ref, ob_ref, od_ref,
                   r_ref, kk_ref, decf_ref, bf_ref, kmf_ref, decb_ref, bb_ref, kmb_ref, vt_ref,
                   q_ref, kc_ref, vc_ref, *, rope):
    g = od_ref.shape[1]
    gk = kc_ref.shape[1]
    h = _norm_mod(x_ref[...], g_ref[...], sh_ref[0], sc_ref[0]).astype(BF16)
    e = e_ref[...]

    pa = _dot(h, w_ref[:, 0:4 * g])
    pc = _dot(h, w_ref[:, 7 * g:9 * g])

    oa_ref[...] = jnp.concatenate([pa[:, :g], pa[:, 2 * g:]], axis=1)
    _split_store(r_ref, pa[:, :g])
    k = pa[:, g:2 * g]
    lr = pa[:, 3 * g:4 * g]
    for s, tile in enumerate(_to_tiles(pa[:, 2 * g:3 * g])):
        vt_ref[s] = tile
    kx = k * kk_w_ref[...]
    norm = jnp.sqrt(_seg_sum(kx * kx, e))
    kk = kx / jnp.maximum(norm, 1e-12)
    _split_store(kk_ref, kk)
    lr_t = jnp.tanh(lr).astype(BF16)
    lr_b = lr.astype(BF16)
    for d, (dec_ref, b_ref, km_ref) in enumerate(((decf_ref, bf_ref, kmf_ref), (decb_ref, bb_ref, kmb_ref))):
        z = w0_ref[d:d + 1, :] + _dot(lr_t, w2_ref[d])
        softplus_neg = jnp.maximum(-z, 0.0) + jnp.log1p(jnp.exp(-jnp.abs(z)))
        w = -softplus_neg - 0.5
        _split_store(dec_ref, jnp.exp(-jnp.exp(w)))
        a = jax.nn.sigmoid(a0_ref[d:d + 1, :] + _dot(lr_b, a2_ref[d]))
        _split_store(b_ref, kk * a)
        _split_store(km_ref, k * (1.0 + (a - 1.0) * ka_ref[...]))

    def normed(x, gain, seg):
        ms = _seg_sum(x * x, seg) * (1.0 / HEAD_DIM)
        return x * lax.rsqrt(ms + NORM_EPS) * gain

    q = normed(pc[:, :g], qg_ref[...], e)
    kc = normed(pc[:, g:g + gk], kg_ref[...], ek_ref[...])
    if rope:
        cos = cos_ref[...]
        sin = sin_ref[...]
        q = q * cos + _swap_halves(q) * sin
        kc = kc * cos[:, :gk] + _swap_halves(kc) * sin[:, :gk]
    q_ref[...] = (q * HEAD_DIM ** -0.5).astype(BF16)
    kc_ref[...] = kc.astype(BF16)
    vc_ref[...] = pc[:, g + gk:].astype(BF16)

    ob_ref[...] = _dot(h, w_ref[:, 4 * g:7 * g]).astype(ob_ref.dtype)

    od_ref[...] = _dot(h, w_ref[:, 9 * g:10 * g])


def inproj(x, gain, shift, scale, w, rows_per_mod, rwkv_params, gqa_params, seq_len, rope):
    r, d = x.shape
    g = w.shape[1] // 10
    gk = g // 2
    tm = _tile(min(rows_per_mod, seq_len), 512)
    tpb = rows_per_mod // tm
    nseq = seq_len // tm
    modspec = pl.BlockSpec((1, 1, d), lambda i: (i // tpb, 0, 0))
    full2 = lambda a: pl.BlockSpec(a.shape, lambda i: (0,) * a.ndim)
    rows = lambda n: pl.BlockSpec((tm, n), lambda i: (i, 0))
    q_gain, k_gain, cos, sin, ekv = gqa_params
    table = pl.BlockSpec((tm, g), lambda i: (i % nseq, 0))
    split = pl.BlockSpec((g // LANES, tm, LANES), lambda i: (0, i, 0))
    out_specs = ([rows(3 * g), rows(3 * g), rows(g)] + [split] * 8
                 + [pl.BlockSpec((tm // SCAN_SUB, HEAD_DIM, g), lambda i: (i, 0, 0))] + [rows(g), rows(gk), rows(gk)])
    out_shape = ([jax.ShapeDtypeStruct((r, 3 * g), F32), jax.ShapeDtypeStruct((r, 3 * g), BF16),
                  jax.ShapeDtypeStruct((r, g), F32)] + [jax.ShapeDtypeStruct((g // LANES, r, LANES), F32)] * 8
                 + [jax.ShapeDtypeStruct((r // SCAN_SUB, HEAD_DIM, g), F32), jax.ShapeDtypeStruct((r, g), BF16),
                    jax.ShapeDtypeStruct((r, gk), BF16), jax.ShapeDtypeStruct((r, gk), BF16)])
    return pl.pallas_call(
        functools.partial(_inproj_kernel, rope=rope),
        grid=(r // tm,),
        in_specs=[rows(d), pl.BlockSpec((1, d), lambda i: (0, 0)), modspec, modspec, full2(w)]
                 + [full2(p) for p in rwkv_params] + [full2(q_gain), full2(k_gain), table, table, full2(ekv)],
        out_specs=out_specs,
        out_shape=out_shape,
        compiler_params=_params(("arbitrary",)),
        name="inproj",
    )(x, gain, shift, scale, w, *rwkv_params, q_gain, k_gain, cos, sin, ekv)


def _mix_mlp_kernel(yf_ref, yb_ref, p_ref, kmf_ref, kmb_ref, rk_ref, g2_ref, gnw_ref, gnb_ref, e_ref,
                    b_ref, c_ref, d_ref, wo_ref, x_ref, gate1_ref,
                    gn2_ref, sh_ref, sc_ref, gate2_ref, w1_ref, w2_ref, *rest):
    final_ref, o_ref = rest if len(rest) == 2 else (None, rest[0])
    g = b_ref.shape[1]
    e = e_ref[...]
    r = p_ref[:, :g]
    v = p_ref[:, g:2 * g]
    lr = p_ref[:, 2 * g:3 * g]
    y = _from_tiles([yf_ref[s] + yb_ref[s] for s in range(yf_ref.shape[0])])
    mu = _seg_sum(y, e) * (1.0 / HEAD_DIM)
    yc = y - mu
    var = _seg_sum(yc * yc, e) * (1.0 / HEAD_DIM)
    yn = yc * lax.rsqrt(var + RWKV_GN_EPS) * gnw_ref[...] + gnb_ref[...]
    bonus = _seg_sum(r * (_split_load(kmf_ref) + _split_load(kmb_ref)) * rk_ref[...], e) * v
    a = (yn + bonus) * _dot(jax.nn.sigmoid(lr).astype(BF16), g2_ref[...])
    acc = _dot(b_ref[...], wo_ref[1])
    acc += _dot(c_ref[...], wo_ref[2])
    acc += _dot(d_ref[...], wo_ref[3])
    acc += _dot(a.astype(BF16), wo_ref[0])
    x = x_ref[...] + gate1_ref[0] * acc
    h = _norm_mod(x, gn2_ref[...], sh_ref[0], sc_ref[0]).astype(BF16)
    acc = None
    for c0 in range(0, w1_ref.shape[1], MLP_CHUNK):
        u = jnp.maximum(_dot(h, w1_ref[:, c0:c0 + MLP_CHUNK]), 0.0)
        part = _dot((u * u).astype(BF16), w2_ref[c0:c0 + MLP_CHUNK, :])
        acc = part if acc is None else acc + part
    out = x + gate2_ref[0] * acc
    if final_ref is not None:
        ms = jnp.mean(out * out, axis=-1, keepdims=True)
        out = out * lax.rsqrt(ms + NORM_EPS) * final_ref[...]
    o_ref[...] = out


def mix_mlp(yf, yb, pa, km_f, km_b, readout_params, mix_bcd, w4, x, gate1, gn2, shift, scale, gate2, w1, w2, rows_per_mod,
            final_gain=None):
    r, d = x.shape
    g = w4.shape[1]
    assert w1.shape[1] % MLP_CHUNK == 0
    tm = _tile(rows_per_mod, 512)
    tpb = rows_per_mod // tm
    rows = lambda n: pl.BlockSpec((tm, n), lambda i: (i, 0))
    tiles = pl.BlockSpec((tm // SCAN_SUB, HEAD_DIM, g), lambda i: (i, 0, 0))
    split = pl.BlockSpec((g // LANES, tm, LANES), lambda i: (0, i, 0))
    full2 = lambda a: pl.BlockSpec(a.shape, lambda i: (0,) * a.ndim)
    resident = lambda a: pl.BlockSpec(a.shape, lambda i: (0,) * a.ndim, pipeline_mode=pl.Buffered(1))
    modspec = pl.BlockSpec((1, 1, d), lambda i: (i // tpb, 0, 0))
    final = () if final_gain is None else (final_gain,)
    return pl.pallas_call(
        _mix_mlp_kernel,
        grid=(r // tm,),
        in_specs=[tiles, tiles, rows(3 * g), split, split] + [full2(p) for p in readout_params]
                 + [rows(g), rows(g), rows(g), resident(w4), rows(d), modspec,
                    pl.BlockSpec((1, d), lambda i: (0, 0)), modspec, modspec, modspec, resident(w1), resident(w2)]
                 + [full2(p) for p in final],
        out_specs=rows(d),
        out_shape=jax.ShapeDtypeStruct((r, d), F32),
        compiler_params=_params(("arbitrary",)),
        name="mix_mlp",
    )(yf, yb, pa, km_f, km_b, *readout_params, *mix_bcd, w4, x, gate1, gn2, shift, scale, gate2, w1, w2, *final)


def _pad_heads(q, n_heads, kv_width, rep):
    assert kv_width % LANES == 0
    t = q.shape[0]
    lane = lax.broadcasted_iota(jnp.int32, (t, LANES), 1)
    lo_half = lane < HEAD_DIM
    zero = jnp.zeros((t, LANES), q.dtype)
    rows = []
    for h in range(n_heads):
        src = h * HEAD_DIM
        dst = (h // rep) * HEAD_DIM
        piece = q[:, src // LANES * LANES:(src // LANES + 1) * LANES]
        if src % LANES != dst % LANES:
            piece = pltpu.roll(piece.astype(F32), HEAD_DIM, 1).astype(q.dtype)
        piece = jnp.where(lo_half if dst % LANES == 0 else ~lo_half, piece, zero)
        blocks = [piece if j == dst // LANES else zero for j in range(kv_width // LANES)]
        rows.append(jnp.concatenate(blocks, axis=1) if len(blocks) > 1 else piece)
    return jnp.concatenate(rows, axis=0)


def _gather_heads(res, n_heads, rep):
    t = res.shape[0] // n_heads
    lane = lax.broadcasted_iota(jnp.int32, (t, LANES), 1)
    lo_half = lane < HEAD_DIM
    pieces = []
    for h in range(n_heads):
        src = (h // rep) * HEAD_DIM
        dst = h * HEAD_DIM
        piece = res[h * t:(h + 1) * t, src // LANES * LANES:(src // LANES + 1) * LANES]
        if src % LANES != dst % LANES:
            piece = pltpu.roll(piece, HEAD_DIM, 1)
        pieces.append(piece)
    blocks = [jnp.where(lo_half, pieces[2 * j], pieces[2 * j + 1]) for j in range(n_heads // 2)]
    return jnp.concatenate(blocks, axis=1)


def _attn_kernel(q_ref, k_ref, v_ref, o_ref, *, rep):
    n_heads = q_ref.shape[1] // HEAD_DIM
    lk = k_ref.shape[1]
    q = _pad_heads(q_ref[...], n_heads, k_ref.shape[2], rep)
    edges = list(range(0, lk, ATTN_CHUNK)) + [lk]
    m = l = acc = None
    for k0, k1 in zip(edges[:-1], edges[1:]):
        s = _dot_nt(q, k_ref[0, k0:k1, :])
        m_new = jnp.max(s, axis=-1, keepdims=True)
        if m is not None:
            m_new = jnp.maximum(m, m_new)
        p = jnp.exp(s - m_new)
        l_new = jnp.sum(p, axis=-1, keepdims=True)
        acc_new = _dot(p.astype(BF16), v_ref[0, k0:k1, :])
        if m is not None:
            alpha = jnp.exp(m - m_new)
            l_new = alpha * l + l_new
            acc_new = alpha * acc + acc_new
        m, l, acc = m_new, l_new, acc_new
    o_ref[...] = _gather_heads(acc / l, n_heads, rep).astype(o_ref.dtype)


def attention(q, k, v, seq_len):
    r, gq = q.shape
    nb, lk, kw = k.shape
    rep = gq // kw
    tq = _tile(seq_len, 128)
    nq = seq_len // tq
    return pl.pallas_call(
        functools.partial(_attn_kernel, rep=rep),
        grid=(nb, nq),
        in_specs=[pl.BlockSpec((tq, gq), lambda i, j: (i * nq + j, 0)),
                  pl.BlockSpec((1, lk, kw), lambda i, j: (i, 0, 0)),
                  pl.BlockSpec((1, lk, kw), lambda i, j: (i, 0, 0))],
        out_specs=pl.BlockSpec((tq, gq), lambda i, j: (i * nq + j, 0)),
        out_shape=jax.ShapeDtypeStruct((r, gq), BF16),
        compiler_params=_params(("arbitrary", "arbitrary")),
        name="attention",
    )(q, k, v)


def _na_kernel(p_ref, pc_ref, bias_ref, o_ref, *, rows, kh):
    g = o_ref.shape[2]
    n_heads = g // HEAD_DIM
    band = kh * GRID_W
    kc = pc_ref[0, :, g:2 * g]
    vc = pc_ref[0, :, 2 * g:3 * g]

    def one_row(r, carry):
        rs = jnp.clip(r - kh // 2, 0, rows - kh)
        q0 = pl.multiple_of(r * GRID_W, GRID_W)
        k0 = pl.multiple_of(rs * GRID_W, GRID_W)
        q = _pad_heads(p_ref[0, pl.ds(q0, GRID_W), 0:g], n_heads, g, 1)
        s1 = _dot_nt(q, p_ref[0, pl.ds(k0, band), g:2 * g]) + bias_ref[r - rs]
        s2 = _dot_nt(q, kc)
        m = jnp.maximum(jnp.max(s1, axis=-1, keepdims=True), jnp.max(s2, axis=-1, keepdims=True))
        p1 = jnp.exp(s1 - m)
        p2 = jnp.exp(s2 - m)
        l = jnp.sum(p1, axis=-1, keepdims=True) + jnp.sum(p2, axis=-1, keepdims=True)
        res = (_dot(p1.astype(BF16), p_ref[0, pl.ds(k0, band), 2 * g:3 * g]) + _dot(p2.astype(BF16), vc)) / l
        o_ref[0, pl.ds(q0, GRID_W), :] = _gather_heads(res, n_heads, 1).astype(o_ref.dtype)
        return carry

    lax.fori_loop(0, rows, one_row, 0, unroll=8)


def neighbourhood_attention(p, pc, bias):
    nb, l, g3 = p.shape
    g = g3 // 3
    lc = pc.shape[1]
    rows = l // GRID_W
    kh = min(NA_ROWS, rows)
    return pl.pallas_call(
        functools.partial(_na_kernel, rows=rows, kh=kh),
        grid=(nb,),
        in_specs=[pl.BlockSpec((1, l, g3), lambda i: (i, 0, 0)),
                  pl.BlockSpec((1, lc, g3), lambda i: (i, 0, 0)),
                  pl.BlockSpec(bias.shape, lambda i: (0, 0, 0))],
        out_specs=pl.BlockSpec((1, l, g), lambda i: (i, 0, 0)),
        out_shape=jax.ShapeDtypeStruct((nb, l, g), BF16),
        compiler_params=_params(("arbitrary",)),
        name="neighbourhood_attention",
    )(p, pc, bias)


def _na_bias_kernel(rpb_ref, o_ref, *, kh):
    h = pl.program_id(0)
    q = lax.broadcasted_iota(jnp.int32, (GRID_W, GRID_W), 0)
    k = lax.broadcasted_iota(jnp.int32, (GRID_W, GRID_W), 1)
    start = jnp.clip(q - NA_COLS // 2, 0, GRID_W - NA_COLS)
    in_win = (k >= start) & (k < start + NA_COLS)
    off = k - q + NA_COLS - 1
    neg = jnp.full((GRID_W, GRID_W), -jnp.inf, F32)
    blocks = []
    for ro in range(2 * NA_ROWS - 1):
        t = neg
        for c in range(2 * NA_COLS - 1):
            t = jnp.where(off == c, rpb_ref[h, ro, c], t)
        blocks.append(jnp.where(in_win, t, neg))
    for di in range(kh):
        for i in range(kh):
            o_ref[di, :, i * GRID_W:(i + 1) * GRID_W] = blocks[i - di + NA_ROWS - 1]


def na_bias_table(rpb, rows):
    kh = min(NA_ROWS, rows)
    nh = rpb.shape[0]
    return pl.pallas_call(
        functools.partial(_na_bias_kernel, kh=kh),
        grid=(nh,),
        in_specs=[pl.BlockSpec(memory_space=pltpu.SMEM)],
        out_specs=pl.BlockSpec((kh, GRID_W, kh * GRID_W), lambda i: (0, i, 0)),
        out_shape=jax.ShapeDtypeStruct((kh, nh * GRID_W, kh * GRID_W), F32),
        compiler_params=_params(("arbitrary",)),
        name="na_bias",
    )(rpb)


def _pool_kernel(x_ref, inv_ref, w_ref, scale_ref, o_ref):
    x = x_ref[0]
    n, g = x.shape
    pg = g // len(POOL_WINDOWS)
    pad = max(POOL_WINDOWS) // 2
    zeros = jnp.zeros((pad, g), F32)
    xe = jnp.concatenate([zeros, x, zeros], axis=0)
    ne = n + 2 * pad
    group = lax.broadcasted_iota(jnp.int32, xe.shape, 1) // pg
    fwd = xe
    bwd = xe
    cur = 1
    total = jnp.zeros_like(xe)
    for j, w in enumerate(POOL_WINDOWS):
        half = w // 2
        while cur < half:
            fwd = fwd + pltpu.roll(fwd, ne - cur, 0)
            bwd = bwd + pltpu.roll(bwd, cur, 0)
            cur *= 2
        total = jnp.where(group == j, pltpu.roll(bwd, 1, 0) + fwd, total)
    diff = total[pad:pad + n] * inv_ref[...] - x
    o_ref[0] = (_dot(diff.astype(BF16), w_ref[...]) * scale_ref[...]).astype(o_ref.dtype)


def _pool_inverse_counts(n, g):
    t = np.arange(n)
    cols = []
    for w in POOL_WINDOWS:
        lo = np.clip(t - w // 2, 0, n)
        hi = np.clip(t - w // 2 + w, 0, n)
        cols.append(np.repeat((1.0 / (hi - lo))[:, None], g // len(POOL_WINDOWS), axis=1))
    return jnp.asarray(np.concatenate(cols, axis=1), dtype=F32)


def pool_mixer(p, w_bd, scale):
    b, n, g = p.shape
    return pl.pallas_call(
        _pool_kernel,
        grid=(b,),
        in_specs=[pl.BlockSpec((1, n, g), lambda i: (i, 0, 0)),
                  pl.BlockSpec((n, g), lambda i: (0, 0)),
                  pl.BlockSpec((g, g), lambda i: (0, 0)),
                  pl.BlockSpec((1, g), lambda i: (0, 0))],
        out_specs=pl.BlockSpec((1, n, g), lambda i: (i, 0, 0)),
        out_shape=jax.ShapeDtypeStruct((b, n, g), BF16),
        compiler_params=_params(("arbitrary",)),
        name="pool_mixer",
    )(p, _pool_inverse_counts(n, g), w_bd, scale)


def _wkv_scan_kernel(rf_ref, rb_ref, kkf_ref, kkb_ref, decf_ref, bf_ref, kmf_ref, decb_ref, bb_ref, kmb_ref,
                     vtf_ref, vtb_ref, e_ref, s0_ref, yf_ref, yb_ref, send_ref, s_scr):
    npair, nb, t_blk, _ = rf_ref.shape
    g = npair * LANES
    step_id = pl.program_id(0)

    @pl.when(step_id == 0)
    def _():
        s_scr[...] = s0_ref[...]

    yf_ref[...] = jnp.zeros_like(yf_ref)
    yb_ref[...] = jnp.zeros_like(yb_ref)
    lane = lax.broadcasted_iota(jnp.int32, (HEAD_DIM, g), 1) % SCAN_SUB
    lane128 = lax.broadcasted_iota(jnp.int32, (HEAD_DIM, LANES), 1)
    dirs = ((rf_ref, kkf_ref, decf_ref, bf_ref, kmf_ref, vtf_ref, yf_ref),
            (rb_ref, kkb_ref, decb_ref, bb_ref, kmb_ref, vtb_ref, yb_ref))
    ents = HEAD_DIM // 4
    for d in range(2):
        pltpu.matmul_push_rhs(e_ref[...], 0, d)
        pltpu.matmul_acc_lhs(0, jnp.zeros((16, g), BF16), d, load_staged_rhs=0)
        pltpu.matmul_pop(0, (16, g), F32, d)

    def write_y(d, b, y, tt_done, valid):
        y_ref = dirs[d][-1]
        tl = tt_done if d == 0 else t_blk - 1 - tt_done
        tl = jnp.clip(tl, 0, t_blk - 1)
        hit = (lane == tl % SCAN_SUB) & valid
        pltpu.store(y_ref.at[b, tl // SCAN_SUB], y, mask=hit)

    def rows_at(d, tt):
        tl = jnp.clip(tt if d == 0 else t_blk - 1 - tt, 0, t_blk - 1)

        def row(ref, b):
            slabs = [ref[j, b, pl.ds(tl, 8, stride=0), :] for j in range(npair)]
            return jnp.concatenate([jnp.concatenate(slabs, axis=1)] * (HEAD_DIM // 8), axis=0)

        return row

    for d, (r_ref, kk_ref, dec_ref, b_ref, km_ref, vt_ref, y_ref) in enumerate(dirs):
        row0 = rows_at(d, 0)
        for b in range(nb):
            pltpu.matmul_acc_lhs(b * ents, s_scr[d, b].astype(BF16) * row0(kk_ref, b).astype(BF16), d)

    def one_token(tt, carry):
        for b in range(nb):
            for d, (r_ref, kk_ref, dec_ref, b_ref, km_ref, vt_ref, y_ref) in enumerate(dirs):
                tl = tt if d == 0 else t_blk - 1 - tt
                row = rows_at(d, tt)
                nxt = rows_at(d, tt + 1)
                idx = (lane128 // SCAN_SUB) * SCAN_SUB + tl % SCAN_SUB
                sa = pltpu.matmul_pop(b * ents, (HEAD_DIM, g), F32, d)
                write_y(d, b, pltpu.matmul_pop((nb + b) * ents, (HEAD_DIM, g), F32, d), tt - 1, tt > 0)
                vcol = jnp.concatenate([jnp.take_along_axis(vt_ref[b, tl // SCAN_SUB][:, j * LANES:(j + 1) * LANES], idx, axis=1)
                                        for j in range(g // LANES)], axis=1)
                s = s_scr[d, b] * row(dec_ref, b) - sa * row(b_ref, b) + vcol * row(km_ref, b)
                s_scr[d, b] = s
                s_bf = s.astype(BF16)
                pltpu.matmul_acc_lhs((nb + b) * ents, s_bf * row(r_ref, b).astype(BF16), d)
                pltpu.matmul_acc_lhs(b * ents, s_bf * nxt(kk_ref, b).astype(BF16), d)
        return carry

    lax.fori_loop(0, t_blk, one_token, 0, unroll=8)
    for d in range(2):
        for b in range(nb):
            write_y(d, b, pltpu.matmul_pop((nb + b) * ents, (HEAD_DIM, g), F32, d), t_blk - 1, True)
            pltpu.matmul_pop(b * ents, (HEAD_DIM, g), F32, d)

    @pl.when(step_id == pl.num_programs(0) - 1)
    def _():
        send_ref[...] = s_scr[...]


def wkv_scan(r, kk, dec_f, b_f, km_f, dec_b, b_b, km_b, vt, eseg, s0):
    npair, nb, l, _ = kk.shape
    g = npair * LANES
    t_blk = SCAN_BLOCK
    nsub = t_blk // SCAN_SUB
    nblk = l // t_blk
    assert l % t_blk == 0
    seq_f = pl.BlockSpec((npair, nb, t_blk, LANES), lambda i: (0, 0, i, 0))
    seq_b = pl.BlockSpec((npair, nb, t_blk, LANES), lambda i: (0, 0, nblk - 1 - i, 0))
    vt_f = pl.BlockSpec((nb, nsub, HEAD_DIM, g), lambda i: (0, i, 0, 0))
    vt_b = pl.BlockSpec((nb, nsub, HEAD_DIM, g), lambda i: (0, nblk - 1 - i, 0, 0))
    state = pl.BlockSpec(s0.shape, lambda i: (0, 0, 0, 0))
    return pl.pallas_call(
        _wkv_scan_kernel,
        grid=(nblk,),
        in_specs=[seq_f, seq_b, seq_f, seq_b, seq_f, seq_f, seq_f, seq_b, seq_b, seq_b,
                  vt_f, vt_b, pl.BlockSpec(eseg.shape, lambda i: (0, 0)), state],
        out_specs=[vt_f, vt_b, state],
        out_shape=[jax.ShapeDtypeStruct(vt.shape, F32), jax.ShapeDtypeStruct(vt.shape, F32),
                   jax.ShapeDtypeStruct(s0.shape, F32)],
        scratch_shapes=[pltpu.VMEM(s0.shape, F32)],
        compiler_params=_params(("arbitrary",)),
        name="wkv_scan",
    )(r, r, kk, kk, dec_f, b_f, km_f, dec_b, b_b, km_b, vt, vt, eseg, s0)


def _block_ones(n, seg):
    idx = np.arange(n) // seg
    return jnp.asarray(idx[:, None] == idx[None, :], dtype=BF16)


def _padded_rows(w, offset, n):
    return jnp.zeros((n, w.shape[1]), w.dtype).at[offset:offset + w.shape[0]].set(w)


def _rope_tables(n_tokens, n_rep):
    t = jnp.arange(n_tokens)
    row = (t // GRID_W).astype(F32)
    col = (t % GRID_W).astype(F32)
    n_freq = HEAD_DIM // 4
    inv_freq = ROPE_THETA ** (-jnp.arange(n_freq, dtype=F32) / n_freq)
    ang = jnp.concatenate([row[:, None] * inv_freq, col[:, None] * inv_freq], axis=-1)
    cos, sin = jnp.cos(ang), jnp.sin(ang)
    cos_h = jnp.concatenate([cos, cos], axis=-1)
    sin_h = jnp.concatenate([-sin, sin], axis=-1)
    return jnp.tile(cos_h, (1, n_rep)), jnp.tile(sin_h, (1, n_rep))


def kernel(x, c, ctx, c_ctx, ada_w, ada_b, norm1_g, norm2_g, w_in, w_out, rwkv_w0, rwkv_w2, rwkv_a0, rwkv_a2,
           rwkv_k_k, rwkv_k_a, rwkv_r_k, rwkv_g2, rwkv_gn_w, rwkv_gn_b, na_rpb, gqa_q_gain, gqa_k_gain, pool_w,
           pool_scale, mlp_w1, mlp_w2, final_g):
    nb, l, d = x.shape
    lc = ctx.shape[1]
    depth = ada_w.shape[0]
    g = d // 4
    nh = g // HEAD_DIM
    nkv = nh // 2
    kvw = nkv * HEAD_DIM
    dr = rwkv_w2.shape[2]
    ir = rwkv_a2.shape[2]
    gr = rwkv_g2.shape[1]
    assert 2 * dr + 2 * ir + gr <= g and nb + 1 <= 8
    rows = l // GRID_W

    cpad = jnp.zeros((8, d), F32).at[:nb].set(c).at[nb].set(c_ctx)
    mods = adaln(cpad, ada_w, ada_b)

    eseg = _block_ones(g, HEAD_DIM)
    ekv = _block_ones(kvw, HEAD_DIM)
    cos_q, sin_q = _rope_tables(l, nh)
    ones_c, zeros_c = jnp.ones((lc, g), F32), jnp.zeros((lc, g), F32)

    splits = np.cumsum([0, g, g, g, dr, dr, ir, ir, gr, g, g, g, g, kvw, kvw, g])
    lowrank_w = splits[8] - splits[3]

    xl = x.reshape(nb * l, d)
    xc = ctx.reshape(nb * lc, d)
    zero_state = jnp.zeros((2, nb, HEAD_DIM, g), F32)

    for i in range(depth):
        need_ctx_out = i < depth - 1
        mod_l = mods[i, :nb].reshape(nb, N_MOD, 1, d)
        mod_c = mods[i, nb].reshape(N_MOD, 1, 1, d)
        ml = [mod_l[:, k] for k in range(N_MOD)]
        mc = [mod_c[k] for k in range(N_MOD)]

        wi = w_in[i]
        w_inp = jnp.concatenate([wi[:, :splits[8]], jnp.zeros((d, g - lowrank_w), F32),
                                 wi[:, splits[8]:splits[9]] * HEAD_DIM ** -0.5, wi[:, splits[9]:]], axis=1).astype(BF16)
        g1 = norm1_g[i].reshape(1, d)
        g2 = norm2_g[i].reshape(1, d)

        w2p = jnp.stack([_padded_rows(rwkv_w2[i, 0], 0, g), _padded_rows(rwkv_w2[i, 1], dr, g)]).astype(BF16)
        a2p = jnp.stack([_padded_rows(rwkv_a2[i, 0], 2 * dr, g),
                         _padded_rows(rwkv_a2[i, 1], 2 * dr + ir, g)]).astype(BF16)
        g2p = _padded_rows(rwkv_g2[i], 2 * dr + 2 * ir, g).astype(BF16)
        r_k = rwkv_r_k[i].reshape(1, g)
        gn_w = rwkv_gn_w[i].reshape(1, g)
        gn_b = rwkv_gn_b[i].reshape(1, g)
        rwkv_params = (rwkv_w0[i], rwkv_a0[i], w2p, a2p, rwkv_k_k[i].reshape(1, g), rwkv_k_a[i].reshape(1, g), eseg)
        q_gain = jnp.tile(gqa_q_gain[i], nh).reshape(1, g)
        k_gain = jnp.tile(gqa_k_gain[i], nkv).reshape(1, kvw)

        proj_l = inproj(xl, g1, ml[0], ml[1], w_inp, l, rwkv_params, (q_gain, k_gain, cos_q, sin_q, ekv), l, True)
        proj_c = inproj(xc, g1, mc[0], mc[1], w_inp, nb * lc, rwkv_params, (q_gain, k_gain, ones_c, zeros_c, ekv), lc, False)
        pa_l, pb_l, pd_l = proj_l[:3]
        pa_c, pb_c, pd_c = proj_c[:3]

        def scan_inputs(proj, seq):
            rows4 = [t.reshape(g // LANES, nb, seq, LANES) for t in proj[3:11]]
            return rows4, proj[11].reshape(nb, seq // SCAN_SUB, HEAD_DIM, g)

        rows4_c, vt_c = scan_inputs(proj_c, lc)
        yf_c, yb_c, state_c = wkv_scan(*rows4_c, vt_c, eseg, zero_state)
        rows4_l, vt_l = scan_inputs(proj_l, l)
        yf_l, yb_l, _ = wkv_scan(*rows4_l, vt_l, eseg, state_c)

        bias = na_bias_table(na_rpb[i], rows)
        b_l = neighbourhood_attention(pb_l.reshape(nb, l, 3 * g), pb_c.reshape(nb, lc, 3 * g), bias).reshape(nb * l, g)

        qc_l, kc_l, vc_l = proj_l[12:15]
        qc_c, kc_c, vc_c = proj_c[12:15]
        kc_c, vc_c = kc_c.reshape(nb, lc, kvw), vc_c.reshape(nb, lc, kvw)
        k_all = jnp.concatenate([kc_c, kc_l.reshape(nb, l, kvw)], axis=1)
        v_all = jnp.concatenate([vc_c, vc_l.reshape(nb, l, kvw)], axis=1)
        c_l = attention(qc_l, k_all, v_all, l)

        w_bd = jax.scipy.linalg.block_diag(*[pool_w[i, k] for k in range(len(POOL_WINDOWS))]).astype(BF16)
        p_scale = pool_scale[i].reshape(1, g)
        d_l = pool_mixer(pd_l.reshape(nb, l, g), w_bd, p_scale).reshape(nb * l, g)

        w_out4 = w_out[i].reshape(4, g, d).astype(BF16)
        w1 = mlp_w1[i].astype(BF16)
        w2 = mlp_w2[i].astype(BF16)
        readout_params = (r_k, g2p, gn_w, gn_b, eseg)
        xl = mix_mlp(yf_l.reshape(-1, HEAD_DIM, g), yb_l.reshape(-1, HEAD_DIM, g), pa_l, proj_l[7], proj_l[10],
                     readout_params, (b_l, c_l, d_l), w_out4, xl, ml[2], g2, ml[3], ml[4], ml[5], w1, w2, l,
                     final_gain=None if need_ctx_out else final_g.reshape(1, d))

        if need_ctx_out:
            b_c = attention(pb_c[:, :g], pb_c[:, g:2 * g].reshape(nb, lc, g), pb_c[:, 2 * g:].reshape(nb, lc, g), lc)
            c_c = attention(qc_c, kc_c, vc_c, lc)
            d_c = pool_mixer(pd_c.reshape(nb, lc, g), w_bd, p_scale).reshape(nb * lc, g)
            xc = mix_mlp(yf_c.reshape(-1, HEAD_DIM, g), yb_c.reshape(-1, HEAD_DIM, g), pa_c, proj_c[7], proj_c[10],
                         readout_params, (b_c, c_c, d_c), w_out4, xc, mc[2], g2, mc[3], mc[4], mc[5], w1, w2, nb * lc)

    return xl.reshape(nb, l, d)
```

```python
import functools

import jax
import jax.numpy as jnp
import numpy as np
from jax import lax
from jax.experimental import pallas as pl
from jax.experimental.pallas import tpu as pltpu

F32 = jnp.float32
BF16 = jnp.bfloat16

HEAD_DIM = 64
GRID_W = 64
NA_ROWS = 8
NA_COLS = 16
ROPE_THETA = 10000.0
POOL_WINDOWS = (2, 4, 8, 16)
NORM_EPS = 1e-6
RWKV_GN_EPS = 1e-5 * HEAD_DIM
N_MOD = 6
LANES = 128
SCAN_BLOCK = 128
SCAN_SUB = 64
LOG2E = 1.4426950408889634
ATTN_CHUNK = 2048
ATTN_T_CHUNK = 512
ATTN_STREAMS = 4
REDUCE_SLAB = 256
MLP_CHUNK = 1024
VMEM_LIMIT = 52 * 1024 * 1024


def _params(sem):
    return pltpu.CompilerParams(dimension_semantics=sem, vmem_limit_bytes=VMEM_LIMIT)


def _tile(n, pref):
    t = min(n, pref)
    assert n % t == 0, (n, pref)
    return t


def _dot(a, b):
    return jnp.dot(a, b, preferred_element_type=F32)


def _dot_nt(a, b):
    return lax.dot_general(a, b, (((1,), (1,)), ((), ())), preferred_element_type=F32)


def _seg_sum(x, e):
    hi = x.astype(BF16)
    lo = (x - hi.astype(F32)).astype(BF16)
    return _dot(hi, e) + _dot(lo, e)


def _norm_mod(x, g, shift, scale):
    ms = jnp.mean(x * x, axis=-1, keepdims=True)
    h = x * lax.rsqrt(ms + NORM_EPS) * g
    return h * (1.0 + scale) + shift


def _adaln_kernel(c_ref, w_ref, b_ref, o_ref):
    c = c_ref[...]
    s = c * jax.nn.sigmoid(c)
    o_ref[0] = jnp.dot(s, w_ref[0], preferred_element_type=F32, precision=lax.Precision.HIGHEST) + b_ref[0]


def adaln(cpad, ada_w, ada_b):
    depth, d, n = ada_w.shape
    tn = _tile(n, 1536)
    return pl.pallas_call(
        _adaln_kernel,
        grid=(depth, n // tn),
        in_specs=[pl.BlockSpec((8, d), lambda i, j: (0, 0)),
                  pl.BlockSpec((1, d, tn), lambda i, j: (i, 0, j)),
                  pl.BlockSpec((1, 1, tn), lambda i, j: (i, 0, j))],
        out_specs=pl.BlockSpec((1, 8, tn), lambda i, j: (i, 0, j)),
        out_shape=jax.ShapeDtypeStruct((depth, 8, n), F32),
        compiler_params=_params(("arbitrary", "arbitrary")),
        name="adaln",
    )(cpad, ada_w, ada_b.reshape(depth, 1, n))


def _swap_halves(y):
    n = y.shape[-1]
    lane = lax.broadcasted_iota(jnp.int32, y.shape, 1)
    half = HEAD_DIM // 2
    return jnp.where(lane % HEAD_DIM < half, pltpu.roll(y, n - half, 1), pltpu.roll(y, half, 1))


def _to_tiles(x):
    g = x.shape[1]
    tiles = []
    for s in range(x.shape[0] // SCAN_SUB):
        blk = x[s * SCAN_SUB:(s + 1) * SCAN_SUB, :]
        tiles.append(jnp.concatenate([blk[:, h * HEAD_DIM:(h + 1) * HEAD_DIM].T for h in range(g // HEAD_DIM)], axis=1))
    return tiles


def _from_tiles(tiles):
    g = tiles[0].shape[1]
    return jnp.concatenate(
        [jnp.concatenate([t[:, h * SCAN_SUB:(h + 1) * SCAN_SUB].T for h in range(g // HEAD_DIM)], axis=1) for t in tiles],
        axis=0)


def _split_store(o_ref, val):
    for j in range(o_ref.shape[0]):
        o_ref[j] = val[:, j * LANES:(j + 1) * LANES]


def _split_load(ref):
    return jnp.concatenate([ref[j] for j in range(ref.shape[0])], axis=1)


def _inproj_kernel(x_ref, g_ref, sh_ref, sc_ref, w_ref,
                   w0_ref, a0_ref, w2_ref, a2_ref, kk_w_ref, ka_ref, e_ref,
                   qg_ref, kg_ref, cos_ref, sin_ref, ek_ref,
                   oa_ref, ob_ref, od_ref,
                   r_ref, kk_ref, decf_ref, bf_ref, kmf_ref, decb_ref, bb_ref, kmb_ref, vt_ref,
                   q_ref, kc_ref, vc_ref, *, rope):
    g = od_ref.shape[1]
    gk = kc_ref.shape[1]
    h = _norm_mod(x_ref[...], g_ref[...], sh_ref[0], sc_ref[0]).astype(BF16)
    e = e_ref[...]

    pa = _dot(h, w_ref[:, 0:4 * g])
    pc = _dot(h, w_ref[:, 7 * g:9 * g])

    oa_ref[...] = jnp.concatenate([pa[:, :g], pa[:, 2 * g:]], axis=1)
    _split_store(r_ref, pa[:, :g])
    k = pa[:, g:2 * g]
    lr = pa[:, 3 * g:4 * g]
    for s, tile in enumerate(_to_tiles(pa[:, 2 * g:3 * g])):
        vt_ref[s] = tile
    kx = k * kk_w_ref[...]
    norm = jnp.sqrt(_seg_sum(kx * kx, e))
    kk = kx / jnp.maximum(norm, 1e-12)
    _split_store(kk_ref, kk)
    lr_t = jnp.tanh(lr).astype(BF16)
    lr_b = lr.astype(BF16)
    for d, (dec_ref, b_ref, km_ref) in enumerate(((decf_ref, bf_ref, kmf_ref), (decb_ref, bb_ref, kmb_ref))):
        z = w0_ref[d:d + 1, :] + _dot(lr_t, w2_ref[d])
        softplus_neg = jnp.maximum(-z, 0.0) + jnp.log1p(jnp.exp(-jnp.abs(z)))
        w = -softplus_neg - 0.5
        _split_store(dec_ref, jnp.exp(-jnp.exp(w)))
        a = jax.nn.sigmoid(a0_ref[d:d + 1, :] + _dot(lr_b, a2_ref[d]))
        _split_store(b_ref, kk * a)
        _split_store(km_ref, k * (1.0 + (a - 1.0) * ka_ref[...]))

    def normed(x, gain, seg):
        ms = _seg_sum(x * x, seg) * (1.0 / HEAD_DIM)
        return x * lax.rsqrt(ms + NORM_EPS) * gain

    q = normed(pc[:, :g], qg_ref[...], e)
    kc = normed(pc[:, g:g + gk], kg_ref[...], ek_ref[...])
    if rope:
        cos = cos_ref[...]
        sin = sin_ref[...]
        q = q * cos + _swap_halves(q) * sin
        kc = kc * cos[:, :gk] + _swap_halves(kc) * sin[:, :gk]
    q_ref[...] = (q * (HEAD_DIM ** -0.5 * LOG2E)).astype(BF16)
    kc_ref[...] = kc.astype(BF16)
    vc_ref[...] = pc[:, g + gk:].astype(BF16)

    ob_ref[...] = _dot(h, w_ref[:, 4 * g:7 * g]).astype(ob_ref.dtype)

    od_ref[...] = _dot(h, w_ref[:, 9 * g:10 * g])


def inproj(x, gain, shift, scale, w, rows_per_mod, rwkv_params, gqa_params, seq_len, rope):
    r, d = x.shape
    g = w.shape[1] // 10
    gk = g // 2
    tm = _tile(min(rows_per_mod, seq_len), 512)
    tpb = rows_per_mod // tm
    nseq = seq_len // tm
    modspec = pl.BlockSpec((1, 1, d), lambda i: (i // tpb, 0, 0))
    full2 = lambda a: pl.BlockSpec(a.shape, lambda i: (0,) * a.ndim)
    rows = lambda n: pl.BlockSpec((tm, n), lambda i: (i, 0))
    q_gain, k_gain, cos, sin, ekv = gqa_params
    table = pl.BlockSpec((tm, g), lambda i: (i % nseq, 0))
    split = pl.BlockSpec((g // LANES, tm, LANES), lambda i: (0, i, 0))
    out_specs = ([rows(3 * g), rows(3 * g), rows(g)] + [split] * 8
                 + [pl.BlockSpec((tm // SCAN_SUB, HEAD_DIM, g), lambda i: (i, 0, 0))] + [rows(g), rows(gk), rows(gk)])
    out_shape = ([jax.ShapeDtypeStruct((r, 3 * g), F32), jax.ShapeDtypeStruct((r, 3 * g), BF16),
                  jax.ShapeDtypeStruct((r, g), F32)] + [jax.ShapeDtypeStruct((g // LANES, r, LANES), F32)] * 8
                 + [jax.ShapeDtypeStruct((r // SCAN_SUB, HEAD_DIM, g), F32), jax.ShapeDtypeStruct((r, g), BF16),
                    jax.ShapeDtypeStruct((r, gk), BF16), jax.ShapeDtypeStruct((r, gk), BF16)])
    return pl.pallas_call(
        functools.partial(_inproj_kernel, rope=rope),
        grid=(r // tm,),
        in_specs=[rows(d), pl.BlockSpec((1, d), lambda i: (0, 0)), modspec, modspec, full2(w)]
                 + [full2(p) for p in rwkv_params] + [full2(q_gain), full2(k_gain), table, table, full2(ekv)],
        out_specs=out_specs,
        out_shape=out_shape,
        compiler_params=_params(("arbitrary",)),
        name="inproj",
    )(x, gain, shift, scale, w, *rwkv_params, q_gain, k_gain, cos, sin, ekv)


def _mix_mlp_kernel(yf_ref, yb_ref, p_ref, kmf_ref, kmb_ref, rk_ref, g2_ref, gnw_ref, gnb_ref, e_ref,
                    b_ref, c_ref, d_ref, wo_ref, x_ref, gate1_ref,
                    gn2_ref, sh_ref, sc_ref, gate2_ref, w1_ref, w2_ref, *rest):
    final_ref, o_ref = rest if len(rest) == 2 else (None, rest[0])
    g = b_ref.shape[1]
    e = e_ref[...]
    r = p_ref[:, :g]
    v = p_ref[:, g:2 * g]
    lr = p_ref[:, 2 * g:3 * g]
    y = _from_tiles([yf_ref[s] + yb_ref[s] for s in range(yf_ref.shape[0])])
    mu = _seg_sum(y, e) * (1.0 / HEAD_DIM)
    yc = y - mu
    var = _seg_sum(yc * yc, e) * (1.0 / HEAD_DIM)
    yn = yc * lax.rsqrt(var + RWKV_GN_EPS) * gnw_ref[...] + gnb_ref[...]
    bonus = _seg_sum(r * (_split_load(kmf_ref) + _split_load(kmb_ref)) * rk_ref[...], e) * v
    a = (yn + bonus) * _dot(jax.nn.sigmoid(lr).astype(BF16), g2_ref[...])
    acc = _dot(b_ref[...], wo_ref[1])
    acc += _dot(c_ref[...], wo_ref[2])
    acc += _dot(d_ref[...], wo_ref[3])
    acc += _dot(a.astype(BF16), wo_ref[0])
    x = x_ref[...] + gate1_ref[0] * acc
    h = _norm_mod(x, gn2_ref[...], sh_ref[0], sc_ref[0]).astype(BF16)
    acc = None
    for c0 in range(0, w1_ref.shape[1], MLP_CHUNK):
        u = jnp.maximum(_dot(h, w1_ref[:, c0:c0 + MLP_CHUNK]), 0.0)
        part = _dot((u * u).astype(BF16), w2_ref[c0:c0 + MLP_CHUNK, :])
        acc = part if acc is None else acc + part
    out = x + gate2_ref[0] * acc
    if final_ref is not None:
        ms = jnp.mean(out * out, axis=-1, keepdims=True)
        out = out * lax.rsqrt(ms + NORM_EPS) * final_ref[...]
    o_ref[...] = out


def mix_mlp(yf, yb, pa, km_f, km_b, readout_params, mix_bcd, w4, x, gate1, gn2, shift, scale, gate2, w1, w2, rows_per_mod,
            final_gain=None):
    r, d = x.shape
    g = w4.shape[1]
    assert w1.shape[1] % MLP_CHUNK == 0
    tm = _tile(rows_per_mod, 512)
    tpb = rows_per_mod // tm
    rows = lambda n: pl.BlockSpec((tm, n), lambda i: (i, 0))
    tiles = pl.BlockSpec((tm // SCAN_SUB, HEAD_DIM, g), lambda i: (i, 0, 0))
    split = pl.BlockSpec((g // LANES, tm, LANES), lambda i: (0, i, 0))
    full2 = lambda a: pl.BlockSpec(a.shape, lambda i: (0,) * a.ndim)
    resident = lambda a: pl.BlockSpec(a.shape, lambda i: (0,) * a.ndim, pipeline_mode=pl.Buffered(1))
    modspec = pl.BlockSpec((1, 1, d), lambda i: (i // tpb, 0, 0))
    final = () if final_gain is None else (final_gain,)
    return pl.pallas_call(
        _mix_mlp_kernel,
        grid=(r // tm,),
        in_specs=[tiles, tiles, rows(3 * g), split, split] + [full2(p) for p in readout_params]
                 + [rows(g), rows(g), rows(g), resident(w4), rows(d), modspec,
                    pl.BlockSpec((1, d), lambda i: (0, 0)), modspec, modspec, modspec, resident(w1), resident(w2)]
                 + [full2(p) for p in final],
        out_specs=rows(d),
        out_shape=jax.ShapeDtypeStruct((r, d), F32),
        compiler_params=_params(("arbitrary",)),
        name="mix_mlp",
    )(yf, yb, pa, km_f, km_b, *readout_params, *mix_bcd, w4, x, gate1, gn2, shift, scale, gate2, w1, w2, *final)


def _pad_heads(q, n_heads, kv_width, rep):
    assert kv_width % LANES == 0
    t = q.shape[0]
    lane = lax.broadcasted_iota(jnp.int32, (t, LANES), 1)
    lo_half = lane < HEAD_DIM
    zero = jnp.zeros((t, LANES), q.dtype)
    rows = []
    for h in range(n_heads):
        src = h * HEAD_DIM
        dst = (h // rep) * HEAD_DIM
        piece = q[:, src // LANES * LANES:(src // LANES + 1) * LANES]
        if src % LANES != dst % LANES:
            piece = pltpu.roll(piece.astype(F32), HEAD_DIM, 1).astype(q.dtype)
        piece = jnp.where(lo_half if dst % LANES == 0 else ~lo_half, piece, zero)
        blocks = [piece if j == dst // LANES else zero for j in range(kv_width // LANES)]
        rows.append(jnp.concatenate(blocks, axis=1) if len(blocks) > 1 else piece)
    return jnp.concatenate(rows, axis=0)


def _gather_heads(res, n_heads, rep):
    t = res.shape[0] // n_heads
    lane = lax.broadcasted_iota(jnp.int32, (t, LANES), 1)
    lo_half = lane < HEAD_DIM
    pieces = []
    for h in range(n_heads):
        src = (h // rep) * HEAD_DIM
        dst = h * HEAD_DIM
        piece = res[h * t:(h + 1) * t, src // LANES * LANES:(src // LANES + 1) * LANES]
        if src % LANES != dst % LANES:
            piece = pltpu.roll(piece, HEAD_DIM, 1)
        pieces.append(piece)
    blocks = [jnp.where(lo_half, pieces[2 * j], pieces[2 * j + 1]) for j in range(n_heads // 2)]
    return jnp.concatenate(blocks, axis=1)


def _attn_kernel(q_ref, k_ref, v_ref, o_ref, *, rep, base2):
    exp = jnp.exp2 if base2 else jnp.exp
    n_heads = q_ref.shape[1] // HEAD_DIM
    lk = k_ref.shape[1]
    q = _pad_heads(q_ref[...], n_heads, k_ref.shape[2], rep)
    edges = list(range(0, lk, ATTN_CHUNK)) + [lk]
    m = l = acc = None
    for k0, k1 in zip(edges[:-1], edges[1:]):
        s = _dot_nt(q, k_ref[0, k0:k1, :])
        m_new = jnp.max(s, axis=-1, keepdims=True)
        if m is not None:
            m_new = jnp.maximum(m, m_new)
        p = exp(s - m_new)
        l_new = jnp.sum(p, axis=-1, keepdims=True)
        acc_new = _dot(p.astype(BF16), v_ref[0, k0:k1, :])
        if m is not None:
            alpha = exp(m - m_new)
            l_new = alpha * l + l_new
            acc_new = alpha * acc + acc_new
        m, l, acc = m_new, l_new, acc_new
    o_ref[...] = _gather_heads(acc / l, n_heads, rep).astype(o_ref.dtype)


def attention(q, k, v, seq_len, base2):
    r, gq = q.shape
    nb, lk, kw = k.shape
    rep = gq // kw
    tq = _tile(seq_len, 128)
    nq = seq_len // tq
    return pl.pallas_call(
        functools.partial(_attn_kernel, rep=rep, base2=base2),
        grid=(nb, nq),
        in_specs=[pl.BlockSpec((tq, gq), lambda i, j: (i * nq + j, 0)),
                  pl.BlockSpec((1, lk, kw), lambda i, j: (i, 0, 0)),
                  pl.BlockSpec((1, lk, kw), lambda i, j: (i, 0, 0))],
        out_specs=pl.BlockSpec((tq, gq), lambda i, j: (i * nq + j, 0)),
        out_shape=jax.ShapeDtypeStruct((r, gq), BF16),
        compiler_params=_params(("arbitrary", "arbitrary")),
        name="attention",
    )(q, k, v)


def _col_reduce(x, op):
    n, c = x.shape
    part = op(x.reshape(n // REDUCE_SLAB, REDUCE_SLAB, c), axis=0)
    return op(part, axis=0, keepdims=True)


def _attn_t_kernel(q_ref, k_ref, vt_ref, o_ref, *, rep):
    n_heads = q_ref.shape[1] // HEAD_DIM
    tq = q_ref.shape[0] // ATTN_STREAMS
    lk = k_ref.shape[1]
    kw = k_ref.shape[2]
    w = rep * tq
    qs = [_pad_heads(q_ref[st * tq:(st + 1) * tq, :], n_heads, kw, rep) for st in range(ATTN_STREAMS)]
    edges = list(range(0, lk, ATTN_T_CHUNK)) + [lk]
    chunks = list(zip(edges[:-1], edges[1:]))
    scores = lambda st, c: _dot_nt(k_ref[0, c[0]:c[1], :], qs[st])
    m = [None] * ATTN_STREAMS
    acc = [None] * ATTN_STREAMS
    s_next = [scores(st, chunks[0]) for st in range(ATTN_STREAMS)]
    for i, (k0, k1) in enumerate(chunks):
        for st in range(ATTN_STREAMS):
            s = s_next[st]
            m_new = _col_reduce(s, jnp.max)
            if m[st] is not None:
                m_new = jnp.maximum(m[st], m_new)
            p = jnp.exp2(s - m_new).astype(BF16)
            acc_new = _dot(vt_ref[0, :, k0:k1], p)
            if i + 1 < len(chunks):
                s_next[st] = scores(st, chunks[i + 1])
            if m[st] is not None:
                acc_new = jnp.exp2(m[st] - m_new) * acc[st] + acc_new
            m[st], acc[st] = m_new, acc_new
    lo_half = lax.broadcasted_iota(jnp.int32, (tq, LANES), 1) < HEAD_DIM
    for st in range(ATTN_STREAMS):
        a = acc[st]
        res = jnp.concatenate([a[j * HEAD_DIM:(j + 1) * HEAD_DIM, j * w:(j + 1) * w] / a[kw:kw + 1, j * w:(j + 1) * w]
                               for j in range(n_heads // rep)], axis=0)
        t0 = res[:, :tq].T
        t1 = res[:, tq:].T
        out = jnp.concatenate([jnp.where(lo_half, t0, pltpu.roll(t1, HEAD_DIM, 1)),
                               jnp.where(lo_half, pltpu.roll(t0, HEAD_DIM, 1), t1)], axis=1)
        o_ref[st * tq:(st + 1) * tq, :] = out.astype(o_ref.dtype)


def attention_t(q, k, vt, seq_len):
    r, gq = q.shape
    nb, lk, kw = k.shape
    assert gq == 4 * HEAD_DIM and kw == 2 * HEAD_DIM
    tq = ATTN_STREAMS * LANES
    assert seq_len % tq == 0
    nq = seq_len // tq
    return pl.pallas_call(
        functools.partial(_attn_t_kernel, rep=2),
        grid=(nb, nq),
        in_specs=[pl.BlockSpec((tq, gq), lambda i, j: (i * nq + j, 0)),
                  pl.BlockSpec((1, lk, kw), lambda i, j: (i, 0, 0)),
                  pl.BlockSpec((1, vt.shape[1], lk), lambda i, j: (i, 0, 0))],
        out_specs=pl.BlockSpec((tq, gq), lambda i, j: (i * nq + j, 0)),
        out_shape=jax.ShapeDtypeStruct((r, gq), BF16),
        compiler_params=_params(("arbitrary", "arbitrary")),
        name="attention_t",
    )(q, k, vt)


def _na_kernel(p_ref, pc_ref, bias_ref, o_ref, *, rows, kh):
    g = o_ref.shape[2]
    n_heads = g // HEAD_DIM
    band = kh * GRID_W
    kc = pc_ref[0, :, g:2 * g]
    vc = pc_ref[0, :, 2 * g:3 * g]

    def one_row(r, carry):
        rs = jnp.clip(r - kh // 2, 0, rows - kh)
        q0 = pl.multiple_of(r * GRID_W, GRID_W)
        k0 = pl.multiple_of(rs * GRID_W, GRID_W)
        q = _pad_heads(p_ref[0, pl.ds(q0, GRID_W), 0:g], n_heads, g, 1)
        s1 = _dot_nt(q, p_ref[0, pl.ds(k0, band), g:2 * g]) + bias_ref[r - rs]
        s2 = _dot_nt(q, kc)
        m = jnp.maximum(jnp.max(s1, axis=-1, keepdims=True), jnp.max(s2, axis=-1, keepdims=True))
        p1 = jnp.exp(s1 - m)
        p2 = jnp.exp(s2 - m)
        l = jnp.sum(p1, axis=-1, keepdims=True) + jnp.sum(p2, axis=-1, keepdims=True)
        res = (_dot(p1.astype(BF16), p_ref[0, pl.ds(k0, band), 2 * g:3 * g]) + _dot(p2.astype(BF16), vc)) / l
        o_ref[0, pl.ds(q0, GRID_W), :] = _gather_heads(res, n_heads, 1).astype(o_ref.dtype)
        return carry

    lax.fori_loop(0, rows, one_row, 0, unroll=8)


def neighbourhood_attention(p, pc, bias):
    nb, l, g3 = p.shape
    g = g3 // 3
    lc = pc.shape[1]
    rows = l // GRID_W
    kh = min(NA_ROWS, rows)
    return pl.pallas_call(
        functools.partial(_na_kernel, rows=rows, kh=kh),
        grid=(nb,),
        in_specs=[pl.BlockSpec((1, l, g3), lambda i: (i, 0, 0)),
                  pl.BlockSpec((1, lc, g3), lambda i: (i, 0, 0)),
                  pl.BlockSpec(bias.shape, lambda i: (0, 0, 0))],
        out_specs=pl.BlockSpec((1, l, g), lambda i: (i, 0, 0)),
        out_shape=jax.ShapeDtypeStruct((nb, l, g), BF16),
        compiler_params=_params(("arbitrary",)),
        name="neighbourhood_attention",
    )(p, pc, bias)


def _na_bias_kernel(rpb_ref, o_ref, *, kh):
    h = pl.program_id(0)
    q = lax.broadcasted_iota(jnp.int32, (GRID_W, GRID_W), 0)
    k = lax.broadcasted_iota(jnp.int32, (GRID_W, GRID_W), 1)
    start = jnp.clip(q - NA_COLS // 2, 0, GRID_W - NA_COLS)
    in_win = (k >= start) & (k < start + NA_COLS)
    off = k - q + NA_COLS - 1
    neg = jnp.full((GRID_W, GRID_W), -jnp.inf, F32)
    blocks = []
    for ro in range(2 * NA_ROWS - 1):
        t = neg
        for c in range(2 * NA_COLS - 1):
            t = jnp.where(off == c, rpb_ref[h, ro, c], t)
        blocks.append(jnp.where(in_win, t, neg))
    for di in range(kh):
        for i in range(kh):
            o_ref[di, :, i * GRID_W:(i + 1) * GRID_W] = blocks[i - di + NA_ROWS - 1]


def na_bias_table(rpb, rows):
    kh = min(NA_ROWS, rows)
    nh = rpb.shape[0]
    return pl.pallas_call(
        functools.partial(_na_bias_kernel, kh=kh),
        grid=(nh,),
        in_specs=[pl.BlockSpec(memory_space=pltpu.SMEM)],
        out_specs=pl.BlockSpec((kh, GRID_W, kh * GRID_W), lambda i: (0, i, 0)),
        out_shape=jax.ShapeDtypeStruct((kh, nh * GRID_W, kh * GRID_W), F32),
        compiler_params=_params(("arbitrary",)),
        name="na_bias",
    )(rpb)


def _pool_kernel(x_ref, inv_ref, w_ref, scale_ref, o_ref):
    x = x_ref[0]
    n, g = x.shape
    pg = g // len(POOL_WINDOWS)
    pad = max(POOL_WINDOWS) // 2
    zeros = jnp.zeros((pad, g), F32)
    xe = jnp.concatenate([zeros, x, zeros], axis=0)
    ne = n + 2 * pad
    group = lax.broadcasted_iota(jnp.int32, xe.shape, 1) // pg
    fwd = xe
    bwd = xe
    cur = 1
    total = jnp.zeros_like(xe)
    for j, w in enumerate(POOL_WINDOWS):
        half = w // 2
        while cur < half:
            fwd = fwd + pltpu.roll(fwd, ne - cur, 0)
            bwd = bwd + pltpu.roll(bwd, cur, 0)
            cur *= 2
        total = jnp.where(group == j, pltpu.roll(bwd, 1, 0) + fwd, total)
    diff = total[pad:pad + n] * inv_ref[...] - x
    o_ref[0] = (_dot(diff.astype(BF16), w_ref[...]) * scale_ref[...]).astype(o_ref.dtype)


def _pool_inverse_counts(n, g):
    t = np.arange(n)
    cols = []
    for w in POOL_WINDOWS:
        lo = np.clip(t - w // 2, 0, n)
        hi = np.clip(t - w // 2 + w, 0, n)
        cols.append(np.repeat((1.0 / (hi - lo))[:, None], g // len(POOL_WINDOWS), axis=1))
    return jnp.asarray(np.concatenate(cols, axis=1), dtype=F32)


def pool_mixer(p, w_bd, scale):
    b, n, g = p.shape
    return pl.pallas_call(
        _pool_kernel,
        grid=(b,),
        in_specs=[pl.BlockSpec((1, n, g), lambda i: (i, 0, 0)),
                  pl.BlockSpec((n, g), lambda i: (0, 0)),
                  pl.BlockSpec((g, g), lambda i: (0, 0)),
                  pl.BlockSpec((1, g), lambda i: (0, 0))],
        out_specs=pl.BlockSpec((1, n, g), lambda i: (i, 0, 0)),
        out_shape=jax.ShapeDtypeStruct((b, n, g), BF16),
        compiler_params=_params(("arbitrary",)),
        name="pool_mixer",
    )(p, _pool_inverse_counts(n, g), w_bd, scale)


def _wkv_scan_kernel(rf_ref, rb_ref, kkf_ref, kkb_ref, decf_ref, bf_ref, kmf_ref, decb_ref, bb_ref, kmb_ref,
                     vtf_ref, vtb_ref, e_ref, s0_ref, yf_ref, yb_ref, send_ref, s_scr):
    npair, nb, t_blk, _ = rf_ref.shape
    g = npair * LANES
    step_id = pl.program_id(0)

    @pl.when(step_id == 0)
    def _():
        s_scr[...] = s0_ref[...]

    yf_ref[...] = jnp.zeros_like(yf_ref)
    yb_ref[...] = jnp.zeros_like(yb_ref)
    lane = lax.broadcasted_iota(jnp.int32, (HEAD_DIM, g), 1) % SCAN_SUB
    lane128 = lax.broadcasted_iota(jnp.int32, (HEAD_DIM, LANES), 1)
    dirs = ((rf_ref, kkf_ref, decf_ref, bf_ref, kmf_ref, vtf_ref, yf_ref),
            (rb_ref, kkb_ref, decb_ref, bb_ref, kmb_ref, vtb_ref, yb_ref))
    ents = HEAD_DIM // 4
    for d in range(2):
        pltpu.matmul_push_rhs(e_ref[...], 0, d)
        pltpu.matmul_acc_lhs(0, jnp.zeros((16, g), BF16), d, load_staged_rhs=0)
        pltpu.matmul_pop(0, (16, g), F32, d)

    def write_y(d, b, y, tt_done, valid):
        y_ref = dirs[d][-1]
        tl = tt_done if d == 0 else t_blk - 1 - tt_done
        tl = jnp.clip(tl, 0, t_blk - 1)
        hit = (lane == tl % SCAN_SUB) & valid
        pltpu.store(y_ref.at[b, tl // SCAN_SUB], y, mask=hit)

    def rows_at(d, tt):
        tl = jnp.clip(tt if d == 0 else t_blk - 1 - tt, 0, t_blk - 1)

        def row(ref, b):
            slabs = [ref[j, b, pl.ds(tl, 8, stride=0), :] for j in range(npair)]
            return jnp.concatenate([jnp.concatenate(slabs, axis=1)] * (HEAD_DIM // 8), axis=0)

        return row

    for d, (r_ref, kk_ref, dec_ref, b_ref, km_ref, vt_ref, y_ref) in enumerate(dirs):
        row0 = rows_at(d, 0)
        for b in range(nb):
            pltpu.matmul_acc_lhs(b * ents, s_scr[d, b].astype(BF16) * row0(kk_ref, b).astype(BF16), d)

    def one_token(tt, carry):
        for b in range(nb):
            for d, (r_ref, kk_ref, dec_ref, b_ref, km_ref, vt_ref, y_ref) in enumerate(dirs):
                tl = tt if d == 0 else t_blk - 1 - tt
                row = rows_at(d, tt)
                nxt = rows_at(d, tt + 1)
                idx = (lane128 // SCAN_SUB) * SCAN_SUB + tl % SCAN_SUB
                sa = pltpu.matmul_pop(b * ents, (HEAD_DIM, g), F32, d)
                write_y(d, b, pltpu.matmul_pop((nb + b) * ents, (HEAD_DIM, g), F32, d), tt - 1, tt > 0)
                vcol = jnp.concatenate([jnp.take_along_axis(vt_ref[b, tl // SCAN_SUB][:, j * LANES:(j + 1) * LANES], idx, axis=1)
                                        for j in range(g // LANES)], axis=1)
                s = s_scr[d, b] * row(dec_ref, b) - sa * row(b_ref, b) + vcol * row(km_ref, b)
                s_scr[d, b] = s
                s_bf = s.astype(BF16)
                pltpu.matmul_acc_lhs((nb + b) * ents, s_bf * row(r_ref, b).astype(BF16), d)
                pltpu.matmul_acc_lhs(b * ents, s_bf * nxt(kk_ref, b).astype(BF16), d)
        return carry

    lax.fori_loop(0, t_blk, one_token, 0, unroll=8)
    for d in range(2):
        for b in range(nb):
            write_y(d, b, pltpu.matmul_pop((nb + b) * ents, (HEAD_DIM, g), F32, d), t_blk - 1, True)
            pltpu.matmul_pop(b * ents, (HEAD_DIM, g), F32, d)

    @pl.when(step_id == pl.num_programs(0) - 1)
    def _():
        send_ref[...] = s_scr[...]


def wkv_scan(r, kk, dec_f, b_f, km_f, dec_b, b_b, km_b, vt, eseg, s0):
    npair, nb, l, _ = kk.shape
    g = npair * LANES
    t_blk = SCAN_BLOCK
    nsub = t_blk // SCAN_SUB
    nblk = l // t_blk
    assert l % t_blk == 0
    seq_f = pl.BlockSpec((npair, nb, t_blk, LANES), lambda i: (0, 0, i, 0))
    seq_b = pl.BlockSpec((npair, nb, t_blk, LANES), lambda i: (0, 0, nblk - 1 - i, 0))
    vt_f = pl.BlockSpec((nb, nsub, HEAD_DIM, g), lambda i: (0, i, 0, 0))
    vt_b = pl.BlockSpec((nb, nsub, HEAD_DIM, g), lambda i: (0, nblk - 1 - i, 0, 0))
    state = pl.BlockSpec(s0.shape, lambda i: (0, 0, 0, 0))
    return pl.pallas_call(
        _wkv_scan_kernel,
        grid=(nblk,),
        in_specs=[seq_f, seq_b, seq_f, seq_b, seq_f, seq_f, seq_f, seq_b, seq_b, seq_b,
                  vt_f, vt_b, pl.BlockSpec(eseg.shape, lambda i: (0, 0)), state],
        out_specs=[vt_f, vt_b, state],
        out_shape=[jax.ShapeDtypeStruct(vt.shape, F32), jax.ShapeDtypeStruct(vt.shape, F32),
                   jax.ShapeDtypeStruct(s0.shape, F32)],
        scratch_shapes=[pltpu.VMEM(s0.shape, F32)],
        compiler_params=_params(("arbitrary",)),
        name="wkv_scan",
    )(r, r, kk, kk, dec_f, b_f, km_f, dec_b, b_b, km_b, vt, vt, eseg, s0)


def _block_ones(n, seg):
    idx = np.arange(n) // seg
    return jnp.asarray(idx[:, None] == idx[None, :], dtype=BF16)


def _padded_rows(w, offset, n):
    return jnp.zeros((n, w.shape[1]), w.dtype).at[offset:offset + w.shape[0]].set(w)


def _rope_tables(n_tokens, n_rep):
    t = jnp.arange(n_tokens)
    row = (t // GRID_W).astype(F32)
    col = (t % GRID_W).astype(F32)
    n_freq = HEAD_DIM // 4
    inv_freq = ROPE_THETA ** (-jnp.arange(n_freq, dtype=F32) / n_freq)
    ang = jnp.concatenate([row[:, None] * inv_freq, col[:, None] * inv_freq], axis=-1)
    cos, sin = jnp.cos(ang), jnp.sin(ang)
    cos_h = jnp.concatenate([cos, cos], axis=-1)
    sin_h = jnp.concatenate([-sin, sin], axis=-1)
    return jnp.tile(cos_h, (1, n_rep)), jnp.tile(sin_h, (1, n_rep))


def kernel(x, c, ctx, c_ctx, ada_w, ada_b, norm1_g, norm2_g, w_in, w_out, rwkv_w0, rwkv_w2, rwkv_a0, rwkv_a2,
           rwkv_k_k, rwkv_k_a, rwkv_r_k, rwkv_g2, rwkv_gn_w, rwkv_gn_b, na_rpb, gqa_q_gain, gqa_k_gain, pool_w,
           pool_scale, mlp_w1, mlp_w2, final_g):
    nb, l, d = x.shape
    lc = ctx.shape[1]
    depth = ada_w.shape[0]
    g = d // 4
    nh = g // HEAD_DIM
    nkv = nh // 2
    kvw = nkv * HEAD_DIM
    dr = rwkv_w2.shape[2]
    ir = rwkv_a2.shape[2]
    gr = rwkv_g2.shape[1]
    assert 2 * dr + 2 * ir + gr <= g and nb + 1 <= 8
    rows = l // GRID_W

    cpad = jnp.zeros((8, d), F32).at[:nb].set(c).at[nb].set(c_ctx)
    mods = adaln(cpad, ada_w, ada_b)

    eseg = _block_ones(g, HEAD_DIM)
    ekv = _block_ones(kvw, HEAD_DIM)
    cos_q, sin_q = _rope_tables(l, nh)
    ones_c, zeros_c = jnp.ones((lc, g), F32), jnp.zeros((lc, g), F32)

    splits = np.cumsum([0, g, g, g, dr, dr, ir, ir, gr, g, g, g, g, kvw, kvw, g])
    lowrank_w = splits[8] - splits[3]

    xl = x.reshape(nb * l, d)
    xc = ctx.reshape(nb * lc, d)
    zero_state = jnp.zeros((2, nb, HEAD_DIM, g), F32)

    for i in range(depth):
        need_ctx_out = i < depth - 1
        mod_l = mods[i, :nb].reshape(nb, N_MOD, 1, d)
        mod_c = mods[i, nb].reshape(N_MOD, 1, 1, d)
        ml = [mod_l[:, k] for k in range(N_MOD)]
        mc = [mod_c[k] for k in range(N_MOD)]

        wi = w_in[i]
        w_inp = jnp.concatenate([wi[:, :splits[8]], jnp.zeros((d, g - lowrank_w), F32),
                                 wi[:, splits[8]:splits[9]] * HEAD_DIM ** -0.5, wi[:, splits[9]:]], axis=1).astype(BF16)
        g1 = norm1_g[i].reshape(1, d)
        g2 = norm2_g[i].reshape(1, d)

        w2p = jnp.stack([_padded_rows(rwkv_w2[i, 0], 0, g), _padded_rows(rwkv_w2[i, 1], dr, g)]).astype(BF16)
        a2p = jnp.stack([_padded_rows(rwkv_a2[i, 0], 2 * dr, g),
                         _padded_rows(rwkv_a2[i, 1], 2 * dr + ir, g)]).astype(BF16)
        g2p = _padded_rows(rwkv_g2[i], 2 * dr + 2 * ir, g).astype(BF16)
        r_k = rwkv_r_k[i].reshape(1, g)
        gn_w = rwkv_gn_w[i].reshape(1, g)
        gn_b = rwkv_gn_b[i].reshape(1, g)
        rwkv_params = (rwkv_w0[i], rwkv_a0[i], w2p, a2p, rwkv_k_k[i].reshape(1, g), rwkv_k_a[i].reshape(1, g), eseg)
        q_gain = jnp.tile(gqa_q_gain[i], nh).reshape(1, g)
        k_gain = jnp.tile(gqa_k_gain[i], nkv).reshape(1, kvw)

        proj_l = inproj(xl, g1, ml[0], ml[1], w_inp, l, rwkv_params, (q_gain, k_gain, cos_q, sin_q, ekv), l, True)
        proj_c = inproj(xc, g1, mc[0], mc[1], w_inp, nb * lc, rwkv_params, (q_gain, k_gain, ones_c, zeros_c, ekv), lc, False)
        pa_l, pb_l, pd_l = proj_l[:3]
        pa_c, pb_c, pd_c = proj_c[:3]

        def scan_inputs(proj, seq):
            rows4 = [t.reshape(g // LANES, nb, seq, LANES) for t in proj[3:11]]
            return rows4, proj[11].reshape(nb, seq // SCAN_SUB, HEAD_DIM, g)

        rows4_c, vt_c = scan_inputs(proj_c, lc)
        yf_c, yb_c, state_c = wkv_scan(*rows4_c, vt_c, eseg, zero_state)
        rows4_l, vt_l = scan_inputs(proj_l, l)
        yf_l, yb_l, _ = wkv_scan(*rows4_l, vt_l, eseg, state_c)

        bias = na_bias_table(na_rpb[i], rows)
        b_l = neighbourhood_attention(pb_l.reshape(nb, l, 3 * g), pb_c.reshape(nb, lc, 3 * g), bias).reshape(nb * l, g)

        qc_l, kc_l, vc_l = proj_l[12:15]
        qc_c, kc_c, vc_c = proj_c[12:15]
        kc_c, vc_c = kc_c.reshape(nb, lc, kvw), vc_c.reshape(nb, lc, kvw)
        k_all = jnp.concatenate([kc_c, kc_l.reshape(nb, l, kvw)], axis=1)
        v_all = jnp.concatenate([vc_c, vc_l.reshape(nb, l, kvw)], axis=1)
        vt_all = jnp.concatenate([jnp.swapaxes(v_all, 1, 2), jnp.ones((nb, 16, lc + l), BF16)], axis=1)
        c_l = attention_t(qc_l, k_all, vt_all, l)

        w_bd = jax.scipy.linalg.block_diag(*[pool_w[i, k] for k in range(len(POOL_WINDOWS))]).astype(BF16)
        p_scale = pool_scale[i].reshape(1, g)
        d_l = pool_mixer(pd_l.reshape(nb, l, g), w_bd, p_scale).reshape(nb * l, g)

        w_out4 = w_out[i].reshape(4, g, d).astype(BF16)
        w1 = mlp_w1[i].astype(BF16)
        w2 = mlp_w2[i].astype(BF16)
        readout_params = (r_k, g2p, gn_w, gn_b, eseg)
        xl = mix_mlp(yf_l.reshape(-1, HEAD_DIM, g), yb_l.reshape(-1, HEAD_DIM, g), pa_l, proj_l[7], proj_l[10],
                     readout_params, (b_l, c_l, d_l), w_out4, xl, ml[2], g2, ml[3], ml[4], ml[5], w1, w2, l,
                     final_gain=None if need_ctx_out else final_g.reshape(1, d))

        if need_ctx_out:
            b_c = attention(pb_c[:, :g], pb_c[:, g:2 * g].reshape(nb, lc, g), pb_c[:, 2 * g:].reshape(nb, lc, g), lc, False)
            c_c = attention(qc_c, kc_c, vc_c, lc, True)
            d_c = pool_mixer(pd_c.reshape(nb, lc, g), w_bd, p_scale).reshape(nb * lc, g)
            xc = mix_mlp(yf_c.reshape(-1, HEAD_DIM, g), yb_c.reshape(-1, HEAD_DIM, g), pa_c, proj_c[7], proj_c[10],
                         readout_params, (b_c, c_c, d_c), w_out4, xc, mc[2], g2, mc[3], mc[4], mc[5], w1, w2, nb * lc)

    return xl.reshape(nb, l, d)
```

```python
import functools

import jax
import jax.numpy as jnp
import numpy as np
from jax import lax
from jax.experimental import pallas as pl
from jax.experimental.pallas import tpu as pltpu

F32 = jnp.float32
BF16 = jnp.bfloat16

HEAD_DIM = 64
GRID_W = 64
NA_ROWS = 8
NA_COLS = 16
ROPE_THETA = 10000.0
POOL_WINDOWS = (2, 4, 8, 16)
NORM_EPS = 1e-6
RWKV_GN_EPS = 1e-5 * HEAD_DIM
N_MOD = 6
LANES = 128
SCAN_BLOCK = 128
SCAN_SUB = 64
LOG2E = 1.4426950408889634
ATTN_CHUNK = 2048
ATTN_T_CHUNK = 256
ATTN_STREAMS = 8
REDUCE_SLAB = 256
MLP_CHUNK = 1024
VMEM_LIMIT = 52 * 1024 * 1024


def _params(sem):
    return pltpu.CompilerParams(dimension_semantics=sem, vmem_limit_bytes=VMEM_LIMIT)


def _tile(n, pref):
    t = min(n, pref)
    assert n % t == 0, (n, pref)
    return t


def _dot(a, b):
    return jnp.dot(a, b, preferred_element_type=F32)


def _dot_nt(a, b):
    return lax.dot_general(a, b, (((1,), (1,)), ((), ())), preferred_element_type=F32)


def _seg_sum(x, e):
    hi = x.astype(BF16)
    lo = (x - hi.astype(F32)).astype(BF16)
    return _dot(hi, e) + _dot(lo, e)


def _norm_mod(x, g, shift, scale):
    ms = jnp.mean(x * x, axis=-1, keepdims=True)
    h = x * lax.rsqrt(ms + NORM_EPS) * g
    return h * (1.0 + scale) + shift


def _adaln_kernel(c_ref, w_ref, b_ref, o_ref):
    c = c_ref[...]
    s = c * jax.nn.sigmoid(c)
    o_ref[0] = jnp.dot(s, w_ref[0], preferred_element_type=F32, precision=lax.Precision.HIGHEST) + b_ref[0]


def adaln(cpad, ada_w, ada_b):
    depth, d, n = ada_w.shape
    tn = _tile(n, 1536)
    return pl.pallas_call(
        _adaln_kernel,
        grid=(depth, n // tn),
        in_specs=[pl.BlockSpec((8, d), lambda i, j: (0, 0)),
                  pl.BlockSpec((1, d, tn), lambda i, j: (i, 0, j)),
                  pl.BlockSpec((1, 1, tn), lambda i, j: (i, 0, j))],
        out_specs=pl.BlockSpec((1, 8, tn), lambda i, j: (i, 0, j)),
        out_shape=jax.ShapeDtypeStruct((depth, 8, n), F32),
        compiler_params=_params(("arbitrary", "arbitrary")),
        name="adaln",
    )(cpad, ada_w, ada_b.reshape(depth, 1, n))


def _swap_halves(y):
    n = y.shape[-1]
    lane = lax.broadcasted_iota(jnp.int32, y.shape, 1)
    half = HEAD_DIM // 2
    return jnp.where(lane % HEAD_DIM < half, pltpu.roll(y, n - half, 1), pltpu.roll(y, half, 1))


def _to_tiles(x):
    g = x.shape[1]
    tiles = []
    for s in range(x.shape[0] // SCAN_SUB):
        blk = x[s * SCAN_SUB:(s + 1) * SCAN_SUB, :]
        tiles.append(jnp.concatenate([blk[:, h * HEAD_DIM:(h + 1) * HEAD_DIM].T for h in range(g // HEAD_DIM)], axis=1))
    return tiles


def _from_tiles(tiles):
    g = tiles[0].shape[1]
    return jnp.concatenate(
        [jnp.concatenate([t[:, h * SCAN_SUB:(h + 1) * SCAN_SUB].T for h in range(g // HEAD_DIM)], axis=1) for t in tiles],
        axis=0)


def _split_store(o_ref, val):
    for j in range(o_ref.shape[0]):
        o_ref[j] = val[:, j * LANES:(j + 1) * LANES]


def _split_load(ref):
    return jnp.concatenate([ref[j] for j in range(ref.shape[0])], axis=1)


def _inproj_kernel(x_ref, g_ref, sh_ref, sc_ref, w_ref,
                   w0_ref, a0_ref, w2_ref, a2_ref, kk_w_ref, ka_ref, e_ref,
                   qg_ref, kg_ref, cos_ref, sin_ref, ek_ref,
                   oa_ref, ob_ref, od_ref,
                   r_ref, kk_ref, decf_ref, bf_ref, kmf_ref, decb_ref, bb_ref, kmb_ref, vt_ref,
                   q_ref, kc_ref, vc_ref, *, rope):
    g = od_ref.shape[1]
    gk = kc_ref.shape[1]
    h = _norm_mod(x_ref[...], g_ref[...], sh_ref[0], sc_ref[0]).astype(BF16)
    e = e_ref[...]

    pa = _dot(h, w_ref[:, 0:4 * g])
    pc = _dot(h, w_ref[:, 7 * g:9 * g])

    oa_ref[...] = jnp.concatenate([pa[:, :g], pa[:, 2 * g:]], axis=1)
    _split_store(r_ref, pa[:, :g])
    k = pa[:, g:2 * g]
    lr = pa[:, 3 * g:4 * g]
    for s, tile in enumerate(_to_tiles(pa[:, 2 * g:3 * g])):
        vt_ref[s] = tile
    kx = k * kk_w_ref[...]
    norm = jnp.sqrt(_seg_sum(kx * kx, e))
    kk = kx / jnp.maximum(norm, 1e-12)
    _split_store(kk_ref, kk)
    lr_t = jnp.tanh(lr).astype(BF16)
    lr_b = lr.astype(BF16)
    for d, (dec_ref, b_ref, km_ref) in enumerate(((decf_ref, bf_ref, kmf_ref), (decb_ref, bb_ref, kmb_ref))):
        z = w0_ref[d:d + 1, :] + _dot(lr_t, w2_ref[d])
        softplus_neg = jnp.maximum(-z, 0.0) + jnp.log1p(jnp.exp(-jnp.abs(z)))
        w = -softplus_neg - 0.5
        _split_store(dec_ref, jnp.exp(-jnp.exp(w)))
        a = jax.nn.sigmoid(a0_ref[d:d + 1, :] + _dot(lr_b, a2_ref[d]))
        _split_store(b_ref, kk * a)
        _split_store(km_ref, k * (1.0 + (a - 1.0) * ka_ref[...]))

    def normed(x, gain, seg):
        ms = _seg_sum(x * x, seg) * (1.0 / HEAD_DIM)
        return x * lax.rsqrt(ms + NORM_EPS) * gain

    q = normed(pc[:, :g], qg_ref[...], e)
    kc = normed(pc[:, g:g + gk], kg_ref[...], ek_ref[...])
    if rope:
        cos = cos_ref[...]
        sin = sin_ref[...]
        q = q * cos + _swap_halves(q) * sin
        kc = kc * cos[:, :gk] + _swap_halves(kc) * sin[:, :gk]
    q_ref[...] = (q * (HEAD_DIM ** -0.5 * LOG2E)).astype(BF16)
    kc_ref[...] = kc.astype(BF16)
    vc_ref[...] = pc[:, g + gk:].astype(BF16)

    ob_ref[...] = _dot(h, w_ref[:, 4 * g:7 * g]).astype(ob_ref.dtype)

    od_ref[...] = _dot(h, w_ref[:, 9 * g:10 * g])


def inproj(x, gain, shift, scale, w, rows_per_mod, rwkv_params, gqa_params, seq_len, rope):
    r, d = x.shape
    g = w.shape[1] // 10
    gk = g // 2
    tm = _tile(min(rows_per_mod, seq_len), 512)
    tpb = rows_per_mod // tm
    nseq = seq_len // tm
    modspec = pl.BlockSpec((1, 1, d), lambda i: (i // tpb, 0, 0))
    full2 = lambda a: pl.BlockSpec(a.shape, lambda i: (0,) * a.ndim)
    rows = lambda n: pl.BlockSpec((tm, n), lambda i: (i, 0))
    q_gain, k_gain, cos, sin, ekv = gqa_params
    table = pl.BlockSpec((tm, g), lambda i: (i % nseq, 0))
    split = pl.BlockSpec((g // LANES, tm, LANES), lambda i: (0, i, 0))
    out_specs = ([rows(3 * g), rows(3 * g), rows(g)] + [split] * 8
                 + [pl.BlockSpec((tm // SCAN_SUB, HEAD_DIM, g), lambda i: (i, 0, 0))] + [rows(g), rows(gk), rows(gk)])
    out_shape = ([jax.ShapeDtypeStruct((r, 3 * g), F32), jax.ShapeDtypeStruct((r, 3 * g), BF16),
                  jax.ShapeDtypeStruct((r, g), F32)] + [jax.ShapeDtypeStruct((g // LANES, r, LANES), F32)] * 8
                 + [jax.ShapeDtypeStruct((r // SCAN_SUB, HEAD_DIM, g), F32), jax.ShapeDtypeStruct((r, g), BF16),
                    jax.ShapeDtypeStruct((r, gk), BF16), jax.ShapeDtypeStruct((r, gk), BF16)])
    return pl.pallas_call(
        functools.partial(_inproj_kernel, rope=rope),
        grid=(r // tm,),
        in_specs=[rows(d), pl.BlockSpec((1, d), lambda i: (0, 0)), modspec, modspec, full2(w)]
                 + [full2(p) for p in rwkv_params] + [full2(q_gain), full2(k_gain), table, table, full2(ekv)],
        out_specs=out_specs,
        out_shape=out_shape,
        compiler_params=_params(("arbitrary",)),
        name="inproj",
    )(x, gain, shift, scale, w, *rwkv_params, q_gain, k_gain, cos, sin, ekv)


def _mix_mlp_kernel(yf_ref, yb_ref, p_ref, kmf_ref, kmb_ref, rk_ref, g2_ref, gnw_ref, gnb_ref, e_ref,
                    b_ref, c_ref, d_ref, wo_ref, x_ref, gate1_ref,
                    gn2_ref, sh_ref, sc_ref, gate2_ref, w1_ref, w2_ref, *rest):
    final_ref, o_ref = rest if len(rest) == 2 else (None, rest[0])
    g = b_ref.shape[1]
    e = e_ref[...]
    r = p_ref[:, :g]
    v = p_ref[:, g:2 * g]
    lr = p_ref[:, 2 * g:3 * g]
    y = _from_tiles([yf_ref[s] + yb_ref[s] for s in range(yf_ref.shape[0])])
    mu = _seg_sum(y, e) * (1.0 / HEAD_DIM)
    yc = y - mu
    var = _seg_sum(yc * yc, e) * (1.0 / HEAD_DIM)
    yn = yc * lax.rsqrt(var + RWKV_GN_EPS) * gnw_ref[...] + gnb_ref[...]
    bonus = _seg_sum(r * (_split_load(kmf_ref) + _split_load(kmb_ref)) * rk_ref[...], e) * v
    a = (yn + bonus) * _dot(jax.nn.sigmoid(lr).astype(BF16), g2_ref[...])
    acc = _dot(b_ref[...], wo_ref[1])
    acc += _dot(c_ref[...], wo_ref[2])
    acc += _dot(d_ref[...], wo_ref[3])
    acc += _dot(a.astype(BF16), wo_ref[0])
    x = x_ref[...] + gate1_ref[0] * acc
    h = _norm_mod(x, gn2_ref[...], sh_ref[0], sc_ref[0]).astype(BF16)
    acc = None
    for c0 in range(0, w1_ref.shape[1], MLP_CHUNK):
        u = jnp.maximum(_dot(h, w1_ref[:, c0:c0 + MLP_CHUNK]), 0.0)
        part = _dot((u * u).astype(BF16), w2_ref[c0:c0 + MLP_CHUNK, :])
        acc = part if acc is None else acc + part
    out = x + gate2_ref[0] * acc
    if final_ref is not None:
        ms = jnp.mean(out * out, axis=-1, keepdims=True)
        out = out * lax.rsqrt(ms + NORM_EPS) * final_ref[...]
    o_ref[...] = out


def mix_mlp(yf, yb, pa, km_f, km_b, readout_params, mix_bcd, w4, x, gate1, gn2, shift, scale, gate2, w1, w2, rows_per_mod,
            final_gain=None):
    r, d = x.shape
    g = w4.shape[1]
    assert w1.shape[1] % MLP_CHUNK == 0
    tm = _tile(rows_per_mod, 512)
    tpb = rows_per_mod // tm
    rows = lambda n: pl.BlockSpec((tm, n), lambda i: (i, 0))
    tiles = pl.BlockSpec((tm // SCAN_SUB, HEAD_DIM, g), lambda i: (i, 0, 0))
    split = pl.BlockSpec((g // LANES, tm, LANES), lambda i: (0, i, 0))
    full2 = lambda a: pl.BlockSpec(a.shape, lambda i: (0,) * a.ndim)
    resident = lambda a: pl.BlockSpec(a.shape, lambda i: (0,) * a.ndim, pipeline_mode=pl.Buffered(1))
    modspec = pl.BlockSpec((1, 1, d), lambda i: (i // tpb, 0, 0))
    final = () if final_gain is None else (final_gain,)
    return pl.pallas_call(
        _mix_mlp_kernel,
        grid=(r // tm,),
        in_specs=[tiles, tiles, rows(3 * g), split, split] + [full2(p) for p in readout_params]
                 + [rows(g), rows(g), rows(g), resident(w4), rows(d), modspec,
                    pl.BlockSpec((1, d), lambda i: (0, 0)), modspec, modspec, modspec, resident(w1), resident(w2)]
                 + [full2(p) for p in final],
        out_specs=rows(d),
        out_shape=jax.ShapeDtypeStruct((r, d), F32),
        compiler_params=_params(("arbitrary",)),
        name="mix_mlp",
    )(yf, yb, pa, km_f, km_b, *readout_params, *mix_bcd, w4, x, gate1, gn2, shift, scale, gate2, w1, w2, *final)


def _pad_heads(q, n_heads, kv_width, rep):
    assert kv_width % LANES == 0
    t = q.shape[0]
    lane = lax.broadcasted_iota(jnp.int32, (t, LANES), 1)
    lo_half = lane < HEAD_DIM
    zero = jnp.zeros((t, LANES), q.dtype)
    rows = []
    for h in range(n_heads):
        src = h * HEAD_DIM
        dst = (h // rep) * HEAD_DIM
        piece = q[:, src // LANES * LANES:(src // LANES + 1) * LANES]
        if src % LANES != dst % LANES:
            piece = pltpu.roll(piece.astype(F32), HEAD_DIM, 1).astype(q.dtype)
        piece = jnp.where(lo_half if dst % LANES == 0 else ~lo_half, piece, zero)
        blocks = [piece if j == dst // LANES else zero for j in range(kv_width // LANES)]
        rows.append(jnp.concatenate(blocks, axis=1) if len(blocks) > 1 else piece)
    return jnp.concatenate(rows, axis=0)


def _gather_heads(res, n_heads, rep):
    t = res.shape[0] // n_heads
    lane = lax.broadcasted_iota(jnp.int32, (t, LANES), 1)
    lo_half = lane < HEAD_DIM
    pieces = []
    for h in range(n_heads):
        src = (h // rep) * HEAD_DIM
        dst = h * HEAD_DIM
        piece = res[h * t:(h + 1) * t, src // LANES * LANES:(src // LANES + 1) * LANES]
        if src % LANES != dst % LANES:
            piece = pltpu.roll(piece, HEAD_DIM, 1)
        pieces.append(piece)
    blocks = [jnp.where(lo_half, pieces[2 * j], pieces[2 * j + 1]) for j in range(n_heads // 2)]
    return jnp.concatenate(blocks, axis=1)


def _attn_kernel(q_ref, k_ref, v_ref, o_ref, *, rep, base2):
    exp = jnp.exp2 if base2 else jnp.exp
    n_heads = q_ref.shape[1] // HEAD_DIM
    lk = k_ref.shape[1]
    q = _pad_heads(q_ref[...], n_heads, k_ref.shape[2], rep)
    edges = list(range(0, lk, ATTN_CHUNK)) + [lk]
    m = l = acc = None
    for k0, k1 in zip(edges[:-1], edges[1:]):
        s = _dot_nt(q, k_ref[0, k0:k1, :])
        m_new = jnp.max(s, axis=-1, keepdims=True)
        if m is not None:
            m_new = jnp.maximum(m, m_new)
        p = exp(s - m_new)
        l_new = jnp.sum(p, axis=-1, keepdims=True)
        acc_new = _dot(p.astype(BF16), v_ref[0, k0:k1, :])
        if m is not None:
            alpha = exp(m - m_new)
            l_new = alpha * l + l_new
            acc_new = alpha * acc + acc_new
        m, l, acc = m_new, l_new, acc_new
    o_ref[...] = _gather_heads(acc / l, n_heads, rep).astype(o_ref.dtype)


def attention(q, k, v, seq_len, base2):
    r, gq = q.shape
    nb, lk, kw = k.shape
    rep = gq // kw
    tq = _tile(seq_len, 128)
    nq = seq_len // tq
    return pl.pallas_call(
        functools.partial(_attn_kernel, rep=rep, base2=base2),
        grid=(nb, nq),
        in_specs=[pl.BlockSpec((tq, gq), lambda i, j: (i * nq + j, 0)),
                  pl.BlockSpec((1, lk, kw), lambda i, j: (i, 0, 0)),
                  pl.BlockSpec((1, lk, kw), lambda i, j: (i, 0, 0))],
        out_specs=pl.BlockSpec((tq, gq), lambda i, j: (i * nq + j, 0)),
        out_shape=jax.ShapeDtypeStruct((r, gq), BF16),
        compiler_params=_params(("arbitrary", "arbitrary")),
        name="attention",
    )(q, k, v)


def _col_reduce(x, op):
    n, c = x.shape
    part = op(x.reshape(n // REDUCE_SLAB, REDUCE_SLAB, c), axis=0)
    return op(part, axis=0, keepdims=True)


def _attn_t_kernel(q_ref, k_ref, vt_ref, o_ref, *, rep):
    n_heads = q_ref.shape[1] // HEAD_DIM
    tq = q_ref.shape[0] // ATTN_STREAMS
    lk = k_ref.shape[1]
    kw = k_ref.shape[2]
    w = rep * tq
    qs = [_pad_heads(q_ref[st * tq:(st + 1) * tq, :], n_heads, kw, rep) for st in range(ATTN_STREAMS)]
    edges = list(range(0, lk, ATTN_T_CHUNK)) + [lk]
    chunks = list(zip(edges[:-1], edges[1:]))
    scores = lambda st, c: _dot_nt(k_ref[0, c[0]:c[1], :], qs[st])
    m = [None] * ATTN_STREAMS
    acc = [None] * ATTN_STREAMS
    s_next = [scores(st, chunks[0]) for st in range(ATTN_STREAMS)]
    for i, (k0, k1) in enumerate(chunks):
        for st in range(ATTN_STREAMS):
            s = s_next[st]
            m_new = _col_reduce(s, jnp.max)
            if m[st] is not None:
                m_new = jnp.maximum(m[st], m_new)
            p = jnp.exp2(s - m_new).astype(BF16)
            acc_new = _dot(vt_ref[0, :, k0:k1], p)
            if i + 1 < len(chunks):
                s_next[st] = scores(st, chunks[i + 1])
            if m[st] is not None:
                acc_new = jnp.exp2(m[st] - m_new) * acc[st] + acc_new
            m[st], acc[st] = m_new, acc_new
    lo_half = lax.broadcasted_iota(jnp.int32, (tq, LANES), 1) < HEAD_DIM
    for st in range(ATTN_STREAMS):
        a = acc[st]
        res = jnp.concatenate([a[j * HEAD_DIM:(j + 1) * HEAD_DIM, j * w:(j + 1) * w] / a[kw:kw + 1, j * w:(j + 1) * w]
                               for j in range(n_heads // rep)], axis=0)
        t0 = res[:, :tq].T
        t1 = res[:, tq:].T
        out = jnp.concatenate([jnp.where(lo_half, t0, pltpu.roll(t1, HEAD_DIM, 1)),
                               jnp.where(lo_half, pltpu.roll(t0, HEAD_DIM, 1), t1)], axis=1)
        o_ref[st * tq:(st + 1) * tq, :] = out.astype(o_ref.dtype)


def attention_t(q, k, vt, seq_len):
    r, gq = q.shape
    nb, lk, kw = k.shape
    assert gq == 4 * HEAD_DIM and kw == 2 * HEAD_DIM
    tq = ATTN_STREAMS * LANES
    assert seq_len % tq == 0
    nq = seq_len // tq
    return pl.pallas_call(
        functools.partial(_attn_t_kernel, rep=2),
        grid=(nb, nq),
        in_specs=[pl.BlockSpec((tq, gq), lambda i, j: (i * nq + j, 0)),
                  pl.BlockSpec((1, lk, kw), lambda i, j: (i, 0, 0)),
                  pl.BlockSpec((1, vt.shape[1], lk), lambda i, j: (i, 0, 0))],
        out_specs=pl.BlockSpec((tq, gq), lambda i, j: (i * nq + j, 0)),
        out_shape=jax.ShapeDtypeStruct((r, gq), BF16),
        compiler_params=_params(("arbitrary", "arbitrary")),
        name="attention_t",
    )(q, k, vt)


def _na_kernel(p_ref, pc_ref, bias_ref, o_ref, *, rows, kh):
    g = o_ref.shape[2]
    n_heads = g // HEAD_DIM
    band = kh * GRID_W
    kc = pc_ref[0, :, g:2 * g]
    vc = pc_ref[0, :, 2 * g:3 * g]

    def one_row(r, carry):
        rs = jnp.clip(r - kh // 2, 0, rows - kh)
        q0 = pl.multiple_of(r * GRID_W, GRID_W)
        k0 = pl.multiple_of(rs * GRID_W, GRID_W)
        q = _pad_heads(p_ref[0, pl.ds(q0, GRID_W), 0:g], n_heads, g, 1)
        s1 = _dot_nt(q, p_ref[0, pl.ds(k0, band), g:2 * g]) + bias_ref[r - rs]
        s2 = _dot_nt(q, kc)
        m = jnp.maximum(jnp.max(s1, axis=-1, keepdims=True), jnp.max(s2, axis=-1, keepdims=True))
        p1 = jnp.exp(s1 - m)
        p2 = jnp.exp(s2 - m)
        l = jnp.sum(p1, axis=-1, keepdims=True) + jnp.sum(p2, axis=-1, keepdims=True)
        res = (_dot(p1.astype(BF16), p_ref[0, pl.ds(k0, band), 2 * g:3 * g]) + _dot(p2.astype(BF16), vc)) / l
        o_ref[0, pl.ds(q0, GRID_W), :] = _gather_heads(res, n_heads, 1).astype(o_ref.dtype)
        return carry

    lax.fori_loop(0, rows, one_row, 0, unroll=8)


def neighbourhood_attention(p, pc, bias):
    nb, l, g3 = p.shape
    g = g3 // 3
    lc = pc.shape[1]
    rows = l // GRID_W
    kh = min(NA_ROWS, rows)
    return pl.pallas_call(
        functools.partial(_na_kernel, rows=rows, kh=kh),
        grid=(nb,),
        in_specs=[pl.BlockSpec((1, l, g3), lambda i: (i, 0, 0)),
                  pl.BlockSpec((1, lc, g3), lambda i: (i, 0, 0)),
                  pl.BlockSpec(bias.shape, lambda i: (0, 0, 0))],
        out_specs=pl.BlockSpec((1, l, g), lambda i: (i, 0, 0)),
        out_shape=jax.ShapeDtypeStruct((nb, l, g), BF16),
        compiler_params=_params(("arbitrary",)),
        name="neighbourhood_attention",
    )(p, pc, bias)


def _na_bias_kernel(rpb_ref, o_ref, *, kh):
    h = pl.program_id(0)
    q = lax.broadcasted_iota(jnp.int32, (GRID_W, GRID_W), 0)
    k = lax.broadcasted_iota(jnp.int32, (GRID_W, GRID_W), 1)
    start = jnp.clip(q - NA_COLS // 2, 0, GRID_W - NA_COLS)
    in_win = (k >= start) & (k < start + NA_COLS)
    off = k - q + NA_COLS - 1
    neg = jnp.full((GRID_W, GRID_W), -jnp.inf, F32)
    blocks = []
    for ro in range(2 * NA_ROWS - 1):
        t = neg
        for c in range(2 * NA_COLS - 1):
            t = jnp.where(off == c, rpb_ref[h, ro, c], t)
        blocks.append(jnp.where(in_win, t, neg))
    for di in range(kh):
        for i in range(kh):
            o_ref[di, :, i * GRID_W:(i + 1) * GRID_W] = blocks[i - di + NA_ROWS - 1]


def na_bias_table(rpb, rows):
    kh = min(NA_ROWS, rows)
    nh = rpb.shape[0]
    return pl.pallas_call(
        functools.partial(_na_bias_kernel, kh=kh),
        grid=(nh,),
        in_specs=[pl.BlockSpec(memory_space=pltpu.SMEM)],
        out_specs=pl.BlockSpec((kh, GRID_W, kh * GRID_W), lambda i: (0, i, 0)),
        out_shape=jax.ShapeDtypeStruct((kh, nh * GRID_W, kh * GRID_W), F32),
        compiler_params=_params(("arbitrary",)),
        name="na_bias",
    )(rpb)


def _pool_kernel(x_ref, inv_ref, w_ref, scale_ref, o_ref):
    x = x_ref[0]
    n, g = x.shape
    pg = g // len(POOL_WINDOWS)
    pad = max(POOL_WINDOWS) // 2
    zeros = jnp.zeros((pad, g), F32)
    xe = jnp.concatenate([zeros, x, zeros], axis=0)
    ne = n + 2 * pad
    group = lax.broadcasted_iota(jnp.int32, xe.shape, 1) // pg
    fwd = xe
    bwd = xe
    cur = 1
    total = jnp.zeros_like(xe)
    for j, w in enumerate(POOL_WINDOWS):
        half = w // 2
        while cur < half:
            fwd = fwd + pltpu.roll(fwd, ne - cur, 0)
            bwd = bwd + pltpu.roll(bwd, cur, 0)
            cur *= 2
        total = jnp.where(group == j, pltpu.roll(bwd, 1, 0) + fwd, total)
    diff = total[pad:pad + n] * inv_ref[...] - x
    o_ref[0] = (_dot(diff.astype(BF16), w_ref[...]) * scale_ref[...]).astype(o_ref.dtype)


def _pool_inverse_counts(n, g):
    t = np.arange(n)
    cols = []
    for w in POOL_WINDOWS:
        lo = np.clip(t - w // 2, 0, n)
        hi = np.clip(t - w // 2 + w, 0, n)
        cols.append(np.repeat((1.0 / (hi - lo))[:, None], g // len(POOL_WINDOWS), axis=1))
    return jnp.asarray(np.concatenate(cols, axis=1), dtype=F32)


def pool_mixer(p, w_bd, scale):
    b, n, g = p.shape
    return pl.pallas_call(
        _pool_kernel,
        grid=(b,),
        in_specs=[pl.BlockSpec((1, n, g), lambda i: (i, 0, 0)),
                  pl.BlockSpec((n, g), lambda i: (0, 0)),
                  pl.BlockSpec((g, g), lambda i: (0, 0)),
                  pl.BlockSpec((1, g), lambda i: (0, 0))],
        out_specs=pl.BlockSpec((1, n, g), lambda i: (i, 0, 0)),
        out_shape=jax.ShapeDtypeStruct((b, n, g), BF16),
        compiler_params=_params(("arbitrary",)),
        name="pool_mixer",
    )(p, _pool_inverse_counts(n, g), w_bd, scale)


def _wkv_scan_kernel(rf_ref, rb_ref, kkf_ref, kkb_ref, decf_ref, bf_ref, kmf_ref, decb_ref, bb_ref, kmb_ref,
                     vtf_ref, vtb_ref, e_ref, s0_ref, yf_ref, yb_ref, send_ref, s_scr):
    npair, nb, t_blk, _ = rf_ref.shape
    g = npair * LANES
    step_id = pl.program_id(0)

    @pl.when(step_id == 0)
    def _():
        s_scr[...] = s0_ref[...]

    yf_ref[...] = jnp.zeros_like(yf_ref)
    yb_ref[...] = jnp.zeros_like(yb_ref)
    lane = lax.broadcasted_iota(jnp.int32, (HEAD_DIM, g), 1) % SCAN_SUB
    lane128 = lax.broadcasted_iota(jnp.int32, (HEAD_DIM, LANES), 1)
    dirs = ((rf_ref, kkf_ref, decf_ref, bf_ref, kmf_ref, vtf_ref, yf_ref),
            (rb_ref, kkb_ref, decb_ref, bb_ref, kmb_ref, vtb_ref, yb_ref))
    ents = HEAD_DIM // 4
    for d in range(2):
        pltpu.matmul_push_rhs(e_ref[...], 0, d)
        pltpu.matmul_acc_lhs(0, jnp.zeros((16, g), BF16), d, load_staged_rhs=0)
        pltpu.matmul_pop(0, (16, g), F32, d)

    def write_y(d, b, y, tt_done, valid):
        y_ref = dirs[d][-1]
        tl = tt_done if d == 0 else t_blk - 1 - tt_done
        tl = jnp.clip(tl, 0, t_blk - 1)
        hit = (lane == tl % SCAN_SUB) & valid
        pltpu.store(y_ref.at[b, tl // SCAN_SUB], y, mask=hit)

    def rows_at(d, tt):
        tl = jnp.clip(tt if d == 0 else t_blk - 1 - tt, 0, t_blk - 1)

        def row(ref, b):
            slabs = [ref[j, b, pl.ds(tl, 8, stride=0), :] for j in range(npair)]
            return jnp.concatenate([jnp.concatenate(slabs, axis=1)] * (HEAD_DIM // 8), axis=0)

        return row

    for d, (r_ref, kk_ref, dec_ref, b_ref, km_ref, vt_ref, y_ref) in enumerate(dirs):
        row0 = rows_at(d, 0)
        for b in range(nb):
            pltpu.matmul_acc_lhs(b * ents, s_scr[d, b].astype(BF16) * row0(kk_ref, b).astype(BF16), d)

    def one_token(tt, carry):
        for b in range(nb):
            for d, (r_ref, kk_ref, dec_ref, b_ref, km_ref, vt_ref, y_ref) in enumerate(dirs):
                tl = tt if d == 0 else t_blk - 1 - tt
                row = rows_at(d, tt)
                nxt = rows_at(d, tt + 1)
                idx = (lane128 // SCAN_SUB) * SCAN_SUB + tl % SCAN_SUB
                sa = pltpu.matmul_pop(b * ents, (HEAD_DIM, g), F32, d)
                write_y(d, b, pltpu.matmul_pop((nb + b) * ents, (HEAD_DIM, g), F32, d), tt - 1, tt > 0)
                vcol = jnp.concatenate([jnp.take_along_axis(vt_ref[b, tl // SCAN_SUB][:, j * LANES:(j + 1) * LANES], idx, axis=1)
                                        for j in range(g // LANES)], axis=1)
                s = s_scr[d, b] * row(dec_ref, b) - sa * row(b_ref, b) + vcol * row(km_ref, b)
                s_scr[d, b] = s
                s_bf = s.astype(BF16)
                pltpu.matmul_acc_lhs((nb + b) * ents, s_bf * row(r_ref, b).astype(BF16), d)
                pltpu.matmul_acc_lhs(b * ents, s_bf * nxt(kk_ref, b).astype(BF16), d)
        return carry

    lax.fori_loop(0, t_blk, one_token, 0, unroll=8)
    for d in range(2):
        for b in range(nb):
            write_y(d, b, pltpu.matmul_pop((nb + b) * ents, (HEAD_DIM, g), F32, d), t_blk - 1, True)
            pltpu.matmul_pop(b * ents, (HEAD_DIM, g), F32, d)

    @pl.when(step_id == pl.num_programs(0) - 1)
    def _():
        send_ref[...] = s_scr[...]


def wkv_scan(r, kk, dec_f, b_f, km_f, dec_b, b_b, km_b, vt, eseg, s0):
    npair, nb, l, _ = kk.shape
    g = npair * LANES
    t_blk = SCAN_BLOCK
    nsub = t_blk // SCAN_SUB
    nblk = l // t_blk
    assert l % t_blk == 0
    seq_f = pl.BlockSpec((npair, nb, t_blk, LANES), lambda i: (0, 0, i, 0))
    seq_b = pl.BlockSpec((npair, nb, t_blk, LANES), lambda i: (0, 0, nblk - 1 - i, 0))
    vt_f = pl.BlockSpec((nb, nsub, HEAD_DIM, g), lambda i: (0, i, 0, 0))
    vt_b = pl.BlockSpec((nb, nsub, HEAD_DIM, g), lambda i: (0, nblk - 1 - i, 0, 0))
    state = pl.BlockSpec(s0.shape, lambda i: (0, 0, 0, 0))
    return pl.pallas_call(
        _wkv_scan_kernel,
        grid=(nblk,),
        in_specs=[seq_f, seq_b, seq_f, seq_b, seq_f, seq_f, seq_f, seq_b, seq_b, seq_b,
                  vt_f, vt_b, pl.BlockSpec(eseg.shape, lambda i: (0, 0)), state],
        out_specs=[vt_f, vt_b, state],
        out_shape=[jax.ShapeDtypeStruct(vt.shape, F32), jax.ShapeDtypeStruct(vt.shape, F32),
                   jax.ShapeDtypeStruct(s0.shape, F32)],
        scratch_shapes=[pltpu.VMEM(s0.shape, F32)],
        compiler_params=_params(("arbitrary",)),
        name="wkv_scan",
    )(r, r, kk, kk, dec_f, b_f, km_f, dec_b, b_b, km_b, vt, vt, eseg, s0)


def _block_ones(n, seg):
    idx = np.arange(n) // seg
    return jnp.asarray(idx[:, None] == idx[None, :], dtype=BF16)


def _padded_rows(w, offset, n):
    return jnp.zeros((n, w.shape[1]), w.dtype).at[offset:offset + w.shape[0]].set(w)


def _rope_tables(n_tokens, n_rep):
    t = jnp.arange(n_tokens)
    row = (t // GRID_W).astype(F32)
    col = (t % GRID_W).astype(F32)
    n_freq = HEAD_DIM // 4
    inv_freq = ROPE_THETA ** (-jnp.arange(n_freq, dtype=F32) / n_freq)
    ang = jnp.concatenate([row[:, None] * inv_freq, col[:, None] * inv_freq], axis=-1)
    cos, sin = jnp.cos(ang), jnp.sin(ang)
    cos_h = jnp.concatenate([cos, cos], axis=-1)
    sin_h = jnp.concatenate([-sin, sin], axis=-1)
    return jnp.tile(cos_h, (1, n_rep)), jnp.tile(sin_h, (1, n_rep))


def kernel(x, c, ctx, c_ctx, ada_w, ada_b, norm1_g, norm2_g, w_in, w_out, rwkv_w0, rwkv_w2, rwkv_a0, rwkv_a2,
           rwkv_k_k, rwkv_k_a, rwkv_r_k, rwkv_g2, rwkv_gn_w, rwkv_gn_b, na_rpb, gqa_q_gain, gqa_k_gain, pool_w,
           pool_scale, mlp_w1, mlp_w2, final_g):
    nb, l, d = x.shape
    lc = ctx.shape[1]
    depth = ada_w.shape[0]
    g = d // 4
    nh = g // HEAD_DIM
    nkv = nh // 2
    kvw = nkv * HEAD_DIM
    dr = rwkv_w2.shape[2]
    ir = rwkv_a2.shape[2]
    gr = rwkv_g2.shape[1]
    assert 2 * dr + 2 * ir + gr <= g and nb + 1 <= 8
    rows = l // GRID_W

    cpad = jnp.zeros((8, d), F32).at[:nb].set(c).at[nb].set(c_ctx)
    mods = adaln(cpad, ada_w, ada_b)

    eseg = _block_ones(g, HEAD_DIM)
    ekv = _block_ones(kvw, HEAD_DIM)
    cos_q, sin_q = _rope_tables(l, nh)
    ones_c, zeros_c = jnp.ones((lc, g), F32), jnp.zeros((lc, g), F32)

    splits = np.cumsum([0, g, g, g, dr, dr, ir, ir, gr, g, g, g, g, kvw, kvw, g])
    lowrank_w = splits[8] - splits[3]

    xl = x.reshape(nb * l, d)
    xc = ctx.reshape(nb * lc, d)
    zero_state = jnp.zeros((2, nb, HEAD_DIM, g), F32)

    for i in range(depth):
        need_ctx_out = i < depth - 1
        mod_l = mods[i, :nb].reshape(nb, N_MOD, 1, d)
        mod_c = mods[i, nb].reshape(N_MOD, 1, 1, d)
        ml = [mod_l[:, k] for k in range(N_MOD)]
        mc = [mod_c[k] for k in range(N_MOD)]

        wi = w_in[i]
        w_inp = jnp.concatenate([wi[:, :splits[8]], jnp.zeros((d, g - lowrank_w), F32),
                                 wi[:, splits[8]:splits[9]] * HEAD_DIM ** -0.5, wi[:, splits[9]:]], axis=1).astype(BF16)
        g1 = norm1_g[i].reshape(1, d)
        g2 = norm2_g[i].reshape(1, d)

        w2p = jnp.stack([_padded_rows(rwkv_w2[i, 0], 0, g), _padded_rows(rwkv_w2[i, 1], dr, g)]).astype(BF16)
        a2p = jnp.stack([_padded_rows(rwkv_a2[i, 0], 2 * dr, g),
                         _padded_rows(rwkv_a2[i, 1], 2 * dr + ir, g)]).astype(BF16)
        g2p = _padded_rows(rwkv_g2[i], 2 * dr + 2 * ir, g).astype(BF16)
        r_k = rwkv_r_k[i].reshape(1, g)
        gn_w = rwkv_gn_w[i].reshape(1, g)
        gn_b = rwkv_gn_b[i].reshape(1, g)
        rwkv_params = (rwkv_w0[i], rwkv_a0[i], w2p, a2p, rwkv_k_k[i].reshape(1, g), rwkv_k_a[i].reshape(1, g), eseg)
        q_gain = jnp.tile(gqa_q_gain[i], nh).reshape(1, g)
        k_gain = jnp.tile(gqa_k_gain[i], nkv).reshape(1, kvw)

        proj_l = inproj(xl, g1, ml[0], ml[1], w_inp, l, rwkv_params, (q_gain, k_gain, cos_q, sin_q, ekv), l, True)
        proj_c = inproj(xc, g1, mc[0], mc[1], w_inp, nb * lc, rwkv_params, (q_gain, k_gain, ones_c, zeros_c, ekv), lc, False)
        pa_l, pb_l, pd_l = proj_l[:3]
        pa_c, pb_c, pd_c = proj_c[:3]

        def scan_inputs(proj, seq):
            rows4 = [t.reshape(g // LANES, nb, seq, LANES) for t in proj[3:11]]
            return rows4, proj[11].reshape(nb, seq // SCAN_SUB, HEAD_DIM, g)

        rows4_c, vt_c = scan_inputs(proj_c, lc)
        yf_c, yb_c, state_c = wkv_scan(*rows4_c, vt_c, eseg, zero_state)
        rows4_l, vt_l = scan_inputs(proj_l, l)
        yf_l, yb_l, _ = wkv_scan(*rows4_l, vt_l, eseg, state_c)

        bias = na_bias_table(na_rpb[i], rows)
        b_l = neighbourhood_attention(pb_l.reshape(nb, l, 3 * g), pb_c.reshape(nb, lc, 3 * g), bias).reshape(nb * l, g)

        qc_l, kc_l, vc_l = proj_l[12:15]
        qc_c, kc_c, vc_c = proj_c[12:15]
        kc_c, vc_c = kc_c.reshape(nb, lc, kvw), vc_c.reshape(nb, lc, kvw)
        k_all = jnp.concatenate([kc_c, kc_l.reshape(nb, l, kvw)], axis=1)
        v_all = jnp.concatenate([vc_c, vc_l.reshape(nb, l, kvw)], axis=1)
        vt_all = jnp.concatenate([jnp.swapaxes(v_all, 1, 2), jnp.ones((nb, 16, lc + l), BF16)], axis=1)
        c_l = attention_t(qc_l, k_all, vt_all, l)

        w_bd = jax.scipy.linalg.block_diag(*[pool_w[i, k] for k in range(len(POOL_WINDOWS))]).astype(BF16)
        p_scale = pool_scale[i].reshape(1, g)
        d_l = pool_mixer(pd_l.reshape(nb, l, g), w_bd, p_scale).reshape(nb * l, g)

        w_out4 = w_out[i].reshape(4, g, d).astype(BF16)
        w1 = mlp_w1[i].astype(BF16)
        w2 = mlp_w2[i].astype(BF16)
        readout_params = (r_k, g2p, gn_w, gn_b, eseg)
        xl = mix_mlp(yf_l.reshape(-1, HEAD_DIM, g), yb_l.reshape(-1, HEAD_DIM, g), pa_l, proj_l[7], proj_l[10],
                     readout_params, (b_l, c_l, d_l), w_out4, xl, ml[2], g2, ml[3], ml[4], ml[5], w1, w2, l,
                     final_gain=None if need_ctx_out else final_g.reshape(1, d))

        if need_ctx_out:
            b_c = attention(pb_c[:, :g], pb_c[:, g:2 * g].reshape(nb, lc, g), pb_c[:, 2 * g:].reshape(nb, lc, g), lc, False)
            c_c = attention(qc_c, kc_c, vc_c, lc, True)
            d_c = pool_mixer(pd_c.reshape(nb, lc, g), w_bd, p_scale).reshape(nb * lc, g)
            xc = mix_mlp(yf_c.reshape(-1, HEAD_DIM, g), yb_c.reshape(-1, HEAD_DIM, g), pa_c, proj_c[7], proj_c[10],
                         readout_params, (b_c, c_c, d_c), w_out4, xc, mc[2], g2, mc[3], mc[4], mc[5], w1, w2, nb * lc)

    return xl.reshape(nb, l, d)
```

```python
import functools

import jax
import jax.numpy as jnp
import numpy as np
from jax import lax
from jax.experimental import pallas as pl
from jax.experimental.pallas import tpu as pltpu

F32 = jnp.float32
BF16 = jnp.bfloat16

HEAD_DIM = 64
GRID_W = 64
NA_ROWS = 8
NA_COLS = 16
ROPE_THETA = 10000.0
POOL_WINDOWS = (2, 4, 8, 16)
NORM_EPS = 1e-6
RWKV_GN_EPS = 1e-5 * HEAD_DIM
N_MOD = 6
LANES = 128
SCAN_BLOCK = 128
SCAN_SUB = 64
LOG2E = 1.4426950408889634
ATTN_CHUNK = 2048
ATTN_T_CHUNK = 256
ATTN_STREAMS = 8
ONES_ROWS = 16
REDUCE_SLAB = 256
MLP_CHUNK = 1024
VMEM_LIMIT = 52 * 1024 * 1024


def _params(sem):
    return pltpu.CompilerParams(dimension_semantics=sem, vmem_limit_bytes=VMEM_LIMIT)


def _tile(n, pref):
    t = min(n, pref)
    assert n % t == 0, (n, pref)
    return t


def _dot(a, b):
    return jnp.dot(a, b, preferred_element_type=F32)


def _dot_nt(a, b):
    return lax.dot_general(a, b, (((1,), (1,)), ((), ())), preferred_element_type=F32)


def _seg_sum(x, e):
    hi = x.astype(BF16)
    lo = (x - hi.astype(F32)).astype(BF16)
    return _dot(hi, e) + _dot(lo, e)


def _norm_mod(x, g, shift, scale):
    ms = jnp.mean(x * x, axis=-1, keepdims=True)
    h = x * lax.rsqrt(ms + NORM_EPS) * g
    return h * (1.0 + scale) + shift


def _adaln_kernel(c_ref, w_ref, b_ref, o_ref):
    c = c_ref[...]
    s = c * jax.nn.sigmoid(c)
    o_ref[0] = jnp.dot(s, w_ref[0], preferred_element_type=F32, precision=lax.Precision.HIGHEST) + b_ref[0]


def adaln(cpad, ada_w, ada_b):
    depth, d, n = ada_w.shape
    tn = _tile(n, 1536)
    return pl.pallas_call(
        _adaln_kernel,
        grid=(depth, n // tn),
        in_specs=[pl.BlockSpec((8, d), lambda i, j: (0, 0)),
                  pl.BlockSpec((1, d, tn), lambda i, j: (i, 0, j)),
                  pl.BlockSpec((1, 1, tn), lambda i, j: (i, 0, j))],
        out_specs=pl.BlockSpec((1, 8, tn), lambda i, j: (i, 0, j)),
        out_shape=jax.ShapeDtypeStruct((depth, 8, n), F32),
        compiler_params=_params(("arbitrary", "arbitrary")),
        name="adaln",
    )(cpad, ada_w, ada_b.reshape(depth, 1, n))


def _swap_halves(y):
    n = y.shape[-1]
    lane = lax.broadcasted_iota(jnp.int32, y.shape, 1)
    half = HEAD_DIM // 2
    return jnp.where(lane % HEAD_DIM < half, pltpu.roll(y, n - half, 1), pltpu.roll(y, half, 1))


def _to_tiles(x):
    g = x.shape[1]
    tiles = []
    for s in range(x.shape[0] // SCAN_SUB):
        blk = x[s * SCAN_SUB:(s + 1) * SCAN_SUB, :]
        tiles.append(jnp.concatenate([blk[:, h * HEAD_DIM:(h + 1) * HEAD_DIM].T for h in range(g // HEAD_DIM)], axis=1))
    return tiles


def _from_tiles(tiles):
    g = tiles[0].shape[1]
    return jnp.concatenate(
        [jnp.concatenate([t[:, h * SCAN_SUB:(h + 1) * SCAN_SUB].T for h in range(g // HEAD_DIM)], axis=1) for t in tiles],
        axis=0)


def _split_store(o_ref, val):
    for j in range(o_ref.shape[0]):
        o_ref[j] = val[:, j * LANES:(j + 1) * LANES]


def _split_load(ref):
    return jnp.concatenate([ref[j] for j in range(ref.shape[0])], axis=1)


def _inproj_kernel(x_ref, g_ref, sh_ref, sc_ref, w_ref,
                   w0_ref, a0_ref, w2_ref, a2_ref, kk_w_ref, ka_ref, e_ref,
                   qg_ref, kg_ref, cos_ref, sin_ref, ek_ref,
                   oa_ref, ob_ref, od_ref,
                   r_ref, kk_ref, decf_ref, bf_ref, kmf_ref, decb_ref, bb_ref, kmb_ref, vt_ref,
                   q_ref, kc_ref, vc_ref, vct_ref, *, rope):
    g = od_ref.shape[1]
    gk = kc_ref.shape[1]
    h = _norm_mod(x_ref[...], g_ref[...], sh_ref[0], sc_ref[0]).astype(BF16)
    e = e_ref[...]

    pa = _dot(h, w_ref[:, 0:4 * g])
    pc = _dot(h, w_ref[:, 7 * g:9 * g])

    oa_ref[...] = jnp.concatenate([pa[:, :g], pa[:, 2 * g:]], axis=1)
    _split_store(r_ref, pa[:, :g])
    k = pa[:, g:2 * g]
    lr = pa[:, 3 * g:4 * g]
    for s, tile in enumerate(_to_tiles(pa[:, 2 * g:3 * g])):
        vt_ref[s] = tile
    kx = k * kk_w_ref[...]
    norm = jnp.sqrt(_seg_sum(kx * kx, e))
    kk = kx / jnp.maximum(norm, 1e-12)
    _split_store(kk_ref, kk)
    lr_t = jnp.tanh(lr).astype(BF16)
    lr_b = lr.astype(BF16)
    for d, (dec_ref, b_ref, km_ref) in enumerate(((decf_ref, bf_ref, kmf_ref), (decb_ref, bb_ref, kmb_ref))):
        z = w0_ref[d:d + 1, :] + _dot(lr_t, w2_ref[d])
        softplus_neg = jnp.maximum(-z, 0.0) + jnp.log1p(jnp.exp(-jnp.abs(z)))
        w = -softplus_neg - 0.5
        _split_store(dec_ref, jnp.exp(-jnp.exp(w)))
        a = jax.nn.sigmoid(a0_ref[d:d + 1, :] + _dot(lr_b, a2_ref[d]))
        _split_store(b_ref, kk * a)
        _split_store(km_ref, k * (1.0 + (a - 1.0) * ka_ref[...]))

    def normed(x, gain, seg):
        ms = _seg_sum(x * x, seg) * (1.0 / HEAD_DIM)
        return x * lax.rsqrt(ms + NORM_EPS) * gain

    q = normed(pc[:, :g], qg_ref[...], e)
    kc = normed(pc[:, g:g + gk], kg_ref[...], ek_ref[...])
    if rope:
        cos = cos_ref[...]
        sin = sin_ref[...]
        q = q * cos + _swap_halves(q) * sin
        kc = kc * cos[:, :gk] + _swap_halves(kc) * sin[:, :gk]
    q_ref[...] = (q * (HEAD_DIM ** -0.5 * LOG2E)).astype(BF16)
    kc_ref[...] = kc.astype(BF16)
    vc_ref[...] = pc[:, g + gk:].astype(BF16)
    vct_ref[0, :gk, :] = pc[:, g + gk:].T.astype(BF16)
    vct_ref[0, gk:, :] = jnp.ones((ONES_ROWS, vct_ref.shape[2]), BF16)

    ob_ref[...] = _dot(h, w_ref[:, 4 * g:7 * g]).astype(ob_ref.dtype)

    od_ref[...] = _dot(h, w_ref[:, 9 * g:10 * g])


def inproj(x, gain, shift, scale, w, rows_per_mod, rwkv_params, gqa_params, seq_len, rope):
    r, d = x.shape
    g = w.shape[1] // 10
    gk = g // 2
    tm = _tile(min(rows_per_mod, seq_len), 512)
    tpb = rows_per_mod // tm
    nseq = seq_len // tm
    modspec = pl.BlockSpec((1, 1, d), lambda i: (i // tpb, 0, 0))
    full2 = lambda a: pl.BlockSpec(a.shape, lambda i: (0,) * a.ndim)
    rows = lambda n: pl.BlockSpec((tm, n), lambda i: (i, 0))
    q_gain, k_gain, cos, sin, ekv = gqa_params
    table = pl.BlockSpec((tm, g), lambda i: (i % nseq, 0))
    split = pl.BlockSpec((g // LANES, tm, LANES), lambda i: (0, i, 0))
    out_specs = ([rows(3 * g), rows(3 * g), rows(g)] + [split] * 8
                 + [pl.BlockSpec((tm // SCAN_SUB, HEAD_DIM, g), lambda i: (i, 0, 0))] + [rows(g), rows(gk), rows(gk)]
                 + [pl.BlockSpec((1, gk + ONES_ROWS, tm), lambda i: (i // nseq, 0, i % nseq))])
    out_shape = ([jax.ShapeDtypeStruct((r, 3 * g), F32), jax.ShapeDtypeStruct((r, 3 * g), BF16),
                  jax.ShapeDtypeStruct((r, g), F32)] + [jax.ShapeDtypeStruct((g // LANES, r, LANES), F32)] * 8
                 + [jax.ShapeDtypeStruct((r // SCAN_SUB, HEAD_DIM, g), F32), jax.ShapeDtypeStruct((r, g), BF16),
                    jax.ShapeDtypeStruct((r, gk), BF16), jax.ShapeDtypeStruct((r, gk), BF16),
                    jax.ShapeDtypeStruct((r // seq_len, gk + ONES_ROWS, seq_len), BF16)])
    return pl.pallas_call(
        functools.partial(_inproj_kernel, rope=rope),
        grid=(r // tm,),
        in_specs=[rows(d), pl.BlockSpec((1, d), lambda i: (0, 0)), modspec, modspec, full2(w)]
                 + [full2(p) for p in rwkv_params] + [full2(q_gain), full2(k_gain), table, table, full2(ekv)],
        out_specs=out_specs,
        out_shape=out_shape,
        compiler_params=_params(("arbitrary",)),
        name="inproj",
    )(x, gain, shift, scale, w, *rwkv_params, q_gain, k_gain, cos, sin, ekv)


def _mix_mlp_kernel(yf_ref, yb_ref, p_ref, kmf_ref, kmb_ref, rk_ref, g2_ref, gnw_ref, gnb_ref, e_ref,
                    b_ref, c_ref, d_ref, wo_ref, x_ref, gate1_ref,
                    gn2_ref, sh_ref, sc_ref, gate2_ref, w1_ref, w2_ref, *rest):
    final_ref, o_ref = rest if len(rest) == 2 else (None, rest[0])
    g = b_ref.shape[1]
    e = e_ref[...]
    r = p_ref[:, :g]
    v = p_ref[:, g:2 * g]
    lr = p_ref[:, 2 * g:3 * g]
    y = _from_tiles([yf_ref[s] + yb_ref[s] for s in range(yf_ref.shape[0])])
    mu = _seg_sum(y, e) * (1.0 / HEAD_DIM)
    yc = y - mu
    var = _seg_sum(yc * yc, e) * (1.0 / HEAD_DIM)
    yn = yc * lax.rsqrt(var + RWKV_GN_EPS) * gnw_ref[...] + gnb_ref[...]
    bonus = _seg_sum(r * (_split_load(kmf_ref) + _split_load(kmb_ref)) * rk_ref[...], e) * v
    a = (yn + bonus) * _dot(jax.nn.sigmoid(lr).astype(BF16), g2_ref[...])
    acc = _dot(b_ref[...], wo_ref[1])
    acc += _dot(c_ref[...], wo_ref[2])
    acc += _dot(d_ref[...], wo_ref[3])
    acc += _dot(a.astype(BF16), wo_ref[0])
    x = x_ref[...] + gate1_ref[0] * acc
    h = _norm_mod(x, gn2_ref[...], sh_ref[0], sc_ref[0]).astype(BF16)
    acc = None
    for c0 in range(0, w1_ref.shape[1], MLP_CHUNK):
        u = jnp.maximum(_dot(h, w1_ref[:, c0:c0 + MLP_CHUNK]), 0.0)
        part = _dot((u * u).astype(BF16), w2_ref[c0:c0 + MLP_CHUNK, :])
        acc = part if acc is None else acc + part
    out = x + gate2_ref[0] * acc
    if final_ref is not None:
        ms = jnp.mean(out * out, axis=-1, keepdims=True)
        out = out * lax.rsqrt(ms + NORM_EPS) * final_ref[...]
    o_ref[...] = out


def mix_mlp(yf, yb, pa, km_f, km_b, readout_params, mix_bcd, w4, x, gate1, gn2, shift, scale, gate2, w1, w2, rows_per_mod,
            final_gain=None):
    r, d = x.shape
    g = w4.shape[1]
    assert w1.shape[1] % MLP_CHUNK == 0
    tm = _tile(rows_per_mod, 512)
    tpb = rows_per_mod // tm
    rows = lambda n: pl.BlockSpec((tm, n), lambda i: (i, 0))
    tiles = pl.BlockSpec((tm // SCAN_SUB, HEAD_DIM, g), lambda i: (i, 0, 0))
    split = pl.BlockSpec((g // LANES, tm, LANES), lambda i: (0, i, 0))
    full2 = lambda a: pl.BlockSpec(a.shape, lambda i: (0,) * a.ndim)
    resident = lambda a: pl.BlockSpec(a.shape, lambda i: (0,) * a.ndim, pipeline_mode=pl.Buffered(1))
    modspec = pl.BlockSpec((1, 1, d), lambda i: (i // tpb, 0, 0))
    final = () if final_gain is None else (final_gain,)
    return pl.pallas_call(
        _mix_mlp_kernel,
        grid=(r // tm,),
        in_specs=[tiles, tiles, rows(3 * g), split, split] + [full2(p) for p in readout_params]
                 + [rows(g), rows(g), rows(g), resident(w4), rows(d), modspec,
                    pl.BlockSpec((1, d), lambda i: (0, 0)), modspec, modspec, modspec, resident(w1), resident(w2)]
                 + [full2(p) for p in final],
        out_specs=rows(d),
        out_shape=jax.ShapeDtypeStruct((r, d), F32),
        compiler_params=_params(("arbitrary",)),
        name="mix_mlp",
    )(yf, yb, pa, km_f, km_b, *readout_params, *mix_bcd, w4, x, gate1, gn2, shift, scale, gate2, w1, w2, *final)


def _pad_heads(q, n_heads, kv_width, rep):
    assert kv_width % LANES == 0
    t = q.shape[0]
    lane = lax.broadcasted_iota(jnp.int32, (t, LANES), 1)
    lo_half = lane < HEAD_DIM
    zero = jnp.zeros((t, LANES), q.dtype)
    rows = []
    for h in range(n_heads):
        src = h * HEAD_DIM
        dst = (h // rep) * HEAD_DIM
        piece = q[:, src // LANES * LANES:(src // LANES + 1) * LANES]
        if src % LANES != dst % LANES:
            piece = pltpu.roll(piece.astype(F32), HEAD_DIM, 1).astype(q.dtype)
        piece = jnp.where(lo_half if dst % LANES == 0 else ~lo_half, piece, zero)
        blocks = [piece if j == dst // LANES else zero for j in range(kv_width // LANES)]
        rows.append(jnp.concatenate(blocks, axis=1) if len(blocks) > 1 else piece)
    return jnp.concatenate(rows, axis=0)


def _gather_heads(res, n_heads, rep):
    t = res.shape[0] // n_heads
    lane = lax.broadcasted_iota(jnp.int32, (t, LANES), 1)
    lo_half = lane < HEAD_DIM
    pieces = []
    for h in range(n_heads):
        src = (h // rep) * HEAD_DIM
        dst = h * HEAD_DIM
        piece = res[h * t:(h + 1) * t, src // LANES * LANES:(src // LANES + 1) * LANES]
        if src % LANES != dst % LANES:
            piece = pltpu.roll(piece, HEAD_DIM, 1)
        pieces.append(piece)
    blocks = [jnp.where(lo_half, pieces[2 * j], pieces[2 * j + 1]) for j in range(n_heads // 2)]
    return jnp.concatenate(blocks, axis=1)


def _attn_kernel(q_ref, k_ref, v_ref, o_ref, *, rep, base2):
    exp = jnp.exp2 if base2 else jnp.exp
    n_heads = q_ref.shape[1] // HEAD_DIM
    lk = k_ref.shape[1]
    q = _pad_heads(q_ref[...], n_heads, k_ref.shape[2], rep)
    edges = list(range(0, lk, ATTN_CHUNK)) + [lk]
    m = l = acc = None
    for k0, k1 in zip(edges[:-1], edges[1:]):
        s = _dot_nt(q, k_ref[0, k0:k1, :])
        m_new = jnp.max(s, axis=-1, keepdims=True)
        if m is not None:
            m_new = jnp.maximum(m, m_new)
        p = exp(s - m_new)
        l_new = jnp.sum(p, axis=-1, keepdims=True)
        acc_new = _dot(p.astype(BF16), v_ref[0, k0:k1, :])
        if m is not None:
            alpha = exp(m - m_new)
            l_new = alpha * l + l_new
            acc_new = alpha * acc + acc_new
        m, l, acc = m_new, l_new, acc_new
    o_ref[...] = _gather_heads(acc / l, n_heads, rep).astype(o_ref.dtype)


def attention(q, k, v, seq_len, base2):
    r, gq = q.shape
    nb, lk, kw = k.shape
    rep = gq // kw
    tq = _tile(seq_len, 128)
    nq = seq_len // tq
    return pl.pallas_call(
        functools.partial(_attn_kernel, rep=rep, base2=base2),
        grid=(nb, nq),
        in_specs=[pl.BlockSpec((tq, gq), lambda i, j: (i * nq + j, 0)),
                  pl.BlockSpec((1, lk, kw), lambda i, j: (i, 0, 0)),
                  pl.BlockSpec((1, lk, kw), lambda i, j: (i, 0, 0))],
        out_specs=pl.BlockSpec((tq, gq), lambda i, j: (i * nq + j, 0)),
        out_shape=jax.ShapeDtypeStruct((r, gq), BF16),
        compiler_params=_params(("arbitrary", "arbitrary")),
        name="attention",
    )(q, k, v)


def _col_reduce(x, op):
    n, c = x.shape
    part = op(x.reshape(n // REDUCE_SLAB, REDUCE_SLAB, c), axis=0)
    return op(part, axis=0, keepdims=True)


def _attn_t_kernel(q_ref, kc_ref, kl_ref, vtc_ref, vtl_ref, o_ref, *, rep):
    n_heads = q_ref.shape[1] // HEAD_DIM
    tq = q_ref.shape[0] // ATTN_STREAMS
    kw = kl_ref.shape[2]
    w = rep * tq
    qs = [_pad_heads(q_ref[st * tq:(st + 1) * tq, :], n_heads, kw, rep) for st in range(ATTN_STREAMS)]
    chunks = [(kc_ref, vtc_ref, k0, min(k0 + ATTN_T_CHUNK, kc_ref.shape[1])) for k0 in range(0, kc_ref.shape[1], ATTN_T_CHUNK)]
    chunks += [(kl_ref, vtl_ref, k0, min(k0 + ATTN_T_CHUNK, kl_ref.shape[1])) for k0 in range(0, kl_ref.shape[1], ATTN_T_CHUNK)]
    scores = lambda st, c: _dot_nt(c[0][0, c[2]:c[3], :], qs[st])
    m = [None] * ATTN_STREAMS
    acc = [None] * ATTN_STREAMS
    s_next = [scores(st, chunks[0]) for st in range(ATTN_STREAMS)]
    for i, (_, vt_ref, k0, k1) in enumerate(chunks):
        for st in range(ATTN_STREAMS):
            s = s_next[st]
            m_new = _col_reduce(s, jnp.max)
            if m[st] is not None:
                m_new = jnp.maximum(m[st], m_new)
            p = jnp.exp2(s - m_new).astype(BF16)
            acc_new = _dot(vt_ref[0, :, k0:k1], p)
            if i + 1 < len(chunks):
                s_next[st] = scores(st, chunks[i + 1])
            if m[st] is not None:
                acc_new = jnp.exp2(m[st] - m_new) * acc[st] + acc_new
            m[st], acc[st] = m_new, acc_new
    lo_half = lax.broadcasted_iota(jnp.int32, (tq, LANES), 1) < HEAD_DIM
    for st in range(ATTN_STREAMS):
        a = acc[st]
        res = jnp.concatenate([a[j * HEAD_DIM:(j + 1) * HEAD_DIM, j * w:(j + 1) * w] / a[kw:kw + 1, j * w:(j + 1) * w]
                               for j in range(n_heads // rep)], axis=0)
        t0 = res[:, :tq].T
        t1 = res[:, tq:].T
        out = jnp.concatenate([jnp.where(lo_half, t0, pltpu.roll(t1, HEAD_DIM, 1)),
                               jnp.where(lo_half, pltpu.roll(t0, HEAD_DIM, 1), t1)], axis=1)
        o_ref[st * tq:(st + 1) * tq, :] = out.astype(o_ref.dtype)


def attention_t(q, k_ctx, k_seq, vt_ctx, vt_seq):
    r, gq = q.shape
    nb, seq_len, kw = k_seq.shape
    assert gq == 4 * HEAD_DIM and kw == 2 * HEAD_DIM
    tq = ATTN_STREAMS * LANES
    assert seq_len % tq == 0
    nq = seq_len // tq
    whole = lambda a: pl.BlockSpec((1,) + a.shape[1:], lambda i, j: (i, 0, 0))
    return pl.pallas_call(
        functools.partial(_attn_t_kernel, rep=2),
        grid=(nb, nq),
        in_specs=[pl.BlockSpec((tq, gq), lambda i, j: (i * nq + j, 0)),
                  whole(k_ctx), whole(k_seq), whole(vt_ctx), whole(vt_seq)],
        out_specs=pl.BlockSpec((tq, gq), lambda i, j: (i * nq + j, 0)),
        out_shape=jax.ShapeDtypeStruct((r, gq), BF16),
        compiler_params=_params(("arbitrary", "arbitrary")),
        name="attention_t",
    )(q, k_ctx, k_seq, vt_ctx, vt_seq)


def _na_kernel(p_ref, pc_ref, bias_ref, o_ref, *, rows, kh):
    g = o_ref.shape[2]
    n_heads = g // HEAD_DIM
    band = kh * GRID_W
    kc = pc_ref[0, :, g:2 * g]
    vc = pc_ref[0, :, 2 * g:3 * g]

    def one_row(r, carry):
        rs = jnp.clip(r - kh // 2, 0, rows - kh)
        q0 = pl.multiple_of(r * GRID_W, GRID_W)
        k0 = pl.multiple_of(rs * GRID_W, GRID_W)
        q = _pad_heads(p_ref[0, pl.ds(q0, GRID_W), 0:g], n_heads, g, 1)
        s1 = _dot_nt(q, p_ref[0, pl.ds(k0, band), g:2 * g]) + bias_ref[r - rs]
        s2 = _dot_nt(q, kc)
        m = jnp.maximum(jnp.max(s1, axis=-1, keepdims=True), jnp.max(s2, axis=-1, keepdims=True))
        p1 = jnp.exp(s1 - m)
        p2 = jnp.exp(s2 - m)
        l = jnp.sum(p1, axis=-1, keepdims=True) + jnp.sum(p2, axis=-1, keepdims=True)
        res = (_dot(p1.astype(BF16), p_ref[0, pl.ds(k0, band), 2 * g:3 * g]) + _dot(p2.astype(BF16), vc)) / l
        o_ref[0, pl.ds(q0, GRID_W), :] = _gather_heads(res, n_heads, 1).astype(o_ref.dtype)
        return carry

    lax.fori_loop(0, rows, one_row, 0, unroll=8)


def neighbourhood_attention(p, pc, bias):
    nb, l, g3 = p.shape
    g = g3 // 3
    lc = pc.shape[1]
    rows = l // GRID_W
    kh = min(NA_ROWS, rows)
    return pl.pallas_call(
        functools.partial(_na_kernel, rows=rows, kh=kh),
        grid=(nb,),
        in_specs=[pl.BlockSpec((1, l, g3), lambda i: (i, 0, 0)),
                  pl.BlockSpec((1, lc, g3), lambda i: (i, 0, 0)),
                  pl.BlockSpec(bias.shape, lambda i: (0, 0, 0))],
        out_specs=pl.BlockSpec((1, l, g), lambda i: (i, 0, 0)),
        out_shape=jax.ShapeDtypeStruct((nb, l, g), BF16),
        compiler_params=_params(("arbitrary",)),
        name="neighbourhood_attention",
    )(p, pc, bias)


def _na_bias_kernel(rpb_ref, o_ref, *, kh):
    h = pl.program_id(0)
    q = lax.broadcasted_iota(jnp.int32, (GRID_W, GRID_W), 0)
    k = lax.broadcasted_iota(jnp.int32, (GRID_W, GRID_W), 1)
    start = jnp.clip(q - NA_COLS // 2, 0, GRID_W - NA_COLS)
    in_win = (k >= start) & (k < start + NA_COLS)
    off = k - q + NA_COLS - 1
    neg = jnp.full((GRID_W, GRID_W), -jnp.inf, F32)
    blocks = []
    for ro in range(2 * NA_ROWS - 1):
        t = neg
        for c in range(2 * NA_COLS - 1):
            t = jnp.where(off == c, rpb_ref[h, ro, c], t)
        blocks.append(jnp.where(in_win, t, neg))
    for di in range(kh):
        for i in range(kh):
            o_ref[di, :, i * GRID_W:(i + 1) * GRID_W] = blocks[i - di + NA_ROWS - 1]


def na_bias_table(rpb, rows):
    kh = min(NA_ROWS, rows)
    nh = rpb.shape[0]
    return pl.pallas_call(
        functools.partial(_na_bias_kernel, kh=kh),
        grid=(nh,),
        in_specs=[pl.BlockSpec(memory_space=pltpu.SMEM)],
        out_specs=pl.BlockSpec((kh, GRID_W, kh * GRID_W), lambda i: (0, i, 0)),
        out_shape=jax.ShapeDtypeStruct((kh, nh * GRID_W, kh * GRID_W), F32),
        compiler_params=_params(("arbitrary",)),
        name="na_bias",
    )(rpb)


def _pool_kernel(x_ref, inv_ref, w_ref, scale_ref, o_ref):
    x = x_ref[0]
    n, g = x.shape
    pg = g // len(POOL_WINDOWS)
    pad = max(POOL_WINDOWS) // 2
    zeros = jnp.zeros((pad, g), F32)
    xe = jnp.concatenate([zeros, x, zeros], axis=0)
    ne = n + 2 * pad
    group = lax.broadcasted_iota(jnp.int32, xe.shape, 1) // pg
    fwd = xe
    bwd = xe
    cur = 1
    total = jnp.zeros_like(xe)
    for j, w in enumerate(POOL_WINDOWS):
        half = w // 2
        while cur < half:
            fwd = fwd + pltpu.roll(fwd, ne - cur, 0)
            bwd = bwd + pltpu.roll(bwd, cur, 0)
            cur *= 2
        total = jnp.where(group == j, pltpu.roll(bwd, 1, 0) + fwd, total)
    diff = total[pad:pad + n] * inv_ref[...] - x
    o_ref[0] = (_dot(diff.astype(BF16), w_ref[...]) * scale_ref[...]).astype(o_ref.dtype)


def _pool_inverse_counts(n, g):
    t = np.arange(n)
    cols = []
    for w in POOL_WINDOWS:
        lo = np.clip(t - w // 2, 0, n)
        hi = np.clip(t - w // 2 + w, 0, n)
        cols.append(np.repeat((1.0 / (hi - lo))[:, None], g // len(POOL_WINDOWS), axis=1))
    return jnp.asarray(np.concatenate(cols, axis=1), dtype=F32)


def pool_mixer(p, w_bd, scale):
    b, n, g = p.shape
    return pl.pallas_call(
        _pool_kernel,
        grid=(b,),
        in_specs=[pl.BlockSpec((1, n, g), lambda i: (i, 0, 0)),
                  pl.BlockSpec((n, g), lambda i: (0, 0)),
                  pl.BlockSpec((g, g), lambda i: (0, 0)),
                  pl.BlockSpec((1, g), lambda i: (0, 0))],
        out_specs=pl.BlockSpec((1, n, g), lambda i: (i, 0, 0)),
        out_shape=jax.ShapeDtypeStruct((b, n, g), BF16),
        compiler_params=_params(("arbitrary",)),
        name="pool_mixer",
    )(p, _pool_inverse_counts(n, g), w_bd, scale)


def _wkv_scan_kernel(rf_ref, rb_ref, kkf_ref, kkb_ref, decf_ref, bf_ref, kmf_ref, decb_ref, bb_ref, kmb_ref,
                     vtf_ref, vtb_ref, e_ref, s0_ref, yf_ref, yb_ref, send_ref, s_scr):
    npair, nb, t_blk, _ = rf_ref.shape
    g = npair * LANES
    step_id = pl.program_id(0)

    @pl.when(step_id == 0)
    def _():
        s_scr[...] = s0_ref[...]

    yf_ref[...] = jnp.zeros_like(yf_ref)
    yb_ref[...] = jnp.zeros_like(yb_ref)
    lane = lax.broadcasted_iota(jnp.int32, (HEAD_DIM, g), 1) % SCAN_SUB
    lane128 = lax.broadcasted_iota(jnp.int32, (HEAD_DIM, LANES), 1)
    dirs = ((rf_ref, kkf_ref, decf_ref, bf_ref, kmf_ref, vtf_ref, yf_ref),
            (rb_ref, kkb_ref, decb_ref, bb_ref, kmb_ref, vtb_ref, yb_ref))
    ents = HEAD_DIM // 4
    for d in range(2):
        pltpu.matmul_push_rhs(e_ref[...], 0, d)
        pltpu.matmul_acc_lhs(0, jnp.zeros((16, g), BF16), d, load_staged_rhs=0)
        pltpu.matmul_pop(0, (16, g), F32, d)

    def write_y(d, b, y, tt_done, valid):
        y_ref = dirs[d][-1]
        tl = tt_done if d == 0 else t_blk - 1 - tt_done
        tl = jnp.clip(tl, 0, t_blk - 1)
        hit = (lane == tl % SCAN_SUB) & valid
        pltpu.store(y_ref.at[b, tl // SCAN_SUB], y, mask=hit)

    def rows_at(d, tt):
        tl = jnp.clip(tt if d == 0 else t_blk - 1 - tt, 0, t_blk - 1)

        def row(ref, b):
            slabs = [ref[j, b, pl.ds(tl, 8, stride=0), :] for j in range(npair)]
            return jnp.concatenate([jnp.concatenate(slabs, axis=1)] * (HEAD_DIM // 8), axis=0)

        return row

    for d, (r_ref, kk_ref, dec_ref, b_ref, km_ref, vt_ref, y_ref) in enumerate(dirs):
        row0 = rows_at(d, 0)
        for b in range(nb):
            pltpu.matmul_acc_lhs(b * ents, s_scr[d, b].astype(BF16) * row0(kk_ref, b).astype(BF16), d)

    def one_token(tt, carry):
        for b in range(nb):
            for d, (r_ref, kk_ref, dec_ref, b_ref, km_ref, vt_ref, y_ref) in enumerate(dirs):
                tl = tt if d == 0 else t_blk - 1 - tt
                row = rows_at(d, tt)
                nxt = rows_at(d, tt + 1)
                idx = (lane128 // SCAN_SUB) * SCAN_SUB + tl % SCAN_SUB
                sa = pltpu.matmul_pop(b * ents, (HEAD_DIM, g), F32, d)
                write_y(d, b, pltpu.matmul_pop((nb + b) * ents, (HEAD_DIM, g), F32, d), tt - 1, tt > 0)
                vcol = jnp.concatenate([jnp.take_along_axis(vt_ref[b, tl // SCAN_SUB][:, j * LANES:(j + 1) * LANES], idx, axis=1)
                                        for j in range(g // LANES)], axis=1)
                s = s_scr[d, b] * row(dec_ref, b) - sa * row(b_ref, b) + vcol * row(km_ref, b)
                s_scr[d, b] = s
                s_bf = s.astype(BF16)
                pltpu.matmul_acc_lhs((nb + b) * ents, s_bf * row(r_ref, b).astype(BF16), d)
                pltpu.matmul_acc_lhs(b * ents, s_bf * nxt(kk_ref, b).astype(BF16), d)
        return carry

    lax.fori_loop(0, t_blk, one_token, 0, unroll=8)
    for d in range(2):
        for b in range(nb):
            write_y(d, b, pltpu.matmul_pop((nb + b) * ents, (HEAD_DIM, g), F32, d), t_blk - 1, True)
            pltpu.matmul_pop(b * ents, (HEAD_DIM, g), F32, d)

    @pl.when(step_id == pl.num_programs(0) - 1)
    def _():
        send_ref[...] = s_scr[...]


def wkv_scan(r, kk, dec_f, b_f, km_f, dec_b, b_b, km_b, vt, eseg, s0):
    npair, nb, l, _ = kk.shape
    g = npair * LANES
    t_blk = SCAN_BLOCK
    nsub = t_blk // SCAN_SUB
    nblk = l // t_blk
    assert l % t_blk == 0
    seq_f = pl.BlockSpec((npair, nb, t_blk, LANES), lambda i: (0, 0, i, 0))
    seq_b = pl.BlockSpec((npair, nb, t_blk, LANES), lambda i: (0, 0, nblk - 1 - i, 0))
    vt_f = pl.BlockSpec((nb, nsub, HEAD_DIM, g), lambda i: (0, i, 0, 0))
    vt_b = pl.BlockSpec((nb, nsub, HEAD_DIM, g), lambda i: (0, nblk - 1 - i, 0, 0))
    state = pl.BlockSpec(s0.shape, lambda i: (0, 0, 0, 0))
    return pl.pallas_call(
        _wkv_scan_kernel,
        grid=(nblk,),
        in_specs=[seq_f, seq_b, seq_f, seq_b, seq_f, seq_f, seq_f, seq_b, seq_b, seq_b,
                  vt_f, vt_b, pl.BlockSpec(eseg.shape, lambda i: (0, 0)), state],
        out_specs=[vt_f, vt_b, state],
        out_shape=[jax.ShapeDtypeStruct(vt.shape, F32), jax.ShapeDtypeStruct(vt.shape, F32),
                   jax.ShapeDtypeStruct(s0.shape, F32)],
        scratch_shapes=[pltpu.VMEM(s0.shape, F32)],
        compiler_params=_params(("arbitrary",)),
        name="wkv_scan",
    )(r, r, kk, kk, dec_f, b_f, km_f, dec_b, b_b, km_b, vt, vt, eseg, s0)


def _block_ones(n, seg):
    idx = np.arange(n) // seg
    return jnp.asarray(idx[:, None] == idx[None, :], dtype=BF16)


def _padded_rows(w, offset, n):
    return jnp.zeros((n, w.shape[1]), w.dtype).at[offset:offset + w.shape[0]].set(w)


def _rope_tables(n_tokens, n_rep):
    t = jnp.arange(n_tokens)
    row = (t // GRID_W).astype(F32)
    col = (t % GRID_W).astype(F32)
    n_freq = HEAD_DIM // 4
    inv_freq = ROPE_THETA ** (-jnp.arange(n_freq, dtype=F32) / n_freq)
    ang = jnp.concatenate([row[:, None] * inv_freq, col[:, None] * inv_freq], axis=-1)
    cos, sin = jnp.cos(ang), jnp.sin(ang)
    cos_h = jnp.concatenate([cos, cos], axis=-1)
    sin_h = jnp.concatenate([-sin, sin], axis=-1)
    return jnp.tile(cos_h, (1, n_rep)), jnp.tile(sin_h, (1, n_rep))


def kernel(x, c, ctx, c_ctx, ada_w, ada_b, norm1_g, norm2_g, w_in, w_out, rwkv_w0, rwkv_w2, rwkv_a0, rwkv_a2,
           rwkv_k_k, rwkv_k_a, rwkv_r_k, rwkv_g2, rwkv_gn_w, rwkv_gn_b, na_rpb, gqa_q_gain, gqa_k_gain, pool_w,
           pool_scale, mlp_w1, mlp_w2, final_g):
    nb, l, d = x.shape
    lc = ctx.shape[1]
    depth = ada_w.shape[0]
    g = d // 4
    nh = g // HEAD_DIM
    nkv = nh // 2
    kvw = nkv * HEAD_DIM
    dr = rwkv_w2.shape[2]
    ir = rwkv_a2.shape[2]
    gr = rwkv_g2.shape[1]
    assert 2 * dr + 2 * ir + gr <= g and nb + 1 <= 8
    rows = l // GRID_W

    cpad = jnp.zeros((8, d), F32).at[:nb].set(c).at[nb].set(c_ctx)
    mods = adaln(cpad, ada_w, ada_b)

    eseg = _block_ones(g, HEAD_DIM)
    ekv = _block_ones(kvw, HEAD_DIM)
    cos_q, sin_q = _rope_tables(l, nh)
    ones_c, zeros_c = jnp.ones((lc, g), F32), jnp.zeros((lc, g), F32)

    splits = np.cumsum([0, g, g, g, dr, dr, ir, ir, gr, g, g, g, g, kvw, kvw, g])
    lowrank_w = splits[8] - splits[3]

    xl = x.reshape(nb * l, d)
    xc = ctx.reshape(nb * lc, d)
    zero_state = jnp.zeros((2, nb, HEAD_DIM, g), F32)

    for i in range(depth):
        need_ctx_out = i < depth - 1
        mod_l = mods[i, :nb].reshape(nb, N_MOD, 1, d)
        mod_c = mods[i, nb].reshape(N_MOD, 1, 1, d)
        ml = [mod_l[:, k] for k in range(N_MOD)]
        mc = [mod_c[k] for k in range(N_MOD)]

        wi = w_in[i]
        w_inp = jnp.concatenate([wi[:, :splits[8]], jnp.zeros((d, g - lowrank_w), F32),
                                 wi[:, splits[8]:splits[9]] * HEAD_DIM ** -0.5, wi[:, splits[9]:]], axis=1).astype(BF16)
        g1 = norm1_g[i].reshape(1, d)
        g2 = norm2_g[i].reshape(1, d)

        w2p = jnp.stack([_padded_rows(rwkv_w2[i, 0], 0, g), _padded_rows(rwkv_w2[i, 1], dr, g)]).astype(BF16)
        a2p = jnp.stack([_padded_rows(rwkv_a2[i, 0], 2 * dr, g),
                         _padded_rows(rwkv_a2[i, 1], 2 * dr + ir, g)]).astype(BF16)
        g2p = _padded_rows(rwkv_g2[i], 2 * dr + 2 * ir, g).astype(BF16)
        r_k = rwkv_r_k[i].reshape(1, g)
        gn_w = rwkv_gn_w[i].reshape(1, g)
        gn_b = rwkv_gn_b[i].reshape(1, g)
        rwkv_params = (rwkv_w0[i], rwkv_a0[i], w2p, a2p, rwkv_k_k[i].reshape(1, g), rwkv_k_a[i].reshape(1, g), eseg)
        q_gain = jnp.tile(gqa_q_gain[i], nh).reshape(1, g)
        k_gain = jnp.tile(gqa_k_gain[i], nkv).reshape(1, kvw)

        proj_l = inproj(xl, g1, ml[0], ml[1], w_inp, l, rwkv_params, (q_gain, k_gain, cos_q, sin_q, ekv), l, True)
        proj_c = inproj(xc, g1, mc[0], mc[1], w_inp, nb * lc, rwkv_params, (q_gain, k_gain, ones_c, zeros_c, ekv), lc, False)
        pa_l, pb_l, pd_l = proj_l[:3]
        pa_c, pb_c, pd_c = proj_c[:3]

        def scan_inputs(proj, seq):
            rows4 = [t.reshape(g // LANES, nb, seq, LANES) for t in proj[3:11]]
            return rows4, proj[11].reshape(nb, seq // SCAN_SUB, HEAD_DIM, g)

        rows4_c, vt_c = scan_inputs(proj_c, lc)
        yf_c, yb_c, state_c = wkv_scan(*rows4_c, vt_c, eseg, zero_state)
        rows4_l, vt_l = scan_inputs(proj_l, l)
        yf_l, yb_l, _ = wkv_scan(*rows4_l, vt_l, eseg, state_c)

        bias = na_bias_table(na_rpb[i], rows)
        b_l = neighbourhood_attention(pb_l.reshape(nb, l, 3 * g), pb_c.reshape(nb, lc, 3 * g), bias).reshape(nb * l, g)

        qc_l, kc_l = proj_l[12:14]
        qc_c, kc_c, vc_c = proj_c[12:15]
        kc_c, vc_c = kc_c.reshape(nb, lc, kvw), vc_c.reshape(nb, lc, kvw)
        c_l = attention_t(qc_l, kc_c, kc_l.reshape(nb, l, kvw), proj_c[15], proj_l[15])

        w_bd = jax.scipy.linalg.block_diag(*[pool_w[i, k] for k in range(len(POOL_WINDOWS))]).astype(BF16)
        p_scale = pool_scale[i].reshape(1, g)
        d_l = pool_mixer(pd_l.reshape(nb, l, g), w_bd, p_scale).reshape(nb * l, g)

        w_out4 = w_out[i].reshape(4, g, d).astype(BF16)
        w1 = mlp_w1[i].astype(BF16)
        w2 = mlp_w2[i].astype(BF16)
        readout_params = (r_k, g2p, gn_w, gn_b, eseg)
        xl = mix_mlp(yf_l.reshape(-1, HEAD_DIM, g), yb_l.reshape(-1, HEAD_DIM, g), pa_l, proj_l[7], proj_l[10],
                     readout_params, (b_l, c_l, d_l), w_out4, xl, ml[2], g2, ml[3], ml[4], ml[5], w1, w2, l,
                     final_gain=None if need_ctx_out else final_g.reshape(1, d))

        if need_ctx_out:
            b_c = attention(pb_c[:, :g], pb_c[:, g:2 * g].reshape(nb, lc, g), pb_c[:, 2 * g:].reshape(nb, lc, g), lc, False)
            c_c = attention(qc_c, kc_c, vc_c, lc, True)
            d_c = pool_mixer(pd_c.reshape(nb, lc, g), w_bd, p_scale).reshape(nb * lc, g)
            xc = mix_mlp(yf_c.reshape(-1, HEAD_DIM, g), yb_c.reshape(-1, HEAD_DIM, g), pa_c, proj_c[7], proj_c[10],
                         readout_params, (b_c, c_c, d_c), w_out4, xc, mc[2], g2, mc[3], mc[4], mc[5], w1, w2, nb * lc)

    return xl.reshape(nb, l, d)
```
